```python
import math
import jax, jax.numpy as jnp
from jax import lax
import numpy as np

D_MODEL = 2048
BATCH = 4
SEQ = 4096
DEPTH = 2

GRID_W = 64
CTX_LEN = 256
MIX_WIDTH = D_MODEL
ATTN_WIDTH = MIX_WIDTH // 2
SSM_WIDTH = MIX_WIDTH - ATTN_WIDTH
HEAD_DIM = 128
N_Q_HEADS = ATTN_WIDTH // HEAD_DIM
N_KV_HEADS = 2
Q_PER_KV = N_Q_HEADS // N_KV_HEADS
KV_WIDTH = N_KV_HEADS * HEAD_DIM
WINDOW = 128
BLOCK = 128
ROPE_THETA = 10000.0
SSM_GROUP = 16
N_SSM_GROUPS = SSM_WIDTH // SSM_GROUP
SSM_STATE = 64
DT_MIN = 0.001
DT_MAX = 0.1
N_EXPERTS = 16
N_EXPERT_GROUPS = 4
EXPERTS_PER_GROUP = N_EXPERTS // N_EXPERT_GROUPS
TOP_K = 2
D_EXPERT = D_MODEL // 2
IN_COLS = ATTN_WIDTH + 2 * KV_WIDTH + SSM_WIDTH
N_MOD = 6
DEEPNORM_ALPHA = (2 * DEPTH) ** 0.25
DEEPNORM_BETA = (8 * DEPTH) ** -0.25
LN_EPS = 1e-5
NEG_INF = -1e30

kernel_name = "hymba_swa_s5_grouped_moe_deepnorm"


def layer_norm(x, g, b):
    xf = x.astype(jnp.float32)
    mu = jnp.mean(xf, axis=-1, keepdims=True)
    var = jnp.mean(jnp.square(xf - mu), axis=-1, keepdims=True)
    y = (xf - mu) * lax.rsqrt(var + LN_EPS) * g.astype(jnp.float32) + b.astype(jnp.float32)
    return y.astype(x.dtype)


def rms_norm(x, g):
    xf = x.astype(jnp.float32)
    y = xf * lax.rsqrt(jnp.mean(jnp.square(xf), axis=-1, keepdims=True) + LN_EPS) * g.astype(jnp.float32)
    return y.astype(x.dtype)


def axial_rope_tables(n_rows):
    half = HEAD_DIM // 2
    inv_freq = ROPE_THETA ** (-jnp.arange(0, half, 2, dtype=jnp.float32) / half)
    row = jnp.repeat(jnp.arange(n_rows, dtype=jnp.float32), GRID_W)
    col = jnp.tile(jnp.arange(GRID_W, dtype=jnp.float32), n_rows)
    ang_r = row[:, None] * inv_freq[None, :]
    ang_c = col[:, None] * inv_freq[None, :]
    return (jnp.cos(ang_r), jnp.sin(ang_r), jnp.cos(ang_c), jnp.sin(ang_c))


def _rotate(x, cos, sin):
    m = x.shape[-1] // 2
    x1, x2 = x[..., :m], x[..., m:]
    return jnp.concatenate([x1 * cos - x2 * sin, x1 * sin + x2 * cos], axis=-1)


def apply_axial_rope(x, tables):
    cos_r, sin_r, cos_c, sin_c = [t[:, None, :].astype(x.dtype) for t in tables]
    half = HEAD_DIM // 2
    return jnp.concatenate([_rotate(x[..., :half], cos_r, sin_r),
                            _rotate(x[..., half:], cos_c, sin_c)], axis=-1)


def window_attention_latent(q, k, v, kc, vc, sink):
    bsz, n_tok = q.shape[:2]
    n_blk = n_tok // BLOCK
    scale = HEAD_DIM ** -0.5
    qb = q.reshape(bsz, n_blk, BLOCK, N_KV_HEADS, Q_PER_KV, HEAD_DIM)
    pad = ((0, 0), (BLOCK, BLOCK), (0, 0), (0, 0))
    kp = jnp.pad(k, pad).reshape(bsz, n_blk + 2, BLOCK, N_KV_HEADS, HEAD_DIM)
    vp = jnp.pad(v, pad).reshape(bsz, n_blk + 2, BLOCK, N_KV_HEADS, HEAD_DIM)
    kw = jnp.concatenate([kp[:, :-2], kp[:, 1:-1], kp[:, 2:]], axis=2)
    vw = jnp.concatenate([vp[:, :-2], vp[:, 1:-1], vp[:, 2:]], axis=2)
    s_band = jnp.einsum('bnqhgd,bnkhd->bnhgqk', qb, kw, preferred_element_type=jnp.float32) * scale
    blk = jnp.arange(n_blk)[:, None, None] * BLOCK
    qpos = blk + jnp.arange(BLOCK)[None, :, None]
    kpos = blk - BLOCK + jnp.arange(3 * BLOCK)[None, None, :]
    valid = (jnp.abs(qpos - kpos) <= WINDOW) & (kpos >= 0) & (kpos < n_tok)
    s_band = jnp.where(valid[None, :, None, None], s_band, NEG_INF)
    s_ctx = jnp.einsum('bnqhgd,bkhd->bnhgqk', qb, kc, preferred_element_type=jnp.float32) * scale
    s_sink = jnp.broadcast_to(sink.astype(jnp.float32).reshape(N_KV_HEADS, Q_PER_KV)[None, None, :, :, None, None],
                              s_band.shape[:-1] + (1,))
    p = jax.nn.softmax(jnp.concatenate([s_band, s_ctx, s_sink], axis=-1), axis=-1)
    n_band = 3 * BLOCK
    n_ctx = kc.shape[1]
    p_band = p[..., :n_band].astype(v.dtype)
    p_ctx = p[..., n_band:n_band + n_ctx].astype(v.dtype)
    o = (jnp.einsum('bnhgqk,bnkhd->bnqhgd', p_band, vw)
         + jnp.einsum('bnhgqk,bkhd->bnqhgd', p_ctx, vc))
    return o.reshape(bsz, n_tok, ATTN_WIDTH)


def context_attention(qc, kc, vc, sink):
    bsz, n_ctx = qc.shape[:2]
    scale = HEAD_DIM ** -0.5
    qg = qc.reshape(bsz, n_ctx, N_KV_HEADS, Q_PER_KV, HEAD_DIM)
    s = jnp.einsum('bqhgd,bkhd->bhgqk', qg, kc, preferred_element_type=jnp.float32) * scale
    s_sink = jnp.broadcast_to(sink.astype(jnp.float32).reshape(N_KV_HEADS, Q_PER_KV)[None, :, :, None, None],
                              s.shape[:-1] + (1,))
    p = jax.nn.softmax(jnp.concatenate([s, s_sink], axis=-1), axis=-1)
    o = jnp.einsum('bhgqk,bkhd->bqhgd', p[..., :n_ctx].astype(vc.dtype), vc)
    return o.reshape(bsz, n_ctx, ATTN_WIDTH)


def s5_discretise(lam_re, lam_im, log_dt, b_re, b_im):
    lam = lax.complex(lam_re.astype(jnp.float32), lam_im.astype(jnp.float32))
    dt = jnp.exp(log_dt.astype(jnp.float32))[:, None]
    a_bar = jnp.exp(lam * dt)
    b_bar = ((a_bar - 1.0) / lam)[..., None] * lax.complex(b_re.astype(jnp.float32), b_im.astype(jnp.float32))
    return a_bar, b_bar


def _linear_combine(left, right):
    a_l, b_l = left
    a_r, b_r = right
    return (a_r * a_l, a_r * b_l + b_r)


def s5_scan(u, a_bar, b_bar, h0, reverse):
    n_steps = u.shape[1]
    bu = lax.complex(jnp.einsum('btgp,gnp->btgn', u, b_bar.real),
                     jnp.einsum('btgp,gnp->btgn', u, b_bar.imag))
    if reverse:
        bu = jnp.flip(bu, axis=1)
    bu = bu.at[:, 0].add(a_bar[None] * h0)
    a = jnp.broadcast_to(a_bar, (1, n_steps) + a_bar.shape)
    _, h = lax.associative_scan(_linear_combine, (a, bu), axis=1)
    h_final = h[:, -1]
    if reverse:
        h = jnp.flip(h, axis=1)
    return h, h_final


def s5_readout(h, c_re, c_im):
    return jnp.einsum('btgn,gpn->btgp', h.real, c_re) - jnp.einsum('btgn,gpn->btgp', h.imag, c_im)


def s5_glu(y, w_glu, b_glu):
    z = jax.nn.gelu(y)
    return z * jax.nn.sigmoid(z @ w_glu + b_glu)


def s5_mixer(s, sc, lam_re, lam_im, log_dt, b_re, b_im, c_re, c_im, d_skip, w_glu, b_glu, need_ctx_out):
    bsz, n_tok = s.shape[:2]
    n_ctx = sc.shape[1]
    u = s.astype(jnp.float32).reshape(bsz, n_tok, N_SSM_GROUPS, SSM_GROUP)
    uc = sc.astype(jnp.float32).reshape(bsz, n_ctx, N_SSM_GROUPS, SSM_GROUP)
    dsk = d_skip.astype(jnp.float32).reshape(N_SSM_GROUPS, SSM_GROUP)
    y = dsk * u
    yc = dsk * uc if need_ctx_out else None
    h_zero = jnp.zeros((bsz, N_SSM_GROUPS, SSM_STATE), jnp.complex64)
    for direction, reverse in enumerate((False, True)):
        a_bar, b_bar = s5_discretise(lam_re[direction], lam_im[direction], log_dt[direction],
                                     b_re[direction], b_im[direction])
        cr = c_re[direction].astype(jnp.float32)
        ci = c_im[direction].astype(jnp.float32)
        hc, hc_final = s5_scan(uc, a_bar, b_bar, h_zero, reverse)
        h, _ = s5_scan(u, a_bar, b_bar, hc_final, reverse)
        y = y + s5_readout(h, cr, ci)
        if need_ctx_out:
            yc = yc + s5_readout(hc, cr, ci)
    out = s5_glu(y.reshape(bsz, n_tok, SSM_WIDTH).astype(s.dtype), w_glu, b_glu)
    out_c = s5_glu(yc.reshape(bsz, n_ctx, SSM_WIDTH).astype(s.dtype), w_glu, b_glu) if need_ctx_out else None
    return out, out_c


def hybrid_mixer(u, uc, rope, w_in, sink, lam_re, lam_im, log_dt, b_re, b_im, c_re, c_im, d_skip,
                 w_glu, b_glu, g_attn, g_ssm, w_out, need_ctx_out):
    bsz, n_tok = u.shape[:2]
    n_ctx = uc.shape[1]
    splits = (ATTN_WIDTH, ATTN_WIDTH + KV_WIDTH, ATTN_WIDTH + 2 * KV_WIDTH)
    q, k, v, s = jnp.split(u @ w_in, splits, axis=-1)
    qc, kc, vc, sc = jnp.split(uc @ w_in, splits, axis=-1)
    q = apply_axial_rope(q.reshape(bsz, n_tok, N_Q_HEADS, HEAD_DIM), rope)
    k = apply_axial_rope(k.reshape(bsz, n_tok, N_KV_HEADS, HEAD_DIM), rope)
    v = v.reshape(bsz, n_tok, N_KV_HEADS, HEAD_DIM)
    kc = kc.reshape(bsz, n_ctx, N_KV_HEADS, HEAD_DIM)
    vc = vc.reshape(bsz, n_ctx, N_KV_HEADS, HEAD_DIM)
    attn = window_attention_latent(q, k, v, kc, vc, sink)
    ssm, ssm_c = s5_mixer(s, sc, lam_re, lam_im, log_dt, b_re, b_im, c_re, c_im, d_skip, w_glu, b_glu, need_ctx_out)
    y = jnp.concatenate([rms_norm(attn, g_attn), rms_norm(ssm, g_ssm)], axis=-1) @ w_out
    if not need_ctx_out:
        return y, None
    attn_c = context_attention(qc.reshape(bsz, n_ctx, N_Q_HEADS, HEAD_DIM), kc, vc, sink)
    yc = jnp.concatenate([rms_norm(attn_c, g_attn), rms_norm(ssm_c, g_ssm)], axis=-1) @ w_out
    return y, yc


def grouped_top2_route(u, w_router, b_router):
    logits = (u @ w_router).astype(jnp.float32) + b_router.astype(jnp.float32)
    probs = jax.nn.softmax(logits, axis=-1)
    pg = probs.reshape(probs.shape[:-1] + (N_EXPERT_GROUPS, EXPERTS_PER_GROUP))
    group_score = jnp.sum(lax.top_k(pg, TOP_K)[0], axis=-1)
    group_mask = jax.nn.one_hot(jnp.argmax(group_score, axis=-1), N_EXPERT_GROUPS, dtype=jnp.bool_)
    masked = jnp.where(group_mask[..., None], pg, -1.0).reshape(probs.shape)
    top_v, top_i = lax.top_k(masked, TOP_K)
    w = top_v / jnp.sum(top_v, axis=-1, keepdims=True)
    return jnp.sum(jax.nn.one_hot(top_i, N_EXPERTS, dtype=jnp.float32) * w[..., None], axis=-2)


def moe_ffn(u, combine, w_gate, w_up, w_down):
    out = jnp.zeros(u.shape, u.dtype)
    for e in range(N_EXPERTS):
        h = jax.nn.silu(u @ w_gate[e]) * (u @ w_up[e])
        out = out + combine[..., e:e + 1].astype(u.dtype) * (h @ w_down[e])
    return out


def setup_inputs(seed: int = 0) -> dict:
    key = jax.random.key(seed)
    ks = jax.random.split(key, 32)
    f32 = jnp.float32

    def nrm(k, shape, scale):
        return scale * jax.random.normal(k, shape, f32)

    G, N, P, E, F = N_SSM_GROUPS, SSM_STATE, SSM_GROUP, N_EXPERTS, D_EXPERT
    return {
        "x": nrm(ks[0], (BATCH, SEQ, D_MODEL), 1.0),
        "c": nrm(ks[1], (BATCH, D_MODEL), 1.0),
        "ctx": nrm(ks[2], (BATCH, CTX_LEN, D_MODEL), 1.0),
        "c_ctx": nrm(ks[3], (D_MODEL,), 1.0),
        "w_mod": nrm(ks[4], (DEPTH, D_MODEL, N_MOD * D_MODEL), 0.5 * D_MODEL ** -0.5),
        "b_mod": nrm(ks[5], (DEPTH, N_MOD * D_MODEL), 0.02),
        "w_in": nrm(ks[6], (DEPTH, D_MODEL, IN_COLS), D_MODEL ** -0.5),
        "attn_sink": nrm(ks[7], (DEPTH, N_Q_HEADS), 0.5),
        "ssm_lambda_re": -0.5 * jnp.exp(nrm(ks[8], (DEPTH, 2, G, N), 0.05)),
        "ssm_lambda_im": math.pi * jnp.arange(N, dtype=f32) * jnp.ones((DEPTH, 2, G, 1), f32)
                         + nrm(ks[9], (DEPTH, 2, G, N), 0.01),
        "ssm_log_dt": jax.random.uniform(ks[10], (DEPTH, 2, G), f32, math.log(DT_MIN), math.log(DT_MAX)),
        "ssm_b_re": nrm(ks[11], (DEPTH, 2, G, N, P), (2.0 * P) ** -0.5),
        "ssm_b_im": nrm(ks[12], (DEPTH, 2, G, N, P), (2.0 * P) ** -0.5),
        "ssm_c_re": nrm(ks[13], (DEPTH, 2, G, P, N), (2.0 * N) ** -0.5),
        "ssm_c_im": nrm(ks[14], (DEPTH, 2, G, P, N), (2.0 * N) ** -0.5),
        "ssm_d": nrm(ks[15], (DEPTH, SSM_WIDTH), 1.0),
        "w_glu": nrm(ks[16], (DEPTH, SSM_WIDTH, SSM_WIDTH), SSM_WIDTH ** -0.5),
        "b_glu": nrm(ks[17], (DEPTH, SSM_WIDTH), 0.02),
        "g_attn_out": 1.0 + nrm(ks[18], (DEPTH, ATTN_WIDTH), 0.02),
        "g_ssm_out": 1.0 + nrm(ks[19], (DEPTH, SSM_WIDTH), 0.02),
        "w_out": nrm(ks[20], (DEPTH, MIX_WIDTH, D_MODEL), DEEPNORM_BETA * MIX_WIDTH ** -0.5),
        "ln1_g": 1.0 + nrm(ks[21], (DEPTH, D_MODEL), 0.02),
        "ln1_b": nrm(ks[22], (DEPTH, D_MODEL), 0.02),
        "w_router": nrm(ks[23], (D_MODEL, E), D_MODEL ** -0.5),
        "b_router": nrm(ks[24], (E,), 0.01),
        "w_expert_gate": nrm(ks[25], (DEPTH, E, D_MODEL, F), D_MODEL ** -0.5),
        "w_expert_up": nrm(ks[26], (DEPTH, E, D_MODEL, F), D_MODEL ** -0.5),
        "w_expert_down": nrm(ks[27], (DEPTH, E, F, D_MODEL), DEEPNORM_BETA * F ** -0.5),
        "ln2_g": 1.0 + nrm(ks[28], (DEPTH, D_MODEL), 0.02),
        "ln2_b": nrm(ks[29], (DEPTH, D_MODEL), 0.02),
    }


def reference(x, c, ctx, c_ctx, w_mod, b_mod, w_in, attn_sink, ssm_lambda_re, ssm_lambda_im, ssm_log_dt,
              ssm_b_re, ssm_b_im, ssm_c_re, ssm_c_im, ssm_d, w_glu, b_glu, g_attn_out, g_ssm_out, w_out,
              ln1_g, ln1_b, w_router, b_router, w_expert_gate, w_expert_up, w_expert_down, ln2_g, ln2_b):
    n_tok = x.shape[1]
    n_rows = n_tok // GRID_W
    rope = axial_rope_tables(n_rows)
    xc = ctx
    for i in range(DEPTH):
        last = i == DEPTH - 1
        mod = (jax.nn.silu(c) @ w_mod[i] + b_mod[i])[:, None, :]
        sh1, sc1, g1, sh2, sc2, g2 = jnp.split(mod, N_MOD, axis=-1)
        modc = jax.nn.silu(c_ctx) @ w_mod[i] + b_mod[i]
        sh1c, sc1c, g1c, sh2c, sc2c, g2c = jnp.split(modc, N_MOD, axis=-1)

        u = x * (1.0 + sc1) + sh1
        uc = xc * (1.0 + sc1c) + sh1c
        y, yc = hybrid_mixer(u, uc, rope, w_in[i], attn_sink[i], ssm_lambda_re[i], ssm_lambda_im[i],
                             ssm_log_dt[i], ssm_b_re[i], ssm_b_im[i], ssm_c_re[i], ssm_c_im[i], ssm_d[i],
                             w_glu[i], b_glu[i], g_attn_out[i], g_ssm_out[i], w_out[i], not last)
        x = layer_norm(DEEPNORM_ALPHA * x + g1 * y, ln1_g[i], ln1_b[i])

        u2 = x * (1.0 + sc2) + sh2
        if last:
            f = moe_ffn(u2, grouped_top2_route(u2, w_router, b_router),
                        w_expert_gate[i], w_expert_up[i], w_expert_down[i])
            x = layer_norm(DEEPNORM_ALPHA * x + g2 * f, ln2_g[i], ln2_b[i])
        else:
            xc = layer_norm(DEEPNORM_ALPHA * xc + g1c * yc, ln1_g[i], ln1_b[i])
            u2c = xc * (1.0 + sc2c) + sh2c
            u_all = jnp.concatenate([u2, u2c], axis=1)
            f_all = moe_ffn(u_all, grouped_top2_route(u_all, w_router, b_router),
                            w_expert_gate[i], w_expert_up[i], w_expert_down[i])
            f, fc = f_all[:, :n_tok], f_all[:, n_tok:]
            x = layer_norm(DEEPNORM_ALPHA * x + g2 * f, ln2_g[i], ln2_b[i])
            xc = layer_norm(DEEPNORM_ALPHA * xc + g2c * fc, ln2_g[i], ln2_b[i])
    return x
```

```python
import functools
import math

import jax
import jax.numpy as jnp
from jax import lax
from jax.experimental import pallas as pl
from jax.experimental.pallas import tpu as pltpu

F32 = jnp.float32
BF16 = jnp.bfloat16

HEAD_DIM = 128
N_Q_HEADS = 8
N_KV_HEADS = 2
Q_PER_KV = N_Q_HEADS // N_KV_HEADS
ATTN_WIDTH = N_Q_HEADS * HEAD_DIM
KV_WIDTH = N_KV_HEADS * HEAD_DIM
SSM_WIDTH = 1024
ATTN_BLOCK = 128
GRID_W = 64
ROPE_THETA = 10000.0
SSM_P = 16
SSM_G = SSM_WIDTH // SSM_P
SSM_N = 64
CHUNK = 64
GC = SSM_P * CHUNK
N_EXPERTS = 16
N_EXPERT_GROUPS = 4
EXPERTS_PER_GROUP = 4
N_MOD = 6
LN_EPS = 1e-5
NEG_INF = -1e30
LANES = 128
MOD_ROWS = 8
ROW_TILE = 256
MOE_TILE = 256
VMEM_LIMIT = 56 * 1024 * 1024


def _cparams(*sem):
    return pltpu.CompilerParams(dimension_semantics=sem, vmem_limit_bytes=VMEM_LIMIT)


def _dot(a, b):
    return jnp.dot(a, b, preferred_element_type=F32)


def _dot_nt(a, b):
    return lax.dot_general(a, b, (((1,), (1,)), ((), ())), preferred_element_type=F32)


def _mod_kernel(ct_ref, w_ref, b_ref, o_ref, ab_ref, *, n_rows, tn):
    d = ct_ref.shape[0]

    @pl.when((pl.program_id(0) == 0) & (pl.program_id(1) == 0))
    def _():
        ct = ct_ref[...]
        a = ct * jax.nn.sigmoid(ct)
        for r in range(n_rows):
            ab_ref[r] = jnp.broadcast_to(a[:, r:r + 1], (d, LANES))

    o_ref[...] = jnp.zeros(o_ref.shape, F32)
    for j in range(tn // LANES):
        w = w_ref[0, :, j * LANES:(j + 1) * LANES]
        bias = b_ref[0, :, j * LANES:(j + 1) * LANES]
        for r in range(n_rows):
            acc = jnp.sum(ab_ref[r] * w, axis=0, keepdims=True)
            o_ref[0, r:r + 1, j * LANES:(j + 1) * LANES] = acc + bias


def _modulation(c, c_ctx, w_mod, b_mod):
    depth, d, n_out = w_mod.shape
    n_rows = c.shape[0] + 1
    assert n_rows <= MOD_ROWS
    tn = 512
    ct = jnp.zeros((MOD_ROWS, d), F32).at[:c.shape[0]].set(c).at[c.shape[0]].set(c_ctx).T
    return pl.pallas_call(
        functools.partial(_mod_kernel, n_rows=n_rows, tn=tn),
        grid=(depth, n_out // tn),
        in_specs=[
            pl.BlockSpec((d, MOD_ROWS), lambda l, j: (0, 0)),
            pl.BlockSpec((1, d, tn), lambda l, j: (l, 0, j)),
            pl.BlockSpec((1, 1, tn), lambda l, j: (l, 0, j)),
        ],
        out_specs=pl.BlockSpec((1, MOD_ROWS, tn), lambda l, j: (l, 0, j)),
        out_shape=jax.ShapeDtypeStruct((depth, MOD_ROWS, n_out), F32),
        scratch_shapes=[pltpu.VMEM((n_rows, d, LANES), F32)],
        compiler_params=_cparams("arbitrary", "arbitrary"),
        name="modulation",
    )(ct, w_mod, b_mod.reshape(depth, 1, n_out))


def _rope(xh, cos, sin_signed):
    lane = lax.broadcasted_iota(jnp.int32, xh.shape, 1)
    swapped = jnp.where((lane % 64) < 32, pltpu.roll(xh, 96, 1), pltpu.roll(xh, 32, 1))
    return xh * cos + swapped * sin_signed


def _inproj_kernel(x_ref, mod_ref, w_ref, cos_ref, sin_ref, q_ref, k_ref, v_ref, s_ref, *, rope):
    sh = mod_ref[0, 0:1, :]
    sc = mod_ref[0, 1:2, :]
    u = (x_ref[0] * (1.0 + sc) + sh).astype(BF16)
    q = _dot(u, w_ref[:, 0:ATTN_WIDTH])
    k = _dot(u, w_ref[:, ATTN_WIDTH:ATTN_WIDTH + KV_WIDTH])
    v = _dot(u, w_ref[:, ATTN_WIDTH + KV_WIDTH:ATTN_WIDTH + 2 * KV_WIDTH])
    s = _dot(u, w_ref[:, ATTN_WIDTH + 2 * KV_WIDTH:])
    scale = HEAD_DIM ** -0.5
    if rope:
        cos = cos_ref[...]
        sin = sin_ref[...]
    for h in range(N_Q_HEADS):
        qh = q[:, h * HEAD_DIM:(h + 1) * HEAD_DIM]
        if rope:
            qh = _rope(qh, cos, sin)
        q_ref[0, :, h * HEAD_DIM:(h + 1) * HEAD_DIM] = (qh * scale).astype(BF16)
    for h in range(N_KV_HEADS):
        kh = k[:, h * HEAD_DIM:(h + 1) * HEAD_DIM]
        if rope:
            kh = _rope(kh, cos, sin)
        k_ref[0, :, h * HEAD_DIM:(h + 1) * HEAD_DIM] = kh.astype(BF16)
    v_ref[0] = v.astype(BF16)
    s_ref[0] = s.astype(BF16)


def _in_projection(x, mod3, row_of_batch, w_in_bf, cos_t, sin_t, rope):
    bsz, n_tok, d = x.shape
    tm = min(ROW_TILE, n_tok)
    n_cols = w_in_bf.shape[1]
    out = lambda w: jax.ShapeDtypeStruct((bsz, n_tok, w), BF16)
    ospec = lambda w: pl.BlockSpec((1, tm, w), lambda b, t: (b, t, 0))
    return pl.pallas_call(
        functools.partial(_inproj_kernel, rope=rope),
        grid=(bsz, n_tok // tm),
        in_specs=[
            pl.BlockSpec((1, tm, d), lambda b, t: (b, t, 0)),
            pl.BlockSpec((1, N_MOD, d), lambda b, t: (row_of_batch(b), 0, 0)),
            pl.BlockSpec((d, n_cols), lambda b, t: (0, 0)),
            pl.BlockSpec((tm, HEAD_DIM), lambda b, t: (t, 0)),
            pl.BlockSpec((tm, HEAD_DIM), lambda b, t: (t, 0)),
        ],
        out_specs=[ospec(ATTN_WIDTH), ospec(KV_WIDTH), ospec(KV_WIDTH), ospec(SSM_WIDTH)],
        out_shape=[out(ATTN_WIDTH), out(KV_WIDTH), out(KV_WIDTH), out(SSM_WIDTH)],
        compiler_params=_cparams("parallel", "parallel"),
        name="in_projection",
    )(x, mod3, w_in_bf, cos_t, sin_t)


def _rope_tables(n_tok):
    half = HEAD_DIM // 2
    inv_freq = ROPE_THETA ** (-jnp.arange(0, half, 2, dtype=F32) / half)
    t = jnp.arange(n_tok)
    row = (t // GRID_W).astype(F32)
    col = (t % GRID_W).astype(F32)
    ang_r = row[:, None] * inv_freq[None, :]
    ang_c = col[:, None] * inv_freq[None, :]
    cos_t = jnp.concatenate([jnp.cos(ang_r), jnp.cos(ang_r), jnp.cos(ang_c), jnp.cos(ang_c)], axis=-1)
    sin_t = jnp.concatenate([-jnp.sin(ang_r), jnp.sin(ang_r), -jnp.sin(ang_c), jnp.sin(ang_c)], axis=-1)
    return cos_t, sin_t


def _attn_kernel(sink_ref, q_ref, *refs, has_band, n_blk):
    if has_band:
        kp_ref, ko_ref, kn_ref, vp_ref, vo_ref, vn_ref, kc_ref, vc_ref, g_ref, o_ref = refs
    else:
        kc_ref, vc_ref, g_ref, o_ref = refs
    blk = pl.program_id(1)
    rows = Q_PER_KV * ATTN_BLOCK
    qi = lax.broadcasted_iota(jnp.int32, (rows, ATTN_BLOCK), 0) % ATTN_BLOCK
    kj = lax.broadcasted_iota(jnp.int32, (rows, ATTN_BLOCK), 1)
    row_head = lax.broadcasted_iota(jnp.int32, (rows, 1), 0) // ATTN_BLOCK
    heads = []
    for h in range(N_KV_HEADS):
        hs = slice(h * HEAD_DIM, (h + 1) * HEAD_DIM)
        q = jnp.concatenate(
            [q_ref[0, :, (h * Q_PER_KV + g) * HEAD_DIM:(h * Q_PER_KV + g + 1) * HEAD_DIM] for g in range(Q_PER_KV)],
            axis=0)
        sink = jnp.zeros((rows, 1), F32)
        for g in range(Q_PER_KV):
            sink = jnp.where(row_head == g, sink_ref[h * Q_PER_KV + g], sink)
        s_c = _dot_nt(q, kc_ref[0, :, hs])
        m = jnp.maximum(jnp.max(s_c, axis=-1, keepdims=True), sink)
        if has_band:
            s_p = jnp.where((kj >= qi) & (blk > 0), _dot_nt(q, kp_ref[0, :, hs]), NEG_INF)
            s_o = _dot_nt(q, ko_ref[0, :, hs])
            s_n = jnp.where((kj <= qi) & (blk < n_blk - 1), _dot_nt(q, kn_ref[0, :, hs]), NEG_INF)
            m = jnp.maximum(m, jnp.max(jnp.maximum(jnp.maximum(s_p, s_o), s_n), axis=-1, keepdims=True))
        p_c = jnp.exp(s_c - m)
        denom = jnp.sum(p_c, axis=-1, keepdims=True) + jnp.exp(sink - m)
        acc = _dot(p_c.astype(BF16), vc_ref[0, :, hs])
        if has_band:
            for s_x, v_ref in ((s_p, vp_ref), (s_o, vo_ref), (s_n, vn_ref)):
                p_x = jnp.exp(s_x - m)
                denom = denom + jnp.sum(p_x, axis=-1, keepdims=True)
                acc = acc + _dot(p_x.astype(BF16), v_ref[0, :, hs])
        o = acc / denom
        for g in range(Q_PER_KV):
            heads.append(o[g * ATTN_BLOCK:(g + 1) * ATTN_BLOCK, :])
    ss = heads[0] * heads[0]
    for o in heads[1:]:
        ss = ss + o * o
    inv = lax.rsqrt(jnp.sum(ss, axis=-1, keepdims=True) / ATTN_WIDTH + LN_EPS)
    for i, o in enumerate(heads):
        cs = slice(i * HEAD_DIM, (i + 1) * HEAD_DIM)
        o_ref[0, :, cs] = (o * inv * g_ref[:, cs]).astype(BF16)


def _attention(q, k, v, kc, vc, sink, g_attn, has_band):
    bsz, n_tok, _ = q.shape
    n_ctx = kc.shape[1]
    n_blk = n_tok // ATTN_BLOCK
    qspec = pl.BlockSpec((1, ATTN_BLOCK, ATTN_WIDTH), lambda b, n: (b, n, 0))
    kvspec = lambda f: pl.BlockSpec((1, ATTN_BLOCK, KV_WIDTH), lambda b, n: (b, f(n), 0))
    cspec = pl.BlockSpec((1, n_ctx, KV_WIDTH), lambda b, n: (b, 0, 0))
    prev = lambda n: jnp.maximum(n - 1, 0)
    own = lambda n: n
    nxt = lambda n: jnp.minimum(n + 1, n_blk - 1)
    in_specs = [pl.BlockSpec(memory_space=pltpu.SMEM), qspec]
    args = [sink, q]
    if has_band:
        in_specs += [kvspec(prev), kvspec(own), kvspec(nxt), kvspec(prev), kvspec(own), kvspec(nxt)]
        args += [k, k, k, v, v, v]
    in_specs += [cspec, cspec, pl.BlockSpec((1, ATTN_WIDTH), lambda b, n: (0, 0))]
    args += [kc, vc, g_attn.reshape(1, ATTN_WIDTH)]
    return pl.pallas_call(
        functools.partial(_attn_kernel, has_band=has_band, n_blk=n_blk),
        grid=(bsz, n_blk),
        in_specs=in_specs,
        out_specs=qspec,
        out_shape=jax.ShapeDtypeStruct((bsz, n_tok, ATTN_WIDTH), BF16),
        compiler_params=_cparams("parallel", "parallel"),
        name="attention",
    )(*args)


def _cpow(lam_re, lam_im, dt, expo):
    mag = jnp.exp(expo * (lam_re * dt))
    ang = expo * (lam_im * dt)
    return mag * jnp.cos(ang), mag * jnp.sin(ang)


def _s5_param_kernel(prow_ref, pcol_ref, bt_ref, cr_ref, ct_ref, dd_ref, m_ref, w_ref, v_ref, al_ref, kmat_ref):
    n, p_dim, lc = SSM_N, SSM_P, CHUNK
    lane_k = lax.broadcasted_iota(jnp.int32, (n, 2 * lc), 1)
    s_sub = lax.broadcasted_iota(jnp.int32, (lc, n), 0).astype(F32)
    t_lane = lax.broadcasted_iota(jnp.int32, (n, lc), 1).astype(F32)
    e_slabs = [[] for _ in range(p_dim)]
    pcat, w_parts, v_parts, al_parts = [], [], [], []
    for d in range(2):
        lre, lim = prow_ref[0, d, 0:1, :], prow_ref[0, d, 1:2, :]
        dt = jnp.exp(prow_ref[0, d, 2:3, :])
        lre_c, lim_c = pcol_ref[0, d, :, 0:1], pcol_ref[0, d, :, 1:2]
        dt_c = jnp.exp(pcol_ref[0, d, :, 2:3])
        a_re, a_im = _cpow(lre, lim, dt, 1.0)
        den = lre * lre + lim * lim
        x_re, x_im = a_re - 1.0, a_im
        beta_re = (x_re * lre + x_im * lim) / den
        beta_im = (x_im * lre - x_re * lim) / den
        b_re, b_im = bt_ref[0, d, 0], bt_ref[0, d, 1]
        bb_re = beta_re * b_re - beta_im * b_im
        bb_im = beta_re * b_im + beta_im * b_re
        c_re, c_im = cr_ref[0, d, 0], cr_ref[0, d, 1]
        for q in range(p_dim):
            bq_re, bq_im = bb_re[q:q + 1, :], bb_im[q:q + 1, :]
            e_slabs[q] += [c_re * bq_re - c_im * bq_im, -(c_re * bq_im + c_im * bq_re)]
        if d == 0:
            ok = lane_k < lc
            expo = jnp.where(ok, lane_k, 0).astype(F32)
        else:
            ok = (lane_k == 0) | (lane_k > lc)
            expo = jnp.where(lane_k > lc, 2 * lc - lane_k, 0).astype(F32)
        k_re, k_im = _cpow(lre_c, lim_c, dt_c, expo)
        pcat += [jnp.where(ok, k_re, 0.0), jnp.where(ok, k_im, 0.0)]
        w_re, w_im = _cpow(lre, lim, dt, (lc - 1.0 - s_sub) if d == 0 else s_sub)
        w_parts.append((w_re, w_im, bb_re, bb_im))
        v_re, v_im = _cpow(lre_c, lim_c, dt_c, (t_lane + 1.0) if d == 0 else (lc - t_lane))
        v_parts.append((v_re, v_im, ct_ref[0, d, 0], ct_ref[0, d, 1]))
        al_parts.append(_cpow(lre, lim, dt, float(lc)))

    ecat = jnp.concatenate([jnp.concatenate(e_slabs[q], axis=1) for q in range(p_dim)], axis=0)
    pcat = jnp.concatenate(pcat, axis=0)
    kmat = jnp.dot(ecat, pcat, preferred_element_type=F32, precision=lax.Precision.HIGHEST)
    lane0 = lax.broadcasted_iota(jnp.int32, kmat.shape, 1) == 0
    kmat_ref[...] = kmat + jnp.where(lane0, dd_ref[0], 0.0)

    lane_m = lax.broadcasted_iota(jnp.int32, (lc, 2 * lc), 1)

    def q_body(q, carry):
        for pp in range(p_dim // 2):
            r0 = q * p_dim + 2 * pp
            ka = jnp.broadcast_to(kmat_ref[pl.ds(r0, 1), :], (lc, 2 * lc))
            kb = jnp.broadcast_to(kmat_ref[pl.ds(r0 + 1, 1), :], (lc, 2 * lc))
            ra = pltpu.roll(ka, 0, 1, stride=1, stride_axis=0)
            rb = pltpu.roll(kb, lc, 1, stride=1, stride_axis=0)
            blk = jnp.where(lane_m < lc, ra, rb)
            m_ref[0, pl.ds(pl.multiple_of(q * lc, lc), lc), pp * 2 * lc:(pp + 1) * 2 * lc] = blk.astype(BF16)
        return carry

    lax.fori_loop(0, p_dim, q_body, 0)

    for q in range(p_dim):
        cols = {}
        for d, (w_re, w_im, bb_re, bb_im) in enumerate(w_parts):
            bq_re, bq_im = bb_re[q:q + 1, :], bb_im[q:q + 1, :]
            cols[d] = (w_re * bq_re - w_im * bq_im, w_re * bq_im + w_im * bq_re)
        slab = jnp.concatenate([cols[0][0], cols[1][0], cols[0][1], cols[1][1]], axis=1)
        w_ref[0, q * lc:(q + 1) * lc, :] = slab.astype(BF16)

    for pp in range(p_dim // 2):
        parts = {}
        for d, (v_re, v_im, ct_re, ct_im) in enumerate(v_parts):
            re_l, im_l = [], []
            for p in (2 * pp, 2 * pp + 1):
                cr_c, ci_c = ct_re[:, p:p + 1], ct_im[:, p:p + 1]
                re_l.append(cr_c * v_re - ci_c * v_im)
                im_l.append(-(cr_c * v_im + ci_c * v_re))
            parts[d] = (jnp.concatenate(re_l, axis=1), jnp.concatenate(im_l, axis=1))
        cs = slice(pp * 2 * lc, (pp + 1) * 2 * lc)
        v_ref[0, 0 * n:1 * n, cs] = parts[0][0].astype(BF16)
        v_ref[0, 1 * n:2 * n, cs] = parts[1][0].astype(BF16)
        v_ref[0, 2 * n:3 * n, cs] = parts[0][1].astype(BF16)
        v_ref[0, 3 * n:4 * n, cs] = parts[1][1].astype(BF16)

    al_ref[0] = jnp.concatenate([al_parts[0][0], al_parts[1][0], al_parts[0][1], al_parts[1][1]], axis=1)


def _s5_operators(lam_re, lam_im, log_dt, b_re, b_im, c_re, c_im, d_skip):
    depth = lam_re.shape[0]
    dg = depth * SSM_G
    g_first = lambda a: jnp.moveaxis(a, 2, 1).reshape((dg,) + a.shape[1:2] + a.shape[3:])
    lam_re_g, lam_im_g = g_first(lam_re), g_first(lam_im)
    ldt_g = jnp.broadcast_to(g_first(log_dt)[..., None], lam_re_g.shape)
    prow = jnp.stack([lam_re_g, lam_im_g, ldt_g], axis=2)
    pcol = jnp.swapaxes(prow, 2, 3)
    bt = jnp.stack([jnp.swapaxes(g_first(b_re), 2, 3), jnp.swapaxes(g_first(b_im), 2, 3)], axis=2)
    cr = jnp.stack([g_first(c_re), g_first(c_im)], axis=2)
    ct = jnp.swapaxes(cr, 3, 4)
    dd = (d_skip.reshape(dg, 1, SSM_P) * jnp.eye(SSM_P, dtype=F32)[None]).reshape(dg, SSM_P * SSM_P, 1)
    blk = lambda *s: pl.BlockSpec((1,) + s, lambda i: (i,) + (0,) * len(s))
    return pl.pallas_call(
        _s5_param_kernel,
        grid=(dg,),
        in_specs=[blk(2, 3, SSM_N), blk(2, SSM_N, 3), blk(2, 2, SSM_P, SSM_N), blk(2, 2, SSM_P, SSM_N),
                  blk(2, 2, SSM_N, SSM_P), blk(SSM_P * SSM_P, 1)],
        out_specs=[blk(GC, GC), blk(GC, 4 * SSM_N), blk(4 * SSM_N, GC), blk(1, 4 * SSM_N)],
        out_shape=[jax.ShapeDtypeStruct((dg, GC, GC), BF16),
                   jax.ShapeDtypeStruct((dg, GC, 4 * SSM_N), BF16),
                   jax.ShapeDtypeStruct((dg, 4 * SSM_N, GC), BF16),
                   jax.ShapeDtypeStruct((dg, 1, 4 * SSM_N), F32)],
        scratch_shapes=[pltpu.VMEM((SSM_P * SSM_P, 2 * CHUNK), F32)],
        compiler_params=_cparams("parallel"),
        name="s5_operators",
    )(prow, pcol, bt, cr, ct, dd)


def _s5_sum_kernel(u_ref, w_ref, s_ref):
    s_ref[0] = _dot(u_ref[0], w_ref[0])


def _s5_scan_kernel(s_ref, al_ref, h_ref, *, n_ctx_chunks, n_chunks):
    half = 2 * SSM_N
    a_re, a_im = al_ref[:, :half], al_ref[:, half:]
    is_fwd = lax.broadcasted_iota(jnp.int32, a_re.shape, 1) < SSM_N

    def body(i, carry):
        h_re, h_im = carry
        cf = i
        cr = jnp.where(i < n_ctx_chunks, n_ctx_chunks - 1 - i, n_chunks - 1 - (i - n_ctx_chunks))
        s_f, s_r = s_ref[cf], s_ref[cr]
        h_ref[cf, :, 0:SSM_N] = h_re[:, 0:SSM_N]
        h_ref[cr, :, SSM_N:half] = h_re[:, SSM_N:half]
        h_ref[cf, :, half:half + SSM_N] = h_im[:, 0:SSM_N]
        h_ref[cr, :, half + SSM_N:] = h_im[:, SSM_N:half]
        s_re = jnp.where(is_fwd, s_f[:, :half], s_r[:, :half])
        s_im = jnp.where(is_fwd, s_f[:, half:], s_r[:, half:])
        return a_re * h_re - a_im * h_im + s_re, a_re * h_im + a_im * h_re + s_im

    zero = jnp.zeros(a_re.shape, F32)
    lax.fori_loop(0, n_chunks, body, (zero, zero))


def _s5_out_kernel(u_ref, m_ref, h_ref, v_ref, y_ref):
    y = _dot(u_ref[0], m_ref[0]) + _dot(h_ref[0].astype(BF16), v_ref[0])
    y_ref[0] = y.astype(BF16)


def _s5_apply(s_lat, s_ctx, m_op, w_op, v_op, al, need_ctx_out):
    bsz, n_tok, _ = s_lat.shape
    n_ctx = s_ctx.shape[1]
    ncc, n_chunks = n_ctx // CHUNK, (n_ctx + n_tok) // CHUNK
    rows = bsz * n_chunks
    s_all = jnp.concatenate([s_ctx, s_lat], axis=1)
    u = s_all.reshape(bsz, n_chunks, CHUNK, SSM_G, SSM_P).transpose(3, 0, 1, 4, 2).reshape(SSM_G, rows, GC)
    gspec = lambda r, c: pl.BlockSpec((1, r, c), lambda g: (g, 0, 0))
    sums = pl.pallas_call(
        _s5_sum_kernel,
        grid=(SSM_G,),
        in_specs=[gspec(rows, GC), gspec(GC, 4 * SSM_N)],
        out_specs=gspec(rows, 4 * SSM_N),
        out_shape=jax.ShapeDtypeStruct((SSM_G, rows, 4 * SSM_N), F32),
        compiler_params=_cparams("parallel"),
        name="s5_chunk_sums",
    )(u, w_op)
    sums_t = sums.reshape(SSM_G, bsz, n_chunks, 4 * SSM_N).transpose(2, 0, 1, 3).reshape(n_chunks, SSM_G * bsz, 4 * SSM_N)
    al_rows = jnp.broadcast_to(al, (SSM_G, bsz, 4 * SSM_N)).reshape(SSM_G * bsz, 4 * SSM_N)
    rt = 64
    states_t = pl.pallas_call(
        functools.partial(_s5_scan_kernel, n_ctx_chunks=ncc, n_chunks=n_chunks),
        grid=(SSM_G * bsz // rt,),
        in_specs=[pl.BlockSpec((n_chunks, rt, 4 * SSM_N), lambda r: (0, r, 0)),
                  pl.BlockSpec((rt, 4 * SSM_N), lambda r: (r, 0))],
        out_specs=pl.BlockSpec((n_chunks, rt, 4 * SSM_N), lambda r: (0, r, 0)),
        out_shape=jax.ShapeDtypeStruct((n_chunks, SSM_G * bsz, 4 * SSM_N), F32),
        compiler_params=_cparams("parallel"),
        name="s5_state_scan",
    )(sums_t, al_rows)
    states = states_t.reshape(n_chunks, SSM_G, bsz, 4 * SSM_N).transpose(1, 2, 0, 3).reshape(SSM_G, rows, 4 * SSM_N)
    y = pl.pallas_call(
        _s5_out_kernel,
        grid=(SSM_G,),
        in_specs=[gspec(rows, GC), gspec(GC, GC), gspec(rows, 4 * SSM_N), gspec(4 * SSM_N, GC)],
        out_specs=gspec(rows, GC),
        out_shape=jax.ShapeDtypeStruct((SSM_G, rows, GC), BF16),
        compiler_params=_cparams("parallel"),
        name="s5_chunk_outputs",
    )(u, m_op, states, v_op)
    y = y.reshape(SSM_G, bsz, n_chunks, SSM_P, CHUNK).transpose(1, 2, 4, 0, 3).reshape(bsz, n_ctx + n_tok, SSM_WIDTH)
    return y[:, n_ctx:], (y[:, :n_ctx] if need_ctx_out else None)


def _glu_kernel(y_ref, w_ref, b_ref, g_ref, o_ref):
    y = y_ref[0].astype(F32)
    z = 0.5 * y * (1.0 + jnp.tanh(math.sqrt(2.0 / math.pi) * (y + 0.044715 * (y * y * y))))
    t = _dot(z.astype(BF16), w_ref[...]) + b_ref[...]
    o = z * jax.nn.sigmoid(t)
    inv = lax.rsqrt(jnp.mean(o * o, axis=-1, keepdims=True) + LN_EPS)
    o_ref[0] = (o * inv * g_ref[...]).astype(BF16)


def _glu(y, w_glu_bf, b_glu, g_ssm):
    bsz, n_tok, w = y.shape
    tm = min(ROW_TILE, n_tok)
    row = lambda a: a.reshape(1, w)
    tspec = pl.BlockSpec((1, tm, w), lambda b, t: (b, t, 0))
    vspec = pl.BlockSpec((1, w), lambda b, t: (0, 0))
    return pl.pallas_call(
        _glu_kernel,
        grid=(bsz, n_tok // tm),
        in_specs=[tspec, pl.BlockSpec((w, w), lambda b, t: (0, 0)), vspec, vspec],
        out_specs=tspec,
        out_shape=jax.ShapeDtypeStruct((bsz, n_tok, w), BF16),
        compiler_params=_cparams("parallel", "parallel"),
        name="s5_glu",
    )(y, w_glu_bf, row(b_glu), row(g_ssm))


def _layer_norm(z, g, b):
    mu = jnp.mean(z, axis=-1, keepdims=True)
    zc = z - mu
    var = jnp.mean(zc * zc, axis=-1, keepdims=True)
    return zc * lax.rsqrt(var + LN_EPS) * g + b


def _first_argmax(vals):
    best_i = jnp.zeros(vals[0].shape, jnp.int32)
    best_v = vals[0]
    for j in range(1, len(vals)):
        better = vals[j] > best_v
        best_i = jnp.where(better, j, best_i)
        best_v = jnp.where(better, vals[j], best_v)
    return best_i, best_v


def _route(logit_rows):
    m = functools.reduce(jnp.maximum, logit_rows)
    p = [jnp.exp(l - m) for l in logit_rows]
    scores = []
    for g in range(N_EXPERT_GROUPS):
        a, b, c, d = p[4 * g:4 * g + 4]
        hi1, lo1, hi2, lo2 = jnp.maximum(a, b), jnp.minimum(a, b), jnp.maximum(c, d), jnp.minimum(c, d)
        scores.append(jnp.maximum(hi1, hi2) + jnp.maximum(jnp.minimum(hi1, hi2), jnp.maximum(lo1, lo2)))
    grp, _ = _first_argmax(scores)
    sel = []
    for j in range(EXPERTS_PER_GROUP):
        v = p[j]
        for g in range(1, N_EXPERT_GROUPS):
            v = jnp.where(grp == g, p[4 * g + j], v)
        sel.append(v)
    i1, v1 = _first_argmax(sel)
    i2, v2 = _first_argmax([jnp.where(i1 == j, -1.0, sel[j]) for j in range(EXPERTS_PER_GROUP)])
    tot = v1 + v2
    return grp * EXPERTS_PER_GROUP + i1, grp * EXPERTS_PER_GROUP + i2, v1 / tot, v2 / tot


def _outproj_kernel(a_ref, s_ref, x_ref, mod_ref, w_ref, lng_ref, lnb_ref, wr_ref, br_ref,
                    x1_ref, u2_ref, ri_ref, rw_ref, *, alpha):
    y = _dot(a_ref[0], w_ref[0:ATTN_WIDTH, :]) + _dot(s_ref[0], w_ref[ATTN_WIDTH:, :])
    g1 = mod_ref[0, 2:3, :]
    sh2 = mod_ref[0, 3:4, :]
    sc2 = mod_ref[0, 4:5, :]
    x1 = _layer_norm(alpha * x_ref[0] + g1 * y, lng_ref[...], lnb_ref[...])
    x1_ref[0] = x1
    u2 = (x1 * (1.0 + sc2) + sh2).astype(BF16)
    u2_ref[0] = u2
    logits = _dot_nt(wr_ref[...], u2) + br_ref[...]
    e1, e2, w1, w2 = _route([logits[e:e + 1, :] for e in range(N_EXPERTS)])
    ri_ref[0, 0:1, :] = e1
    ri_ref[0, 1:2, :] = e2
    rw_ref[0, 0:1, :] = w1
    rw_ref[0, 1:2, :] = w2


def _out_projection(attn_n, ssm_n, x, mod3, row_of_batch, w_out_bf, ln_g, ln_b, w_router_t_bf, b_router, alpha):
    bsz, n_tok, d = x.shape
    tm = min(ROW_TILE, n_tok)
    tspec = lambda w: pl.BlockSpec((1, tm, w), lambda b, t: (b, t, 0))
    vspec = pl.BlockSpec((1, d), lambda b, t: (0, 0))
    rspec = pl.BlockSpec((1, 2, tm), lambda b, t: (b, 0, t))
    return pl.pallas_call(
        functools.partial(_outproj_kernel, alpha=alpha),
        grid=(bsz, n_tok // tm),
        in_specs=[
            tspec(ATTN_WIDTH), tspec(SSM_WIDTH), tspec(d),
            pl.BlockSpec((1, N_MOD, d), lambda b, t: (row_of_batch(b), 0, 0)),
            pl.BlockSpec(w_out_bf.shape, lambda b, t: (0, 0)),
            vspec, vspec,
            pl.BlockSpec((N_EXPERTS, d), lambda b, t: (0, 0)),
            pl.BlockSpec((N_EXPERTS, 1), lambda b, t: (0, 0)),
        ],
        out_specs=[tspec(d), tspec(d), rspec, rspec],
        out_shape=[jax.ShapeDtypeStruct((bsz, n_tok, d), F32), jax.ShapeDtypeStruct((bsz, n_tok, d), BF16),
                   jax.ShapeDtypeStruct((bsz, 2, n_tok), jnp.int32), jax.ShapeDtypeStruct((bsz, 2, n_tok), F32)],
        compiler_params=_cparams("parallel", "parallel"),
        name="out_projection",
    )(attn_n, ssm_n, x, mod3, w_out_bf, ln_g.reshape(1, d), ln_b.reshape(1, d), w_router_t_bf,
      b_router.reshape(N_EXPERTS, 1))


def _ffn_kernel(te_ref, nu_ref, x_ref, wg_ref, wu_ref, wd_ref, y_ref):
    t = pl.program_id(0)

    @pl.when(t < nu_ref[0])
    def _():
        x = x_ref[...]
        g = _dot(x, wg_ref[0])
        u = _dot(x, wu_ref[0])
        h = (g * jax.nn.sigmoid(g) * u).astype(BF16)
        y_ref[...] = _dot(h, wd_ref[0]).astype(BF16)

    @pl.when(t >= nu_ref[0])
    def _():
        y_ref[...] = jnp.zeros(y_ref.shape, BF16)


def _expert_ffn(xs, tile_expert, n_used, wg_bf, wu_bf, wd_bf):
    n_rows, d = xs.shape
    f = wg_bf.shape[2]
    tm = MOE_TILE
    grid_spec = pltpu.PrefetchScalarGridSpec(
        num_scalar_prefetch=2,
        grid=(n_rows // tm,),
        in_specs=[
            pl.BlockSpec((tm, d), lambda t, te, nu: (t, 0)),
            pl.BlockSpec((1, d, f), lambda t, te, nu: (te[t], 0, 0)),
            pl.BlockSpec((1, d, f), lambda t, te, nu: (te[t], 0, 0)),
            pl.BlockSpec((1, f, d), lambda t, te, nu: (te[t], 0, 0)),
        ],
        out_specs=pl.BlockSpec((tm, d), lambda t, te, nu: (t, 0)),
    )
    return pl.pallas_call(
        _ffn_kernel,
        grid_spec=grid_spec,
        out_shape=jax.ShapeDtypeStruct((n_rows, d), BF16),
        compiler_params=_cparams("arbitrary"),
        name="expert_ffn",
    )(tile_expert, n_used, xs, wg_bf, wu_bf, wd_bf)


def _dispatch_plan(e1, e2, tm):
    n = e1.shape[0]
    n_tiles = (2 * n + N_EXPERTS * (tm - 1) + tm - 1) // tm
    e = jnp.concatenate([e1, e2])
    onehot = (e[:, None] == jnp.arange(N_EXPERTS, dtype=jnp.int32)[None, :]).astype(jnp.int32)
    csum = jnp.cumsum(onehot, axis=0)
    pos_in_e = jnp.sum((csum - 1) * onehot, axis=1)
    counts = csum[-1]
    padded = ((counts + tm - 1) // tm) * tm
    ends = jnp.cumsum(padded)
    offs = ends - padded
    dest = offs[e] + pos_in_e
    tok = jnp.concatenate([jnp.arange(n, dtype=jnp.int32)] * 2)
    row_src = jnp.zeros((n_tiles * tm,), jnp.int32).at[dest].set(tok)
    n_used = (ends[-1] // tm).astype(jnp.int32)
    tile_start = jnp.arange(n_tiles, dtype=jnp.int32) * tm
    tile_e = jnp.sum((tile_start[:, None] >= ends[None, :]).astype(jnp.int32), axis=1)
    tile_e = jnp.minimum(tile_e, N_EXPERTS - 1)
    last_e = tile_e[jnp.maximum(n_used - 1, 0)]
    tile_e = jnp.where(jnp.arange(n_tiles) < n_used, tile_e, last_e).astype(jnp.int32)
    return row_src, dest[:n], dest[n:], tile_e, n_used.reshape(1)


def _final_kernel(x_ref, y1_ref, y2_ref, w1_ref, w2_ref, mod_ref, lng_ref, lnb_ref, o_ref, *, alpha):
    f = w1_ref[...] * y1_ref[0].astype(F32) + w2_ref[...] * y2_ref[0].astype(F32)
    g2 = mod_ref[0, 5:6, :]
    o_ref[0] = _layer_norm(alpha * x_ref[0] + g2 * f, lng_ref[...], lnb_ref[...])


def _combine_ln(x1, y1, y2, w1, w2, mod3, row_of_batch, ln_g, ln_b, alpha):
    bsz, n_tok, d = x1.shape
    tm = min(ROW_TILE, n_tok)
    nt = n_tok // tm
    tspec = pl.BlockSpec((1, tm, d), lambda b, t: (b, t, 0))
    wspec = pl.BlockSpec((tm, 1), lambda b, t: (b * nt + t, 0))
    vspec = pl.BlockSpec((1, d), lambda b, t: (0, 0))
    return pl.pallas_call(
        functools.partial(_final_kernel, alpha=alpha),
        grid=(bsz, nt),
        in_specs=[tspec, tspec, tspec, wspec, wspec,
                  pl.BlockSpec((1, N_MOD, d), lambda b, t: (row_of_batch(b), 0, 0)), vspec, vspec],
        out_specs=tspec,
        out_shape=jax.ShapeDtypeStruct((bsz, n_tok, d), F32),
        compiler_params=_cparams("parallel", "parallel"),
        name="combine_post_ln",
    )(x1, y1.reshape(bsz, n_tok, d), y2.reshape(bsz, n_tok, d), w1.reshape(bsz * n_tok, 1),
      w2.reshape(bsz * n_tok, 1), mod3, ln_g.reshape(1, d), ln_b.reshape(1, d))


def kernel(x, c, ctx, c_ctx, w_mod, b_mod, w_in, attn_sink, ssm_lambda_re, ssm_lambda_im, ssm_log_dt, ssm_b_re, ssm_b_im, ssm_c_re, ssm_c_im, ssm_d, w_glu, b_glu, g_attn_out, g_ssm_out, w_out, ln1_g, ln1_b, w_router, b_router, w_expert_gate, w_expert_up, w_expert_down, ln2_g, ln2_b):
    depth = w_mod.shape[0]
    bsz, n_tok, d = x.shape
    n_ctx = ctx.shape[1]
    alpha = (2 * depth) ** 0.25
    assert n_tok % ROW_TILE == 0 and n_tok % ATTN_BLOCK == 0 and n_ctx % ATTN_BLOCK == 0 and n_ctx % CHUNK == 0

    mod = _modulation(c, c_ctx, w_mod, b_mod)
    m_op, w_op, v_op, al = _s5_operators(ssm_lambda_re, ssm_lambda_im, ssm_log_dt, ssm_b_re, ssm_b_im,
                                         ssm_c_re, ssm_c_im, ssm_d)
    cos_t, sin_t = _rope_tables(n_tok)
    w_router_t = w_router.T.astype(BF16)
    lat_row = lambda b: b
    ctx_row = lambda b: bsz

    xc = ctx
    for i in range(depth):
        last = i == depth - 1
        mod3 = mod[i].reshape(MOD_ROWS, N_MOD, d)
        gsl = slice(i * SSM_G, (i + 1) * SSM_G)
        w_in_bf = w_in[i].astype(BF16)

        q, k, v, s = _in_projection(x, mod3, lat_row, w_in_bf, cos_t, sin_t, True)
        qc, kc, vc, sc = _in_projection(xc, mod3, ctx_row, w_in_bf, cos_t, sin_t, False)
        attn_n = _attention(q, k, v, kc, vc, attn_sink[i], g_attn_out[i], True)
        y_ssm, yc_ssm = _s5_apply(s, sc, m_op[gsl], w_op[gsl], v_op[gsl], al[gsl], not last)
        w_glu_bf = w_glu[i].astype(BF16)
        ssm_n = _glu(y_ssm, w_glu_bf, b_glu[i], g_ssm_out[i])
        w_out_bf = w_out[i].astype(BF16)
        x1, u2, ri, rw = _out_projection(attn_n, ssm_n, x, mod3, lat_row, w_out_bf, ln1_g[i], ln1_b[i],
                                         w_router_t, b_router, alpha)
        n_lat = bsz * n_tok
        u2_rows = u2.reshape(n_lat, d)
        e1, e2 = ri[:, 0, :].reshape(n_lat), ri[:, 1, :].reshape(n_lat)
        cw1, cw2 = rw[:, 0, :].reshape(n_lat), rw[:, 1, :].reshape(n_lat)
        if not last:
            attn_c = _attention(qc, None, None, kc, vc, attn_sink[i], g_attn_out[i], False)
            ssm_c = _glu(yc_ssm, w_glu_bf, b_glu[i], g_ssm_out[i])
            xc1, u2c, ric, rwc = _out_projection(attn_c, ssm_c, xc, mod3, ctx_row, w_out_bf, ln1_g[i], ln1_b[i],
                                                 w_router_t, b_router, alpha)
            n_c = bsz * n_ctx
            u2_rows = jnp.concatenate([u2_rows, u2c.reshape(n_c, d)], axis=0)
            e1 = jnp.concatenate([e1, ric[:, 0, :].reshape(n_c)])
            e2 = jnp.concatenate([e2, ric[:, 1, :].reshape(n_c)])
            cw1 = jnp.concatenate([cw1, rwc[:, 0, :].reshape(n_c)])
            cw2 = jnp.concatenate([cw2, rwc[:, 1, :].reshape(n_c)])

        row_src, pos1, pos2, tile_e, n_used = _dispatch_plan(e1, e2, MOE_TILE)
        xs = jnp.take(u2_rows, row_src, axis=0)
        ys = _expert_ffn(xs, tile_e, n_used, w_expert_gate[i].astype(BF16), w_expert_up[i].astype(BF16),
                         w_expert_down[i].astype(BF16))
        y1 = jnp.take(ys, pos1, axis=0)
        y2 = jnp.take(ys, pos2, axis=0)
        x = _combine_ln(x1, y1[:n_lat], y2[:n_lat], cw1[:n_lat], cw2[:n_lat], mod3, lat_row,
                        ln2_g[i], ln2_b[i], alpha)
        if not last:
            xc = _combine_ln(xc1, y1[n_lat:], y2[n_lat:], cw1[n_lat:], cw2[n_lat:], mod3, ctx_row,
                             ln2_g[i], ln2_b[i], alpha)
    return x
```

```python
import functools
import math

import jax
import jax.numpy as jnp
from jax import lax
from jax.experimental import pallas as pl
from jax.experimental.pallas import tpu as pltpu

F32 = jnp.float32
BF16 = jnp.bfloat16

HEAD_DIM = 128
N_Q_HEADS = 8
N_KV_HEADS = 2
Q_PER_KV = N_Q_HEADS // N_KV_HEADS
ATTN_WIDTH = N_Q_HEADS * HEAD_DIM
KV_WIDTH = N_KV_HEADS * HEAD_DIM
SSM_WIDTH = 1024
ATTN_BLOCK = 128
GRID_W = 64
ROPE_THETA = 10000.0
SSM_P = 16
SSM_G = SSM_WIDTH // SSM_P
SSM_N = 64
CHUNK = 64
GC = SSM_P * CHUNK
N_EXPERTS = 16
N_EXPERT_GROUPS = 4
EXPERTS_PER_GROUP = 4
N_MOD = 6
LN_EPS = 1e-5
NEG_INF = -1e30
LANES = 128
MOD_ROWS = 8
ROW_TILE = 256
MOE_TILE = 256
VMEM_LIMIT = 56 * 1024 * 1024

assert 2 * SSM_N == LANES and 2 * CHUNK == LANES


def _cparams(*sem):
    return pltpu.CompilerParams(dimension_semantics=sem, vmem_limit_bytes=VMEM_LIMIT)


def _dot(a, b):
    return jnp.dot(a, b, preferred_element_type=F32)


def _dot_nt(a, b):
    return lax.dot_general(a, b, (((1,), (1,)), ((), ())), preferred_element_type=F32)


def _mod_kernel(ct_ref, w_ref, b_ref, o_ref, ab_ref, *, n_rows, tn):
    d = ct_ref.shape[0]

    @pl.when((pl.program_id(0) == 0) & (pl.program_id(1) == 0))
    def _():
        ct = ct_ref[...]
        a = ct * jax.nn.sigmoid(ct)
        for r in range(n_rows):
            ab_ref[r] = jnp.broadcast_to(a[:, r:r + 1], (d, LANES))

    o_ref[...] = jnp.zeros(o_ref.shape, F32)
    for j in range(tn // LANES):
        w = w_ref[0, :, j * LANES:(j + 1) * LANES]
        bias = b_ref[0, :, j * LANES:(j + 1) * LANES]
        for r in range(n_rows):
            acc = jnp.sum(ab_ref[r] * w, axis=0, keepdims=True)
            o_ref[0, r:r + 1, j * LANES:(j + 1) * LANES] = acc + bias


def _modulation(c, c_ctx, w_mod, b_mod):
    depth, d, n_out = w_mod.shape
    n_rows = c.shape[0] + 1
    assert n_rows <= MOD_ROWS
    tn = 512
    ct = jnp.zeros((MOD_ROWS, d), F32).at[:c.shape[0]].set(c).at[c.shape[0]].set(c_ctx).T
    return pl.pallas_call(
        functools.partial(_mod_kernel, n_rows=n_rows, tn=tn),
        grid=(depth, n_out // tn),
        in_specs=[
            pl.BlockSpec((d, MOD_ROWS), lambda l, j: (0, 0)),
            pl.BlockSpec((1, d, tn), lambda l, j: (l, 0, j)),
            pl.BlockSpec((1, 1, tn), lambda l, j: (l, 0, j)),
        ],
        out_specs=pl.BlockSpec((1, MOD_ROWS, tn), lambda l, j: (l, 0, j)),
        out_shape=jax.ShapeDtypeStruct((depth, MOD_ROWS, n_out), F32),
        scratch_shapes=[pltpu.VMEM((n_rows, d, LANES), F32)],
        compiler_params=_cparams("arbitrary", "arbitrary"),
        name="modulation",
    )(ct, w_mod, b_mod.reshape(depth, 1, n_out))


def _cast_kernel(w_ref, o_ref):
    o_ref[...] = w_ref[...].astype(BF16)


def _cast_bf16(w):
    k, n = w.shape[-2:]
    w3 = w.reshape((-1, k, n))
    tk = min(k, 1024)
    spec = pl.BlockSpec((1, tk, n), lambda i, j: (i, j, 0))
    return pl.pallas_call(
        _cast_kernel,
        grid=(w3.shape[0], k // tk),
        in_specs=[spec],
        out_specs=spec,
        out_shape=jax.ShapeDtypeStruct(w3.shape, BF16),
        compiler_params=_cparams("parallel", "parallel"),
        name="cast_bf16",
    )(w3)


def _rope(xh, cos, sin_signed):
    lane = lax.broadcasted_iota(jnp.int32, xh.shape, 1)
    swapped = jnp.where((lane % 64) < 32, pltpu.roll(xh, 96, 1), pltpu.roll(xh, 32, 1))
    return xh * cos + swapped * sin_signed


def _inproj_kernel(x_ref, mod_ref, w_ref, cos_ref, sin_ref, q_ref, k_ref, v_ref, s_ref, *, rope):
    sh = mod_ref[0, 0:1, :]
    sc = mod_ref[0, 1:2, :]
    u = (x_ref[0] * (1.0 + sc) + sh).astype(BF16)
    q = _dot(u, w_ref[0, :, 0:ATTN_WIDTH])
    k = _dot(u, w_ref[0, :, ATTN_WIDTH:ATTN_WIDTH + KV_WIDTH])
    v = _dot(u, w_ref[0, :, ATTN_WIDTH + KV_WIDTH:ATTN_WIDTH + 2 * KV_WIDTH])
    s = _dot(u, w_ref[0, :, ATTN_WIDTH + 2 * KV_WIDTH:])
    scale = HEAD_DIM ** -0.5
    if rope:
        cos = cos_ref[...]
        sin = sin_ref[...]
    for h in range(N_Q_HEADS):
        qh = q[:, h * HEAD_DIM:(h + 1) * HEAD_DIM]
        if rope:
            qh = _rope(qh, cos, sin)
        q_ref[0, :, h * HEAD_DIM:(h + 1) * HEAD_DIM] = (qh * scale).astype(BF16)
    for h in range(N_KV_HEADS):
        kh = k[:, h * HEAD_DIM:(h + 1) * HEAD_DIM]
        if rope:
            kh = _rope(kh, cos, sin)
        k_ref[0, :, h * HEAD_DIM:(h + 1) * HEAD_DIM] = kh.astype(BF16)
    v_ref[0] = v.astype(BF16)
    s_ref[0] = s.astype(BF16)


def _in_projection(x, mod3, row_of_batch, w_in_bf, layer, cos_t, sin_t, rope):
    bsz, n_tok, d = x.shape
    tm = min(ROW_TILE, n_tok)
    n_cols = w_in_bf.shape[2]
    out = lambda w: jax.ShapeDtypeStruct((bsz, n_tok, w), BF16)
    ospec = lambda w: pl.BlockSpec((1, tm, w), lambda b, t: (b, t, 0))
    return pl.pallas_call(
        functools.partial(_inproj_kernel, rope=rope),
        grid=(bsz, n_tok // tm),
        in_specs=[
            pl.BlockSpec((1, tm, d), lambda b, t: (b, t, 0)),
            pl.BlockSpec((1, N_MOD, d), lambda b, t: (row_of_batch(b), 0, 0)),
            pl.BlockSpec((1, d, n_cols), lambda b, t: (layer, 0, 0)),
            pl.BlockSpec((tm, HEAD_DIM), lambda b, t: (t, 0)),
            pl.BlockSpec((tm, HEAD_DIM), lambda b, t: (t, 0)),
        ],
        out_specs=[ospec(ATTN_WIDTH), ospec(KV_WIDTH), ospec(KV_WIDTH), ospec(SSM_WIDTH)],
        out_shape=[out(ATTN_WIDTH), out(KV_WIDTH), out(KV_WIDTH), out(SSM_WIDTH)],
        compiler_params=_cparams("parallel", "parallel"),
        name="in_projection",
    )(x, mod3, w_in_bf, cos_t, sin_t)


def _rope_tables(n_tok):
    half = HEAD_DIM // 2
    inv_freq = ROPE_THETA ** (-jnp.arange(0, half, 2, dtype=F32) / half)
    t = jnp.arange(n_tok)
    row = (t // GRID_W).astype(F32)
    col = (t % GRID_W).astype(F32)
    ang_r = row[:, None] * inv_freq[None, :]
    ang_c = col[:, None] * inv_freq[None, :]
    cos_t = jnp.concatenate([jnp.cos(ang_r), jnp.cos(ang_r), jnp.cos(ang_c), jnp.cos(ang_c)], axis=-1)
    sin_t = jnp.concatenate([-jnp.sin(ang_r), jnp.sin(ang_r), -jnp.sin(ang_c), jnp.sin(ang_c)], axis=-1)
    return cos_t, sin_t


def _attn_kernel(sink_ref, q_ref, *refs, has_band, n_blk):
    if has_band:
        kp_ref, ko_ref, kn_ref, vp_ref, vo_ref, vn_ref, kc_ref, vc_ref, g_ref, o_ref = refs
    else:
        kc_ref, vc_ref, g_ref, o_ref = refs
    blk = pl.program_id(1)
    rows = Q_PER_KV * ATTN_BLOCK
    qi = lax.broadcasted_iota(jnp.int32, (rows, ATTN_BLOCK), 0) % ATTN_BLOCK
    kj = lax.broadcasted_iota(jnp.int32, (rows, ATTN_BLOCK), 1)
    row_head = lax.broadcasted_iota(jnp.int32, (rows, 1), 0) // ATTN_BLOCK
    heads = []
    for h in range(N_KV_HEADS):
        hs = slice(h * HEAD_DIM, (h + 1) * HEAD_DIM)
        q = jnp.concatenate(
            [q_ref[0, :, (h * Q_PER_KV + g) * HEAD_DIM:(h * Q_PER_KV + g + 1) * HEAD_DIM] for g in range(Q_PER_KV)],
            axis=0)
        sink = jnp.zeros((rows, 1), F32)
        for g in range(Q_PER_KV):
            sink = jnp.where(row_head == g, sink_ref[h * Q_PER_KV + g], sink)
        s_c = _dot_nt(q, kc_ref[0, :, hs])
        m = jnp.maximum(jnp.max(s_c, axis=-1, keepdims=True), sink)
        if has_band:
            s_p = jnp.where((kj >= qi) & (blk > 0), _dot_nt(q, kp_ref[0, :, hs]), NEG_INF)
            s_o = _dot_nt(q, ko_ref[0, :, hs])
            s_n = jnp.where((kj <= qi) & (blk < n_blk - 1), _dot_nt(q, kn_ref[0, :, hs]), NEG_INF)
            m = jnp.maximum(m, jnp.max(jnp.maximum(jnp.maximum(s_p, s_o), s_n), axis=-1, keepdims=True))
        p_c = jnp.exp(s_c - m)
        denom = jnp.sum(p_c, axis=-1, keepdims=True) + jnp.exp(sink - m)
        acc = _dot(p_c.astype(BF16), vc_ref[0, :, hs])
        if has_band:
            for s_x, v_ref in ((s_p, vp_ref), (s_o, vo_ref), (s_n, vn_ref)):
                p_x = jnp.exp(s_x - m)
                denom = denom + jnp.sum(p_x, axis=-1, keepdims=True)
                acc = acc + _dot(p_x.astype(BF16), v_ref[0, :, hs])
        o = acc / denom
        for g in range(Q_PER_KV):
            heads.append(o[g * ATTN_BLOCK:(g + 1) * ATTN_BLOCK, :])
    ss = heads[0] * heads[0]
    for o in heads[1:]:
        ss = ss + o * o
    inv = lax.rsqrt(jnp.sum(ss, axis=-1, keepdims=True) / ATTN_WIDTH + LN_EPS)
    for i, o in enumerate(heads):
        cs = slice(i * HEAD_DIM, (i + 1) * HEAD_DIM)
        o_ref[0, :, cs] = (o * inv * g_ref[:, cs]).astype(BF16)


def _attention(q, k, v, kc, vc, sink, g_attn, has_band):
    bsz, n_tok, _ = q.shape
    n_ctx = kc.shape[1]
    n_blk = n_tok // ATTN_BLOCK
    qspec = pl.BlockSpec((1, ATTN_BLOCK, ATTN_WIDTH), lambda b, n: (b, n, 0))
    kvspec = lambda f: pl.BlockSpec((1, ATTN_BLOCK, KV_WIDTH), lambda b, n: (b, f(n), 0))
    cspec = pl.BlockSpec((1, n_ctx, KV_WIDTH), lambda b, n: (b, 0, 0))
    prev = lambda n: jnp.maximum(n - 1, 0)
    own = lambda n: n
    nxt = lambda n: jnp.minimum(n + 1, n_blk - 1)
    in_specs = [pl.BlockSpec(memory_space=pltpu.SMEM), qspec]
    args = [sink, q]
    if has_band:
        in_specs += [kvspec(prev), kvspec(own), kvspec(nxt), kvspec(prev), kvspec(own), kvspec(nxt)]
        args += [k, k, k, v, v, v]
    in_specs += [cspec, cspec, pl.BlockSpec((1, ATTN_WIDTH), lambda b, n: (0, 0))]
    args += [kc, vc, g_attn.reshape(1, ATTN_WIDTH)]
    return pl.pallas_call(
        functools.partial(_attn_kernel, has_band=has_band, n_blk=n_blk),
        grid=(bsz, n_blk),
        in_specs=in_specs,
        out_specs=qspec,
        out_shape=jax.ShapeDtypeStruct((bsz, n_tok, ATTN_WIDTH), BF16),
        compiler_params=_cparams("parallel", "parallel"),
        name="attention",
    )(*args)


def _cmul(a, b):
    return a[0] * b[0] - a[1] * b[1], a[0] * b[1] + a[1] * b[0]


def _cpow(rho, theta, expo):
    mag = jnp.exp(expo * rho)
    ang = expo * theta
    return mag * jnp.cos(ang), mag * jnp.sin(ang)


def _s5_param_kernel(prow_ref, bt_ref, cr_ref, dd_ref, m_ref, w_ref, vt_ref, al_ref, kmat_ref):
    p_dim, lc = SSM_P, CHUNK
    lam_re, lam_im = prow_ref[0, 0:1, :], prow_ref[0, 1:2, :]
    dt = jnp.exp(prow_ref[0, 2:3, :])
    rho, theta = lam_re * dt, lam_im * dt

    sub8 = lax.broadcasted_iota(jnp.int32, (8, LANES), 0)
    asc = _cpow(rho, theta, sub8.astype(F32))
    desc = _cpow(rho, theta, (8 - sub8).astype(F32))
    ex = jnp.where(sub8 == 0, 8, jnp.where(sub8 == 1, 16, jnp.where(sub8 == 2, 32, jnp.where(sub8 == 3, lc, 1))))
    pw = _cpow(rho, theta, ex.astype(F32))
    row = lambda t, i: (t[0][i:i + 1, :], t[1][i:i + 1, :])
    cat = lambda a, b: (jnp.concatenate([a[0], b[0]], axis=0), jnp.concatenate([a[1], b[1]], axis=0))
    for i in range(3):
        step = row(pw, i)
        asc = cat(asc, _cmul(asc, step))
        desc = cat(_cmul(desc, step), desc)
    a_chunk, a_one = row(pw, 3), row(pw, 4)

    den = lam_re * lam_re + lam_im * lam_im
    x_re, x_im = a_one[0] - 1.0, a_one[1]
    beta = ((x_re * lam_re + x_im * lam_im) / den, (x_im * lam_re - x_re * lam_im) / den)
    bb = _cmul(beta, (bt_ref[0, 0], bt_ref[0, 1]))
    cc = (cr_ref[0, 0], cr_ref[0, 1])

    is_fwd = lax.broadcasted_iota(jnp.int32, (lc, LANES), 1) < SSM_N
    sub = lax.broadcasted_iota(jnp.int32, (lc, LANES), 0)
    pw_w = (jnp.where(is_fwd, desc[0], asc[0]), jnp.where(is_fwd, desc[1], asc[1]))
    pw_v = (jnp.where(is_fwd, asc[0], desc[0]), jnp.where(is_fwd, asc[1], desc[1]))

    top = (jnp.where(is_fwd, asc[0], jnp.where(sub == 0, 1.0, 0.0)), jnp.where(is_fwd, asc[1], 0.0))
    keep = (~is_fwd) & (sub > 0)
    bot = (jnp.where(keep, desc[0], 0.0), jnp.where(keep, desc[1], 0.0))
    lag = cat(top, bot)

    cx = (jnp.concatenate([cc[0]] * p_dim, axis=0), jnp.concatenate([cc[1]] * p_dim, axis=0))
    bx = tuple(jnp.concatenate([jnp.broadcast_to(b[q:q + 1, :], (p_dim, LANES)) for q in range(p_dim)], axis=0)
               for b in bb)
    e_re, e_im = _cmul(cx, bx)
    hi = lax.Precision.HIGHEST
    kmat = (jnp.dot(e_re, lag[0].T, preferred_element_type=F32, precision=hi)
            - jnp.dot(e_im, lag[1].T, preferred_element_type=F32, precision=hi))
    lane0 = lax.broadcasted_iota(jnp.int32, kmat.shape, 1) == 0
    kmat_ref[...] = kmat + jnp.where(lane0, dd_ref[0], 0.0)

    lane_m = lax.broadcasted_iota(jnp.int32, (lc, LANES), 1)

    def q_body(q, carry):
        for pp in range(p_dim // 2):
            r0 = q * p_dim + 2 * pp
            ka = jnp.broadcast_to(kmat_ref[pl.ds(r0, 1), :], (lc, LANES))
            kb = jnp.broadcast_to(kmat_ref[pl.ds(r0 + 1, 1), :], (lc, LANES))
            ra = pltpu.roll(ka, 0, 1, stride=1, stride_axis=0)
            rb = pltpu.roll(kb, lc, 1, stride=1, stride_axis=0)
            blk = jnp.where(lane_m < lc, ra, rb)
            m_ref[0, pl.ds(pl.multiple_of(q * lc, lc), lc), pp * LANES:(pp + 1) * LANES] = blk.astype(BF16)
        return carry

    lax.fori_loop(0, p_dim, q_body, 0)

    for q in range(p_dim):
        w_re, w_im = _cmul(pw_w, row(bb, q))
        w_ref[0, q * lc:(q + 1) * lc, :] = jnp.concatenate([w_re, w_im], axis=1).astype(BF16)
    for p in range(p_dim):
        v_re, v_im = _cmul(pw_v, row(cc, p))
        vt_ref[0, p * lc:(p + 1) * lc, :] = jnp.concatenate([v_re, -v_im], axis=1).astype(BF16)
    al_ref[0] = jnp.concatenate([a_chunk[0], a_chunk[1]], axis=1)


def _s5_operators(lam_re, lam_im, log_dt, b_re, b_im, c_re, c_im, d_skip):
    depth = lam_re.shape[0]
    dg = depth * SSM_G
    vec = lambda a: jnp.moveaxis(a, 1, 2).reshape(dg, LANES)
    mat = lambda a: jnp.moveaxis(a, 1, 3).reshape(dg, SSM_P, LANES)
    ldt = jnp.broadcast_to(log_dt[..., None], lam_re.shape)
    prow = jnp.stack([vec(lam_re), vec(lam_im), vec(ldt)], axis=1)
    bt = jnp.stack([mat(jnp.swapaxes(b_re, 3, 4)), mat(jnp.swapaxes(b_im, 3, 4))], axis=1)
    cr = jnp.stack([mat(c_re), mat(c_im)], axis=1)
    dd = (d_skip.reshape(dg, 1, SSM_P) * jnp.eye(SSM_P, dtype=F32)[None]).reshape(dg, SSM_P * SSM_P, 1)
    blk = lambda *s: pl.BlockSpec((1,) + s, lambda i: (i,) + (0,) * len(s))
    return pl.pallas_call(
        _s5_param_kernel,
        grid=(dg,),
        in_specs=[blk(3, LANES), blk(2, SSM_P, LANES), blk(2, SSM_P, LANES), blk(SSM_P * SSM_P, 1)],
        out_specs=[blk(GC, GC), blk(GC, 2 * LANES), blk(GC, 2 * LANES), blk(1, 2 * LANES)],
        out_shape=[jax.ShapeDtypeStruct((dg, GC, GC), BF16),
                   jax.ShapeDtypeStruct((dg, GC, 2 * LANES), BF16),
                   jax.ShapeDtypeStruct((dg, GC, 2 * LANES), BF16),
                   jax.ShapeDtypeStruct((dg, 1, 2 * LANES), F32)],
        scratch_shapes=[pltpu.VMEM((SSM_P * SSM_P, LANES), F32)],
        compiler_params=_cparams("parallel"),
        name="s5_operators",
    )(prow, bt, cr, dd)


def _s5_sum_kernel(u_ref, w_ref, s_ref):
    s_ref[0] = _dot(u_ref[0], w_ref[0])


def _s5_scan_kernel(s_ref, al_ref, h_ref, *, n_ctx_chunks, n_chunks):
    a_re, a_im = al_ref[:, :LANES], al_ref[:, LANES:]
    is_fwd = lax.broadcasted_iota(jnp.int32, a_re.shape, 1) < SSM_N

    def body(i, carry):
        h_re, h_im = carry
        cf = i
        cr = jnp.where(i < n_ctx_chunks, n_ctx_chunks - 1 - i, n_chunks - 1 - (i - n_ctx_chunks))
        s_f, s_r = s_ref[cf], s_ref[cr]
        h_ref[cf, :, 0:SSM_N] = h_re[:, 0:SSM_N]
        h_ref[cr, :, SSM_N:LANES] = h_re[:, SSM_N:LANES]
        h_ref[cf, :, LANES:LANES + SSM_N] = h_im[:, 0:SSM_N]
        h_ref[cr, :, LANES + SSM_N:] = h_im[:, SSM_N:LANES]
        s_re = jnp.where(is_fwd, s_f[:, :LANES], s_r[:, :LANES])
        s_im = jnp.where(is_fwd, s_f[:, LANES:], s_r[:, LANES:])
        return a_re * h_re - a_im * h_im + s_re, a_re * h_im + a_im * h_re + s_im

    zero = jnp.zeros(a_re.shape, F32)
    lax.fori_loop(0, n_chunks, body, (zero, zero))


def _s5_out_kernel(u_ref, m_ref, h_ref, vt_ref, y_ref):
    y = _dot(u_ref[0], m_ref[0]) + _dot_nt(h_ref[0].astype(BF16), vt_ref[0])
    y_ref[0] = y.astype(BF16)


def _s5_apply(s_lat, s_ctx, ops, layer):
    m_op, w_op, vt_op, al = ops
    bsz, n_tok, _ = s_lat.shape
    n_ctx = s_ctx.shape[1]
    ncc, n_chunks = n_ctx // CHUNK, (n_ctx + n_tok) // CHUNK
    rows = bsz * n_chunks
    g0 = layer * SSM_G
    s_all = jnp.concatenate([s_ctx, s_lat], axis=1)
    u = s_all.reshape(bsz, n_chunks, CHUNK, SSM_G, SSM_P).transpose(3, 0, 1, 4, 2).reshape(SSM_G, rows, GC)
    gspec = lambda r, c: pl.BlockSpec((1, r, c), lambda g: (g, 0, 0))
    ospec = lambda r, c: pl.BlockSpec((1, r, c), lambda g: (g + g0, 0, 0))
    sums = pl.pallas_call(
        _s5_sum_kernel,
        grid=(SSM_G,),
        in_specs=[gspec(rows, GC), ospec(GC, 2 * LANES)],
        out_specs=gspec(rows, 2 * LANES),
        out_shape=jax.ShapeDtypeStruct((SSM_G, rows, 2 * LANES), F32),
        compiler_params=_cparams("parallel"),
        name="s5_chunk_sums",
    )(u, w_op)
    sums_t = sums.reshape(SSM_G, bsz, n_chunks, 2 * LANES).transpose(2, 0, 1, 3).reshape(n_chunks, SSM_G * bsz, 2 * LANES)
    al_rows = jnp.broadcast_to(al[g0:g0 + SSM_G], (SSM_G, bsz, 2 * LANES)).reshape(SSM_G * bsz, 2 * LANES)
    rt = 64
    states_t = pl.pallas_call(
        functools.partial(_s5_scan_kernel, n_ctx_chunks=ncc, n_chunks=n_chunks),
        grid=(SSM_G * bsz // rt,),
        in_specs=[pl.BlockSpec((n_chunks, rt, 2 * LANES), lambda r: (0, r, 0)),
                  pl.BlockSpec((rt, 2 * LANES), lambda r: (r, 0))],
        out_specs=pl.BlockSpec((n_chunks, rt, 2 * LANES), lambda r: (0, r, 0)),
        out_shape=jax.ShapeDtypeStruct((n_chunks, SSM_G * bsz, 2 * LANES), F32),
        compiler_params=_cparams("parallel"),
        name="s5_state_scan",
    )(sums_t, al_rows)
    states = states_t.reshape(n_chunks, SSM_G, bsz, 2 * LANES).transpose(1, 2, 0, 3).reshape(SSM_G, rows, 2 * LANES)
    y = pl.pallas_call(
        _s5_out_kernel,
        grid=(SSM_G,),
        in_specs=[gspec(rows, GC), ospec(GC, GC), gspec(rows, 2 * LANES), ospec(GC, 2 * LANES)],
        out_specs=gspec(rows, GC),
        out_shape=jax.ShapeDtypeStruct((SSM_G, rows, GC), BF16),
        compiler_params=_cparams("parallel"),
        name="s5_chunk_outputs",
    )(u, m_op, states, vt_op)
    return y.reshape(SSM_G, bsz, n_chunks, SSM_P, CHUNK).transpose(1, 2, 4, 0, 3).reshape(bsz, n_ctx + n_tok, SSM_WIDTH)


def _glu_kernel(y_ref, w_ref, b_ref, g_ref, o_ref):
    y = y_ref[0].astype(F32)
    z = 0.5 * y * (1.0 + jnp.tanh(math.sqrt(2.0 / math.pi) * (y + 0.044715 * (y * y * y))))
    t = _dot(z.astype(BF16), w_ref[0]) + b_ref[...]
    o = z * jax.nn.sigmoid(t)
    inv = lax.rsqrt(jnp.mean(o * o, axis=-1, keepdims=True) + LN_EPS)
    o_ref[0] = (o * inv * g_ref[...]).astype(BF16)


def _glu(y_all, tok_off, n_tok, w_glu_bf, layer, b_glu, g_ssm):
    bsz, _, w = y_all.shape
    tm = min(ROW_TILE, n_tok)
    assert tok_off % tm == 0
    row = lambda a: a.reshape(1, w)
    vspec = pl.BlockSpec((1, w), lambda b, t: (0, 0))
    return pl.pallas_call(
        _glu_kernel,
        grid=(bsz, n_tok // tm),
        in_specs=[pl.BlockSpec((1, tm, w), lambda b, t: (b, t + tok_off // tm, 0)),
                  pl.BlockSpec((1, w, w), lambda b, t: (layer, 0, 0)), vspec, vspec],
        out_specs=pl.BlockSpec((1, tm, w), lambda b, t: (b, t, 0)),
        out_shape=jax.ShapeDtypeStruct((bsz, n_tok, w), BF16),
        compiler_params=_cparams("parallel", "parallel"),
        name="s5_glu",
    )(y_all, w_glu_bf, row(b_glu), row(g_ssm))


def _layer_norm(z, g, b):
    mu = jnp.mean(z, axis=-1, keepdims=True)
    zc = z - mu
    var = jnp.mean(zc * zc, axis=-1, keepdims=True)
    return zc * lax.rsqrt(var + LN_EPS) * g + b


def _first_argmax(vals):
    best_i = jnp.zeros(vals[0].shape, jnp.int32)
    best_v = vals[0]
    for j in range(1, len(vals)):
        better = vals[j] > best_v
        best_i = jnp.where(better, j, best_i)
        best_v = jnp.where(better, vals[j], best_v)
    return best_i, best_v


def _route(logit_rows):
    m = functools.reduce(jnp.maximum, logit_rows)
    p = [jnp.exp(l - m) for l in logit_rows]
    scores = []
    for g in range(N_EXPERT_GROUPS):
        a, b, c, d = p[4 * g:4 * g + 4]
        hi1, lo1, hi2, lo2 = jnp.maximum(a, b), jnp.minimum(a, b), jnp.maximum(c, d), jnp.minimum(c, d)
        scores.append(jnp.maximum(hi1, hi2) + jnp.maximum(jnp.minimum(hi1, hi2), jnp.maximum(lo1, lo2)))
    grp, _ = _first_argmax(scores)
    sel = []
    for j in range(EXPERTS_PER_GROUP):
        v = p[j]
        for g in range(1, N_EXPERT_GROUPS):
            v = jnp.where(grp == g, p[4 * g + j], v)
        sel.append(v)
    i1, v1 = _first_argmax(sel)
    i2, v2 = _first_argmax([jnp.where(i1 == j, -1.0, sel[j]) for j in range(EXPERTS_PER_GROUP)])
    tot = v1 + v2
    return grp * EXPERTS_PER_GROUP + i1, grp * EXPERTS_PER_GROUP + i2, v1 / tot, v2 / tot


def _outproj_kernel(a_ref, s_ref, x_ref, mod_ref, w_ref, lng_ref, lnb_ref, wr_ref, br_ref, *rest, alpha):
    x1_ref, u2_ref, ri_ref, rw_ref = rest[-4:]
    y = _dot(a_ref[0], w_ref[0, 0:ATTN_WIDTH, :]) + _dot(s_ref[0], w_ref[0, ATTN_WIDTH:, :])
    g1 = mod_ref[0, 2:3, :]
    sh2 = mod_ref[0, 3:4, :]
    sc2 = mod_ref[0, 4:5, :]
    x1 = _layer_norm(alpha * x_ref[0] + g1 * y, lng_ref[...], lnb_ref[...])
    x1_ref[0] = x1
    u2 = (x1 * (1.0 + sc2) + sh2).astype(BF16)
    u2_ref[...] = u2
    logits = _dot_nt(wr_ref[...], u2) + br_ref[...]
    e1, e2, w1, w2 = _route([logits[e:e + 1, :] for e in range(N_EXPERTS)])
    ri_ref[0, 0:1, :] = e1
    ri_ref[0, 1:2, :] = e2
    rw_ref[0, 0:1, :] = w1
    rw_ref[0, 1:2, :] = w2


def _out_projection(attn_n, ssm_n, x, mod3, row_of_batch, w_out_bf, layer, ln_g, ln_b, w_router_t_bf, b_router,
                    alpha, u2_rows, row_off, u2_prev=None):
    bsz, n_tok, d = x.shape
    tm = min(ROW_TILE, n_tok)
    nt = n_tok // tm
    assert row_off % tm == 0
    tspec = lambda w: pl.BlockSpec((1, tm, w), lambda b, t: (b, t, 0))
    vspec = pl.BlockSpec((1, d), lambda b, t: (0, 0))
    rspec = pl.BlockSpec((1, 2, tm), lambda b, t: (b, 0, t))
    in_specs = [
        tspec(ATTN_WIDTH), tspec(SSM_WIDTH), tspec(d),
        pl.BlockSpec((1, N_MOD, d), lambda b, t: (row_of_batch(b), 0, 0)),
        pl.BlockSpec((1,) + w_out_bf.shape[1:], lambda b, t: (layer, 0, 0)),
        vspec, vspec,
        pl.BlockSpec((N_EXPERTS, d), lambda b, t: (0, 0)),
        pl.BlockSpec((N_EXPERTS, 1), lambda b, t: (0, 0)),
    ]
    args = [attn_n, ssm_n, x, mod3, w_out_bf, ln_g.reshape(1, d), ln_b.reshape(1, d), w_router_t_bf,
            b_router.reshape(N_EXPERTS, 1)]
    aliases = {}
    if u2_prev is not None:
        in_specs.append(pl.BlockSpec(memory_space=pl.ANY))
        args.append(u2_prev)
        aliases = {len(args) - 1: 1}
    return pl.pallas_call(
        functools.partial(_outproj_kernel, alpha=alpha),
        grid=(bsz, nt),
        in_specs=in_specs,
        out_specs=[tspec(d), pl.BlockSpec((tm, d), lambda b, t: (row_off // tm + b * nt + t, 0)), rspec, rspec],
        out_shape=[jax.ShapeDtypeStruct((bsz, n_tok, d), F32), jax.ShapeDtypeStruct((u2_rows, d), BF16),
                   jax.ShapeDtypeStruct((bsz, 2, n_tok), jnp.int32), jax.ShapeDtypeStruct((bsz, 2, n_tok), F32)],
        input_output_aliases=aliases,
        compiler_params=_cparams("parallel", "parallel"),
        name="out_projection",
    )(*args)


def _ffn_kernel(te_ref, nu_ref, x_ref, wg_ref, wu_ref, wd_ref, y_ref):
    t = pl.program_id(0)

    @pl.when(t < nu_ref[0])
    def _():
        x = x_ref[...]
        g = _dot(x, wg_ref[0])
        u = _dot(x, wu_ref[0])
        h = (g * jax.nn.sigmoid(g) * u).astype(BF16)
        y_ref[...] = _dot(h, wd_ref[0]).astype(BF16)

    @pl.when(t >= nu_ref[0])
    def _():
        y_ref[...] = jnp.zeros(y_ref.shape, BF16)


def _expert_ffn(xs, tile_expert, n_used, wg_bf, wu_bf, wd_bf, layer):
    n_rows, d = xs.shape
    f = wg_bf.shape[2]
    tm = MOE_TILE
    e0 = layer * N_EXPERTS
    grid_spec = pltpu.PrefetchScalarGridSpec(
        num_scalar_prefetch=2,
        grid=(n_rows // tm,),
        in_specs=[
            pl.BlockSpec((tm, d), lambda t, te, nu: (t, 0)),
            pl.BlockSpec((1, d, f), lambda t, te, nu: (e0 + te[t], 0, 0)),
            pl.BlockSpec((1, d, f), lambda t, te, nu: (e0 + te[t], 0, 0)),
            pl.BlockSpec((1, f, d), lambda t, te, nu: (e0 + te[t], 0, 0)),
        ],
        out_specs=pl.BlockSpec((tm, d), lambda t, te, nu: (t, 0)),
    )
    return pl.pallas_call(
        _ffn_kernel,
        grid_spec=grid_spec,
        out_shape=jax.ShapeDtypeStruct((n_rows, d), BF16),
        compiler_params=_cparams("arbitrary"),
        name="expert_ffn",
    )(tile_expert, n_used, xs, wg_bf, wu_bf, wd_bf)


def _dispatch_plan(e1, e2, tm):
    n = e1.shape[0]
    n_tiles = (2 * n + N_EXPERTS * (tm - 1) + tm - 1) // tm
    e = jnp.concatenate([e1, e2])
    onehot = (e[:, None] == jnp.arange(N_EXPERTS, dtype=jnp.int32)[None, :]).astype(jnp.int32)
    csum = jnp.cumsum(onehot, axis=0)
    pos_in_e = jnp.sum((csum - 1) * onehot, axis=1)
    counts = csum[-1]
    padded = ((counts + tm - 1) // tm) * tm
    ends = jnp.cumsum(padded)
    offs = ends - padded
    dest = offs[e] + pos_in_e
    tok = jnp.concatenate([jnp.arange(n, dtype=jnp.int32)] * 2)
    row_src = jnp.zeros((n_tiles * tm,), jnp.int32).at[dest].set(tok, mode="promise_in_bounds", unique_indices=True)
    n_used = (ends[-1] // tm).astype(jnp.int32)
    tile_start = jnp.arange(n_tiles, dtype=jnp.int32) * tm
    tile_e = jnp.sum((tile_start[:, None] >= ends[None, :]).astype(jnp.int32), axis=1)
    tile_e = jnp.minimum(tile_e, N_EXPERTS - 1)
    last_e = tile_e[jnp.maximum(n_used - 1, 0)]
    tile_e = jnp.where(jnp.arange(n_tiles) < n_used, tile_e, last_e).astype(jnp.int32)
    return row_src, dest[:n], dest[n:], tile_e, n_used.reshape(1)


def _take_rows(a, idx):
    return a.at[idx].get(mode="promise_in_bounds")


def _final_kernel(x_ref, y1_ref, y2_ref, w1_ref, w2_ref, mod_ref, lng_ref, lnb_ref, o_ref, *, alpha):
    f = w1_ref[...] * y1_ref[...].astype(F32) + w2_ref[...] * y2_ref[...].astype(F32)
    g2 = mod_ref[0, 5:6, :]
    o_ref[0] = _layer_norm(alpha * x_ref[0] + g2 * f, lng_ref[...], lnb_ref[...])


def _combine_ln(x1, y1, y2, w1, w2, row_off, mod3, row_of_batch, ln_g, ln_b, alpha):
    bsz, n_tok, d = x1.shape
    tm = min(ROW_TILE, n_tok)
    nt = n_tok // tm
    assert row_off % tm == 0
    tspec = pl.BlockSpec((1, tm, d), lambda b, t: (b, t, 0))
    yspec = pl.BlockSpec((tm, d), lambda b, t: (row_off // tm + b * nt + t, 0))
    wspec = pl.BlockSpec((tm, 1), lambda b, t: (row_off // tm + b * nt + t, 0))
    vspec = pl.BlockSpec((1, d), lambda b, t: (0, 0))
    return pl.pallas_call(
        functools.partial(_final_kernel, alpha=alpha),
        grid=(bsz, nt),
        in_specs=[tspec, yspec, yspec, wspec, wspec,
                  pl.BlockSpec((1, N_MOD, d), lambda b, t: (row_of_batch(b), 0, 0)), vspec, vspec],
        out_specs=tspec,
        out_shape=jax.ShapeDtypeStruct((bsz, n_tok, d), F32),
        compiler_params=_cparams("parallel", "parallel"),
        name="combine_post_ln",
    )(x1, y1, y2, w1, w2, mod3, ln_g.reshape(1, d), ln_b.reshape(1, d))


def kernel(x, c, ctx, c_ctx, w_mod, b_mod, w_in, attn_sink, ssm_lambda_re, ssm_lambda_im, ssm_log_dt, ssm_b_re, ssm_b_im, ssm_c_re, ssm_c_im, ssm_d, w_glu, b_glu, g_attn_out, g_ssm_out, w_out, ln1_g, ln1_b, w_router, b_router, w_expert_gate, w_expert_up, w_expert_down, ln2_g, ln2_b):
    depth = w_mod.shape[0]
    bsz, n_tok, d = x.shape
    n_ctx = ctx.shape[1]
    alpha = (2 * depth) ** 0.25
    assert n_tok % ROW_TILE == 0 and n_tok % ATTN_BLOCK == 0 and n_ctx % ATTN_BLOCK == 0 and n_ctx % CHUNK == 0

    mod = _modulation(c, c_ctx, w_mod, b_mod)
    s5_ops = _s5_operators(ssm_lambda_re, ssm_lambda_im, ssm_log_dt, ssm_b_re, ssm_b_im,
                           ssm_c_re, ssm_c_im, ssm_d)
    cos_t, sin_t = _rope_tables(n_tok)
    w_router_t = w_router.T.astype(BF16)
    w_in_bf, w_glu_bf, w_out_bf = w_in.astype(BF16), w_glu.astype(BF16), w_out.astype(BF16)
    wg_bf, wu_bf, wd_bf = _cast_bf16(w_expert_gate), _cast_bf16(w_expert_up), _cast_bf16(w_expert_down)
    lat_row = lambda b: b
    ctx_row = lambda b: bsz
    n_lat, n_c = bsz * n_tok, bsz * n_ctx

    xc = ctx
    for i in range(depth):
        last = i == depth - 1
        mod3 = mod[i].reshape(MOD_ROWS, N_MOD, d)
        n_moe = n_lat if last else n_lat + n_c

        q, k, v, s = _in_projection(x, mod3, lat_row, w_in_bf, i, cos_t, sin_t, True)
        qc, kc, vc, sc = _in_projection(xc, mod3, ctx_row, w_in_bf, i, cos_t, sin_t, False)
        attn_n = _attention(q, k, v, kc, vc, attn_sink[i], g_attn_out[i], True)
        y_all = _s5_apply(s, sc, s5_ops, i)
        ssm_n = _glu(y_all, n_ctx, n_tok, w_glu_bf, i, b_glu[i], g_ssm_out[i])
        x1, u2, ri, rw = _out_projection(attn_n, ssm_n, x, mod3, lat_row, w_out_bf, i, ln1_g[i], ln1_b[i],
                                         w_router_t, b_router, alpha, n_moe, 0)
        if not last:
            attn_c = _attention(qc, None, None, kc, vc, attn_sink[i], g_attn_out[i], False)
            ssm_c = _glu(y_all, 0, n_ctx, w_glu_bf, i, b_glu[i], g_ssm_out[i])
            xc1, u2, ric, rwc = _out_projection(attn_c, ssm_c, xc, mod3, ctx_row, w_out_bf, i, ln1_g[i], ln1_b[i],
                                                w_router_t, b_router, alpha, n_moe, n_lat, u2_prev=u2)
            ri = jnp.concatenate([ri.transpose(1, 0, 2).reshape(2, n_lat), ric.transpose(1, 0, 2).reshape(2, n_c)], axis=1)
            rw = jnp.concatenate([rw.transpose(1, 0, 2).reshape(2, n_lat), rwc.transpose(1, 0, 2).reshape(2, n_c)], axis=1)
        else:
            ri = ri.transpose(1, 0, 2).reshape(2, n_lat)
            rw = rw.transpose(1, 0, 2).reshape(2, n_lat)

        row_src, pos1, pos2, tile_e, n_used = _dispatch_plan(ri[0], ri[1], MOE_TILE)
        xs = _take_rows(u2, row_src)
        ys = _expert_ffn(xs, tile_e, n_used, wg_bf, wu_bf, wd_bf, i)
        y1 = _take_rows(ys, pos1)
        y2 = _take_rows(ys, pos2)
        cw1, cw2 = rw[0].reshape(n_moe, 1), rw[1].reshape(n_moe, 1)
        x = _combine_ln(x1, y1, y2, cw1, cw2, 0, mod3, lat_row, ln2_g[i], ln2_b[i], alpha)
        if not last:
            xc = _combine_ln(xc1, y1, y2, cw1, cw2, n_lat, mod3, ctx_row, ln2_g[i], ln2_b[i], alpha)
    return x
```

```python
import functools
import math

import jax
import jax.numpy as jnp
from jax import lax
from jax.experimental import pallas as pl
from jax.experimental.pallas import tpu as pltpu

F32 = jnp.float32
BF16 = jnp.bfloat16

HEAD_DIM = 128
N_Q_HEADS = 8
N_KV_HEADS = 2
Q_PER_KV = N_Q_HEADS // N_KV_HEADS
ATTN_WIDTH = N_Q_HEADS * HEAD_DIM
KV_WIDTH = N_KV_HEADS * HEAD_DIM
SSM_WIDTH = 1024
ATTN_BLOCK = 128
GRID_W = 64
ROPE_THETA = 10000.0
SSM_P = 16
SSM_G = SSM_WIDTH // SSM_P
SSM_N = 64
CHUNK = 64
GC = SSM_P * CHUNK
N_EXPERTS = 16
N_EXPERT_GROUPS = 4
EXPERTS_PER_GROUP = 4
N_MOD = 6
LN_EPS = 1e-5
NEG_INF = -1e30
LANES = 128
MOD_ROWS = 8
ROW_TILE = 256
MOE_TILE = 256
VMEM_LIMIT = 56 * 1024 * 1024

assert 2 * SSM_N == LANES and 2 * CHUNK == LANES


def _cparams(*sem):
    return pltpu.CompilerParams(dimension_semantics=sem, vmem_limit_bytes=VMEM_LIMIT)


def _dot(a, b):
    return jnp.dot(a, b, preferred_element_type=F32)


def _dot_nt(a, b):
    return lax.dot_general(a, b, (((1,), (1,)), ((), ())), preferred_element_type=F32)


def _mod_kernel(ct_ref, w_ref, b_ref, o_ref, ab_ref, *, n_rows, tn):
    d = ct_ref.shape[0]

    @pl.when((pl.program_id(0) == 0) & (pl.program_id(1) == 0))
    def _():
        ct = ct_ref[...]
        a = ct * jax.nn.sigmoid(ct)
        for r in range(n_rows):
            ab_ref[r] = jnp.broadcast_to(a[:, r:r + 1], (d, LANES))

    o_ref[...] = jnp.zeros(o_ref.shape, F32)
    for j in range(tn // LANES):
        w = w_ref[0, :, j * LANES:(j + 1) * LANES]
        bias = b_ref[0, :, j * LANES:(j + 1) * LANES]
        for r in range(n_rows):
            acc = jnp.sum(ab_ref[r] * w, axis=0, keepdims=True)
            o_ref[0, r:r + 1, j * LANES:(j + 1) * LANES] = acc + bias


def _modulation(c, c_ctx, w_mod, b_mod):
    depth, d, n_out = w_mod.shape
    n_rows = c.shape[0] + 1
    assert n_rows <= MOD_ROWS
    tn = 512
    ct = jnp.zeros((MOD_ROWS, d), F32).at[:c.shape[0]].set(c).at[c.shape[0]].set(c_ctx).T
    return pl.pallas_call(
        functools.partial(_mod_kernel, n_rows=n_rows, tn=tn),
        grid=(depth, n_out // tn),
        in_specs=[
            pl.BlockSpec((d, MOD_ROWS), lambda l, j: (0, 0)),
            pl.BlockSpec((1, d, tn), lambda l, j: (l, 0, j)),
            pl.BlockSpec((1, 1, tn), lambda l, j: (l, 0, j)),
        ],
        out_specs=pl.BlockSpec((1, MOD_ROWS, tn), lambda l, j: (l, 0, j)),
        out_shape=jax.ShapeDtypeStruct((depth, MOD_ROWS, n_out), F32),
        scratch_shapes=[pltpu.VMEM((n_rows, d, LANES), F32)],
        compiler_params=_cparams("arbitrary", "arbitrary"),
        name="modulation",
    )(ct, w_mod, b_mod.reshape(depth, 1, n_out))


def _cast_kernel(after_ref, w_ref, o_ref):
    del after_ref
    o_ref[...] = w_ref[...].astype(BF16)


def _cast_layer_bf16(w, layer, after):
    _, n_e, k, n = w.shape
    w3 = w.reshape((-1, k, n))
    return pl.pallas_call(
        _cast_kernel,
        grid=(n_e,),
        in_specs=[pl.BlockSpec(memory_space=pltpu.SMEM), pl.BlockSpec((1, k, n), lambda e: (layer * n_e + e, 0, 0))],
        out_specs=pl.BlockSpec((1, k, n), lambda e: (e, 0, 0)),
        out_shape=jax.ShapeDtypeStruct((n_e, k, n), BF16),
        compiler_params=_cparams("parallel"),
        name="cast_bf16",
    )(after, w3)


def _rope(xh, cos, sin_signed):
    lane = lax.broadcasted_iota(jnp.int32, xh.shape, 1)
    swapped = jnp.where((lane % 64) < 32, pltpu.roll(xh, 96, 1), pltpu.roll(xh, 32, 1))
    return xh * cos + swapped * sin_signed


def _inproj_kernel(x_ref, mod_ref, w_ref, cos_ref, sin_ref, q_ref, k_ref, v_ref, *, rope):
    sh = mod_ref[0, 0:1, :]
    sc = mod_ref[0, 1:2, :]
    u = (x_ref[0] * (1.0 + sc) + sh).astype(BF16)
    q = _dot(u, w_ref[0, :, 0:ATTN_WIDTH])
    k = _dot(u, w_ref[0, :, ATTN_WIDTH:ATTN_WIDTH + KV_WIDTH])
    v = _dot(u, w_ref[0, :, ATTN_WIDTH + KV_WIDTH:ATTN_WIDTH + 2 * KV_WIDTH])
    scale = HEAD_DIM ** -0.5
    if rope:
        cos = cos_ref[...]
        sin = sin_ref[...]
    for h in range(N_Q_HEADS):
        qh = q[:, h * HEAD_DIM:(h + 1) * HEAD_DIM]
        if rope:
            qh = _rope(qh, cos, sin)
        q_ref[0, :, h * HEAD_DIM:(h + 1) * HEAD_DIM] = (qh * scale).astype(BF16)
    for h in range(N_KV_HEADS):
        kh = k[:, h * HEAD_DIM:(h + 1) * HEAD_DIM]
        if rope:
            kh = _rope(kh, cos, sin)
        k_ref[0, :, h * HEAD_DIM:(h + 1) * HEAD_DIM] = kh.astype(BF16)
    v_ref[0] = v.astype(BF16)


def _in_projection(x, mod3, row_of_batch, w_in_bf, layer, cos_t, sin_t, rope):
    bsz, n_tok, d = x.shape
    tm = min(ROW_TILE, n_tok)
    n_cols = ATTN_WIDTH + 2 * KV_WIDTH
    out = lambda w: jax.ShapeDtypeStruct((bsz, n_tok, w), BF16)
    ospec = lambda w: pl.BlockSpec((1, tm, w), lambda b, t: (b, t, 0))
    return pl.pallas_call(
        functools.partial(_inproj_kernel, rope=rope),
        grid=(bsz, n_tok // tm),
        in_specs=[
            pl.BlockSpec((1, tm, d), lambda b, t: (b, t, 0)),
            pl.BlockSpec((1, N_MOD, d), lambda b, t: (row_of_batch(b), 0, 0)),
            pl.BlockSpec((1, d, n_cols), lambda b, t: (layer, 0, 0)),
            pl.BlockSpec((tm, HEAD_DIM), lambda b, t: (t, 0)),
            pl.BlockSpec((tm, HEAD_DIM), lambda b, t: (t, 0)),
        ],
        out_specs=[ospec(ATTN_WIDTH), ospec(KV_WIDTH), ospec(KV_WIDTH)],
        out_shape=[out(ATTN_WIDTH), out(KV_WIDTH), out(KV_WIDTH)],
        compiler_params=_cparams("parallel", "parallel"),
        name="in_projection",
    )(x, mod3, w_in_bf, cos_t, sin_t)


S5_TILE = 1024
S5_TILE_ROWS = 2 * S5_TILE // LANES


def _to_group_layout(st):
    x4 = st.reshape(SSM_G, SSM_P, S5_TILE // LANES, LANES)
    lane = lax.broadcasted_iota(jnp.int32, x4.shape[:1] + x4.shape[2:], 2)
    roll64 = lambda a: pltpu.roll(a.reshape(-1, LANES), CHUNK, 1).reshape(a.shape)
    out = []
    for k in range(SSM_P // 2):
        a, b = x4[:, 2 * k], x4[:, 2 * k + 1]
        h0 = jnp.where(lane < CHUNK, a, roll64(b))
        h1 = jnp.where(lane < CHUNK, roll64(a), b)
        out.append(jnp.concatenate([h0, h1], axis=1))
    return out


def _from_group_layout(blocks):
    n8 = S5_TILE // LANES
    lane = lax.broadcasted_iota(jnp.int32, (SSM_G, n8, LANES), 2)
    roll64 = lambda a: pltpu.roll(a.reshape(-1, LANES), CHUNK, 1).reshape(a.shape)
    chans = []
    for blk in blocks:
        a0, b1 = blk[:, 0:n8], blk[:, n8:]
        chans.append(jnp.where(lane < CHUNK, a0, roll64(b1)))
        chans.append(jnp.where(lane < CHUNK, roll64(a0), b1))
    return jnp.stack(chans, axis=1).reshape(SSM_WIDTH, S5_TILE)


def _sproj_kernel(x_ref, mod_ref, w_ref, *rest):
    u_ref = rest[-1]
    sh = mod_ref[0, 0:1, :]
    sc = mod_ref[0, 1:2, :]
    x = x_ref[...].reshape(S5_TILE, x_ref.shape[-1])
    u = (x * (1.0 + sc) + sh).astype(BF16)
    st = _dot_nt(w_ref[0], u)
    for k, blk in enumerate(_to_group_layout(st)):
        u_ref[:, :, k * LANES:(k + 1) * LANES] = blk.astype(BF16)


def _s_projection(x, mod3, mod_row, ws_t_bf, layer, n_rows, row_blk_off, u_prev=None):
    bsz, n_tok, d = x.shape
    if u_prev is None:
        assert n_tok % S5_TILE == 0
        grid = (bsz, n_tok // S5_TILE)
        xspec = pl.BlockSpec((1, S5_TILE, d), lambda b, j: (b, j, 0))
    else:
        assert bsz * n_tok == S5_TILE and n_tok % LANES == 0
        grid = (1, 1)
        xspec = pl.BlockSpec((bsz, n_tok, d), lambda b, j: (0, 0, 0))
    nj = grid[1]
    in_specs = [xspec,
                pl.BlockSpec((1, N_MOD, d), lambda b, j: (mod_row(b), 0, 0)),
                pl.BlockSpec((1, SSM_WIDTH, d), lambda b, j: (layer, 0, 0))]
    args = [x, mod3, ws_t_bf]
    aliases = {}
    if u_prev is not None:
        in_specs.append(pl.BlockSpec(memory_space=pl.ANY))
        args.append(u_prev)
        aliases = {3: 0}
    return pl.pallas_call(
        _sproj_kernel,
        grid=grid,
        in_specs=in_specs,
        out_specs=pl.BlockSpec((SSM_G, S5_TILE_ROWS, GC), lambda b, j: (0, row_blk_off + b * nj + j, 0)),
        out_shape=jax.ShapeDtypeStruct((SSM_G, n_rows, GC), BF16),
        input_output_aliases=aliases,
        compiler_params=_cparams("parallel", "parallel"),
        name="s_projection",
    )(*args)


def _rope_tables(n_tok):
    half = HEAD_DIM // 2
    inv_freq = ROPE_THETA ** (-jnp.arange(0, half, 2, dtype=F32) / half)
    t = jnp.arange(n_tok)
    row = (t // GRID_W).astype(F32)
    col = (t % GRID_W).astype(F32)
    ang_r = row[:, None] * inv_freq[None, :]
    ang_c = col[:, None] * inv_freq[None, :]
    cos_t = jnp.concatenate([jnp.cos(ang_r), jnp.cos(ang_r), jnp.cos(ang_c), jnp.cos(ang_c)], axis=-1)
    sin_t = jnp.concatenate([-jnp.sin(ang_r), jnp.sin(ang_r), -jnp.sin(ang_c), jnp.sin(ang_c)], axis=-1)
    return cos_t, sin_t


def _attn_kernel(sink_ref, q_ref, *refs, has_band, n_blk):
    if has_band:
        kp_ref, ko_ref, kn_ref, vp_ref, vo_ref, vn_ref, kc_ref, vc_ref, g_ref, o_ref = refs
    else:
        kc_ref, vc_ref, g_ref, o_ref = refs
    blk = pl.program_id(1)
    rows = Q_PER_KV * ATTN_BLOCK
    qi = lax.broadcasted_iota(jnp.int32, (rows, ATTN_BLOCK), 0) % ATTN_BLOCK
    kj = lax.broadcasted_iota(jnp.int32, (rows, ATTN_BLOCK), 1)
    row_head = lax.broadcasted_iota(jnp.int32, (rows, 1), 0) // ATTN_BLOCK
    heads = []
    for h in range(N_KV_HEADS):
        hs = slice(h * HEAD_DIM, (h + 1) * HEAD_DIM)
        q = jnp.concatenate(
            [q_ref[0, :, (h * Q_PER_KV + g) * HEAD_DIM:(h * Q_PER_KV + g + 1) * HEAD_DIM] for g in range(Q_PER_KV)],
            axis=0)
        sink = jnp.zeros((rows, 1), F32)
        for g in range(Q_PER_KV):
            sink = jnp.where(row_head == g, sink_ref[h * Q_PER_KV + g], sink)
        s_c = _dot_nt(q, kc_ref[0, :, hs])
        m = jnp.maximum(jnp.max(s_c, axis=-1, keepdims=True), sink)
        if has_band:
            s_p = jnp.where((kj >= qi) & (blk > 0), _dot_nt(q, kp_ref[0, :, hs]), NEG_INF)
            s_o = _dot_nt(q, ko_ref[0, :, hs])
            s_n = jnp.where((kj <= qi) & (blk < n_blk - 1), _dot_nt(q, kn_ref[0, :, hs]), NEG_INF)
            m = jnp.maximum(m, jnp.max(jnp.maximum(jnp.maximum(s_p, s_o), s_n), axis=-1, keepdims=True))
        p_c = jnp.exp(s_c - m)
        denom = jnp.sum(p_c, axis=-1, keepdims=True) + jnp.exp(sink - m)
        acc = _dot(p_c.astype(BF16), vc_ref[0, :, hs])
        if has_band:
            for s_x, v_ref in ((s_p, vp_ref), (s_o, vo_ref), (s_n, vn_ref)):
                p_x = jnp.exp(s_x - m)
                denom = denom + jnp.sum(p_x, axis=-1, keepdims=True)
                acc = acc + _dot(p_x.astype(BF16), v_ref[0, :, hs])
        o = acc / denom
        for g in range(Q_PER_KV):
            heads.append(o[g * ATTN_BLOCK:(g + 1) * ATTN_BLOCK, :])
    ss = heads[0] * heads[0]
    for o in heads[1:]:
        ss = ss + o * o
    inv = lax.rsqrt(jnp.sum(ss, axis=-1, keepdims=True) / ATTN_WIDTH + LN_EPS)
    for i, o in enumerate(heads):
        cs = slice(i * HEAD_DIM, (i + 1) * HEAD_DIM)
        o_ref[0, :, cs] = (o * inv * g_ref[:, cs]).astype(BF16)


def _attention(q, k, v, kc, vc, sink, g_attn, has_band):
    bsz, n_tok, _ = q.shape
    n_ctx = kc.shape[1]
    n_blk = n_tok // ATTN_BLOCK
    qspec = pl.BlockSpec((1, ATTN_BLOCK, ATTN_WIDTH), lambda b, n: (b, n, 0))
    kvspec = lambda f: pl.BlockSpec((1, ATTN_BLOCK, KV_WIDTH), lambda b, n: (b, f(n), 0))
    cspec = pl.BlockSpec((1, n_ctx, KV_WIDTH), lambda b, n: (b, 0, 0))
    prev = lambda n: jnp.maximum(n - 1, 0)
    own = lambda n: n
    nxt = lambda n: jnp.minimum(n + 1, n_blk - 1)
    in_specs = [pl.BlockSpec(memory_space=pltpu.SMEM), qspec]
    args = [sink, q]
    if has_band:
        in_specs += [kvspec(prev), kvspec(own), kvspec(nxt), kvspec(prev), kvspec(own), kvspec(nxt)]
        args += [k, k, k, v, v, v]
    in_specs += [cspec, cspec, pl.BlockSpec((1, ATTN_WIDTH), lambda b, n: (0, 0))]
    args += [kc, vc, g_attn.reshape(1, ATTN_WIDTH)]
    return pl.pallas_call(
        functools.partial(_attn_kernel, has_band=has_band, n_blk=n_blk),
        grid=(bsz, n_blk),
        in_specs=in_specs,
        out_specs=qspec,
        out_shape=jax.ShapeDtypeStruct((bsz, n_tok, ATTN_WIDTH), BF16),
        compiler_params=_cparams("parallel", "parallel"),
        name="attention",
    )(*args)


def _cmul(a, b):
    return a[0] * b[0] - a[1] * b[1], a[0] * b[1] + a[1] * b[0]


def _cpow(rho, theta, expo):
    mag = jnp.exp(expo * rho)
    ang = expo * theta
    return mag * jnp.cos(ang), mag * jnp.sin(ang)


def _s5_param_kernel(after_ref, prow_ref, bt_ref, cr_ref, dd_ref, m_ref, w_ref, vt_ref, al_ref, kmat_ref):
    del after_ref
    p_dim, lc = SSM_P, CHUNK
    lam_re, lam_im = prow_ref[0, 0:1, :], prow_ref[0, 1:2, :]
    dt = jnp.exp(prow_ref[0, 2:3, :])
    rho, theta = lam_re * dt, lam_im * dt

    sub8 = lax.broadcasted_iota(jnp.int32, (8, LANES), 0)
    asc = _cpow(rho, theta, sub8.astype(F32))
    desc = _cpow(rho, theta, (8 - sub8).astype(F32))
    ex = jnp.where(sub8 == 0, 8, jnp.where(sub8 == 1, 16, jnp.where(sub8 == 2, 32, jnp.where(sub8 == 3, lc, 1))))
    pw = _cpow(rho, theta, ex.astype(F32))
    row = lambda t, i: (t[0][i:i + 1, :], t[1][i:i + 1, :])
    cat = lambda a, b: (jnp.concatenate([a[0], b[0]], axis=0), jnp.concatenate([a[1], b[1]], axis=0))
    for i in range(3):
        step = row(pw, i)
        asc = cat(asc, _cmul(asc, step))
        desc = cat(_cmul(desc, step), desc)
    a_chunk, a_one = row(pw, 3), row(pw, 4)

    den = lam_re * lam_re + lam_im * lam_im
    x_re, x_im = a_one[0] - 1.0, a_one[1]
    beta = ((x_re * lam_re + x_im * lam_im) / den, (x_im * lam_re - x_re * lam_im) / den)
    bb = _cmul(beta, (bt_ref[0, 0], bt_ref[0, 1]))
    cc = (cr_ref[0, 0], cr_ref[0, 1])

    is_fwd = lax.broadcasted_iota(jnp.int32, (lc, LANES), 1) < SSM_N
    sub = lax.broadcasted_iota(jnp.int32, (lc, LANES), 0)
    pw_w = (jnp.where(is_fwd, desc[0], asc[0]), jnp.where(is_fwd, desc[1], asc[1]))
    pw_v = (jnp.where(is_fwd, asc[0], desc[0]), jnp.where(is_fwd, asc[1], desc[1]))

    top = (jnp.where(is_fwd, asc[0], jnp.where(sub == 0, 1.0, 0.0)), jnp.where(is_fwd, asc[1], 0.0))
    keep = (~is_fwd) & (sub > 0)
    bot = (jnp.where(keep, desc[0], 0.0), jnp.where(keep, desc[1], 0.0))
    lag = cat(top, bot)

    cx = (jnp.concatenate([cc[0]] * p_dim, axis=0), jnp.concatenate([cc[1]] * p_dim, axis=0))
    bx = tuple(jnp.concatenate([jnp.broadcast_to(b[q:q + 1, :], (p_dim, LANES)) for q in range(p_dim)], axis=0)
               for b in bb)
    e_re, e_im = _cmul(cx, bx)
    hi = lax.Precision.HIGHEST
    kmat = (jnp.dot(e_re, lag[0].T, preferred_element_type=F32, precision=hi)
            - jnp.dot(e_im, lag[1].T, preferred_element_type=F32, precision=hi))
    lane0 = lax.broadcasted_iota(jnp.int32, kmat.shape, 1) == 0
    kmat_ref[...] = kmat + jnp.where(lane0, dd_ref[0], 0.0)

    lane_m = lax.broadcasted_iota(jnp.int32, (lc, LANES), 1)

    def q_body(q, carry):
        for pp in range(p_dim // 2):
            r0 = q * p_dim + 2 * pp
            ka = jnp.broadcast_to(kmat_ref[pl.ds(r0, 1), :], (lc, LANES))
            kb = jnp.broadcast_to(kmat_ref[pl.ds(r0 + 1, 1), :], (lc, LANES))
            ra = pltpu.roll(ka, 0, 1, stride=1, stride_axis=0)
            rb = pltpu.roll(kb, lc, 1, stride=1, stride_axis=0)
            blk = jnp.where(lane_m < lc, ra, rb)
            m_ref[0, pl.ds(pl.multiple_of(q * lc, lc), lc), pp * LANES:(pp + 1) * LANES] = blk.astype(BF16)
        return carry

    lax.fori_loop(0, p_dim, q_body, 0)

    for q in range(p_dim):
        w_re, w_im = _cmul(pw_w, row(bb, q))
        w_ref[0, q * lc:(q + 1) * lc, :] = jnp.concatenate([w_re, w_im], axis=1).astype(BF16)
    for p in range(p_dim):
        v_re, v_im = _cmul(pw_v, row(cc, p))
        vt_ref[0, p * lc:(p + 1) * lc, :] = jnp.concatenate([v_re, -v_im], axis=1).astype(BF16)
    al_ref[0] = jnp.concatenate([a_chunk[0], a_chunk[1]], axis=1)


def _s5_operators(lam_re, lam_im, log_dt, b_re, b_im, c_re, c_im, d_skip, layer, after):
    depth = lam_re.shape[0]
    dg = depth * SSM_G
    g0 = layer * SSM_G
    vec = lambda a: jnp.moveaxis(a, 1, 2).reshape(dg, LANES)
    mat = lambda a: jnp.moveaxis(a, 1, 3).reshape(dg, SSM_P, LANES)
    ldt = jnp.broadcast_to(log_dt[..., None], lam_re.shape)
    prow = jnp.stack([vec(lam_re), vec(lam_im), vec(ldt)], axis=1)
    bt = jnp.stack([mat(jnp.swapaxes(b_re, 3, 4)), mat(jnp.swapaxes(b_im, 3, 4))], axis=1)
    cr = jnp.stack([mat(c_re), mat(c_im)], axis=1)
    dd = (d_skip.reshape(dg, 1, SSM_P) * jnp.eye(SSM_P, dtype=F32)[None]).reshape(dg, SSM_P * SSM_P, 1)
    iblk = lambda *s: pl.BlockSpec((1,) + s, lambda i: (g0 + i,) + (0,) * len(s))
    oblk = lambda *s: pl.BlockSpec((1,) + s, lambda i: (i,) + (0,) * len(s))
    return pl.pallas_call(
        _s5_param_kernel,
        grid=(SSM_G,),
        in_specs=[pl.BlockSpec(memory_space=pltpu.SMEM),
                  iblk(3, LANES), iblk(2, SSM_P, LANES), iblk(2, SSM_P, LANES), iblk(SSM_P * SSM_P, 1)],
        out_specs=[oblk(GC, GC), oblk(GC, 2 * LANES), oblk(GC, 2 * LANES), oblk(1, 2 * LANES)],
        out_shape=[jax.ShapeDtypeStruct((SSM_G, GC, GC), BF16),
                   jax.ShapeDtypeStruct((SSM_G, GC, 2 * LANES), BF16),
                   jax.ShapeDtypeStruct((SSM_G, GC, 2 * LANES), BF16),
                   jax.ShapeDtypeStruct((SSM_G, 1, 2 * LANES), F32)],
        scratch_shapes=[pltpu.VMEM((SSM_P * SSM_P, LANES), F32)],
        compiler_params=_cparams("parallel"),
        name="s5_operators",
    )(after, prow, bt, cr, dd)


def _s5_sum_kernel(u_ref, w_ref, s_ref):
    s_ref[0] = _dot(u_ref[0], w_ref[0])


def _s5_scan_kernel(s_ref, al_ref, h_ref, *, n_ctx_chunks, n_chunks):
    a_re, a_im = al_ref[:, :LANES], al_ref[:, LANES:]
    is_fwd = lax.broadcasted_iota(jnp.int32, a_re.shape, 1) < SSM_N

    def body(i, carry):
        h_re, h_im = carry
        cf = i
        cr = jnp.where(i < n_ctx_chunks, n_ctx_chunks - 1 - i, n_chunks - 1 - (i - n_ctx_chunks))
        s_f, s_r = s_ref[cf], s_ref[cr]
        h_ref[cf, :, 0:SSM_N] = h_re[:, 0:SSM_N]
        h_ref[cr, :, SSM_N:LANES] = h_re[:, SSM_N:LANES]
        h_ref[cf, :, LANES:LANES + SSM_N] = h_im[:, 0:SSM_N]
        h_ref[cr, :, LANES + SSM_N:] = h_im[:, SSM_N:LANES]
        s_re = jnp.where(is_fwd, s_f[:, :LANES], s_r[:, :LANES])
        s_im = jnp.where(is_fwd, s_f[:, LANES:], s_r[:, LANES:])
        return a_re * h_re - a_im * h_im + s_re, a_re * h_im + a_im * h_re + s_im

    zero = jnp.zeros(a_re.shape, F32)
    lax.fori_loop(0, n_chunks, body, (zero, zero))


def _s5_out_kernel(u_ref, m_ref, h_ref, vt_ref, y_ref):
    y = _dot(u_ref[0], m_ref[0]) + _dot_nt(h_ref[0].astype(BF16), vt_ref[0])
    y_ref[0] = y.astype(BF16)


def _rows_to_chunks(a, bsz, n_tok, n_ctx):
    n8, nj, ncb, st = S5_TILE // LANES, n_tok // S5_TILE, n_ctx // LANES, a.shape[-1]
    n_lat = bsz * nj * S5_TILE_ROWS
    lat = a[:, :n_lat].reshape(SSM_G, bsz, nj, 2, n8, st).transpose(2, 4, 3, 0, 1, 5).reshape(-1, SSM_G * bsz, st)
    ctx = a[:, n_lat:].reshape(SSM_G, 2, bsz, ncb, st).transpose(3, 1, 0, 2, 4).reshape(-1, SSM_G * bsz, st)
    return jnp.concatenate([ctx, lat], axis=0)


def _chunks_to_rows(a, bsz, n_tok, n_ctx):
    n8, nj, ncb, st = S5_TILE // LANES, n_tok // S5_TILE, n_ctx // LANES, a.shape[-1]
    ncc = n_ctx // CHUNK
    lat = a[ncc:].reshape(nj, n8, 2, SSM_G, bsz, st).transpose(3, 4, 0, 2, 1, 5).reshape(SSM_G, -1, st)
    ctx = a[:ncc].reshape(ncb, 2, SSM_G, bsz, st).transpose(2, 1, 3, 0, 4).reshape(SSM_G, -1, st)
    return jnp.concatenate([lat, ctx], axis=1)


def _s5_apply(u, bsz, n_tok, n_ctx, ops):
    m_op, w_op, vt_op, al = ops
    ncc, n_chunks = n_ctx // CHUNK, (n_ctx + n_tok) // CHUNK
    rows = u.shape[1]
    assert rows == bsz * n_chunks
    gspec = lambda r, c: pl.BlockSpec((1, r, c), lambda g: (g, 0, 0))
    ospec = gspec
    sums = pl.pallas_call(
        _s5_sum_kernel,
        grid=(SSM_G,),
        in_specs=[gspec(rows, GC), ospec(GC, 2 * LANES)],
        out_specs=gspec(rows, 2 * LANES),
        out_shape=jax.ShapeDtypeStruct((SSM_G, rows, 2 * LANES), F32),
        compiler_params=_cparams("parallel"),
        name="s5_chunk_sums",
    )(u, w_op)
    sums_t = _rows_to_chunks(sums, bsz, n_tok, n_ctx)
    al_rows = jnp.broadcast_to(al, (SSM_G, bsz, 2 * LANES)).reshape(SSM_G * bsz, 2 * LANES)
    rt = 64
    states_t = pl.pallas_call(
        functools.partial(_s5_scan_kernel, n_ctx_chunks=ncc, n_chunks=n_chunks),
        grid=(SSM_G * bsz // rt,),
        in_specs=[pl.BlockSpec((n_chunks, rt, 2 * LANES), lambda r: (0, r, 0)),
                  pl.BlockSpec((rt, 2 * LANES), lambda r: (r, 0))],
        out_specs=pl.BlockSpec((n_chunks, rt, 2 * LANES), lambda r: (0, r, 0)),
        out_shape=jax.ShapeDtypeStruct((n_chunks, SSM_G * bsz, 2 * LANES), F32),
        compiler_params=_cparams("parallel"),
        name="s5_state_scan",
    )(sums_t, al_rows)
    states = _chunks_to_rows(states_t, bsz, n_tok, n_ctx)
    return pl.pallas_call(
        _s5_out_kernel,
        grid=(SSM_G,),
        in_specs=[gspec(rows, GC), ospec(GC, GC), gspec(rows, 2 * LANES), ospec(GC, 2 * LANES)],
        out_specs=gspec(rows, GC),
        out_shape=jax.ShapeDtypeStruct((SSM_G, rows, GC), BF16),
        compiler_params=_cparams("parallel"),
        name="s5_chunk_outputs",
    )(u, m_op, states, vt_op)


def _glu_kernel(y_ref, w_ref, b_ref, g_ref, o_ref):
    blocks = [y_ref[:, :, k * LANES:(k + 1) * LANES].astype(F32) for k in range(SSM_P // 2)]
    y = _from_group_layout(blocks).T
    z = 0.5 * y * (1.0 + jnp.tanh(math.sqrt(2.0 / math.pi) * (y + 0.044715 * (y * y * y))))
    t = _dot(z.astype(BF16), w_ref[0]) + b_ref[...]
    o = z * jax.nn.sigmoid(t)
    inv = lax.rsqrt(jnp.mean(o * o, axis=-1, keepdims=True) + LN_EPS)
    o_ref[...] = (o * inv * g_ref[...]).astype(BF16).reshape(o_ref.shape)


def _glu(y, bsz, n_tok, row_blk_off, per_batch, w_glu_bf, layer, b_glu, g_ssm):
    w = SSM_WIDTH
    if per_batch:
        grid = (bsz, n_tok // S5_TILE)
        ospec = pl.BlockSpec((1, S5_TILE, w), lambda b, j: (b, j, 0))
    else:
        assert bsz * n_tok == S5_TILE
        grid = (1, 1)
        ospec = pl.BlockSpec((bsz, n_tok, w), lambda b, j: (0, 0, 0))
    nj = grid[1]
    row = lambda a: a.reshape(1, w)
    vspec = pl.BlockSpec((1, w), lambda b, j: (0, 0))
    return pl.pallas_call(
        _glu_kernel,
        grid=grid,
        in_specs=[pl.BlockSpec((SSM_G, S5_TILE_ROWS, GC), lambda b, j: (0, row_blk_off + b * nj + j, 0)),
                  pl.BlockSpec((1, w, w), lambda b, j: (layer, 0, 0)), vspec, vspec],
        out_specs=ospec,
        out_shape=jax.ShapeDtypeStruct((bsz, n_tok, w), BF16),
        compiler_params=_cparams("parallel", "parallel"),
        name="s5_glu",
    )(y, w_glu_bf, row(b_glu), row(g_ssm))


def _layer_norm(z, g, b):
    mu = jnp.mean(z, axis=-1, keepdims=True)
    zc = z - mu
    var = jnp.mean(zc * zc, axis=-1, keepdims=True)
    return zc * lax.rsqrt(var + LN_EPS) * g + b


def _first_argmax(vals):
    best_i = jnp.zeros(vals[0].shape, jnp.int32)
    best_v = vals[0]
    for j in range(1, len(vals)):
        better = vals[j] > best_v
        best_i = jnp.where(better, j, best_i)
        best_v = jnp.where(better, vals[j], best_v)
    return best_i, best_v


def _route(logit_rows):
    m = functools.reduce(jnp.maximum, logit_rows)
    p = [jnp.exp(l - m) for l in logit_rows]
    scores = []
    for g in range(N_EXPERT_GROUPS):
        a, b, c, d = p[4 * g:4 * g + 4]
        hi1, lo1, hi2, lo2 = jnp.maximum(a, b), jnp.minimum(a, b), jnp.maximum(c, d), jnp.minimum(c, d)
        scores.append(jnp.maximum(hi1, hi2) + jnp.maximum(jnp.minimum(hi1, hi2), jnp.maximum(lo1, lo2)))
    grp, _ = _first_argmax(scores)
    sel = []
    for j in range(EXPERTS_PER_GROUP):
        v = p[j]
        for g in range(1, N_EXPERT_GROUPS):
            v = jnp.where(grp == g, p[4 * g + j], v)
        sel.append(v)
    i1, v1 = _first_argmax(sel)
    i2, v2 = _first_argmax([jnp.where(i1 == j, -1.0, sel[j]) for j in range(EXPERTS_PER_GROUP)])
    tot = v1 + v2
    return grp * EXPERTS_PER_GROUP + i1, grp * EXPERTS_PER_GROUP + i2, v1 / tot, v2 / tot


def _outproj_kernel(a_ref, s_ref, x_ref, mod_ref, w_ref, lng_ref, lnb_ref, wr_ref, br_ref, *rest, alpha):
    x1_ref, u2_ref, ri_ref, rw_ref = rest[-4:]
    y = _dot(a_ref[0], w_ref[0, 0:ATTN_WIDTH, :]) + _dot(s_ref[0], w_ref[0, ATTN_WIDTH:, :])
    g1 = mod_ref[0, 2:3, :]
    sh2 = mod_ref[0, 3:4, :]
    sc2 = mod_ref[0, 4:5, :]
    x1 = _layer_norm(alpha * x_ref[0] + g1 * y, lng_ref[...], lnb_ref[...])
    x1_ref[0] = x1
    u2 = (x1 * (1.0 + sc2) + sh2).astype(BF16)
    u2_ref[...] = u2
    logits = _dot_nt(wr_ref[...], u2) + br_ref[...]
    e1, e2, w1, w2 = _route([logits[e:e + 1, :] for e in range(N_EXPERTS)])
    ri_ref[0, 0:1, :] = e1
    ri_ref[0, 1:2, :] = e2
    rw_ref[0, 0:1, :] = w1
    rw_ref[0, 1:2, :] = w2


def _out_projection(attn_n, ssm_n, x, mod3, row_of_batch, w_out_bf, layer, ln_g, ln_b, w_router_t_bf, b_router,
                    alpha, u2_rows, row_off, u2_prev=None):
    bsz, n_tok, d = x.shape
    tm = min(ROW_TILE, n_tok)
    nt = n_tok // tm
    assert row_off % tm == 0
    tspec = lambda w: pl.BlockSpec((1, tm, w), lambda b, t: (b, t, 0))
    vspec = pl.BlockSpec((1, d), lambda b, t: (0, 0))
    rspec = pl.BlockSpec((1, 2, tm), lambda b, t: (b, 0, t))
    in_specs = [
        tspec(ATTN_WIDTH), tspec(SSM_WIDTH), tspec(d),
        pl.BlockSpec((1, N_MOD, d), lambda b, t: (row_of_batch(b), 0, 0)),
        pl.BlockSpec((1,) + w_out_bf.shape[1:], lambda b, t: (layer, 0, 0)),
        vspec, vspec,
        pl.BlockSpec((N_EXPERTS, d), lambda b, t: (0, 0)),
        pl.BlockSpec((N_EXPERTS, 1), lambda b, t: (0, 0)),
    ]
    args = [attn_n, ssm_n, x, mod3, w_out_bf, ln_g.reshape(1, d), ln_b.reshape(1, d), w_router_t_bf,
            b_router.reshape(N_EXPERTS, 1)]
    aliases = {}
    if u2_prev is not None:
        in_specs.append(pl.BlockSpec(memory_space=pl.ANY))
        args.append(u2_prev)
        aliases = {len(args) - 1: 1}
    return pl.pallas_call(
        functools.partial(_outproj_kernel, alpha=alpha),
        grid=(bsz, nt),
        in_specs=in_specs,
        out_specs=[tspec(d), pl.BlockSpec((tm, d), lambda b, t: (row_off // tm + b * nt + t, 0)), rspec, rspec],
        out_shape=[jax.ShapeDtypeStruct((bsz, n_tok, d), F32), jax.ShapeDtypeStruct((u2_rows, d), BF16),
                   jax.ShapeDtypeStruct((bsz, 2, n_tok), jnp.int32), jax.ShapeDtypeStruct((bsz, 2, n_tok), F32)],
        input_output_aliases=aliases,
        compiler_params=_cparams("parallel", "parallel"),
        name="out_projection",
    )(*args)


def _ffn_kernel(te_ref, nu_ref, x_ref, wg_ref, wu_ref, wd_ref, y_ref):
    t = pl.program_id(0)

    @pl.when(t < nu_ref[0])
    def _():
        x = x_ref[...]
        g = _dot(x, wg_ref[0])
        u = _dot(x, wu_ref[0])
        h = (g * jax.nn.sigmoid(g) * u).astype(BF16)
        y_ref[...] = _dot(h, wd_ref[0]).astype(BF16)

    @pl.when(t >= nu_ref[0])
    def _():
        y_ref[...] = jnp.zeros(y_ref.shape, BF16)


def _expert_ffn(xs, tile_expert, n_used, wg_bf, wu_bf, wd_bf):
    n_rows, d = xs.shape
    f = wg_bf.shape[2]
    tm = MOE_TILE
    grid_spec = pltpu.PrefetchScalarGridSpec(
        num_scalar_prefetch=2,
        grid=(n_rows // tm,),
        in_specs=[
            pl.BlockSpec((tm, d), lambda t, te, nu: (t, 0)),
            pl.BlockSpec((1, d, f), lambda t, te, nu: (te[t], 0, 0)),
            pl.BlockSpec((1, d, f), lambda t, te, nu: (te[t], 0, 0)),
            pl.BlockSpec((1, f, d), lambda t, te, nu: (te[t], 0, 0)),
        ],
        out_specs=pl.BlockSpec((tm, d), lambda t, te, nu: (t, 0)),
    )
    return pl.pallas_call(
        _ffn_kernel,
        grid_spec=grid_spec,
        out_shape=jax.ShapeDtypeStruct((n_rows, d), BF16),
        compiler_params=_cparams("arbitrary"),
        name="expert_ffn",
    )(tile_expert, n_used, xs, wg_bf, wu_bf, wd_bf)


def _dispatch_plan(e1, e2, tm):
    n = e1.shape[0]
    n_tiles = (2 * n + N_EXPERTS * (tm - 1) + tm - 1) // tm
    e = jnp.concatenate([e1, e2])
    onehot = (e[:, None] == jnp.arange(N_EXPERTS, dtype=jnp.int32)[None, :]).astype(jnp.int32)
    csum = jnp.cumsum(onehot, axis=0)
    pos_in_e = jnp.sum((csum - 1) * onehot, axis=1)
    counts = csum[-1]
    padded = ((counts + tm - 1) // tm) * tm
    ends = jnp.cumsum(padded)
    offs = ends - padded
    dest = offs[e] + pos_in_e
    tok = jnp.concatenate([jnp.arange(n, dtype=jnp.int32)] * 2)
    row_src = jnp.zeros((n_tiles * tm,), jnp.int32).at[dest].set(tok, mode="promise_in_bounds", unique_indices=True)
    n_used = (ends[-1] // tm).astype(jnp.int32)
    tile_start = jnp.arange(n_tiles, dtype=jnp.int32) * tm
    tile_e = jnp.sum((tile_start[:, None] >= ends[None, :]).astype(jnp.int32), axis=1)
    tile_e = jnp.minimum(tile_e, N_EXPERTS - 1)
    last_e = tile_e[jnp.maximum(n_used - 1, 0)]
    tile_e = jnp.where(jnp.arange(n_tiles) < n_used, tile_e, last_e).astype(jnp.int32)
    return row_src, dest[:n], dest[n:], tile_e, n_used.reshape(1)


def _take_rows(a, idx):
    return a.at[idx].get(mode="promise_in_bounds")


def _final_kernel(x_ref, y1_ref, y2_ref, w1_ref, w2_ref, mod_ref, lng_ref, lnb_ref, o_ref, *, alpha):
    f = w1_ref[...] * y1_ref[...].astype(F32) + w2_ref[...] * y2_ref[...].astype(F32)
    g2 = mod_ref[0, 5:6, :]
    o_ref[0] = _layer_norm(alpha * x_ref[0] + g2 * f, lng_ref[...], lnb_ref[...])


def _combine_ln(x1, y1, y2, w1, w2, row_off, mod3, row_of_batch, ln_g, ln_b, alpha):
    bsz, n_tok, d = x1.shape
    tm = min(ROW_TILE, n_tok)
    nt = n_tok // tm
    assert row_off % tm == 0
    tspec = pl.BlockSpec((1, tm, d), lambda b, t: (b, t, 0))
    yspec = pl.BlockSpec((tm, d), lambda b, t: (row_off // tm + b * nt + t, 0))
    wspec = pl.BlockSpec((tm, 1), lambda b, t: (row_off // tm + b * nt + t, 0))
    vspec = pl.BlockSpec((1, d), lambda b, t: (0, 0))
    return pl.pallas_call(
        functools.partial(_final_kernel, alpha=alpha),
        grid=(bsz, nt),
        in_specs=[tspec, yspec, yspec, wspec, wspec,
                  pl.BlockSpec((1, N_MOD, d), lambda b, t: (row_of_batch(b), 0, 0)), vspec, vspec],
        out_specs=tspec,
        out_shape=jax.ShapeDtypeStruct((bsz, n_tok, d), F32),
        compiler_params=_cparams("parallel", "parallel"),
        name="combine_post_ln",
    )(x1, y1, y2, w1, w2, mod3, ln_g.reshape(1, d), ln_b.reshape(1, d))


def kernel(x, c, ctx, c_ctx, w_mod, b_mod, w_in, attn_sink, ssm_lambda_re, ssm_lambda_im, ssm_log_dt, ssm_b_re, ssm_b_im, ssm_c_re, ssm_c_im, ssm_d, w_glu, b_glu, g_attn_out, g_ssm_out, w_out, ln1_g, ln1_b, w_router, b_router, w_expert_gate, w_expert_up, w_expert_down, ln2_g, ln2_b):
    depth = w_mod.shape[0]
    bsz, n_tok, d = x.shape
    n_ctx = ctx.shape[1]
    alpha = (2 * depth) ** 0.25
    assert n_tok % ROW_TILE == 0 and n_tok % ATTN_BLOCK == 0 and n_ctx % ATTN_BLOCK == 0 and n_ctx % CHUNK == 0

    mod = _modulation(c, c_ctx, w_mod, b_mod)
    s5_params = (ssm_lambda_re, ssm_lambda_im, ssm_log_dt, ssm_b_re, ssm_b_im, ssm_c_re, ssm_c_im, ssm_d)
    expert_w = (w_expert_gate, w_expert_up, w_expert_down)
    s5_ops = _s5_operators(*s5_params, 0, jnp.zeros((1,), jnp.int32))
    cos_t, sin_t = _rope_tables(n_tok)
    w_router_t = w_router.T.astype(BF16)
    w_in_bf, w_glu_bf, w_out_bf = w_in.astype(BF16), w_glu.astype(BF16), w_out.astype(BF16)
    ws_t_bf = jnp.swapaxes(w_in[:, :, ATTN_WIDTH + 2 * KV_WIDTH:], 1, 2).astype(BF16)
    lat_row = lambda b: b
    ctx_row = lambda b: bsz
    n_lat, n_c = bsz * n_tok, bsz * n_ctx
    lat_blocks = n_lat // S5_TILE
    s5_rows = (lat_blocks + 1) * S5_TILE_ROWS
    after_ffn = None

    xc = ctx
    for i in range(depth):
        last = i == depth - 1
        mod3 = mod[i].reshape(MOD_ROWS, N_MOD, d)
        n_moe = n_lat if last else n_lat + n_c

        q, k, v = _in_projection(x, mod3, lat_row, w_in_bf, i, cos_t, sin_t, True)
        qc, kc, vc = _in_projection(xc, mod3, ctx_row, w_in_bf, i, cos_t, sin_t, False)
        u = _s_projection(x, mod3, lat_row, ws_t_bf, i, s5_rows, 0)
        u = _s_projection(xc, mod3, ctx_row, ws_t_bf, i, s5_rows, lat_blocks, u_prev=u)
        attn_n = _attention(q, k, v, kc, vc, attn_sink[i], g_attn_out[i], True)
        y_s5 = _s5_apply(u, bsz, n_tok, n_ctx, s5_ops)
        ssm_n = _glu(y_s5, bsz, n_tok, 0, True, w_glu_bf, i, b_glu[i], g_ssm_out[i])
        x1, u2, ri, rw = _out_projection(attn_n, ssm_n, x, mod3, lat_row, w_out_bf, i, ln1_g[i], ln1_b[i],
                                         w_router_t, b_router, alpha, n_moe, 0)
        if not last:
            attn_c = _attention(qc, None, None, kc, vc, attn_sink[i], g_attn_out[i], False)
            ssm_c = _glu(y_s5, bsz, n_ctx, lat_blocks, False, w_glu_bf, i, b_glu[i], g_ssm_out[i])
            xc1, u2, ric, rwc = _out_projection(attn_c, ssm_c, xc, mod3, ctx_row, w_out_bf, i, ln1_g[i], ln1_b[i],
                                                w_router_t, b_router, alpha, n_moe, n_lat, u2_prev=u2)
            ri = jnp.concatenate([ri.transpose(1, 0, 2).reshape(2, n_lat), ric.transpose(1, 0, 2).reshape(2, n_c)], axis=1)
            rw = jnp.concatenate([rw.transpose(1, 0, 2).reshape(2, n_lat), rwc.transpose(1, 0, 2).reshape(2, n_c)], axis=1)
        else:
            ri = ri.transpose(1, 0, 2).reshape(2, n_lat)
            rw = rw.transpose(1, 0, 2).reshape(2, n_lat)

        row_src, pos1, pos2, tile_e, n_used = _dispatch_plan(ri[0], ri[1], MOE_TILE)
        xs = _take_rows(u2, row_src)
        w_after = n_used if after_ffn is None else after_ffn
        wg_bf, wu_bf, wd_bf = (_cast_layer_bf16(w, i, w_after) for w in expert_w)
        if not last:
            s5_ops = _s5_operators(*s5_params, i + 1, n_used)
        ys = _expert_ffn(xs, tile_e, n_used, wg_bf, wu_bf, wd_bf)
        after_ffn = ys[0:1, 0:1].astype(jnp.int32).reshape(1)
        y1 = _take_rows(ys, pos1)
        y2 = _take_rows(ys, pos2)
        cw1, cw2 = rw[0].reshape(n_moe, 1), rw[1].reshape(n_moe, 1)
        x = _combine_ln(x1, y1, y2, cw1, cw2, 0, mod3, lat_row, ln2_g[i], ln2_b[i], alpha)
        if not last:
            xc = _combine_ln(xc1, y1, y2, cw1, cw2, n_lat, mod3, ctx_row, ln2_g[i], ln2_b[i], alpha)
    return x
```

```python
import functools
import math

import jax
import jax.numpy as jnp
from jax import lax
from jax.experimental import pallas as pl
from jax.experimental.pallas import tpu as pltpu

F32 = jnp.float32
BF16 = jnp.bfloat16

HEAD_DIM = 128
N_Q_HEADS = 8
N_KV_HEADS = 2
Q_PER_KV = N_Q_HEADS // N_KV_HEADS
ATTN_WIDTH = N_Q_HEADS * HEAD_DIM
KV_WIDTH = N_KV_HEADS * HEAD_DIM
SSM_WIDTH = 1024
ATTN_BLOCK = 128
GRID_W = 64
ROPE_THETA = 10000.0
SSM_P = 16
SSM_G = SSM_WIDTH // SSM_P
SSM_N = 64
CHUNK = 64
GC = SSM_P * CHUNK
N_EXPERTS = 16
N_EXPERT_GROUPS = 4
EXPERTS_PER_GROUP = 4
N_MOD = 6
LN_EPS = 1e-5
NEG_INF = -1e30
LANES = 128
MOD_ROWS = 8
ROW_TILE = 256
MOE_TILE = 256
VMEM_LIMIT = 56 * 1024 * 1024

assert 2 * SSM_N == LANES and 2 * CHUNK == LANES


def _cparams(*sem):
    return pltpu.CompilerParams(dimension_semantics=sem, vmem_limit_bytes=VMEM_LIMIT)


def _dot(a, b):
    return jnp.dot(a, b, preferred_element_type=F32)


def _dot_nt(a, b):
    return lax.dot_general(a, b, (((1,), (1,)), ((), ())), preferred_element_type=F32)


def _mod_kernel(ct_ref, w_ref, b_ref, o_ref, ab_ref, *, n_rows, tn):
    d = ct_ref.shape[0]

    @pl.when((pl.program_id(0) == 0) & (pl.program_id(1) == 0))
    def _():
        ct = ct_ref[...]
        a = ct * jax.nn.sigmoid(ct)
        for r in range(n_rows):
            ab_ref[r] = jnp.broadcast_to(a[:, r:r + 1], (d, LANES))

    o_ref[...] = jnp.zeros(o_ref.shape, F32)
    for j in range(tn // LANES):
        w = w_ref[0, :, j * LANES:(j + 1) * LANES]
        bias = b_ref[0, :, j * LANES:(j + 1) * LANES]
        for r in range(n_rows):
            acc = jnp.sum(ab_ref[r] * w, axis=0, keepdims=True)
            o_ref[0, r:r + 1, j * LANES:(j + 1) * LANES] = acc + bias


def _modulation(c, c_ctx, w_mod, b_mod):
    depth, d, n_out = w_mod.shape
    n_rows = c.shape[0] + 1
    assert n_rows <= MOD_ROWS
    tn = 512
    ct = jnp.zeros((MOD_ROWS, d), F32).at[:c.shape[0]].set(c).at[c.shape[0]].set(c_ctx).T
    return pl.pallas_call(
        functools.partial(_mod_kernel, n_rows=n_rows, tn=tn),
        grid=(depth, n_out // tn),
        in_specs=[
            pl.BlockSpec((d, MOD_ROWS), lambda l, j: (0, 0)),
            pl.BlockSpec((1, d, tn), lambda l, j: (l, 0, j)),
            pl.BlockSpec((1, 1, tn), lambda l, j: (l, 0, j)),
        ],
        out_specs=pl.BlockSpec((1, MOD_ROWS, tn), lambda l, j: (l, 0, j)),
        out_shape=jax.ShapeDtypeStruct((depth, MOD_ROWS, n_out), F32),
        scratch_shapes=[pltpu.VMEM((n_rows, d, LANES), F32)],
        compiler_params=_cparams("arbitrary", "arbitrary"),
        name="modulation",
    )(ct, w_mod, b_mod.reshape(depth, 1, n_out))


def _rope(xh, cos, sin_signed):
    lane = lax.broadcasted_iota(jnp.int32, xh.shape, 1)
    swapped = jnp.where((lane % 64) < 32, pltpu.roll(xh, 96, 1), pltpu.roll(xh, 32, 1))
    return xh * cos + swapped * sin_signed


def _inproj_kernel(x_ref, mod_ref, w_ref, cos_ref, sin_ref, q_ref, k_ref, v_ref, *, rope):
    sh = mod_ref[0, 0:1, :]
    sc = mod_ref[0, 1:2, :]
    u = (x_ref[0] * (1.0 + sc) + sh).astype(BF16)
    q = _dot(u, w_ref[0, :, 0:ATTN_WIDTH])
    k = _dot(u, w_ref[0, :, ATTN_WIDTH:ATTN_WIDTH + KV_WIDTH])
    v = _dot(u, w_ref[0, :, ATTN_WIDTH + KV_WIDTH:ATTN_WIDTH + 2 * KV_WIDTH])
    scale = HEAD_DIM ** -0.5
    if rope:
        cos = cos_ref[...]
        sin = sin_ref[...]
    for h in range(N_Q_HEADS):
        qh = q[:, h * HEAD_DIM:(h + 1) * HEAD_DIM]
        if rope:
            qh = _rope(qh, cos, sin)
        q_ref[0, :, h * HEAD_DIM:(h + 1) * HEAD_DIM] = (qh * scale).astype(BF16)
    for h in range(N_KV_HEADS):
        kh = k[:, h * HEAD_DIM:(h + 1) * HEAD_DIM]
        if rope:
            kh = _rope(kh, cos, sin)
        k_ref[0, :, h * HEAD_DIM:(h + 1) * HEAD_DIM] = kh.astype(BF16)
    v_ref[0] = v.astype(BF16)


def _in_projection(x, mod3, row_of_batch, w_in_bf, layer, cos_t, sin_t, rope):
    bsz, n_tok, d = x.shape
    tm = min(ROW_TILE, n_tok)
    n_cols = ATTN_WIDTH + 2 * KV_WIDTH
    out = lambda w: jax.ShapeDtypeStruct((bsz, n_tok, w), BF16)
    ospec = lambda w: pl.BlockSpec((1, tm, w), lambda b, t: (b, t, 0))
    return pl.pallas_call(
        functools.partial(_inproj_kernel, rope=rope),
        grid=(bsz, n_tok // tm),
        in_specs=[
            pl.BlockSpec((1, tm, d), lambda b, t: (b, t, 0)),
            pl.BlockSpec((1, N_MOD, d), lambda b, t: (row_of_batch(b), 0, 0)),
            pl.BlockSpec((1, d, n_cols), lambda b, t: (layer, 0, 0)),
            pl.BlockSpec((tm, HEAD_DIM), lambda b, t: (t, 0)),
            pl.BlockSpec((tm, HEAD_DIM), lambda b, t: (t, 0)),
        ],
        out_specs=[ospec(ATTN_WIDTH), ospec(KV_WIDTH), ospec(KV_WIDTH)],
        out_shape=[out(ATTN_WIDTH), out(KV_WIDTH), out(KV_WIDTH)],
        compiler_params=_cparams("parallel", "parallel"),
        name="in_projection",
    )(x, mod3, w_in_bf, cos_t, sin_t)


S5_TILE = 1024
S5_TILE_ROWS = 2 * S5_TILE // LANES


def _to_group_layout(st):
    x4 = st.reshape(SSM_G, SSM_P, S5_TILE // LANES, LANES)
    lane = lax.broadcasted_iota(jnp.int32, x4.shape[:1] + x4.shape[2:], 2)
    roll64 = lambda a: pltpu.roll(a.reshape(-1, LANES), CHUNK, 1).reshape(a.shape)
    out = []
    for k in range(SSM_P // 2):
        a, b = x4[:, 2 * k], x4[:, 2 * k + 1]
        h0 = jnp.where(lane < CHUNK, a, roll64(b))
        h1 = jnp.where(lane < CHUNK, roll64(a), b)
        out.append(jnp.concatenate([h0, h1], axis=1))
    return out


def _from_group_layout(blocks):
    n8 = S5_TILE // LANES
    lane = lax.broadcasted_iota(jnp.int32, (SSM_G, n8, LANES), 2)
    roll64 = lambda a: pltpu.roll(a.reshape(-1, LANES), CHUNK, 1).reshape(a.shape)
    chans = []
    for blk in blocks:
        a0, b1 = blk[:, 0:n8], blk[:, n8:]
        chans.append(jnp.where(lane < CHUNK, a0, roll64(b1)))
        chans.append(jnp.where(lane < CHUNK, roll64(a0), b1))
    return jnp.stack(chans, axis=1).reshape(SSM_WIDTH, S5_TILE)


def _sproj_kernel(x_ref, mod_ref, w_ref, *rest, n_steps):
    u_ref = rest[-1]

    @pl.when(pl.program_id(0) < n_steps)
    def _():
        sh = mod_ref[0, 0:1, :]
        sc = mod_ref[0, 1:2, :]
        x = x_ref[...].reshape(S5_TILE, x_ref.shape[-1])
        u = (x * (1.0 + sc) + sh).astype(BF16)
        st = _dot_nt(w_ref[0], u)
        for k, blk in enumerate(_to_group_layout(st)):
            u_ref[:, :, k * LANES:(k + 1) * LANES] = blk.astype(BF16)

    @pl.when(pl.program_id(0) >= n_steps)
    def _():
        u_ref[...] = jnp.zeros(u_ref.shape, BF16)


def _s_projection(x, mod3, mod_row, ws_t_bf, layer, n_rows, row_blk_off, u_prev=None):
    bsz, n_tok, d = x.shape
    if u_prev is None:
        assert n_tok % S5_TILE == 0
        nj = n_tok // S5_TILE
        n_steps = bsz * nj
        n_fill = n_rows // S5_TILE_ROWS - n_steps
        tile = lambda i: jnp.minimum(i, n_steps - 1)
        xspec = pl.BlockSpec((1, S5_TILE, d), lambda i: (tile(i) // nj, tile(i) % nj, 0))
        mspec = pl.BlockSpec((1, N_MOD, d), lambda i: (mod_row(tile(i) // nj), 0, 0))
    else:
        assert bsz * n_tok == S5_TILE and n_tok % LANES == 0
        n_steps, n_fill = 1, 0
        xspec = pl.BlockSpec((bsz, n_tok, d), lambda i: (0, 0, 0))
        mspec = pl.BlockSpec((1, N_MOD, d), lambda i: (mod_row(0), 0, 0))
    in_specs = [xspec, mspec, pl.BlockSpec((1, SSM_WIDTH, d), lambda i: (layer, 0, 0))]
    args = [x, mod3, ws_t_bf]
    aliases = {}
    if u_prev is not None:
        in_specs.append(pl.BlockSpec(memory_space=pl.ANY))
        args.append(u_prev)
        aliases = {3: 0}
    return pl.pallas_call(
        functools.partial(_sproj_kernel, n_steps=n_steps),
        grid=(n_steps + n_fill,),
        in_specs=in_specs,
        out_specs=pl.BlockSpec((SSM_G, S5_TILE_ROWS, GC), lambda i: (0, row_blk_off + i, 0)),
        out_shape=jax.ShapeDtypeStruct((SSM_G, n_rows, GC), BF16),
        input_output_aliases=aliases,
        compiler_params=_cparams("arbitrary"),
        name="s_projection",
    )(*args)


def _rope_tables(n_tok):
    half = HEAD_DIM // 2
    inv_freq = ROPE_THETA ** (-jnp.arange(0, half, 2, dtype=F32) / half)
    t = jnp.arange(n_tok)
    row = (t // GRID_W).astype(F32)
    col = (t % GRID_W).astype(F32)
    ang_r = row[:, None] * inv_freq[None, :]
    ang_c = col[:, None] * inv_freq[None, :]
    cos_t = jnp.concatenate([jnp.cos(ang_r), jnp.cos(ang_r), jnp.cos(ang_c), jnp.cos(ang_c)], axis=-1)
    sin_t = jnp.concatenate([-jnp.sin(ang_r), jnp.sin(ang_r), -jnp.sin(ang_c), jnp.sin(ang_c)], axis=-1)
    return cos_t, sin_t


def _attn_kernel(sink_ref, q_ref, *refs, has_band, n_blk):
    if has_band:
        kp_ref, ko_ref, kn_ref, vp_ref, vo_ref, vn_ref, kc_ref, vc_ref, g_ref, o_ref = refs
    else:
        kc_ref, vc_ref, g_ref, o_ref = refs
    blk = pl.program_id(1)
    rows = Q_PER_KV * ATTN_BLOCK
    qi = lax.broadcasted_iota(jnp.int32, (rows, ATTN_BLOCK), 0) % ATTN_BLOCK
    kj = lax.broadcasted_iota(jnp.int32, (rows, ATTN_BLOCK), 1)
    row_head = lax.broadcasted_iota(jnp.int32, (rows, 1), 0) // ATTN_BLOCK
    heads = []
    for h in range(N_KV_HEADS):
        hs = slice(h * HEAD_DIM, (h + 1) * HEAD_DIM)
        q = jnp.concatenate(
            [q_ref[0, :, (h * Q_PER_KV + g) * HEAD_DIM:(h * Q_PER_KV + g + 1) * HEAD_DIM] for g in range(Q_PER_KV)],
            axis=0)
        sink = jnp.zeros((rows, 1), F32)
        for g in range(Q_PER_KV):
            sink = jnp.where(row_head == g, sink_ref[h * Q_PER_KV + g], sink)
        s_c = _dot_nt(q, kc_ref[0, :, hs])
        m = jnp.maximum(jnp.max(s_c, axis=-1, keepdims=True), sink)
        if has_band:
            s_p = jnp.where((kj >= qi) & (blk > 0), _dot_nt(q, kp_ref[0, :, hs]), NEG_INF)
            s_o = _dot_nt(q, ko_ref[0, :, hs])
            s_n = jnp.where((kj <= qi) & (blk < n_blk - 1), _dot_nt(q, kn_ref[0, :, hs]), NEG_INF)
            m = jnp.maximum(m, jnp.max(jnp.maximum(jnp.maximum(s_p, s_o), s_n), axis=-1, keepdims=True))
        p_c = jnp.exp(s_c - m)
        denom = jnp.sum(p_c, axis=-1, keepdims=True) + jnp.exp(sink - m)
        acc = _dot(p_c.astype(BF16), vc_ref[0, :, hs])
        if has_band:
            for s_x, v_ref in ((s_p, vp_ref), (s_o, vo_ref), (s_n, vn_ref)):
                p_x = jnp.exp(s_x - m)
                denom = denom + jnp.sum(p_x, axis=-1, keepdims=True)
                acc = acc + _dot(p_x.astype(BF16), v_ref[0, :, hs])
        o = acc / denom
        for g in range(Q_PER_KV):
            heads.append(o[g * ATTN_BLOCK:(g + 1) * ATTN_BLOCK, :])
    ss = heads[0] * heads[0]
    for o in heads[1:]:
        ss = ss + o * o
    inv = lax.rsqrt(jnp.sum(ss, axis=-1, keepdims=True) / ATTN_WIDTH + LN_EPS)
    for i, o in enumerate(heads):
        cs = slice(i * HEAD_DIM, (i + 1) * HEAD_DIM)
        o_ref[0, :, cs] = (o * inv * g_ref[:, cs]).astype(BF16)


def _attention(q, k, v, kc, vc, sink, g_attn, has_band):
    bsz, n_tok, _ = q.shape
    n_ctx = kc.shape[1]
    n_blk = n_tok // ATTN_BLOCK
    qspec = pl.BlockSpec((1, ATTN_BLOCK, ATTN_WIDTH), lambda b, n: (b, n, 0))
    kvspec = lambda f: pl.BlockSpec((1, ATTN_BLOCK, KV_WIDTH), lambda b, n: (b, f(n), 0))
    cspec = pl.BlockSpec((1, n_ctx, KV_WIDTH), lambda b, n: (b, 0, 0))
    prev = lambda n: jnp.maximum(n - 1, 0)
    own = lambda n: n
    nxt = lambda n: jnp.minimum(n + 1, n_blk - 1)
    in_specs = [pl.BlockSpec(memory_space=pltpu.SMEM), qspec]
    args = [sink, q]
    if has_band:
        in_specs += [kvspec(prev), kvspec(own), kvspec(nxt), kvspec(prev), kvspec(own), kvspec(nxt)]
        args += [k, k, k, v, v, v]
    in_specs += [cspec, cspec, pl.BlockSpec((1, ATTN_WIDTH), lambda b, n: (0, 0))]
    args += [kc, vc, g_attn.reshape(1, ATTN_WIDTH)]
    return pl.pallas_call(
        functools.partial(_attn_kernel, has_band=has_band, n_blk=n_blk),
        grid=(bsz, n_blk),
        in_specs=in_specs,
        out_specs=qspec,
        out_shape=jax.ShapeDtypeStruct((bsz, n_tok, ATTN_WIDTH), BF16),
        compiler_params=_cparams("parallel", "parallel"),
        name="attention",
    )(*args)


def _cmul(a, b):
    return a[0] * b[0] - a[1] * b[1], a[0] * b[1] + a[1] * b[0]


def _cpow(rho, theta, expo):
    mag = jnp.exp(expo * rho)
    ang = expo * theta
    return mag * jnp.cos(ang), mag * jnp.sin(ang)


def _s5_param_kernel(after_ref, prow_ref, bt_ref, cr_ref, dd_ref, m_ref, w_ref, vt_ref, al_ref, kmat_ref):
    del after_ref
    p_dim, lc = SSM_P, CHUNK
    lam_re, lam_im = prow_ref[0, 0:1, :], prow_ref[0, 1:2, :]
    dt = jnp.exp(prow_ref[0, 2:3, :])
    rho, theta = lam_re * dt, lam_im * dt

    sub8 = lax.broadcasted_iota(jnp.int32, (8, LANES), 0)
    asc = _cpow(rho, theta, sub8.astype(F32))
    desc = _cpow(rho, theta, (8 - sub8).astype(F32))
    ex = jnp.where(sub8 == 0, 8, jnp.where(sub8 == 1, 16, jnp.where(sub8 == 2, 32, jnp.where(sub8 == 3, lc, 1))))
    pw = _cpow(rho, theta, ex.astype(F32))
    row = lambda t, i: (t[0][i:i + 1, :], t[1][i:i + 1, :])
    cat = lambda a, b: (jnp.concatenate([a[0], b[0]], axis=0), jnp.concatenate([a[1], b[1]], axis=0))
    for i in range(3):
        step = row(pw, i)
        asc = cat(asc, _cmul(asc, step))
        desc = cat(_cmul(desc, step), desc)
    a_chunk, a_one = row(pw, 3), row(pw, 4)

    den = lam_re * lam_re + lam_im * lam_im
    x_re, x_im = a_one[0] - 1.0, a_one[1]
    beta = ((x_re * lam_re + x_im * lam_im) / den, (x_im * lam_re - x_re * lam_im) / den)
    bb = _cmul(beta, (bt_ref[0, 0], bt_ref[0, 1]))
    cc = (cr_ref[0, 0], cr_ref[0, 1])

    is_fwd = lax.broadcasted_iota(jnp.int32, (lc, LANES), 1) < SSM_N
    sub = lax.broadcasted_iota(jnp.int32, (lc, LANES), 0)
    pw_w = (jnp.where(is_fwd, desc[0], asc[0]), jnp.where(is_fwd, desc[1], asc[1]))
    pw_v = (jnp.where(is_fwd, asc[0], desc[0]), jnp.where(is_fwd, asc[1], desc[1]))

    top = (jnp.where(is_fwd, asc[0], jnp.where(sub == 0, 1.0, 0.0)), jnp.where(is_fwd, asc[1], 0.0))
    keep = (~is_fwd) & (sub > 0)
    bot = (jnp.where(keep, desc[0], 0.0), jnp.where(keep, desc[1], 0.0))
    lag = cat(top, bot)

    cx = (jnp.concatenate([cc[0]] * p_dim, axis=0), jnp.concatenate([cc[1]] * p_dim, axis=0))
    bx = tuple(jnp.concatenate([jnp.broadcast_to(b[q:q + 1, :], (p_dim, LANES)) for q in range(p_dim)], axis=0)
               for b in bb)
    e_re, e_im = _cmul(cx, bx)
    hi = lax.Precision.HIGHEST
    kmat = (jnp.dot(e_re, lag[0].T, preferred_element_type=F32, precision=hi)
            - jnp.dot(e_im, lag[1].T, preferred_element_type=F32, precision=hi))
    lane0 = lax.broadcasted_iota(jnp.int32, kmat.shape, 1) == 0
    kmat_ref[...] = kmat + jnp.where(lane0, dd_ref[0], 0.0)

    lane_m = lax.broadcasted_iota(jnp.int32, (lc, LANES), 1)

    def q_body(q, carry):
        for pp in range(p_dim // 2):
            r0 = q * p_dim + 2 * pp
            ka = jnp.broadcast_to(kmat_ref[pl.ds(r0, 1), :], (lc, LANES))
            kb = jnp.broadcast_to(kmat_ref[pl.ds(r0 + 1, 1), :], (lc, LANES))
            ra = pltpu.roll(ka, 0, 1, stride=1, stride_axis=0)
            rb = pltpu.roll(kb, lc, 1, stride=1, stride_axis=0)
            blk = jnp.where(lane_m < lc, ra, rb)
            m_ref[0, pl.ds(pl.multiple_of(q * lc, lc), lc), pp * LANES:(pp + 1) * LANES] = blk.astype(BF16)
        return carry

    lax.fori_loop(0, p_dim, q_body, 0)

    for q in range(p_dim):
        w_re, w_im = _cmul(pw_w, row(bb, q))
        w_ref[0, q * lc:(q + 1) * lc, :] = jnp.concatenate([w_re, w_im], axis=1).astype(BF16)
    for p in range(p_dim):
        v_re, v_im = _cmul(pw_v, row(cc, p))
        vt_ref[0, p * lc:(p + 1) * lc, :] = jnp.concatenate([v_re, -v_im], axis=1).astype(BF16)
    al_ref[0] = jnp.concatenate([a_chunk[0], a_chunk[1]], axis=1)


def _s5_operators(lam_re, lam_im, log_dt, b_re, b_im, c_re, c_im, d_skip, layer, after):
    depth = lam_re.shape[0]
    dg = depth * SSM_G
    g0 = layer * SSM_G
    vec = lambda a: jnp.moveaxis(a, 1, 2).reshape(dg, LANES)
    mat = lambda a: jnp.moveaxis(a, 1, 3).reshape(dg, SSM_P, LANES)
    ldt = jnp.broadcast_to(log_dt[..., None], lam_re.shape)
    prow = jnp.stack([vec(lam_re), vec(lam_im), vec(ldt)], axis=1)
    bt = jnp.stack([mat(jnp.swapaxes(b_re, 3, 4)), mat(jnp.swapaxes(b_im, 3, 4))], axis=1)
    cr = jnp.stack([mat(c_re), mat(c_im)], axis=1)
    dd = (d_skip.reshape(dg, 1, SSM_P) * jnp.eye(SSM_P, dtype=F32)[None]).reshape(dg, SSM_P * SSM_P, 1)
    iblk = lambda *s: pl.BlockSpec((1,) + s, lambda i: (g0 + i,) + (0,) * len(s))
    oblk = lambda *s: pl.BlockSpec((1,) + s, lambda i: (i,) + (0,) * len(s))
    return pl.pallas_call(
        _s5_param_kernel,
        grid=(SSM_G,),
        in_specs=[pl.BlockSpec(memory_space=pltpu.SMEM),
                  iblk(3, LANES), iblk(2, SSM_P, LANES), iblk(2, SSM_P, LANES), iblk(SSM_P * SSM_P, 1)],
        out_specs=[oblk(GC, GC), oblk(GC, 2 * LANES), oblk(GC, 2 * LANES), oblk(1, 2 * LANES)],
        out_shape=[jax.ShapeDtypeStruct((SSM_G, GC, GC), BF16),
                   jax.ShapeDtypeStruct((SSM_G, GC, 2 * LANES), BF16),
                   jax.ShapeDtypeStruct((SSM_G, GC, 2 * LANES), BF16),
                   jax.ShapeDtypeStruct((SSM_G, 1, 2 * LANES), F32)],
        scratch_shapes=[pltpu.VMEM((SSM_P * SSM_P, LANES), F32)],
        compiler_params=_cparams("parallel"),
        name="s5_operators",
    )(after, prow, bt, cr, dd)


def _s5_sum_kernel(u_ref, w_ref, s_ref):
    s_ref[0] = _dot(u_ref[0], w_ref[0])


def _s5_scan_kernel(s_ref, al_ref, h_ref, *, n_ctx_chunks, n_chunks):
    a_re, a_im = al_ref[:, :LANES], al_ref[:, LANES:]
    is_fwd = lax.broadcasted_iota(jnp.int32, a_re.shape, 1) < SSM_N

    def body(i, carry):
        h_re, h_im = carry
        cf = i
        cr = jnp.where(i < n_ctx_chunks, n_ctx_chunks - 1 - i, n_chunks - 1 - (i - n_ctx_chunks))
        s_f, s_r = s_ref[cf], s_ref[cr]
        h_ref[cf, :, 0:SSM_N] = h_re[:, 0:SSM_N]
        h_ref[cr, :, SSM_N:LANES] = h_re[:, SSM_N:LANES]
        h_ref[cf, :, LANES:LANES + SSM_N] = h_im[:, 0:SSM_N]
        h_ref[cr, :, LANES + SSM_N:] = h_im[:, SSM_N:LANES]
        s_re = jnp.where(is_fwd, s_f[:, :LANES], s_r[:, :LANES])
        s_im = jnp.where(is_fwd, s_f[:, LANES:], s_r[:, LANES:])
        return a_re * h_re - a_im * h_im + s_re, a_re * h_im + a_im * h_re + s_im

    zero = jnp.zeros(a_re.shape, F32)
    lax.fori_loop(0, n_chunks, body, (zero, zero))


def _s5_out_kernel(u_ref, m_ref, h_ref, vt_ref, y_ref):
    y = _dot(u_ref[0], m_ref[0]) + _dot_nt(h_ref[0].astype(BF16), vt_ref[0])
    y_ref[0] = y.astype(BF16)


def _rows_to_chunks(a, bsz, n_tok, n_ctx):
    n8, nj, ncb, st = S5_TILE // LANES, n_tok // S5_TILE, n_ctx // LANES, a.shape[-1]
    n_lat = bsz * nj * S5_TILE_ROWS
    lat = a[:, :n_lat].reshape(SSM_G, bsz, nj, 2, n8, st).transpose(2, 4, 3, 0, 1, 5).reshape(-1, SSM_G * bsz, st)
    ctx = a[:, n_lat:].reshape(SSM_G, 2, bsz, ncb, st).transpose(3, 1, 0, 2, 4).reshape(-1, SSM_G * bsz, st)
    return jnp.concatenate([ctx, lat], axis=0)


def _chunks_to_rows(a, bsz, n_tok, n_ctx):
    n8, nj, ncb, st = S5_TILE // LANES, n_tok // S5_TILE, n_ctx // LANES, a.shape[-1]
    ncc = n_ctx // CHUNK
    lat = a[ncc:].reshape(nj, n8, 2, SSM_G, bsz, st).transpose(3, 4, 0, 2, 1, 5).reshape(SSM_G, -1, st)
    ctx = a[:ncc].reshape(ncb, 2, SSM_G, bsz, st).transpose(2, 1, 3, 0, 4).reshape(SSM_G, -1, st)
    return jnp.concatenate([lat, ctx], axis=1)


def _s5_apply(u, bsz, n_tok, n_ctx, ops):
    m_op, w_op, vt_op, al = ops
    ncc, n_chunks = n_ctx // CHUNK, (n_ctx + n_tok) // CHUNK
    rows = u.shape[1]
    assert rows == bsz * n_chunks
    gspec = lambda r, c: pl.BlockSpec((1, r, c), lambda g: (g, 0, 0))
    ospec = gspec
    sums = pl.pallas_call(
        _s5_sum_kernel,
        grid=(SSM_G,),
        in_specs=[gspec(rows, GC), ospec(GC, 2 * LANES)],
        out_specs=gspec(rows, 2 * LANES),
        out_shape=jax.ShapeDtypeStruct((SSM_G, rows, 2 * LANES), F32),
        compiler_params=_cparams("parallel"),
        name="s5_chunk_sums",
    )(u, w_op)
    sums_t = _rows_to_chunks(sums, bsz, n_tok, n_ctx)
    al_rows = jnp.broadcast_to(al, (SSM_G, bsz, 2 * LANES)).reshape(SSM_G * bsz, 2 * LANES)
    rt = 64
    states_t = pl.pallas_call(
        functools.partial(_s5_scan_kernel, n_ctx_chunks=ncc, n_chunks=n_chunks),
        grid=(SSM_G * bsz // rt,),
        in_specs=[pl.BlockSpec((n_chunks, rt, 2 * LANES), lambda r: (0, r, 0)),
                  pl.BlockSpec((rt, 2 * LANES), lambda r: (r, 0))],
        out_specs=pl.BlockSpec((n_chunks, rt, 2 * LANES), lambda r: (0, r, 0)),
        out_shape=jax.ShapeDtypeStruct((n_chunks, SSM_G * bsz, 2 * LANES), F32),
        compiler_params=_cparams("parallel"),
        name="s5_state_scan",
    )(sums_t, al_rows)
    states = _chunks_to_rows(states_t, bsz, n_tok, n_ctx)
    return pl.pallas_call(
        _s5_out_kernel,
        grid=(SSM_G,),
        in_specs=[gspec(rows, GC), ospec(GC, GC), gspec(rows, 2 * LANES), ospec(GC, 2 * LANES)],
        out_specs=gspec(rows, GC),
        out_shape=jax.ShapeDtypeStruct((SSM_G, rows, GC), BF16),
        compiler_params=_cparams("parallel"),
        name="s5_chunk_outputs",
    )(u, m_op, states, vt_op)


def _glu_kernel(y_ref, w_ref, b_ref, g_ref, o_ref):
    blocks = [y_ref[:, :, k * LANES:(k + 1) * LANES].astype(F32) for k in range(SSM_P // 2)]
    y = _from_group_layout(blocks).T
    z = 0.5 * y * (1.0 + jnp.tanh(math.sqrt(2.0 / math.pi) * (y + 0.044715 * (y * y * y))))
    t = _dot(z.astype(BF16), w_ref[0]) + b_ref[...]
    o = z * jax.nn.sigmoid(t)
    inv = lax.rsqrt(jnp.mean(o * o, axis=-1, keepdims=True) + LN_EPS)
    o_ref[...] = (o * inv * g_ref[...]).astype(BF16).reshape(o_ref.shape)


def _glu(y, bsz, n_tok, row_blk_off, per_batch, w_glu_bf, layer, b_glu, g_ssm):
    w = SSM_WIDTH
    if per_batch:
        grid = (bsz, n_tok // S5_TILE)
        ospec = pl.BlockSpec((1, S5_TILE, w), lambda b, j: (b, j, 0))
    else:
        assert bsz * n_tok == S5_TILE
        grid = (1, 1)
        ospec = pl.BlockSpec((bsz, n_tok, w), lambda b, j: (0, 0, 0))
    nj = grid[1]
    row = lambda a: a.reshape(1, w)
    vspec = pl.BlockSpec((1, w), lambda b, j: (0, 0))
    return pl.pallas_call(
        _glu_kernel,
        grid=grid,
        in_specs=[pl.BlockSpec((SSM_G, S5_TILE_ROWS, GC), lambda b, j: (0, row_blk_off + b * nj + j, 0)),
                  pl.BlockSpec((1, w, w), lambda b, j: (layer, 0, 0)), vspec, vspec],
        out_specs=ospec,
        out_shape=jax.ShapeDtypeStruct((bsz, n_tok, w), BF16),
        compiler_params=_cparams("parallel", "parallel"),
        name="s5_glu",
    )(y, w_glu_bf, row(b_glu), row(g_ssm))


def _layer_norm(z, g, b):
    mu = jnp.mean(z, axis=-1, keepdims=True)
    zc = z - mu
    var = jnp.mean(zc * zc, axis=-1, keepdims=True)
    return zc * lax.rsqrt(var + LN_EPS) * g + b


def _first_argmax(vals):
    best_i = jnp.zeros(vals[0].shape, jnp.int32)
    best_v = vals[0]
    for j in range(1, len(vals)):
        better = vals[j] > best_v
        best_i = jnp.where(better, j, best_i)
        best_v = jnp.where(better, vals[j], best_v)
    return best_i, best_v


def _route(logit_rows):
    m = functools.reduce(jnp.maximum, logit_rows)
    p = [jnp.exp(l - m) for l in logit_rows]
    scores = []
    for g in range(N_EXPERT_GROUPS):
        a, b, c, d = p[4 * g:4 * g + 4]
        hi1, lo1, hi2, lo2 = jnp.maximum(a, b), jnp.minimum(a, b), jnp.maximum(c, d), jnp.minimum(c, d)
        scores.append(jnp.maximum(hi1, hi2) + jnp.maximum(jnp.minimum(hi1, hi2), jnp.maximum(lo1, lo2)))
    grp, _ = _first_argmax(scores)
    sel = []
    for j in range(EXPERTS_PER_GROUP):
        v = p[j]
        for g in range(1, N_EXPERT_GROUPS):
            v = jnp.where(grp == g, p[4 * g + j], v)
        sel.append(v)
    i1, v1 = _first_argmax(sel)
    i2, v2 = _first_argmax([jnp.where(i1 == j, -1.0, sel[j]) for j in range(EXPERTS_PER_GROUP)])
    tot = v1 + v2
    return grp * EXPERTS_PER_GROUP + i1, grp * EXPERTS_PER_GROUP + i2, v1 / tot, v2 / tot


def _outproj_kernel(a_ref, s_ref, x_ref, mod_ref, w_ref, lng_ref, lnb_ref, wr_ref, br_ref, *rest, alpha, n_steps):
    x1_ref, u2_ref, ri_ref, rw_ref = rest[-4:]

    @pl.when(pl.program_id(0) < n_steps)
    def _():
        y = _dot(a_ref[0], w_ref[0, 0:ATTN_WIDTH, :]) + _dot(s_ref[0], w_ref[0, ATTN_WIDTH:, :])
        g1 = mod_ref[0, 2:3, :]
        sh2 = mod_ref[0, 3:4, :]
        sc2 = mod_ref[0, 4:5, :]
        x1 = _layer_norm(alpha * x_ref[0] + g1 * y, lng_ref[...], lnb_ref[...])
        x1_ref[0] = x1
        u2 = (x1 * (1.0 + sc2) + sh2).astype(BF16)
        u2_ref[...] = u2
        logits = _dot_nt(wr_ref[...], u2) + br_ref[...]
        e1, e2, w1, w2 = _route([logits[e:e + 1, :] for e in range(N_EXPERTS)])
        ri_ref[0, 0:1, :] = e1
        ri_ref[0, 1:2, :] = e2
        rw_ref[0, 0:1, :] = w1
        rw_ref[0, 1:2, :] = w2

    @pl.when(pl.program_id(0) >= n_steps)
    def _():
        u2_ref[...] = jnp.zeros(u2_ref.shape, BF16)


def _out_projection(attn_n, ssm_n, x, mod3, row_of_batch, w_out_bf, layer, ln_g, ln_b, w_router_t_bf, b_router,
                    alpha, u2_rows, row_off, u2_prev=None):
    bsz, n_tok, d = x.shape
    tm = min(ROW_TILE, n_tok)
    nt = n_tok // tm
    n_steps = bsz * nt
    assert row_off % tm == 0 and u2_rows % tm == 0
    n_fill = 0 if u2_prev is not None else u2_rows // tm - n_steps
    bt = lambda i: (jnp.minimum(i, n_steps - 1) // nt, jnp.minimum(i, n_steps - 1) % nt)
    tspec = lambda w: pl.BlockSpec((1, tm, w), lambda i: bt(i) + (0,))
    vspec = pl.BlockSpec((1, d), lambda i: (0, 0))
    rspec = pl.BlockSpec((1, 2, tm), lambda i: (bt(i)[0], 0, bt(i)[1]))
    in_specs = [
        tspec(ATTN_WIDTH), tspec(SSM_WIDTH), tspec(d),
        pl.BlockSpec((1, N_MOD, d), lambda i: (row_of_batch(bt(i)[0]), 0, 0)),
        pl.BlockSpec((1,) + w_out_bf.shape[1:], lambda i: (layer, 0, 0)),
        vspec, vspec,
        pl.BlockSpec((N_EXPERTS, d), lambda i: (0, 0)),
        pl.BlockSpec((N_EXPERTS, 1), lambda i: (0, 0)),
    ]
    args = [attn_n, ssm_n, x, mod3, w_out_bf, ln_g.reshape(1, d), ln_b.reshape(1, d), w_router_t_bf,
            b_router.reshape(N_EXPERTS, 1)]
    aliases = {}
    if u2_prev is not None:
        in_specs.append(pl.BlockSpec(memory_space=pl.ANY))
        args.append(u2_prev)
        aliases = {len(args) - 1: 1}
    return pl.pallas_call(
        functools.partial(_outproj_kernel, alpha=alpha, n_steps=n_steps),
        grid=(n_steps + n_fill,),
        in_specs=in_specs,
        out_specs=[tspec(d), pl.BlockSpec((tm, d), lambda i: (row_off // tm + i, 0)), rspec, rspec],
        out_shape=[jax.ShapeDtypeStruct((bsz, n_tok, d), F32), jax.ShapeDtypeStruct((u2_rows, d), BF16),
                   jax.ShapeDtypeStruct((bsz, 2, n_tok), jnp.int32), jax.ShapeDtypeStruct((bsz, 2, n_tok), F32)],
        input_output_aliases=aliases,
        compiler_params=_cparams("arbitrary"),
        name="out_projection",
    )(*args)


CAST_ROWS = 256


def _ffn_kernel(te_ref, nx_ref, nu_ref, x_ref, wg_hbm, wu_hbm, wd_hbm, y_ref,
                stage_g, stage_u, stage_d, wg_bf, wu_bf, wd_bf, sem, *, e0):
    t = pl.program_id(0)
    e = te_ref[t]
    pairs = ((wg_hbm, stage_g, wg_bf), (wu_hbm, stage_u, wu_bf), (wd_hbm, stage_d, wd_bf))

    def fetch(expert):
        return [pltpu.make_async_copy(hbm.at[e0 + expert], stage, sem.at[i])
                for i, (hbm, stage, _) in enumerate(pairs)]

    @pl.when(t == 0)
    def _():
        for cp in fetch(e):
            cp.start()

    first_of_run = (t == 0) | (te_ref[jnp.maximum(t - 1, 0)] != e)

    @pl.when(first_of_run & (t < nu_ref[0]))
    def _():
        for cp in fetch(e):
            cp.wait()
        for _, stage, dst in pairs:
            def cast_rows(i, carry, stage=stage, dst=dst):
                rows = pl.ds(pl.multiple_of(i * CAST_ROWS, CAST_ROWS), CAST_ROWS)
                dst[rows, :] = stage[rows, :].astype(BF16)
                return carry
            lax.fori_loop(0, stage.shape[0] // CAST_ROWS, cast_rows, 0)

        @pl.when(nx_ref[t] >= 0)
        def _():
            for cp in fetch(nx_ref[t]):
                cp.start()

    @pl.when(t < nu_ref[0])
    def _():
        x = x_ref[...]
        g = _dot(x, wg_bf[...])
        u = _dot(x, wu_bf[...])
        h = (g * jax.nn.sigmoid(g) * u).astype(BF16)
        y_ref[...] = _dot(h, wd_bf[...]).astype(BF16)

    @pl.when(t >= nu_ref[0])
    def _():
        y_ref[...] = jnp.zeros(y_ref.shape, BF16)


def _expert_ffn(xs, tile_expert, next_expert, n_used, w_gate, w_up, w_down, layer):
    n_rows, d = xs.shape
    n_e, f = w_gate.shape[1], w_gate.shape[3]
    tm = MOE_TILE
    flat = lambda w: w.reshape((-1,) + w.shape[2:])
    grid_spec = pltpu.PrefetchScalarGridSpec(
        num_scalar_prefetch=3,
        grid=(n_rows // tm,),
        in_specs=[
            pl.BlockSpec((tm, d), lambda t, te, nx, nu: (t, 0)),
            pl.BlockSpec(memory_space=pl.ANY),
            pl.BlockSpec(memory_space=pl.ANY),
            pl.BlockSpec(memory_space=pl.ANY),
        ],
        out_specs=pl.BlockSpec((tm, d), lambda t, te, nx, nu: (t, 0)),
        scratch_shapes=[pltpu.VMEM((d, f), F32), pltpu.VMEM((d, f), F32), pltpu.VMEM((f, d), F32),
                        pltpu.VMEM((d, f), BF16), pltpu.VMEM((d, f), BF16), pltpu.VMEM((f, d), BF16),
                        pltpu.SemaphoreType.DMA((3,))],
    )
    return pl.pallas_call(
        functools.partial(_ffn_kernel, e0=layer * n_e),
        grid_spec=grid_spec,
        out_shape=jax.ShapeDtypeStruct((n_rows, d), BF16),
        compiler_params=_cparams("arbitrary"),
        name="expert_ffn",
    )(tile_expert, next_expert, n_used, xs, flat(w_gate), flat(w_up), flat(w_down))


def _dispatch_plan(e1, e2, tm):
    n = e1.shape[0]
    n_tiles = (2 * n + N_EXPERTS * (tm - 1) + tm - 1) // tm
    e = jnp.concatenate([e1, e2])
    onehot = (e[:, None] == jnp.arange(N_EXPERTS, dtype=jnp.int32)[None, :]).astype(jnp.int32)
    csum = jnp.cumsum(onehot, axis=0)
    pos_in_e = jnp.sum((csum - 1) * onehot, axis=1)
    counts = csum[-1]
    padded = ((counts + tm - 1) // tm) * tm
    ends = jnp.cumsum(padded)
    offs = ends - padded
    dest = offs[e] + pos_in_e
    tok = jnp.concatenate([jnp.arange(n, dtype=jnp.int32)] * 2)
    row_src = jnp.zeros((n_tiles * tm,), jnp.int32).at[dest].set(tok, mode="promise_in_bounds", unique_indices=True)
    n_used = (ends[-1] // tm).astype(jnp.int32)
    tile_start = jnp.arange(n_tiles, dtype=jnp.int32) * tm
    tile_e = jnp.sum((tile_start[:, None] >= ends[None, :]).astype(jnp.int32), axis=1)
    tile_e = jnp.minimum(tile_e, N_EXPERTS - 1)
    last_e = tile_e[jnp.maximum(n_used - 1, 0)]
    tile_e = jnp.where(jnp.arange(n_tiles) < n_used, tile_e, last_e).astype(jnp.int32)
    ids = jnp.arange(N_EXPERTS, dtype=jnp.int32)
    later = (ids[None, :] > ids[:, None]) & (counts[None, :] > 0)
    next_of = jnp.min(jnp.where(later, ids[None, :], N_EXPERTS), axis=1)
    next_of = jnp.where(next_of == N_EXPERTS, -1, next_of).astype(jnp.int32)
    return row_src, dest[:n], dest[n:], tile_e, next_of[tile_e], n_used.reshape(1)


def _take_rows(a, idx):
    return a.at[idx].get(mode="promise_in_bounds")


def _final_kernel(x_ref, y1_ref, y2_ref, w1_ref, w2_ref, mod_ref, lng_ref, lnb_ref, o_ref, *, alpha):
    f = w1_ref[...] * y1_ref[...].astype(F32) + w2_ref[...] * y2_ref[...].astype(F32)
    g2 = mod_ref[0, 5:6, :]
    o_ref[0] = _layer_norm(alpha * x_ref[0] + g2 * f, lng_ref[...], lnb_ref[...])


def _combine_ln(x1, y1, y2, w1, w2, row_off, mod3, row_of_batch, ln_g, ln_b, alpha):
    bsz, n_tok, d = x1.shape
    tm = min(ROW_TILE, n_tok)
    nt = n_tok // tm
    assert row_off % tm == 0
    tspec = pl.BlockSpec((1, tm, d), lambda b, t: (b, t, 0))
    yspec = pl.BlockSpec((tm, d), lambda b, t: (row_off // tm + b * nt + t, 0))
    wspec = pl.BlockSpec((tm, 1), lambda b, t: (row_off // tm + b * nt + t, 0))
    vspec = pl.BlockSpec((1, d), lambda b, t: (0, 0))
    return pl.pallas_call(
        functools.partial(_final_kernel, alpha=alpha),
        grid=(bsz, nt),
        in_specs=[tspec, yspec, yspec, wspec, wspec,
                  pl.BlockSpec((1, N_MOD, d), lambda b, t: (row_of_batch(b), 0, 0)), vspec, vspec],
        out_specs=tspec,
        out_shape=jax.ShapeDtypeStruct((bsz, n_tok, d), F32),
        compiler_params=_cparams("parallel", "parallel"),
        name="combine_post_ln",
    )(x1, y1, y2, w1, w2, mod3, ln_g.reshape(1, d), ln_b.reshape(1, d))


def kernel(x, c, ctx, c_ctx, w_mod, b_mod, w_in, attn_sink, ssm_lambda_re, ssm_lambda_im, ssm_log_dt, ssm_b_re, ssm_b_im, ssm_c_re, ssm_c_im, ssm_d, w_glu, b_glu, g_attn_out, g_ssm_out, w_out, ln1_g, ln1_b, w_router, b_router, w_expert_gate, w_expert_up, w_expert_down, ln2_g, ln2_b):
    depth = w_mod.shape[0]
    bsz, n_tok, d = x.shape
    n_ctx = ctx.shape[1]
    alpha = (2 * depth) ** 0.25
    assert n_tok % ROW_TILE == 0 and n_tok % ATTN_BLOCK == 0 and n_ctx % ATTN_BLOCK == 0 and n_ctx % CHUNK == 0

    mod = _modulation(c, c_ctx, w_mod, b_mod)
    s5_params = (ssm_lambda_re, ssm_lambda_im, ssm_log_dt, ssm_b_re, ssm_b_im, ssm_c_re, ssm_c_im, ssm_d)
    expert_w = (w_expert_gate, w_expert_up, w_expert_down)
    s5_ops = _s5_operators(*s5_params, 0, jnp.zeros((1,), jnp.int32))
    cos_t, sin_t = _rope_tables(n_tok)
    w_router_t = w_router.T.astype(BF16)
    w_in_bf, w_glu_bf, w_out_bf = w_in.astype(BF16), w_glu.astype(BF16), w_out.astype(BF16)
    ws_t_bf = jnp.swapaxes(w_in[:, :, ATTN_WIDTH + 2 * KV_WIDTH:], 1, 2).astype(BF16)
    lat_row = lambda b: b
    ctx_row = lambda b: bsz
    n_lat, n_c = bsz * n_tok, bsz * n_ctx
    lat_blocks = n_lat // S5_TILE
    s5_rows = (lat_blocks + 1) * S5_TILE_ROWS

    xc = ctx
    for i in range(depth):
        last = i == depth - 1
        mod3 = mod[i].reshape(MOD_ROWS, N_MOD, d)
        n_moe = n_lat if last else n_lat + n_c

        q, k, v = _in_projection(x, mod3, lat_row, w_in_bf, i, cos_t, sin_t, True)
        qc, kc, vc = _in_projection(xc, mod3, ctx_row, w_in_bf, i, cos_t, sin_t, False)
        u = _s_projection(x, mod3, lat_row, ws_t_bf, i, s5_rows, 0)
        u = _s_projection(xc, mod3, ctx_row, ws_t_bf, i, s5_rows, lat_blocks, u_prev=u)
        attn_n = _attention(q, k, v, kc, vc, attn_sink[i], g_attn_out[i], True)
        y_s5 = _s5_apply(u, bsz, n_tok, n_ctx, s5_ops)
        ssm_n = _glu(y_s5, bsz, n_tok, 0, True, w_glu_bf, i, b_glu[i], g_ssm_out[i])
        x1, u2, ri, rw = _out_projection(attn_n, ssm_n, x, mod3, lat_row, w_out_bf, i, ln1_g[i], ln1_b[i],
                                         w_router_t, b_router, alpha, n_moe, 0)
        if not last:
            attn_c = _attention(qc, None, None, kc, vc, attn_sink[i], g_attn_out[i], False)
            ssm_c = _glu(y_s5, bsz, n_ctx, lat_blocks, False, w_glu_bf, i, b_glu[i], g_ssm_out[i])
            xc1, u2, ric, rwc = _out_projection(attn_c, ssm_c, xc, mod3, ctx_row, w_out_bf, i, ln1_g[i], ln1_b[i],
                                                w_router_t, b_router, alpha, n_moe, n_lat, u2_prev=u2)
            ri = jnp.concatenate([ri.transpose(1, 0, 2).reshape(2, n_lat), ric.transpose(1, 0, 2).reshape(2, n_c)], axis=1)
            rw = jnp.concatenate([rw.transpose(1, 0, 2).reshape(2, n_lat), rwc.transpose(1, 0, 2).reshape(2, n_c)], axis=1)
        else:
            ri = ri.transpose(1, 0, 2).reshape(2, n_lat)
            rw = rw.transpose(1, 0, 2).reshape(2, n_lat)

        row_src, pos1, pos2, tile_e, next_e, n_used = _dispatch_plan(ri[0], ri[1], MOE_TILE)
        xs = _take_rows(u2, row_src)
        if not last:
            s5_ops = _s5_operators(*s5_params, i + 1, n_used)
        ys = _expert_ffn(xs, tile_e, next_e, n_used, *expert_w, i)
        y1 = _take_rows(ys, pos1)
        y2 = _take_rows(ys, pos2)
        cw1, cw2 = rw[0].reshape(n_moe, 1), rw[1].reshape(n_moe, 1)
        x = _combine_ln(x1, y1, y2, cw1, cw2, 0, mod3, lat_row, ln2_g[i], ln2_b[i], alpha)
        if not last:
            xc = _combine_ln(xc1, y1, y2, cw1, cw2, n_lat, mod3, ctx_row, ln2_g[i], ln2_b[i], alpha)
    return x
```

```python
import functools
import math

import jax
import jax.numpy as jnp
from jax import lax
from jax.experimental import pallas as pl
from jax.experimental.pallas import tpu as pltpu

F32 = jnp.float32
BF16 = jnp.bfloat16

HEAD_DIM = 128
N_Q_HEADS = 8
N_KV_HEADS = 2
Q_PER_KV = N_Q_HEADS // N_KV_HEADS
ATTN_WIDTH = N_Q_HEADS * HEAD_DIM
KV_WIDTH = N_KV_HEADS * HEAD_DIM
SSM_WIDTH = 1024
ATTN_BLOCK = 128
GRID_W = 64
ROPE_THETA = 10000.0
SSM_P = 16
SSM_G = SSM_WIDTH // SSM_P
SSM_N = 64
CHUNK = 64
GC = SSM_P * CHUNK
N_EXPERTS = 16
N_EXPERT_GROUPS = 4
EXPERTS_PER_GROUP = 4
N_MOD = 6
LN_EPS = 1e-5
NEG_INF = -1e30
LANES = 128
MOD_ROWS = 8
ROW_TILE = 256
MOE_TILE = 256
VMEM_LIMIT = 56 * 1024 * 1024

assert 2 * SSM_N == LANES and 2 * CHUNK == LANES


def _cparams(*sem):
    return pltpu.CompilerParams(dimension_semantics=sem, vmem_limit_bytes=VMEM_LIMIT)


def _dot(a, b):
    return jnp.dot(a, b, preferred_element_type=F32)


def _dot_nt(a, b):
    return lax.dot_general(a, b, (((1,), (1,)), ((), ())), preferred_element_type=F32)


def _mod_kernel(ct_ref, w_ref, b_ref, o_ref, ab_ref, *, n_rows, tn):
    d = ct_ref.shape[0]

    @pl.when((pl.program_id(0) == 0) & (pl.program_id(1) == 0))
    def _():
        ct = ct_ref[...]
        a = ct * jax.nn.sigmoid(ct)
        for r in range(n_rows):
            ab_ref[r] = jnp.broadcast_to(a[:, r:r + 1], (d, LANES))

    o_ref[...] = jnp.zeros(o_ref.shape, F32)
    sub = 8
    for j in range(tn // LANES):
        cols = slice(j * LANES, (j + 1) * LANES)

        def k_step(kc, accs, cols=cols):
            rows = pl.ds(pl.multiple_of(kc * sub, sub), sub)
            w = w_ref[0, rows, cols]
            return tuple(acc + ab_ref[r, rows, :] * w for r, acc in enumerate(accs))

        zero = jnp.zeros((sub, LANES), F32)
        accs = lax.fori_loop(0, d // sub, k_step, (zero,) * n_rows, unroll=8)
        for r in range(n_rows):
            o_ref[0, r:r + 1, cols] = jnp.sum(accs[r], axis=0, keepdims=True) + b_ref[0, :, cols]


def _modulation(c, c_ctx, w_mod, b_mod):
    depth, d, n_out = w_mod.shape
    n_rows = c.shape[0] + 1
    assert n_rows <= MOD_ROWS
    tn = 512
    ct = jnp.zeros((MOD_ROWS, d), F32).at[:c.shape[0]].set(c).at[c.shape[0]].set(c_ctx).T
    return pl.pallas_call(
        functools.partial(_mod_kernel, n_rows=n_rows, tn=tn),
        grid=(depth, n_out // tn),
        in_specs=[
            pl.BlockSpec((d, MOD_ROWS), lambda l, j: (0, 0)),
            pl.BlockSpec((1, d, tn), lambda l, j: (l, 0, j)),
            pl.BlockSpec((1, 1, tn), lambda l, j: (l, 0, j)),
        ],
        out_specs=pl.BlockSpec((1, MOD_ROWS, tn), lambda l, j: (l, 0, j)),
        out_shape=jax.ShapeDtypeStruct((depth, MOD_ROWS, n_out), F32),
        scratch_shapes=[pltpu.VMEM((n_rows, d, LANES), F32)],
        compiler_params=_cparams("arbitrary", "arbitrary"),
        name="modulation",
    )(ct, w_mod, b_mod.reshape(depth, 1, n_out))


def _rope(xh, cos, sin_signed):
    lane = lax.broadcasted_iota(jnp.int32, xh.shape, 1)
    swapped = jnp.where((lane % 64) < 32, pltpu.roll(xh, 96, 1), pltpu.roll(xh, 32, 1))
    return xh * cos + swapped * sin_signed


def _inproj_kernel(x_ref, mod_ref, w_ref, cos_ref, sin_ref, q_ref, k_ref, v_ref, *, rope):
    sh = mod_ref[0, 0:1, :]
    sc = mod_ref[0, 1:2, :]
    u = (x_ref[0] * (1.0 + sc) + sh).astype(BF16)
    q = _dot(u, w_ref[0, :, 0:ATTN_WIDTH])
    k = _dot(u, w_ref[0, :, ATTN_WIDTH:ATTN_WIDTH + KV_WIDTH])
    v = _dot(u, w_ref[0, :, ATTN_WIDTH + KV_WIDTH:ATTN_WIDTH + 2 * KV_WIDTH])
    scale = HEAD_DIM ** -0.5
    if rope:
        cos = cos_ref[...]
        sin = sin_ref[...]
    for h in range(N_Q_HEADS):
        qh = q[:, h * HEAD_DIM:(h + 1) * HEAD_DIM]
        if rope:
            qh = _rope(qh, cos, sin)
        q_ref[0, :, h * HEAD_DIM:(h + 1) * HEAD_DIM] = (qh * scale).astype(BF16)
    for h in range(N_KV_HEADS):
        kh = k[:, h * HEAD_DIM:(h + 1) * HEAD_DIM]
        if rope:
            kh = _rope(kh, cos, sin)
        k_ref[0, :, h * HEAD_DIM:(h + 1) * HEAD_DIM] = kh.astype(BF16)
    v_ref[0] = v.astype(BF16)


def _in_projection(x, mod3, row_of_batch, w_in_bf, layer, cos_t, sin_t, rope):
    bsz, n_tok, d = x.shape
    tm = min(ROW_TILE, n_tok)
    n_cols = ATTN_WIDTH + 2 * KV_WIDTH
    out = lambda w: jax.ShapeDtypeStruct((bsz, n_tok, w), BF16)
    ospec = lambda w: pl.BlockSpec((1, tm, w), lambda b, t: (b, t, 0))
    return pl.pallas_call(
        functools.partial(_inproj_kernel, rope=rope),
        grid=(bsz, n_tok // tm),
        in_specs=[
            pl.BlockSpec((1, tm, d), lambda b, t: (b, t, 0)),
            pl.BlockSpec((1, N_MOD, d), lambda b, t: (row_of_batch(b), 0, 0)),
            pl.BlockSpec((1, d, n_cols), lambda b, t: (layer, 0, 0)),
            pl.BlockSpec((tm, HEAD_DIM), lambda b, t: (t, 0)),
            pl.BlockSpec((tm, HEAD_DIM), lambda b, t: (t, 0)),
        ],
        out_specs=[ospec(ATTN_WIDTH), ospec(KV_WIDTH), ospec(KV_WIDTH)],
        out_shape=[out(ATTN_WIDTH), out(KV_WIDTH), out(KV_WIDTH)],
        compiler_params=_cparams("parallel", "parallel"),
        name="in_projection",
    )(x, mod3, w_in_bf, cos_t, sin_t)


S5_TILE = 1024
S5_TILE_ROWS = 2 * S5_TILE // LANES


def _to_group_layout(st):
    x4 = st.reshape(SSM_G, SSM_P, S5_TILE // LANES, LANES)
    lane = lax.broadcasted_iota(jnp.int32, x4.shape[:1] + x4.shape[2:], 2)
    roll64 = lambda a: pltpu.roll(a.reshape(-1, LANES), CHUNK, 1).reshape(a.shape)
    out = []
    for k in range(SSM_P // 2):
        a, b = x4[:, 2 * k], x4[:, 2 * k + 1]
        h0 = jnp.where(lane < CHUNK, a, roll64(b))
        h1 = jnp.where(lane < CHUNK, roll64(a), b)
        out.append(jnp.concatenate([h0, h1], axis=1))
    return out


def _from_group_layout(blocks):
    n8 = S5_TILE // LANES
    lane = lax.broadcasted_iota(jnp.int32, (SSM_G, n8, LANES), 2)
    roll64 = lambda a: pltpu.roll(a.reshape(-1, LANES), CHUNK, 1).reshape(a.shape)
    chans = []
    for blk in blocks:
        a0, b1 = blk[:, 0:n8], blk[:, n8:]
        chans.append(jnp.where(lane < CHUNK, a0, roll64(b1)))
        chans.append(jnp.where(lane < CHUNK, roll64(a0), b1))
    return jnp.stack(chans, axis=1).reshape(SSM_WIDTH, S5_TILE)


def _sproj_kernel(x_ref, mod_ref, w_ref, *rest, n_steps):
    u_ref = rest[-1]

    @pl.when(pl.program_id(0) < n_steps)
    def _():
        sh = mod_ref[0, 0:1, :]
        sc = mod_ref[0, 1:2, :]
        x = x_ref[...].reshape(S5_TILE, x_ref.shape[-1])
        u = (x * (1.0 + sc) + sh).astype(BF16)
        st = _dot_nt(w_ref[0], u)
        for k, blk in enumerate(_to_group_layout(st)):
            u_ref[:, :, k * LANES:(k + 1) * LANES] = blk.astype(BF16)

    @pl.when(pl.program_id(0) >= n_steps)
    def _():
        u_ref[...] = jnp.zeros(u_ref.shape, BF16)


def _s_projection(x, mod3, mod_row, ws_t_bf, layer, n_rows, row_blk_off, u_prev=None):
    bsz, n_tok, d = x.shape
    if u_prev is None:
        assert n_tok % S5_TILE == 0
        nj = n_tok // S5_TILE
        n_steps = bsz * nj
        n_fill = n_rows // S5_TILE_ROWS - n_steps
        tile = lambda i: jnp.minimum(i, n_steps - 1)
        xspec = pl.BlockSpec((1, S5_TILE, d), lambda i: (tile(i) // nj, tile(i) % nj, 0))
        mspec = pl.BlockSpec((1, N_MOD, d), lambda i: (mod_row(tile(i) // nj), 0, 0))
    else:
        assert bsz * n_tok == S5_TILE and n_tok % LANES == 0
        n_steps, n_fill = 1, 0
        xspec = pl.BlockSpec((bsz, n_tok, d), lambda i: (0, 0, 0))
        mspec = pl.BlockSpec((1, N_MOD, d), lambda i: (mod_row(0), 0, 0))
    in_specs = [xspec, mspec, pl.BlockSpec((1, SSM_WIDTH, d), lambda i: (layer, 0, 0))]
    args = [x, mod3, ws_t_bf]
    aliases = {}
    if u_prev is not None:
        in_specs.append(pl.BlockSpec(memory_space=pl.ANY))
        args.append(u_prev)
        aliases = {3: 0}
    return pl.pallas_call(
        functools.partial(_sproj_kernel, n_steps=n_steps),
        grid=(n_steps + n_fill,),
        in_specs=in_specs,
        out_specs=pl.BlockSpec((SSM_G, S5_TILE_ROWS, GC), lambda i: (0, row_blk_off + i, 0)),
        out_shape=jax.ShapeDtypeStruct((SSM_G, n_rows, GC), BF16),
        input_output_aliases=aliases,
        compiler_params=_cparams("arbitrary"),
        name="s_projection",
    )(*args)


def _rope_tables(n_tok):
    half = HEAD_DIM // 2
    inv_freq = ROPE_THETA ** (-jnp.arange(0, half, 2, dtype=F32) / half)
    t = jnp.arange(n_tok)
    row = (t // GRID_W).astype(F32)
    col = (t % GRID_W).astype(F32)
    ang_r = row[:, None] * inv_freq[None, :]
    ang_c = col[:, None] * inv_freq[None, :]
    cos_t = jnp.concatenate([jnp.cos(ang_r), jnp.cos(ang_r), jnp.cos(ang_c), jnp.cos(ang_c)], axis=-1)
    sin_t = jnp.concatenate([-jnp.sin(ang_r), jnp.sin(ang_r), -jnp.sin(ang_c), jnp.sin(ang_c)], axis=-1)
    return cos_t, sin_t


def _attn_kernel(sink_ref, q_ref, *refs, has_band, n_blk):
    if has_band:
        kp_ref, ko_ref, kn_ref, vp_ref, vo_ref, vn_ref, kc_ref, vc_ref, g_ref, o_ref = refs
    else:
        kc_ref, vc_ref, g_ref, o_ref = refs
    blk = pl.program_id(1)
    rows = Q_PER_KV * ATTN_BLOCK
    qi = lax.broadcasted_iota(jnp.int32, (rows, ATTN_BLOCK), 0) % ATTN_BLOCK
    kj = lax.broadcasted_iota(jnp.int32, (rows, ATTN_BLOCK), 1)
    row_head = lax.broadcasted_iota(jnp.int32, (rows, 1), 0) // ATTN_BLOCK
    heads = []
    for h in range(N_KV_HEADS):
        hs = slice(h * HEAD_DIM, (h + 1) * HEAD_DIM)
        q = jnp.concatenate(
            [q_ref[0, :, (h * Q_PER_KV + g) * HEAD_DIM:(h * Q_PER_KV + g + 1) * HEAD_DIM] for g in range(Q_PER_KV)],
            axis=0)
        sink = jnp.zeros((rows, 1), F32)
        for g in range(Q_PER_KV):
            sink = jnp.where(row_head == g, sink_ref[h * Q_PER_KV + g], sink)
        s_c = _dot_nt(q, kc_ref[0, :, hs])
        m = jnp.maximum(jnp.max(s_c, axis=-1, keepdims=True), sink)
        if has_band:
            s_p = jnp.where((kj >= qi) & (blk > 0), _dot_nt(q, kp_ref[0, :, hs]), NEG_INF)
            s_o = _dot_nt(q, ko_ref[0, :, hs])
            s_n = jnp.where((kj <= qi) & (blk < n_blk - 1), _dot_nt(q, kn_ref[0, :, hs]), NEG_INF)
            m = jnp.maximum(m, jnp.max(jnp.maximum(jnp.maximum(s_p, s_o), s_n), axis=-1, keepdims=True))
        p_c = jnp.exp(s_c - m)
        denom = jnp.sum(p_c, axis=-1, keepdims=True) + jnp.exp(sink - m)
        acc = _dot(p_c.astype(BF16), vc_ref[0, :, hs])
        if has_band:
            for s_x, v_ref in ((s_p, vp_ref), (s_o, vo_ref), (s_n, vn_ref)):
                p_x = jnp.exp(s_x - m)
                denom = denom + jnp.sum(p_x, axis=-1, keepdims=True)
                acc = acc + _dot(p_x.astype(BF16), v_ref[0, :, hs])
        o = acc / denom
        for g in range(Q_PER_KV):
            heads.append(o[g * ATTN_BLOCK:(g + 1) * ATTN_BLOCK, :])
    ss = heads[0] * heads[0]
    for o in heads[1:]:
        ss = ss + o * o
    inv = lax.rsqrt(jnp.sum(ss, axis=-1, keepdims=True) / ATTN_WIDTH + LN_EPS)
    for i, o in enumerate(heads):
        cs = slice(i * HEAD_DIM, (i + 1) * HEAD_DIM)
        o_ref[0, :, cs] = (o * inv * g_ref[:, cs]).astype(BF16)


def _attention(q, k, v, kc, vc, sink, g_attn, has_band):
    bsz, n_tok, _ = q.shape
    n_ctx = kc.shape[1]
    n_blk = n_tok // ATTN_BLOCK
    qspec = pl.BlockSpec((1, ATTN_BLOCK, ATTN_WIDTH), lambda b, n: (b, n, 0))
    kvspec = lambda f: pl.BlockSpec((1, ATTN_BLOCK, KV_WIDTH), lambda b, n: (b, f(n), 0))
    cspec = pl.BlockSpec((1, n_ctx, KV_WIDTH), lambda b, n: (b, 0, 0))
    prev = lambda n: jnp.maximum(n - 1, 0)
    own = lambda n: n
    nxt = lambda n: jnp.minimum(n + 1, n_blk - 1)
    in_specs = [pl.BlockSpec(memory_space=pltpu.SMEM), qspec]
    args = [sink, q]
    if has_band:
        in_specs += [kvspec(prev), kvspec(own), kvspec(nxt), kvspec(prev), kvspec(own), kvspec(nxt)]
        args += [k, k, k, v, v, v]
    in_specs += [cspec, cspec, pl.BlockSpec((1, ATTN_WIDTH), lambda b, n: (0, 0))]
    args += [kc, vc, g_attn.reshape(1, ATTN_WIDTH)]
    return pl.pallas_call(
        functools.partial(_attn_kernel, has_band=has_band, n_blk=n_blk),
        grid=(bsz, n_blk),
        in_specs=in_specs,
        out_specs=qspec,
        out_shape=jax.ShapeDtypeStruct((bsz, n_tok, ATTN_WIDTH), BF16),
        compiler_params=_cparams("parallel", "parallel"),
        name="attention",
    )(*args)


def _cmul(a, b):
    return a[0] * b[0] - a[1] * b[1], a[0] * b[1] + a[1] * b[0]


def _cpow(rho, theta, expo):
    mag = jnp.exp(expo * rho)
    ang = expo * theta
    return mag * jnp.cos(ang), mag * jnp.sin(ang)


def _s5_param_kernel(after_ref, prow_ref, bt_ref, cr_ref, dd_ref, m_ref, w_ref, vt_ref, al_ref, kmat_ref):
    del after_ref
    p_dim, lc = SSM_P, CHUNK
    lam_re, lam_im = prow_ref[0, 0:1, :], prow_ref[0, 1:2, :]
    dt = jnp.exp(prow_ref[0, 2:3, :])
    rho, theta = lam_re * dt, lam_im * dt

    sub8 = lax.broadcasted_iota(jnp.int32, (8, LANES), 0)
    asc = _cpow(rho, theta, sub8.astype(F32))
    desc = _cpow(rho, theta, (8 - sub8).astype(F32))
    ex = jnp.where(sub8 == 0, 8, jnp.where(sub8 == 1, 16, jnp.where(sub8 == 2, 32, jnp.where(sub8 == 3, lc, 1))))
    pw = _cpow(rho, theta, ex.astype(F32))
    row = lambda t, i: (t[0][i:i + 1, :], t[1][i:i + 1, :])
    cat = lambda a, b: (jnp.concatenate([a[0], b[0]], axis=0), jnp.concatenate([a[1], b[1]], axis=0))
    for i in range(3):
        step = row(pw, i)
        asc = cat(asc, _cmul(asc, step))
        desc = cat(_cmul(desc, step), desc)
    a_chunk, a_one = row(pw, 3), row(pw, 4)

    den = lam_re * lam_re + lam_im * lam_im
    x_re, x_im = a_one[0] - 1.0, a_one[1]
    beta = ((x_re * lam_re + x_im * lam_im) / den, (x_im * lam_re - x_re * lam_im) / den)
    bb = _cmul(beta, (bt_ref[0, 0], bt_ref[0, 1]))
    cc = (cr_ref[0, 0], cr_ref[0, 1])

    is_fwd = lax.broadcasted_iota(jnp.int32, (lc, LANES), 1) < SSM_N
    sub = lax.broadcasted_iota(jnp.int32, (lc, LANES), 0)
    pw_w = (jnp.where(is_fwd, desc[0], asc[0]), jnp.where(is_fwd, desc[1], asc[1]))
    pw_v = (jnp.where(is_fwd, asc[0], desc[0]), jnp.where(is_fwd, asc[1], desc[1]))

    top = (jnp.where(is_fwd, asc[0], jnp.where(sub == 0, 1.0, 0.0)), jnp.where(is_fwd, asc[1], 0.0))
    keep = (~is_fwd) & (sub > 0)
    bot = (jnp.where(keep, desc[0], 0.0), jnp.where(keep, desc[1], 0.0))
    lag = cat(top, bot)

    cx = (jnp.concatenate([cc[0]] * p_dim, axis=0), jnp.concatenate([cc[1]] * p_dim, axis=0))
    bx = tuple(jnp.concatenate([jnp.broadcast_to(b[q:q + 1, :], (p_dim, LANES)) for q in range(p_dim)], axis=0)
               for b in bb)
    e_re, e_im = _cmul(cx, bx)
    hi = lax.Precision.HIGHEST
    kmat = (jnp.dot(e_re, lag[0].T, preferred_element_type=F32, precision=hi)
            - jnp.dot(e_im, lag[1].T, preferred_element_type=F32, precision=hi))
    lane0 = lax.broadcasted_iota(jnp.int32, kmat.shape, 1) == 0
    kmat_ref[...] = kmat + jnp.where(lane0, dd_ref[0], 0.0)

    lane_m = lax.broadcasted_iota(jnp.int32, (lc, LANES), 1)

    def q_body(q, carry):
        for pp in range(p_dim // 2):
            r0 = q * p_dim + 2 * pp
            ka = jnp.broadcast_to(kmat_ref[pl.ds(r0, 1), :], (lc, LANES))
            kb = jnp.broadcast_to(kmat_ref[pl.ds(r0 + 1, 1), :], (lc, LANES))
            ra = pltpu.roll(ka, 0, 1, stride=1, stride_axis=0)
            rb = pltpu.roll(kb, lc, 1, stride=1, stride_axis=0)
            blk = jnp.where(lane_m < lc, ra, rb)
            m_ref[0, pl.ds(pl.multiple_of(q * lc, lc), lc), pp * LANES:(pp + 1) * LANES] = blk.astype(BF16)
        return carry

    lax.fori_loop(0, p_dim, q_body, 0)

    for q in range(p_dim):
        w_re, w_im = _cmul(pw_w, row(bb, q))
        w_ref[0, q * lc:(q + 1) * lc, :] = jnp.concatenate([w_re, w_im], axis=1).astype(BF16)
    for p in range(p_dim):
        v_re, v_im = _cmul(pw_v, row(cc, p))
        vt_ref[0, p * lc:(p + 1) * lc, :] = jnp.concatenate([v_re, -v_im], axis=1).astype(BF16)
    al_ref[0] = jnp.concatenate([a_chunk[0], a_chunk[1]], axis=1)


def _s5_operators(lam_re, lam_im, log_dt, b_re, b_im, c_re, c_im, d_skip, layer, after):
    depth = lam_re.shape[0]
    dg = depth * SSM_G
    g0 = layer * SSM_G
    vec = lambda a: jnp.moveaxis(a, 1, 2).reshape(dg, LANES)
    mat = lambda a: jnp.moveaxis(a, 1, 3).reshape(dg, SSM_P, LANES)
    ldt = jnp.broadcast_to(log_dt[..., None], lam_re.shape)
    prow = jnp.stack([vec(lam_re), vec(lam_im), vec(ldt)], axis=1)
    bt = jnp.stack([mat(jnp.swapaxes(b_re, 3, 4)), mat(jnp.swapaxes(b_im, 3, 4))], axis=1)
    cr = jnp.stack([mat(c_re), mat(c_im)], axis=1)
    dd = (d_skip.reshape(dg, 1, SSM_P) * jnp.eye(SSM_P, dtype=F32)[None]).reshape(dg, SSM_P * SSM_P, 1)
    iblk = lambda *s: pl.BlockSpec((1,) + s, lambda i: (g0 + i,) + (0,) * len(s))
    oblk = lambda *s: pl.BlockSpec((1,) + s, lambda i: (i,) + (0,) * len(s))
    return pl.pallas_call(
        _s5_param_kernel,
        grid=(SSM_G,),
        in_specs=[pl.BlockSpec(memory_space=pltpu.SMEM),
                  iblk(3, LANES), iblk(2, SSM_P, LANES), iblk(2, SSM_P, LANES), iblk(SSM_P * SSM_P, 1)],
        out_specs=[oblk(GC, GC), oblk(GC, 2 * LANES), oblk(GC, 2 * LANES), oblk(1, 2 * LANES)],
        out_shape=[jax.ShapeDtypeStruct((SSM_G, GC, GC), BF16),
                   jax.ShapeDtypeStruct((SSM_G, GC, 2 * LANES), BF16),
                   jax.ShapeDtypeStruct((SSM_G, GC, 2 * LANES), BF16),
                   jax.ShapeDtypeStruct((SSM_G, 1, 2 * LANES), F32)],
        scratch_shapes=[pltpu.VMEM((SSM_P * SSM_P, LANES), F32)],
        compiler_params=_cparams("parallel"),
        name="s5_operators",
    )(after, prow, bt, cr, dd)


def _s5_sum_kernel(u_ref, w_ref, s_ref):
    s_ref[0] = _dot(u_ref[0], w_ref[0])


def _s5_scan_kernel(s_ref, al_ref, h_ref, *, n_ctx_chunks, n_chunks):
    a_re, a_im = al_ref[:, :LANES], al_ref[:, LANES:]
    is_fwd = lax.broadcasted_iota(jnp.int32, a_re.shape, 1) < SSM_N

    def body(i, carry):
        h_re, h_im = carry
        cf = i
        cr = jnp.where(i < n_ctx_chunks, n_ctx_chunks - 1 - i, n_chunks - 1 - (i - n_ctx_chunks))
        s_f, s_r = s_ref[cf], s_ref[cr]
        h_ref[cf, :, 0:SSM_N] = h_re[:, 0:SSM_N]
        h_ref[cr, :, SSM_N:LANES] = h_re[:, SSM_N:LANES]
        h_ref[cf, :, LANES:LANES + SSM_N] = h_im[:, 0:SSM_N]
        h_ref[cr, :, LANES + SSM_N:] = h_im[:, SSM_N:LANES]
        s_re = jnp.where(is_fwd, s_f[:, :LANES], s_r[:, :LANES])
        s_im = jnp.where(is_fwd, s_f[:, LANES:], s_r[:, LANES:])
        return a_re * h_re - a_im * h_im + s_re, a_re * h_im + a_im * h_re + s_im

    zero = jnp.zeros(a_re.shape, F32)
    lax.fori_loop(0, n_chunks, body, (zero, zero))


def _s5_out_kernel(u_ref, m_ref, h_ref, vt_ref, y_ref):
    y = _dot(u_ref[0], m_ref[0]) + _dot_nt(h_ref[0].astype(BF16), vt_ref[0])
    y_ref[0] = y.astype(BF16)


def _rows_to_chunks(a, bsz, n_tok, n_ctx):
    n8, nj, ncb, st = S5_TILE // LANES, n_tok // S5_TILE, n_ctx // LANES, a.shape[-1]
    n_lat = bsz * nj * S5_TILE_ROWS
    lat = a[:, :n_lat].reshape(SSM_G, bsz, nj, 2, n8, st).transpose(2, 4, 3, 0, 1, 5).reshape(-1, SSM_G * bsz, st)
    ctx = a[:, n_lat:].reshape(SSM_G, 2, bsz, ncb, st).transpose(3, 1, 0, 2, 4).reshape(-1, SSM_G * bsz, st)
    return jnp.concatenate([ctx, lat], axis=0)


def _chunks_to_rows(a, bsz, n_tok, n_ctx):
    n8, nj, ncb, st = S5_TILE // LANES, n_tok // S5_TILE, n_ctx // LANES, a.shape[-1]
    ncc = n_ctx // CHUNK
    lat = a[ncc:].reshape(nj, n8, 2, SSM_G, bsz, st).transpose(3, 4, 0, 2, 1, 5).reshape(SSM_G, -1, st)
    ctx = a[:ncc].reshape(ncb, 2, SSM_G, bsz, st).transpose(2, 1, 3, 0, 4).reshape(SSM_G, -1, st)
    return jnp.concatenate([lat, ctx], axis=1)


def _s5_apply(u, bsz, n_tok, n_ctx, ops):
    m_op, w_op, vt_op, al = ops
    ncc, n_chunks = n_ctx // CHUNK, (n_ctx + n_tok) // CHUNK
    rows = u.shape[1]
    assert rows == bsz * n_chunks
    gspec = lambda r, c: pl.BlockSpec((1, r, c), lambda g: (g, 0, 0))
    ospec = gspec
    sums = pl.pallas_call(
        _s5_sum_kernel,
        grid=(SSM_G,),
        in_specs=[gspec(rows, GC), ospec(GC, 2 * LANES)],
        out_specs=gspec(rows, 2 * LANES),
        out_shape=jax.ShapeDtypeStruct((SSM_G, rows, 2 * LANES), F32),
        compiler_params=_cparams("parallel"),
        name="s5_chunk_sums",
    )(u, w_op)
    sums_t = _rows_to_chunks(sums, bsz, n_tok, n_ctx)
    al_rows = jnp.broadcast_to(al, (SSM_G, bsz, 2 * LANES)).reshape(SSM_G * bsz, 2 * LANES)
    rt = 64
    states_t = pl.pallas_call(
        functools.partial(_s5_scan_kernel, n_ctx_chunks=ncc, n_chunks=n_chunks),
        grid=(SSM_G * bsz // rt,),
        in_specs=[pl.BlockSpec((n_chunks, rt, 2 * LANES), lambda r: (0, r, 0)),
                  pl.BlockSpec((rt, 2 * LANES), lambda r: (r, 0))],
        out_specs=pl.BlockSpec((n_chunks, rt, 2 * LANES), lambda r: (0, r, 0)),
        out_shape=jax.ShapeDtypeStruct((n_chunks, SSM_G * bsz, 2 * LANES), F32),
        compiler_params=_cparams("parallel"),
        name="s5_state_scan",
    )(sums_t, al_rows)
    states = _chunks_to_rows(states_t, bsz, n_tok, n_ctx)
    return pl.pallas_call(
        _s5_out_kernel,
        grid=(SSM_G,),
        in_specs=[gspec(rows, GC), ospec(GC, GC), gspec(rows, 2 * LANES), ospec(GC, 2 * LANES)],
        out_specs=gspec(rows, GC),
        out_shape=jax.ShapeDtypeStruct((SSM_G, rows, GC), BF16),
        compiler_params=_cparams("parallel"),
        name="s5_chunk_outputs",
    )(u, m_op, states, vt_op)


def _glu_kernel(y_ref, w_ref, b_ref, g_ref, o_ref):
    blocks = [y_ref[:, :, k * LANES:(k + 1) * LANES].astype(F32) for k in range(SSM_P // 2)]
    y = _from_group_layout(blocks).T
    z = 0.5 * y * (1.0 + jnp.tanh(math.sqrt(2.0 / math.pi) * (y + 0.044715 * (y * y * y))))
    t = _dot(z.astype(BF16), w_ref[0]) + b_ref[...]
    o = z * jax.nn.sigmoid(t)
    inv = lax.rsqrt(jnp.mean(o * o, axis=-1, keepdims=True) + LN_EPS)
    o_ref[...] = (o * inv * g_ref[...]).astype(BF16).reshape(o_ref.shape)


def _glu(y, bsz, n_tok, row_blk_off, per_batch, w_glu_bf, layer, b_glu, g_ssm):
    w = SSM_WIDTH
    if per_batch:
        grid = (bsz, n_tok // S5_TILE)
        ospec = pl.BlockSpec((1, S5_TILE, w), lambda b, j: (b, j, 0))
    else:
        assert bsz * n_tok == S5_TILE
        grid = (1, 1)
        ospec = pl.BlockSpec((bsz, n_tok, w), lambda b, j: (0, 0, 0))
    nj = grid[1]
    row = lambda a: a.reshape(1, w)
    vspec = pl.BlockSpec((1, w), lambda b, j: (0, 0))
    return pl.pallas_call(
        _glu_kernel,
        grid=grid,
        in_specs=[pl.BlockSpec((SSM_G, S5_TILE_ROWS, GC), lambda b, j: (0, row_blk_off + b * nj + j, 0)),
                  pl.BlockSpec((1, w, w), lambda b, j: (layer, 0, 0)), vspec, vspec],
        out_specs=ospec,
        out_shape=jax.ShapeDtypeStruct((bsz, n_tok, w), BF16),
        compiler_params=_cparams("parallel", "parallel"),
        name="s5_glu",
    )(y, w_glu_bf, row(b_glu), row(g_ssm))


def _layer_norm(z, g, b):
    mu = jnp.mean(z, axis=-1, keepdims=True)
    zc = z - mu
    var = jnp.mean(zc * zc, axis=-1, keepdims=True)
    return zc * lax.rsqrt(var + LN_EPS) * g + b


def _first_argmax(vals):
    best_i = jnp.zeros(vals[0].shape, jnp.int32)
    best_v = vals[0]
    for j in range(1, len(vals)):
        better = vals[j] > best_v
        best_i = jnp.where(better, j, best_i)
        best_v = jnp.where(better, vals[j], best_v)
    return best_i, best_v


def _route(logit_rows):
    m = functools.reduce(jnp.maximum, logit_rows)
    p = [jnp.exp(l - m) for l in logit_rows]
    scores = []
    for g in range(N_EXPERT_GROUPS):
        a, b, c, d = p[4 * g:4 * g + 4]
        hi1, lo1, hi2, lo2 = jnp.maximum(a, b), jnp.minimum(a, b), jnp.maximum(c, d), jnp.minimum(c, d)
        scores.append(jnp.maximum(hi1, hi2) + jnp.maximum(jnp.minimum(hi1, hi2), jnp.maximum(lo1, lo2)))
    grp, _ = _first_argmax(scores)
    sel = []
    for j in range(EXPERTS_PER_GROUP):
        v = p[j]
        for g in range(1, N_EXPERT_GROUPS):
            v = jnp.where(grp == g, p[4 * g + j], v)
        sel.append(v)
    i1, v1 = _first_argmax(sel)
    i2, v2 = _first_argmax([jnp.where(i1 == j, -1.0, sel[j]) for j in range(EXPERTS_PER_GROUP)])
    tot = v1 + v2
    return grp * EXPERTS_PER_GROUP + i1, grp * EXPERTS_PER_GROUP + i2, v1 / tot, v2 / tot


def _outproj_kernel(a_ref, s_ref, x_ref, mod_ref, w_ref, lng_ref, lnb_ref, wr_ref, br_ref, *rest, alpha, n_steps):
    x1_ref, u2_ref, ri_ref, rw_ref = rest[-4:]

    @pl.when(pl.program_id(0) < n_steps)
    def _():
        y = _dot(a_ref[0], w_ref[0, 0:ATTN_WIDTH, :]) + _dot(s_ref[0], w_ref[0, ATTN_WIDTH:, :])
        g1 = mod_ref[0, 2:3, :]
        sh2 = mod_ref[0, 3:4, :]
        sc2 = mod_ref[0, 4:5, :]
        x1 = _layer_norm(alpha * x_ref[0] + g1 * y, lng_ref[...], lnb_ref[...])
        x1_ref[0] = x1
        u2 = x1 * (1.0 + sc2) + sh2
        u2_ref[...] = u2
        logits = _dot_nt(wr_ref[...], u2.astype(BF16)) + br_ref[...]
        e1, e2, w1, w2 = _route([logits[e:e + 1, :] for e in range(N_EXPERTS)])
        ri_ref[0, 0:1, :] = e1
        ri_ref[0, 1:2, :] = e2
        rw_ref[0, 0:1, :] = w1
        rw_ref[0, 1:2, :] = w2

    @pl.when(pl.program_id(0) >= n_steps)
    def _():
        u2_ref[...] = jnp.zeros(u2_ref.shape, F32)


def _out_projection(attn_n, ssm_n, x, mod3, row_of_batch, w_out_bf, layer, ln_g, ln_b, w_router_t_bf, b_router,
                    alpha, u2_rows, row_off, u2_prev=None):
    bsz, n_tok, d = x.shape
    tm = min(ROW_TILE, n_tok)
    nt = n_tok // tm
    n_steps = bsz * nt
    assert row_off % tm == 0 and u2_rows % tm == 0
    n_fill = 0 if u2_prev is not None else u2_rows // tm - n_steps
    bt = lambda i: (jnp.minimum(i, n_steps - 1) // nt, jnp.minimum(i, n_steps - 1) % nt)
    tspec = lambda w: pl.BlockSpec((1, tm, w), lambda i: bt(i) + (0,))
    vspec = pl.BlockSpec((1, d), lambda i: (0, 0))
    rspec = pl.BlockSpec((1, 2, tm), lambda i: (bt(i)[0], 0, bt(i)[1]))
    in_specs = [
        tspec(ATTN_WIDTH), tspec(SSM_WIDTH), tspec(d),
        pl.BlockSpec((1, N_MOD, d), lambda i: (row_of_batch(bt(i)[0]), 0, 0)),
        pl.BlockSpec((1,) + w_out_bf.shape[1:], lambda i: (layer, 0, 0)),
        vspec, vspec,
        pl.BlockSpec((N_EXPERTS, d), lambda i: (0, 0)),
        pl.BlockSpec((N_EXPERTS, 1), lambda i: (0, 0)),
    ]
    args = [attn_n, ssm_n, x, mod3, w_out_bf, ln_g.reshape(1, d), ln_b.reshape(1, d), w_router_t_bf,
            b_router.reshape(N_EXPERTS, 1)]
    aliases = {}
    if u2_prev is not None:
        in_specs.append(pl.BlockSpec(memory_space=pl.ANY))
        args.append(u2_prev)
        aliases = {len(args) - 1: 1}
    return pl.pallas_call(
        functools.partial(_outproj_kernel, alpha=alpha, n_steps=n_steps),
        grid=(n_steps + n_fill,),
        in_specs=in_specs,
        out_specs=[tspec(d), pl.BlockSpec((tm, d), lambda i: (row_off // tm + i, 0)), rspec, rspec],
        out_shape=[jax.ShapeDtypeStruct((bsz, n_tok, d), F32), jax.ShapeDtypeStruct((u2_rows, d), F32),
                   jax.ShapeDtypeStruct((bsz, 2, n_tok), jnp.int32), jax.ShapeDtypeStruct((bsz, 2, n_tok), F32)],
        input_output_aliases=aliases,
        compiler_params=_cparams("arbitrary"),
        name="out_projection",
    )(*args)


CAST_ROWS = 256


def _ffn_kernel(te_ref, nx_ref, nu_ref, src_ref, u_hbm, wg_hbm, wu_hbm, wd_hbm, y_ref,
                xbuf, stage_g, stage_u, stage_d, wg_bf, wu_bf, wd_bf, sem, gsem, *, e0):
    t = pl.program_id(0)
    e = te_ref[t]
    tm = xbuf.shape[1]
    pairs = ((wg_hbm, stage_g, wg_bf), (wu_hbm, stage_u, wu_bf), (wd_hbm, stage_d, wd_bf))

    def fetch(expert):
        return [pltpu.make_async_copy(hbm.at[e0 + expert], stage, sem.at[i])
                for i, (hbm, stage, _) in enumerate(pairs)]

    def gather_row(base, i, slot):
        r = src_ref[base + i]
        pltpu.make_async_copy(u_hbm.at[pl.ds(r, 1)], xbuf.at[slot, pl.ds(i, 1)], gsem.at[slot]).start()

    def wait_rows(slot):
        pltpu.make_async_copy(u_hbm.at[pl.ds(0, tm)], xbuf.at[slot], gsem.at[slot]).wait()

    @pl.when(t == 0)
    def _():
        def one_row(i, carry):
            gather_row(0, i, 0)
            return carry
        lax.fori_loop(0, tm, one_row, 0, unroll=8)
        for cp in fetch(e):
            cp.start()

    first_of_run = (t == 0) | (te_ref[jnp.maximum(t - 1, 0)] != e)

    @pl.when(first_of_run & (t < nu_ref[0]))
    def _():
        for cp in fetch(e):
            cp.wait()
        for _, stage, dst in pairs:
            def cast_rows(i, carry, stage=stage, dst=dst):
                rows = pl.ds(pl.multiple_of(i * CAST_ROWS, CAST_ROWS), CAST_ROWS)
                dst[rows, :] = stage[rows, :].astype(BF16)
                return carry
            lax.fori_loop(0, stage.shape[0] // CAST_ROWS, cast_rows, 0)

        @pl.when(nx_ref[t] >= 0)
        def _():
            for cp in fetch(nx_ref[t]):
                cp.start()

    @pl.when(t < nu_ref[0])
    def _():
        slot = t % 2
        wait_rows(slot)
        x = xbuf[slot].astype(BF16)
        base = jnp.minimum(t + 1, pl.num_programs(0) - 1) * tm
        for i in range(tm):
            gather_row(base, i, 1 - slot)
        g = _dot(x, wg_bf[...])
        u = _dot(x, wu_bf[...])
        h = (g * jax.nn.sigmoid(g) * u).astype(BF16)
        y_ref[...] = _dot(h, wd_bf[...]).astype(BF16)

    @pl.when(t == nu_ref[0] - 1)
    def _():
        wait_rows((t + 1) % 2)

    @pl.when(t >= nu_ref[0])
    def _():
        y_ref[...] = jnp.zeros(y_ref.shape, BF16)


def _expert_ffn(u2, row_src, tile_expert, next_expert, n_used, w_gate, w_up, w_down, layer):
    d = u2.shape[1]
    n_rows = row_src.shape[0]
    n_e, f = w_gate.shape[1], w_gate.shape[3]
    tm = MOE_TILE
    flat = lambda w: w.reshape((-1,) + w.shape[2:])
    hbm = pl.BlockSpec(memory_space=pl.ANY)
    grid_spec = pltpu.PrefetchScalarGridSpec(
        num_scalar_prefetch=4,
        grid=(n_rows // tm,),
        in_specs=[hbm, hbm, hbm, hbm],
        out_specs=pl.BlockSpec((tm, d), lambda t, te, nx, nu, src: (t, 0)),
        scratch_shapes=[pltpu.VMEM((2, tm, d), F32),
                        pltpu.VMEM((d, f), F32), pltpu.VMEM((d, f), F32), pltpu.VMEM((f, d), F32),
                        pltpu.VMEM((d, f), BF16), pltpu.VMEM((d, f), BF16), pltpu.VMEM((f, d), BF16),
                        pltpu.SemaphoreType.DMA((3,)), pltpu.SemaphoreType.DMA((2,))],
    )
    return pl.pallas_call(
        functools.partial(_ffn_kernel, e0=layer * n_e),
        grid_spec=grid_spec,
        out_shape=jax.ShapeDtypeStruct((n_rows, d), BF16),
        compiler_params=_cparams("arbitrary"),
        name="expert_ffn",
    )(tile_expert, next_expert, n_used, row_src, u2, flat(w_gate), flat(w_up), flat(w_down))


def _dispatch_plan(e1, e2, tm):
    n = e1.shape[0]
    n_tiles = (2 * n + N_EXPERTS * (tm - 1) + tm - 1) // tm
    e = jnp.concatenate([e1, e2])
    onehot = (e[:, None] == jnp.arange(N_EXPERTS, dtype=jnp.int32)[None, :]).astype(jnp.int32)
    csum = jnp.cumsum(onehot, axis=0)
    pos_in_e = jnp.sum((csum - 1) * onehot, axis=1)
    counts = csum[-1]
    padded = ((counts + tm - 1) // tm) * tm
    ends = jnp.cumsum(padded)
    offs = ends - padded
    dest = offs[e] + pos_in_e
    tok = jnp.concatenate([jnp.arange(n, dtype=jnp.int32)] * 2)
    row_src = jnp.zeros((n_tiles * tm,), jnp.int32).at[dest].set(tok, mode="promise_in_bounds", unique_indices=True)
    n_used = (ends[-1] // tm).astype(jnp.int32)
    tile_start = jnp.arange(n_tiles, dtype=jnp.int32) * tm
    tile_e = jnp.sum((tile_start[:, None] >= ends[None, :]).astype(jnp.int32), axis=1)
    tile_e = jnp.minimum(tile_e, N_EXPERTS - 1)
    last_e = tile_e[jnp.maximum(n_used - 1, 0)]
    tile_e = jnp.where(jnp.arange(n_tiles) < n_used, tile_e, last_e).astype(jnp.int32)
    ids = jnp.arange(N_EXPERTS, dtype=jnp.int32)
    later = (ids[None, :] > ids[:, None]) & (counts[None, :] > 0)
    next_of = jnp.min(jnp.where(later, ids[None, :], N_EXPERTS), axis=1)
    next_of = jnp.where(next_of == N_EXPERTS, -1, next_of).astype(jnp.int32)
    return row_src, dest[:n], dest[n:], tile_e, next_of[tile_e], n_used.reshape(1)


def _take_rows(a, idx):
    return a.at[idx].get(mode="promise_in_bounds")


def _final_kernel(x_ref, y1_ref, y2_ref, w1_ref, w2_ref, mod_ref, lng_ref, lnb_ref, o_ref, *, alpha):
    f = w1_ref[...] * y1_ref[...].astype(F32) + w2_ref[...] * y2_ref[...].astype(F32)
    g2 = mod_ref[0, 5:6, :]
    o_ref[0] = _layer_norm(alpha * x_ref[0] + g2 * f, lng_ref[...], lnb_ref[...])


def _combine_ln(x1, y1, y2, w1, w2, row_off, mod3, row_of_batch, ln_g, ln_b, alpha):
    bsz, n_tok, d = x1.shape
    tm = min(ROW_TILE, n_tok)
    nt = n_tok // tm
    assert row_off % tm == 0
    tspec = pl.BlockSpec((1, tm, d), lambda b, t: (b, t, 0))
    yspec = pl.BlockSpec((tm, d), lambda b, t: (row_off // tm + b * nt + t, 0))
    wspec = pl.BlockSpec((tm, 1), lambda b, t: (row_off // tm + b * nt + t, 0))
    vspec = pl.BlockSpec((1, d), lambda b, t: (0, 0))
    return pl.pallas_call(
        functools.partial(_final_kernel, alpha=alpha),
        grid=(bsz, nt),
        in_specs=[tspec, yspec, yspec, wspec, wspec,
                  pl.BlockSpec((1, N_MOD, d), lambda b, t: (row_of_batch(b), 0, 0)), vspec, vspec],
        out_specs=tspec,
        out_shape=jax.ShapeDtypeStruct((bsz, n_tok, d), F32),
        compiler_params=_cparams("parallel", "parallel"),
        name="combine_post_ln",
    )(x1, y1, y2, w1, w2, mod3, ln_g.reshape(1, d), ln_b.reshape(1, d))


def kernel(x, c, ctx, c_ctx, w_mod, b_mod, w_in, attn_sink, ssm_lambda_re, ssm_lambda_im, ssm_log_dt, ssm_b_re, ssm_b_im, ssm_c_re, ssm_c_im, ssm_d, w_glu, b_glu, g_attn_out, g_ssm_out, w_out, ln1_g, ln1_b, w_router, b_router, w_expert_gate, w_expert_up, w_expert_down, ln2_g, ln2_b):
    depth = w_mod.shape[0]
    bsz, n_tok, d = x.shape
    n_ctx = ctx.shape[1]
    alpha = (2 * depth) ** 0.25
    assert n_tok % ROW_TILE == 0 and n_tok % ATTN_BLOCK == 0 and n_ctx % ATTN_BLOCK == 0 and n_ctx % CHUNK == 0

    mod = _modulation(c, c_ctx, w_mod, b_mod)
    s5_params = (ssm_lambda_re, ssm_lambda_im, ssm_log_dt, ssm_b_re, ssm_b_im, ssm_c_re, ssm_c_im, ssm_d)
    expert_w = (w_expert_gate, w_expert_up, w_expert_down)
    s5_ops = _s5_operators(*s5_params, 0, jnp.zeros((1,), jnp.int32))
    cos_t, sin_t = _rope_tables(n_tok)
    w_router_t = w_router.T.astype(BF16)
    w_in_bf, w_glu_bf, w_out_bf = w_in.astype(BF16), w_glu.astype(BF16), w_out.astype(BF16)
    ws_t_bf = jnp.swapaxes(w_in[:, :, ATTN_WIDTH + 2 * KV_WIDTH:], 1, 2).astype(BF16)
    lat_row = lambda b: b
    ctx_row = lambda b: bsz
    n_lat, n_c = bsz * n_tok, bsz * n_ctx
    lat_blocks = n_lat // S5_TILE
    s5_rows = (lat_blocks + 1) * S5_TILE_ROWS

    xc = ctx
    for i in range(depth):
        last = i == depth - 1
        mod3 = mod[i].reshape(MOD_ROWS, N_MOD, d)
        n_moe = n_lat if last else n_lat + n_c

        q, k, v = _in_projection(x, mod3, lat_row, w_in_bf, i, cos_t, sin_t, True)
        qc, kc, vc = _in_projection(xc, mod3, ctx_row, w_in_bf, i, cos_t, sin_t, False)
        u = _s_projection(x, mod3, lat_row, ws_t_bf, i, s5_rows, 0)
        u = _s_projection(xc, mod3, ctx_row, ws_t_bf, i, s5_rows, lat_blocks, u_prev=u)
        attn_n = _attention(q, k, v, kc, vc, attn_sink[i], g_attn_out[i], True)
        y_s5 = _s5_apply(u, bsz, n_tok, n_ctx, s5_ops)
        ssm_n = _glu(y_s5, bsz, n_tok, 0, True, w_glu_bf, i, b_glu[i], g_ssm_out[i])
        x1, u2, ri, rw = _out_projection(attn_n, ssm_n, x, mod3, lat_row, w_out_bf, i, ln1_g[i], ln1_b[i],
                                         w_router_t, b_router, alpha, n_moe, 0)
        if not last:
            attn_c = _attention(qc, None, None, kc, vc, attn_sink[i], g_attn_out[i], False)
            ssm_c = _glu(y_s5, bsz, n_ctx, lat_blocks, False, w_glu_bf, i, b_glu[i], g_ssm_out[i])
            xc1, u2, ric, rwc = _out_projection(attn_c, ssm_c, xc, mod3, ctx_row, w_out_bf, i, ln1_g[i], ln1_b[i],
                                                w_router_t, b_router, alpha, n_moe, n_lat, u2_prev=u2)
            ri = jnp.concatenate([ri.transpose(1, 0, 2).reshape(2, n_lat), ric.transpose(1, 0, 2).reshape(2, n_c)], axis=1)
            rw = jnp.concatenate([rw.transpose(1, 0, 2).reshape(2, n_lat), rwc.transpose(1, 0, 2).reshape(2, n_c)], axis=1)
        else:
            ri = ri.transpose(1, 0, 2).reshape(2, n_lat)
            rw = rw.transpose(1, 0, 2).reshape(2, n_lat)

        row_src, pos1, pos2, tile_e, next_e, n_used = _dispatch_plan(ri[0], ri[1], MOE_TILE)
        if not last:
            s5_ops = _s5_operators(*s5_params, i + 1, n_used)
        ys = _expert_ffn(u2, row_src, tile_e, next_e, n_used, *expert_w, i)
        y1 = _take_rows(ys, pos1)
        y2 = _take_rows(ys, pos2)
        cw1, cw2 = rw[0].reshape(n_moe, 1), rw[1].reshape(n_moe, 1)
        x = _combine_ln(x1, y1, y2, cw1, cw2, 0, mod3, lat_row, ln2_g[i], ln2_b[i], alpha)
        if not last:
            xc = _combine_ln(xc1, y1, y2, cw1, cw2, n_lat, mod3, ctx_row, ln2_g[i], ln2_b[i], alpha)
    return x
```

```python
import functools
import math

import jax
import jax.numpy as jnp
from jax import lax
from jax.experimental import pallas as pl
from jax.experimental.pallas import tpu as pltpu

F32 = jnp.float32
BF16 = jnp.bfloat16

HEAD_DIM = 128
N_Q_HEADS = 8
N_KV_HEADS = 2
Q_PER_KV = N_Q_HEADS // N_KV_HEADS
ATTN_WIDTH = N_Q_HEADS * HEAD_DIM
KV_WIDTH = N_KV_HEADS * HEAD_DIM
SSM_WIDTH = 1024
ATTN_BLOCK = 128
GRID_W = 64
ROPE_THETA = 10000.0
SSM_P = 16
SSM_G = SSM_WIDTH // SSM_P
SSM_N = 64
CHUNK = 64
GC = SSM_P * CHUNK
N_EXPERTS = 16
N_EXPERT_GROUPS = 4
EXPERTS_PER_GROUP = 4
N_MOD = 6
LN_EPS = 1e-5
NEG_INF = -1e30
LANES = 128
MOD_ROWS = 8
ROW_TILE = 256
MOE_TILE = 256
VMEM_LIMIT = 56 * 1024 * 1024

assert 2 * SSM_N == LANES and 2 * CHUNK == LANES


def _cparams(*sem):
    return pltpu.CompilerParams(dimension_semantics=sem, vmem_limit_bytes=VMEM_LIMIT)


def _dot(a, b):
    return jnp.dot(a, b, preferred_element_type=F32)


def _dot_nt(a, b):
    return lax.dot_general(a, b, (((1,), (1,)), ((), ())), preferred_element_type=F32)


def _mod_kernel(ct_ref, w_ref, b_ref, o_ref, ab_ref, *, n_rows, tn):
    d = ct_ref.shape[0]

    @pl.when((pl.program_id(0) == 0) & (pl.program_id(1) == 0))
    def _():
        ct = ct_ref[...]
        a = ct * jax.nn.sigmoid(ct)
        for r in range(n_rows):
            ab_ref[r] = jnp.broadcast_to(a[:, r:r + 1], (d, LANES))

    o_ref[...] = jnp.zeros(o_ref.shape, F32)
    sub = 8
    for j in range(tn // LANES):
        cols = slice(j * LANES, (j + 1) * LANES)

        def k_step(kc, accs, cols=cols):
            rows = pl.ds(pl.multiple_of(kc * sub, sub), sub)
            w = w_ref[0, rows, cols]
            return tuple(acc + ab_ref[r, rows, :] * w for r, acc in enumerate(accs))

        zero = jnp.zeros((sub, LANES), F32)
        accs = lax.fori_loop(0, d // sub, k_step, (zero,) * n_rows, unroll=8)
        for r in range(n_rows):
            o_ref[0, r:r + 1, cols] = jnp.sum(accs[r], axis=0, keepdims=True) + b_ref[0, :, cols]


def _modulation(c, c_ctx, w_mod, b_mod):
    depth, d, n_out = w_mod.shape
    n_rows = c.shape[0] + 1
    assert n_rows <= MOD_ROWS
    tn = 512
    ct = jnp.zeros((MOD_ROWS, d), F32).at[:c.shape[0]].set(c).at[c.shape[0]].set(c_ctx).T
    return pl.pallas_call(
        functools.partial(_mod_kernel, n_rows=n_rows, tn=tn),
        grid=(depth, n_out // tn),
        in_specs=[
            pl.BlockSpec((d, MOD_ROWS), lambda l, j: (0, 0)),
            pl.BlockSpec((1, d, tn), lambda l, j: (l, 0, j)),
            pl.BlockSpec((1, 1, tn), lambda l, j: (l, 0, j)),
        ],
        out_specs=pl.BlockSpec((1, MOD_ROWS, tn), lambda l, j: (l, 0, j)),
        out_shape=jax.ShapeDtypeStruct((depth, MOD_ROWS, n_out), F32),
        scratch_shapes=[pltpu.VMEM((n_rows, d, LANES), F32)],
        compiler_params=_cparams("arbitrary", "arbitrary"),
        name="modulation",
    )(ct, w_mod, b_mod.reshape(depth, 1, n_out))


def _rope(xh, cos, sin_signed):
    lane = lax.broadcasted_iota(jnp.int32, xh.shape, 1)
    swapped = jnp.where((lane % 64) < 32, pltpu.roll(xh, 96, 1), pltpu.roll(xh, 32, 1))
    return xh * cos + swapped * sin_signed


def _inproj_kernel(x_ref, mod_ref, w_ref, cos_ref, sin_ref, q_ref, k_ref, v_ref, *, rope):
    sh = mod_ref[0, 0:1, :]
    sc = mod_ref[0, 1:2, :]
    u = (x_ref[0] * (1.0 + sc) + sh).astype(BF16)
    q = _dot(u, w_ref[0, :, 0:ATTN_WIDTH])
    k = _dot(u, w_ref[0, :, ATTN_WIDTH:ATTN_WIDTH + KV_WIDTH])
    v = _dot(u, w_ref[0, :, ATTN_WIDTH + KV_WIDTH:ATTN_WIDTH + 2 * KV_WIDTH])
    scale = HEAD_DIM ** -0.5
    if rope:
        cos = cos_ref[...]
        sin = sin_ref[...]
    for h in range(N_Q_HEADS):
        qh = q[:, h * HEAD_DIM:(h + 1) * HEAD_DIM]
        if rope:
            qh = _rope(qh, cos, sin)
        q_ref[0, :, h * HEAD_DIM:(h + 1) * HEAD_DIM] = (qh * scale).astype(BF16)
    for h in range(N_KV_HEADS):
        kh = k[:, h * HEAD_DIM:(h + 1) * HEAD_DIM]
        if rope:
            kh = _rope(kh, cos, sin)
        k_ref[0, :, h * HEAD_DIM:(h + 1) * HEAD_DIM] = kh.astype(BF16)
    v_ref[0] = v.astype(BF16)


def _in_projection(x, mod3, row_of_batch, w_in_bf, layer, cos_t, sin_t, rope):
    bsz, n_tok, d = x.shape
    tm = min(ROW_TILE, n_tok)
    n_cols = ATTN_WIDTH + 2 * KV_WIDTH
    out = lambda w: jax.ShapeDtypeStruct((bsz, n_tok, w), BF16)
    ospec = lambda w: pl.BlockSpec((1, tm, w), lambda b, t: (b, t, 0))
    return pl.pallas_call(
        functools.partial(_inproj_kernel, rope=rope),
        grid=(bsz, n_tok // tm),
        in_specs=[
            pl.BlockSpec((1, tm, d), lambda b, t: (b, t, 0)),
            pl.BlockSpec((1, N_MOD, d), lambda b, t: (row_of_batch(b), 0, 0)),
            pl.BlockSpec((1, d, n_cols), lambda b, t: (layer, 0, 0)),
            pl.BlockSpec((tm, HEAD_DIM), lambda b, t: (t, 0)),
            pl.BlockSpec((tm, HEAD_DIM), lambda b, t: (t, 0)),
        ],
        out_specs=[ospec(ATTN_WIDTH), ospec(KV_WIDTH), ospec(KV_WIDTH)],
        out_shape=[out(ATTN_WIDTH), out(KV_WIDTH), out(KV_WIDTH)],
        compiler_params=_cparams("parallel", "parallel"),
        name="in_projection",
    )(x, mod3, w_in_bf, cos_t, sin_t)


S5_TILE = 1024
S5_TILE_ROWS = 2 * S5_TILE // LANES


def _to_group_layout(st):
    x4 = st.reshape(SSM_G, SSM_P, S5_TILE // LANES, LANES)
    lane = lax.broadcasted_iota(jnp.int32, x4.shape[:1] + x4.shape[2:], 2)
    roll64 = lambda a: pltpu.roll(a.reshape(-1, LANES), CHUNK, 1).reshape(a.shape)
    out = []
    for k in range(SSM_P // 2):
        a, b = x4[:, 2 * k], x4[:, 2 * k + 1]
        h0 = jnp.where(lane < CHUNK, a, roll64(b))
        h1 = jnp.where(lane < CHUNK, roll64(a), b)
        out.append(jnp.concatenate([h0, h1], axis=1))
    return out


def _from_group_layout(blocks):
    n8 = S5_TILE // LANES
    lane = lax.broadcasted_iota(jnp.int32, (SSM_G, n8, LANES), 2)
    roll64 = lambda a: pltpu.roll(a.reshape(-1, LANES), CHUNK, 1).reshape(a.shape)
    chans = []
    for blk in blocks:
        a0, b1 = blk[:, 0:n8], blk[:, n8:]
        chans.append(jnp.where(lane < CHUNK, a0, roll64(b1)))
        chans.append(jnp.where(lane < CHUNK, roll64(a0), b1))
    return jnp.stack(chans, axis=1).reshape(SSM_WIDTH, S5_TILE)


def _sproj_kernel(x_ref, mod_ref, w_ref, *rest, n_steps):
    u_ref = rest[-1]

    @pl.when(pl.program_id(0) < n_steps)
    def _():
        sh = mod_ref[0, 0:1, :]
        sc = mod_ref[0, 1:2, :]
        x = x_ref[...].reshape(S5_TILE, x_ref.shape[-1])
        u = (x * (1.0 + sc) + sh).astype(BF16)
        st = _dot_nt(w_ref[0], u)
        for k, blk in enumerate(_to_group_layout(st)):
            u_ref[:, :, k * LANES:(k + 1) * LANES] = blk.astype(BF16)

    @pl.when(pl.program_id(0) >= n_steps)
    def _():
        u_ref[...] = jnp.zeros(u_ref.shape, BF16)


def _s_projection(x, mod3, mod_row, ws_t_bf, layer, n_rows, row_blk_off, u_prev=None):
    bsz, n_tok, d = x.shape
    if u_prev is None:
        assert n_tok % S5_TILE == 0
        nj = n_tok // S5_TILE
        n_steps = bsz * nj
        n_fill = n_rows // S5_TILE_ROWS - n_steps
        tile = lambda i: jnp.minimum(i, n_steps - 1)
        xspec = pl.BlockSpec((1, S5_TILE, d), lambda i: (tile(i) // nj, tile(i) % nj, 0))
        mspec = pl.BlockSpec((1, N_MOD, d), lambda i: (mod_row(tile(i) // nj), 0, 0))
    else:
        assert bsz * n_tok == S5_TILE and n_tok % LANES == 0
        n_steps, n_fill = 1, 0
        xspec = pl.BlockSpec((bsz, n_tok, d), lambda i: (0, 0, 0))
        mspec = pl.BlockSpec((1, N_MOD, d), lambda i: (mod_row(0), 0, 0))
    in_specs = [xspec, mspec, pl.BlockSpec((1, SSM_WIDTH, d), lambda i: (layer, 0, 0))]
    args = [x, mod3, ws_t_bf]
    aliases = {}
    if u_prev is not None:
        in_specs.append(pl.BlockSpec(memory_space=pl.ANY))
        args.append(u_prev)
        aliases = {3: 0}
    return pl.pallas_call(
        functools.partial(_sproj_kernel, n_steps=n_steps),
        grid=(n_steps + n_fill,),
        in_specs=in_specs,
        out_specs=pl.BlockSpec((SSM_G, S5_TILE_ROWS, GC), lambda i: (0, row_blk_off + i, 0)),
        out_shape=jax.ShapeDtypeStruct((SSM_G, n_rows, GC), BF16),
        input_output_aliases=aliases,
        compiler_params=_cparams("arbitrary"),
        name="s_projection",
    )(*args)


def _rope_tables(n_tok):
    half = HEAD_DIM // 2
    inv_freq = ROPE_THETA ** (-jnp.arange(0, half, 2, dtype=F32) / half)
    t = jnp.arange(n_tok)
    row = (t // GRID_W).astype(F32)
    col = (t % GRID_W).astype(F32)
    ang_r = row[:, None] * inv_freq[None, :]
    ang_c = col[:, None] * inv_freq[None, :]
    cos_t = jnp.concatenate([jnp.cos(ang_r), jnp.cos(ang_r), jnp.cos(ang_c), jnp.cos(ang_c)], axis=-1)
    sin_t = jnp.concatenate([-jnp.sin(ang_r), jnp.sin(ang_r), -jnp.sin(ang_c), jnp.sin(ang_c)], axis=-1)
    return cos_t, sin_t


def _attn_kernel(sink_ref, q_ref, *refs, has_band, n_blk):
    if has_band:
        kp_ref, ko_ref, kn_ref, vp_ref, vo_ref, vn_ref, kc_ref, vc_ref, g_ref, o_ref = refs
    else:
        kc_ref, vc_ref, g_ref, o_ref = refs
    blk = pl.program_id(1)
    rows = Q_PER_KV * ATTN_BLOCK
    qi = lax.broadcasted_iota(jnp.int32, (rows, ATTN_BLOCK), 0) % ATTN_BLOCK
    kj = lax.broadcasted_iota(jnp.int32, (rows, ATTN_BLOCK), 1)
    row_head = lax.broadcasted_iota(jnp.int32, (rows, 1), 0) // ATTN_BLOCK
    heads = []
    for h in range(N_KV_HEADS):
        hs = slice(h * HEAD_DIM, (h + 1) * HEAD_DIM)
        q = jnp.concatenate(
            [q_ref[0, :, (h * Q_PER_KV + g) * HEAD_DIM:(h * Q_PER_KV + g + 1) * HEAD_DIM] for g in range(Q_PER_KV)],
            axis=0)
        sink = jnp.zeros((rows, 1), F32)
        for g in range(Q_PER_KV):
            sink = jnp.where(row_head == g, sink_ref[h * Q_PER_KV + g], sink)
        s_c = _dot_nt(q, kc_ref[0, :, hs])
        m = jnp.maximum(jnp.max(s_c, axis=-1, keepdims=True), sink)
        if has_band:
            s_p = jnp.where((kj >= qi) & (blk > 0), _dot_nt(q, kp_ref[0, :, hs]), NEG_INF)
            s_o = _dot_nt(q, ko_ref[0, :, hs])
            s_n = jnp.where((kj <= qi) & (blk < n_blk - 1), _dot_nt(q, kn_ref[0, :, hs]), NEG_INF)
            m = jnp.maximum(m, jnp.max(jnp.maximum(jnp.maximum(s_p, s_o), s_n), axis=-1, keepdims=True))
        p_c = jnp.exp(s_c - m)
        denom = jnp.sum(p_c, axis=-1, keepdims=True) + jnp.exp(sink - m)
        acc = _dot(p_c.astype(BF16), vc_ref[0, :, hs])
        if has_band:
            for s_x, v_ref in ((s_p, vp_ref), (s_o, vo_ref), (s_n, vn_ref)):
                p_x = jnp.exp(s_x - m)
                denom = denom + jnp.sum(p_x, axis=-1, keepdims=True)
                acc = acc + _dot(p_x.astype(BF16), v_ref[0, :, hs])
        o = acc / denom
        for g in range(Q_PER_KV):
            heads.append(o[g * ATTN_BLOCK:(g + 1) * ATTN_BLOCK, :])
    ss = heads[0] * heads[0]
    for o in heads[1:]:
        ss = ss + o * o
    inv = lax.rsqrt(jnp.sum(ss, axis=-1, keepdims=True) / ATTN_WIDTH + LN_EPS)
    for i, o in enumerate(heads):
        cs = slice(i * HEAD_DIM, (i + 1) * HEAD_DIM)
        o_ref[0, :, cs] = (o * inv * g_ref[:, cs]).astype(BF16)


def _attention(q, k, v, kc, vc, sink, g_attn, has_band):
    bsz, n_tok, _ = q.shape
    n_ctx = kc.shape[1]
    n_blk = n_tok // ATTN_BLOCK
    qspec = pl.BlockSpec((1, ATTN_BLOCK, ATTN_WIDTH), lambda b, n: (b, n, 0))
    kvspec = lambda f: pl.BlockSpec((1, ATTN_BLOCK, KV_WIDTH), lambda b, n: (b, f(n), 0))
    cspec = pl.BlockSpec((1, n_ctx, KV_WIDTH), lambda b, n: (b, 0, 0))
    prev = lambda n: jnp.maximum(n - 1, 0)
    own = lambda n: n
    nxt = lambda n: jnp.minimum(n + 1, n_blk - 1)
    in_specs = [pl.BlockSpec(memory_space=pltpu.SMEM), qspec]
    args = [sink, q]
    if has_band:
        in_specs += [kvspec(prev), kvspec(own), kvspec(nxt), kvspec(prev), kvspec(own), kvspec(nxt)]
        args += [k, k, k, v, v, v]
    in_specs += [cspec, cspec, pl.BlockSpec((1, ATTN_WIDTH), lambda b, n: (0, 0))]
    args += [kc, vc, g_attn.reshape(1, ATTN_WIDTH)]
    return pl.pallas_call(
        functools.partial(_attn_kernel, has_band=has_band, n_blk=n_blk),
        grid=(bsz, n_blk),
        in_specs=in_specs,
        out_specs=qspec,
        out_shape=jax.ShapeDtypeStruct((bsz, n_tok, ATTN_WIDTH), BF16),
        compiler_params=_cparams("parallel", "parallel"),
        name="attention",
    )(*args)


def _cmul(a, b):
    return a[0] * b[0] - a[1] * b[1], a[0] * b[1] + a[1] * b[0]


def _cpow(rho, theta, expo):
    mag = jnp.exp(expo * rho)
    ang = expo * theta
    return mag * jnp.cos(ang), mag * jnp.sin(ang)


def _s5_param_kernel(after_ref, prow_ref, bt_ref, cr_ref, dd_ref, m_ref, w_ref, vt_ref, al_ref, kmat_ref):
    del after_ref
    p_dim, lc = SSM_P, CHUNK
    lam_re, lam_im = prow_ref[0, 0:1, :], prow_ref[0, 1:2, :]
    dt = jnp.exp(prow_ref[0, 2:3, :])
    rho, theta = lam_re * dt, lam_im * dt

    sub8 = lax.broadcasted_iota(jnp.int32, (8, LANES), 0)
    asc = _cpow(rho, theta, sub8.astype(F32))
    desc = _cpow(rho, theta, (8 - sub8).astype(F32))
    ex = jnp.where(sub8 == 0, 8, jnp.where(sub8 == 1, 16, jnp.where(sub8 == 2, 32, jnp.where(sub8 == 3, lc, 1))))
    pw = _cpow(rho, theta, ex.astype(F32))
    row = lambda t, i: (t[0][i:i + 1, :], t[1][i:i + 1, :])
    cat = lambda a, b: (jnp.concatenate([a[0], b[0]], axis=0), jnp.concatenate([a[1], b[1]], axis=0))
    for i in range(3):
        step = row(pw, i)
        asc = cat(asc, _cmul(asc, step))
        desc = cat(_cmul(desc, step), desc)
    a_chunk, a_one = row(pw, 3), row(pw, 4)

    den = lam_re * lam_re + lam_im * lam_im
    x_re, x_im = a_one[0] - 1.0, a_one[1]
    beta = ((x_re * lam_re + x_im * lam_im) / den, (x_im * lam_re - x_re * lam_im) / den)
    bb = _cmul(beta, (bt_ref[0, 0], bt_ref[0, 1]))
    cc = (cr_ref[0, 0], cr_ref[0, 1])

    is_fwd = lax.broadcasted_iota(jnp.int32, (lc, LANES), 1) < SSM_N
    sub = lax.broadcasted_iota(jnp.int32, (lc, LANES), 0)
    pw_w = (jnp.where(is_fwd, desc[0], asc[0]), jnp.where(is_fwd, desc[1], asc[1]))
    pw_v = (jnp.where(is_fwd, asc[0], desc[0]), jnp.where(is_fwd, asc[1], desc[1]))

    top = (jnp.where(is_fwd, asc[0], jnp.where(sub == 0, 1.0, 0.0)), jnp.where(is_fwd, asc[1], 0.0))
    keep = (~is_fwd) & (sub > 0)
    bot = (jnp.where(keep, desc[0], 0.0), jnp.where(keep, desc[1], 0.0))
    lag = cat(top, bot)

    cx = (jnp.concatenate([cc[0]] * p_dim, axis=0), jnp.concatenate([cc[1]] * p_dim, axis=0))
    bx = tuple(jnp.concatenate([jnp.broadcast_to(b[q:q + 1, :], (p_dim, LANES)) for q in range(p_dim)], axis=0)
               for b in bb)
    e_re, e_im = _cmul(cx, bx)
    hi = lax.Precision.HIGHEST
    kmat = (jnp.dot(e_re, lag[0].T, preferred_element_type=F32, precision=hi)
            - jnp.dot(e_im, lag[1].T, preferred_element_type=F32, precision=hi))
    lane0 = lax.broadcasted_iota(jnp.int32, kmat.shape, 1) == 0
    kmat_ref[...] = kmat + jnp.where(lane0, dd_ref[0], 0.0)

    lane_m = lax.broadcasted_iota(jnp.int32, (lc, LANES), 1)

    def q_body(q, carry):
        for pp in range(p_dim // 2):
            r0 = q * p_dim + 2 * pp
            ka = jnp.broadcast_to(kmat_ref[pl.ds(r0, 1), :], (lc, LANES))
            kb = jnp.broadcast_to(kmat_ref[pl.ds(r0 + 1, 1), :], (lc, LANES))
            ra = pltpu.roll(ka, 0, 1, stride=1, stride_axis=0)
            rb = pltpu.roll(kb, lc, 1, stride=1, stride_axis=0)
            blk = jnp.where(lane_m < lc, ra, rb)
            m_ref[0, pl.ds(pl.multiple_of(q * lc, lc), lc), pp * LANES:(pp + 1) * LANES] = blk.astype(BF16)
        return carry

    lax.fori_loop(0, p_dim, q_body, 0)

    for q in range(p_dim):
        w_re, w_im = _cmul(pw_w, row(bb, q))
        w_ref[0, q * lc:(q + 1) * lc, :] = jnp.concatenate([w_re, w_im], axis=1).astype(BF16)
    for p in range(p_dim):
        v_re, v_im = _cmul(pw_v, row(cc, p))
        vt_ref[0, p * lc:(p + 1) * lc, :] = jnp.concatenate([v_re, -v_im], axis=1).astype(BF16)
    al_ref[0] = jnp.concatenate([a_chunk[0], a_chunk[1]], axis=1)


def _s5_operators(lam_re, lam_im, log_dt, b_re, b_im, c_re, c_im, d_skip, layer, after):
    depth = lam_re.shape[0]
    dg = depth * SSM_G
    g0 = layer * SSM_G
    vec = lambda a: jnp.moveaxis(a, 1, 2).reshape(dg, LANES)
    mat = lambda a: jnp.moveaxis(a, 1, 3).reshape(dg, SSM_P, LANES)
    ldt = jnp.broadcast_to(log_dt[..., None], lam_re.shape)
    prow = jnp.stack([vec(lam_re), vec(lam_im), vec(ldt)], axis=1)
    bt = jnp.stack([mat(jnp.swapaxes(b_re, 3, 4)), mat(jnp.swapaxes(b_im, 3, 4))], axis=1)
    cr = jnp.stack([mat(c_re), mat(c_im)], axis=1)
    dd = (d_skip.reshape(dg, 1, SSM_P) * jnp.eye(SSM_P, dtype=F32)[None]).reshape(dg, SSM_P * SSM_P, 1)
    iblk = lambda *s: pl.BlockSpec((1,) + s, lambda i: (g0 + i,) + (0,) * len(s))
    oblk = lambda *s: pl.BlockSpec((1,) + s, lambda i: (i,) + (0,) * len(s))
    return pl.pallas_call(
        _s5_param_kernel,
        grid=(SSM_G,),
        in_specs=[pl.BlockSpec(memory_space=pltpu.SMEM),
                  iblk(3, LANES), iblk(2, SSM_P, LANES), iblk(2, SSM_P, LANES), iblk(SSM_P * SSM_P, 1)],
        out_specs=[oblk(GC, GC), oblk(GC, 2 * LANES), oblk(GC, 2 * LANES), oblk(1, 2 * LANES)],
        out_shape=[jax.ShapeDtypeStruct((SSM_G, GC, GC), BF16),
                   jax.ShapeDtypeStruct((SSM_G, GC, 2 * LANES), BF16),
                   jax.ShapeDtypeStruct((SSM_G, GC, 2 * LANES), BF16),
                   jax.ShapeDtypeStruct((SSM_G, 1, 2 * LANES), F32)],
        scratch_shapes=[pltpu.VMEM((SSM_P * SSM_P, LANES), F32)],
        compiler_params=_cparams("parallel"),
        name="s5_operators",
    )(after, prow, bt, cr, dd)


def _s5_sum_kernel(u_ref, w_ref, s_ref):
    s_ref[0] = _dot(u_ref[0], w_ref[0])


def _s5_scan_kernel(s_ref, al_ref, h_ref, *, n_ctx_chunks, n_chunks):
    a_re, a_im = al_ref[:, :LANES], al_ref[:, LANES:]
    is_fwd = lax.broadcasted_iota(jnp.int32, a_re.shape, 1) < SSM_N

    def body(i, carry):
        h_re, h_im = carry
        cf = i
        cr = jnp.where(i < n_ctx_chunks, n_ctx_chunks - 1 - i, n_chunks - 1 - (i - n_ctx_chunks))
        s_f, s_r = s_ref[cf], s_ref[cr]
        h_ref[cf, :, 0:SSM_N] = h_re[:, 0:SSM_N]
        h_ref[cr, :, SSM_N:LANES] = h_re[:, SSM_N:LANES]
        h_ref[cf, :, LANES:LANES + SSM_N] = h_im[:, 0:SSM_N]
        h_ref[cr, :, LANES + SSM_N:] = h_im[:, SSM_N:LANES]
        s_re = jnp.where(is_fwd, s_f[:, :LANES], s_r[:, :LANES])
        s_im = jnp.where(is_fwd, s_f[:, LANES:], s_r[:, LANES:])
        return a_re * h_re - a_im * h_im + s_re, a_re * h_im + a_im * h_re + s_im

    zero = jnp.zeros(a_re.shape, F32)
    lax.fori_loop(0, n_chunks, body, (zero, zero))


def _s5_out_kernel(u_ref, m_ref, h_ref, vt_ref, y_ref):
    y = _dot(u_ref[0], m_ref[0]) + _dot_nt(h_ref[0].astype(BF16), vt_ref[0])
    y_ref[0] = y.astype(BF16)


def _rows_to_chunks(a, bsz, n_tok, n_ctx):
    n8, nj, ncb, st = S5_TILE // LANES, n_tok // S5_TILE, n_ctx // LANES, a.shape[-1]
    n_lat = bsz * nj * S5_TILE_ROWS
    lat = a[:, :n_lat].reshape(SSM_G, bsz, nj, 2, n8, st).transpose(2, 4, 3, 0, 1, 5).reshape(-1, SSM_G * bsz, st)
    ctx = a[:, n_lat:].reshape(SSM_G, 2, bsz, ncb, st).transpose(3, 1, 0, 2, 4).reshape(-1, SSM_G * bsz, st)
    return jnp.concatenate([ctx, lat], axis=0)


def _chunks_to_rows(a, bsz, n_tok, n_ctx):
    n8, nj, ncb, st = S5_TILE // LANES, n_tok // S5_TILE, n_ctx // LANES, a.shape[-1]
    ncc = n_ctx // CHUNK
    lat = a[ncc:].reshape(nj, n8, 2, SSM_G, bsz, st).transpose(3, 4, 0, 2, 1, 5).reshape(SSM_G, -1, st)
    ctx = a[:ncc].reshape(ncb, 2, SSM_G, bsz, st).transpose(2, 1, 3, 0, 4).reshape(SSM_G, -1, st)
    return jnp.concatenate([lat, ctx], axis=1)


def _s5_apply(u, bsz, n_tok, n_ctx, ops):
    m_op, w_op, vt_op, al = ops
    ncc, n_chunks = n_ctx // CHUNK, (n_ctx + n_tok) // CHUNK
    rows = u.shape[1]
    assert rows == bsz * n_chunks
    gspec = lambda r, c: pl.BlockSpec((1, r, c), lambda g: (g, 0, 0))
    ospec = gspec
    sums = pl.pallas_call(
        _s5_sum_kernel,
        grid=(SSM_G,),
        in_specs=[gspec(rows, GC), ospec(GC, 2 * LANES)],
        out_specs=gspec(rows, 2 * LANES),
        out_shape=jax.ShapeDtypeStruct((SSM_G, rows, 2 * LANES), F32),
        compiler_params=_cparams("parallel"),
        name="s5_chunk_sums",
    )(u, w_op)
    sums_t = _rows_to_chunks(sums, bsz, n_tok, n_ctx)
    al_rows = jnp.broadcast_to(al, (SSM_G, bsz, 2 * LANES)).reshape(SSM_G * bsz, 2 * LANES)
    rt = 64
    states_t = pl.pallas_call(
        functools.partial(_s5_scan_kernel, n_ctx_chunks=ncc, n_chunks=n_chunks),
        grid=(SSM_G * bsz // rt,),
        in_specs=[pl.BlockSpec((n_chunks, rt, 2 * LANES), lambda r: (0, r, 0)),
                  pl.BlockSpec((rt, 2 * LANES), lambda r: (r, 0))],
        out_specs=pl.BlockSpec((n_chunks, rt, 2 * LANES), lambda r: (0, r, 0)),
        out_shape=jax.ShapeDtypeStruct((n_chunks, SSM_G * bsz, 2 * LANES), F32),
        compiler_params=_cparams("parallel"),
        name="s5_state_scan",
    )(sums_t, al_rows)
    states = _chunks_to_rows(states_t, bsz, n_tok, n_ctx)
    return pl.pallas_call(
        _s5_out_kernel,
        grid=(SSM_G,),
        in_specs=[gspec(rows, GC), ospec(GC, GC), gspec(rows, 2 * LANES), ospec(GC, 2 * LANES)],
        out_specs=gspec(rows, GC),
        out_shape=jax.ShapeDtypeStruct((SSM_G, rows, GC), BF16),
        compiler_params=_cparams("parallel"),
        name="s5_chunk_outputs",
    )(u, m_op, states, vt_op)


def _glu_kernel(y_ref, w_ref, b_ref, g_ref, o_ref):
    blocks = [y_ref[:, :, k * LANES:(k + 1) * LANES].astype(F32) for k in range(SSM_P // 2)]
    y = _from_group_layout(blocks).T
    z = 0.5 * y * (1.0 + jnp.tanh(math.sqrt(2.0 / math.pi) * (y + 0.044715 * (y * y * y))))
    t = _dot(z.astype(BF16), w_ref[0]) + b_ref[...]
    o = z * jax.nn.sigmoid(t)
    inv = lax.rsqrt(jnp.mean(o * o, axis=-1, keepdims=True) + LN_EPS)
    o_ref[...] = (o * inv * g_ref[...]).astype(BF16).reshape(o_ref.shape)


def _glu(y, bsz, n_tok, row_blk_off, per_batch, w_glu_bf, layer, b_glu, g_ssm):
    w = SSM_WIDTH
    if per_batch:
        grid = (bsz, n_tok // S5_TILE)
        ospec = pl.BlockSpec((1, S5_TILE, w), lambda b, j: (b, j, 0))
    else:
        assert bsz * n_tok == S5_TILE
        grid = (1, 1)
        ospec = pl.BlockSpec((bsz, n_tok, w), lambda b, j: (0, 0, 0))
    nj = grid[1]
    row = lambda a: a.reshape(1, w)
    vspec = pl.BlockSpec((1, w), lambda b, j: (0, 0))
    return pl.pallas_call(
        _glu_kernel,
        grid=grid,
        in_specs=[pl.BlockSpec((SSM_G, S5_TILE_ROWS, GC), lambda b, j: (0, row_blk_off + b * nj + j, 0)),
                  pl.BlockSpec((1, w, w), lambda b, j: (layer, 0, 0)), vspec, vspec],
        out_specs=ospec,
        out_shape=jax.ShapeDtypeStruct((bsz, n_tok, w), BF16),
        compiler_params=_cparams("parallel", "parallel"),
        name="s5_glu",
    )(y, w_glu_bf, row(b_glu), row(g_ssm))


def _layer_norm(z, g, b):
    mu = jnp.mean(z, axis=-1, keepdims=True)
    zc = z - mu
    var = jnp.mean(zc * zc, axis=-1, keepdims=True)
    return zc * lax.rsqrt(var + LN_EPS) * g + b


def _first_argmax(vals):
    best_i = jnp.zeros(vals[0].shape, jnp.int32)
    best_v = vals[0]
    for j in range(1, len(vals)):
        better = vals[j] > best_v
        best_i = jnp.where(better, j, best_i)
        best_v = jnp.where(better, vals[j], best_v)
    return best_i, best_v


def _route(logit_rows):
    m = functools.reduce(jnp.maximum, logit_rows)
    p = [jnp.exp(l - m) for l in logit_rows]
    scores = []
    for g in range(N_EXPERT_GROUPS):
        a, b, c, d = p[4 * g:4 * g + 4]
        hi1, lo1, hi2, lo2 = jnp.maximum(a, b), jnp.minimum(a, b), jnp.maximum(c, d), jnp.minimum(c, d)
        scores.append(jnp.maximum(hi1, hi2) + jnp.maximum(jnp.minimum(hi1, hi2), jnp.maximum(lo1, lo2)))
    grp, _ = _first_argmax(scores)
    sel = []
    for j in range(EXPERTS_PER_GROUP):
        v = p[j]
        for g in range(1, N_EXPERT_GROUPS):
            v = jnp.where(grp == g, p[4 * g + j], v)
        sel.append(v)
    i1, v1 = _first_argmax(sel)
    i2, v2 = _first_argmax([jnp.where(i1 == j, -1.0, sel[j]) for j in range(EXPERTS_PER_GROUP)])
    tot = v1 + v2
    return grp * EXPERTS_PER_GROUP + i1, grp * EXPERTS_PER_GROUP + i2, v1 / tot, v2 / tot


def _outproj_kernel(a_ref, s_ref, x_ref, mod_ref, w_ref, lng_ref, lnb_ref, wr_ref, br_ref, *rest, alpha, n_steps):
    x1_ref, u2_ref, ri_ref, rw_ref, acc_even, acc_odd = rest[-6:]
    i = pl.program_id(0)

    @pl.when(i == 0)
    def _():
        acc_odd[...] = jnp.zeros(acc_odd.shape, F32)

    def step(acc_mine, acc_prev):
        acc_mine[...] = _dot(a_ref[0], w_ref[0, 0:ATTN_WIDTH, :]) + _dot(s_ref[0], w_ref[0, ATTN_WIDTH:, :])
        y = acc_prev[...]
        g1 = mod_ref[0, 2:3, :]
        sh2 = mod_ref[0, 3:4, :]
        sc2 = mod_ref[0, 4:5, :]
        x1 = _layer_norm(alpha * x_ref[0] + g1 * y, lng_ref[...], lnb_ref[...])
        x1_ref[0] = x1
        u2 = x1 * (1.0 + sc2) + sh2
        u2_ref[...] = u2
        logits = _dot_nt(wr_ref[...], u2.astype(BF16)) + br_ref[...]
        e1, e2, w1, w2 = _route([logits[e:e + 1, :] for e in range(N_EXPERTS)])
        ri_ref[0, 0:1, :] = e1
        ri_ref[0, 1:2, :] = e2
        rw_ref[0, 0:1, :] = w1
        rw_ref[0, 1:2, :] = w2

    @pl.when((i <= n_steps) & (i % 2 == 0))
    def _():
        step(acc_even, acc_odd)

    @pl.when((i <= n_steps) & (i % 2 == 1))
    def _():
        step(acc_odd, acc_even)

    @pl.when(i > n_steps)
    def _():
        u2_ref[...] = jnp.zeros(u2_ref.shape, F32)


def _out_projection(attn_n, ssm_n, x, mod3, row_of_batch, w_out_bf, layer, ln_g, ln_b, w_router_t_bf, b_router,
                    alpha, u2_rows, row_off, u2_prev=None):
    bsz, n_tok, d = x.shape
    tm = min(ROW_TILE, n_tok)
    nt = n_tok // tm
    n_steps = bsz * nt
    assert row_off % tm == 0 and u2_rows % tm == 0
    n_fill = 0 if u2_prev is not None else u2_rows // tm - n_steps
    cur = lambda i: jnp.minimum(i, n_steps - 1)
    fin = lambda i: jnp.clip(i - 1, 0, n_steps - 1)
    pspec = lambda w: pl.BlockSpec((1, tm, w), lambda i: (cur(i) // nt, cur(i) % nt, 0))
    fspec = lambda w: pl.BlockSpec((1, tm, w), lambda i: (fin(i) // nt, fin(i) % nt, 0))
    vspec = pl.BlockSpec((1, d), lambda i: (0, 0))
    rspec = pl.BlockSpec((1, 2, tm), lambda i: (fin(i) // nt, 0, fin(i) % nt))
    in_specs = [
        pspec(ATTN_WIDTH), pspec(SSM_WIDTH), fspec(d),
        pl.BlockSpec((1, N_MOD, d), lambda i: (row_of_batch(fin(i) // nt), 0, 0)),
        pl.BlockSpec((1,) + w_out_bf.shape[1:], lambda i: (layer, 0, 0)),
        vspec, vspec,
        pl.BlockSpec((N_EXPERTS, d), lambda i: (0, 0)),
        pl.BlockSpec((N_EXPERTS, 1), lambda i: (0, 0)),
    ]
    args = [attn_n, ssm_n, x, mod3, w_out_bf, ln_g.reshape(1, d), ln_b.reshape(1, d), w_router_t_bf,
            b_router.reshape(N_EXPERTS, 1)]
    aliases = {}
    if u2_prev is not None:
        in_specs.append(pl.BlockSpec(memory_space=pl.ANY))
        args.append(u2_prev)
        aliases = {len(args) - 1: 1}
    return pl.pallas_call(
        functools.partial(_outproj_kernel, alpha=alpha, n_steps=n_steps),
        grid=(n_steps + 1 + n_fill,),
        in_specs=in_specs,
        out_specs=[fspec(d), pl.BlockSpec((tm, d), lambda i: (row_off // tm + jnp.maximum(i - 1, 0), 0)),
                   rspec, rspec],
        out_shape=[jax.ShapeDtypeStruct((bsz, n_tok, d), F32), jax.ShapeDtypeStruct((u2_rows, d), F32),
                   jax.ShapeDtypeStruct((bsz, 2, n_tok), jnp.int32), jax.ShapeDtypeStruct((bsz, 2, n_tok), F32)],
        scratch_shapes=[pltpu.VMEM((tm, d), F32), pltpu.VMEM((tm, d), F32)],
        input_output_aliases=aliases,
        compiler_params=_cparams("arbitrary"),
        name="out_projection",
    )(*args)


CAST_ROWS = 256
GATHER_AHEAD = 2


def _ffn_kernel(te_ref, nx_ref, nu_ref, src_ref, u_hbm, wg_hbm, wu_hbm, wd_hbm, y_ref,
                xbuf, stage_g, stage_u, stage_d, wg_bf, wu_bf, wd_bf, sem, gsem, *, e0):
    t = pl.program_id(0)
    e = te_ref[t]
    tm = xbuf.shape[1]
    pairs = ((wg_hbm, stage_g, wg_bf), (wu_hbm, stage_u, wu_bf), (wd_hbm, stage_d, wd_bf))

    def fetch(expert):
        return [pltpu.make_async_copy(hbm.at[e0 + expert], stage, sem.at[i])
                for i, (hbm, stage, _) in enumerate(pairs)]

    def gather_row(base, i, slot):
        r = src_ref[base + i]
        pltpu.make_async_copy(u_hbm.at[pl.ds(r, 1)], xbuf.at[slot, pl.ds(i, 1)], gsem.at[slot]).start()

    def wait_rows(slot):
        pltpu.make_async_copy(u_hbm.at[pl.ds(0, tm)], xbuf.at[slot], gsem.at[slot]).wait()

    last_tile = pl.num_programs(0) - 1

    @pl.when(t == 0)
    def _():
        for ahead in range(GATHER_AHEAD):
            def one_row(i, carry, ahead=ahead):
                gather_row(jnp.minimum(ahead, last_tile) * tm, i, ahead)
                return carry
            lax.fori_loop(0, tm, one_row, 0, unroll=8)
        for cp in fetch(e):
            cp.start()

    first_of_run = (t == 0) | (te_ref[jnp.maximum(t - 1, 0)] != e)

    @pl.when(first_of_run & (t < nu_ref[0]))
    def _():
        for cp in fetch(e):
            cp.wait()
        for _, stage, dst in pairs:
            def cast_rows(i, carry, stage=stage, dst=dst):
                rows = pl.ds(pl.multiple_of(i * CAST_ROWS, CAST_ROWS), CAST_ROWS)
                dst[rows, :] = stage[rows, :].astype(BF16)
                return carry
            lax.fori_loop(0, stage.shape[0] // CAST_ROWS, cast_rows, 0)

        @pl.when(nx_ref[t] >= 0)
        def _():
            for cp in fetch(nx_ref[t]):
                cp.start()

    @pl.when(t < nu_ref[0])
    def _():
        n_slots = GATHER_AHEAD + 1
        slot = t % n_slots
        wait_rows(slot)
        x = xbuf[slot].astype(BF16)
        base = jnp.minimum(t + GATHER_AHEAD, last_tile) * tm
        for i in range(tm):
            gather_row(base, i, (t + GATHER_AHEAD) % n_slots)
        g = _dot(x, wg_bf[...])
        u = _dot(x, wu_bf[...])
        h = (g * jax.nn.sigmoid(g) * u).astype(BF16)
        y_ref[...] = _dot(h, wd_bf[...]).astype(BF16)

    @pl.when(t == nu_ref[0] - 1)
    def _():
        for ahead in range(1, GATHER_AHEAD + 1):
            wait_rows((t + ahead) % (GATHER_AHEAD + 1))

    @pl.when(t >= nu_ref[0])
    def _():
        y_ref[...] = jnp.zeros(y_ref.shape, BF16)


def _expert_ffn(u2, row_src, tile_expert, next_expert, n_used, w_gate, w_up, w_down, layer):
    d = u2.shape[1]
    n_rows = row_src.shape[0]
    n_e, f = w_gate.shape[1], w_gate.shape[3]
    tm = MOE_TILE
    flat = lambda w: w.reshape((-1,) + w.shape[2:])
    hbm = pl.BlockSpec(memory_space=pl.ANY)
    grid_spec = pltpu.PrefetchScalarGridSpec(
        num_scalar_prefetch=4,
        grid=(n_rows // tm,),
        in_specs=[hbm, hbm, hbm, hbm],
        out_specs=pl.BlockSpec((tm, d), lambda t, te, nx, nu, src: (t, 0)),
        scratch_shapes=[pltpu.VMEM((GATHER_AHEAD + 1, tm, d), F32),
                        pltpu.VMEM((d, f), F32), pltpu.VMEM((d, f), F32), pltpu.VMEM((f, d), F32),
                        pltpu.VMEM((d, f), BF16), pltpu.VMEM((d, f), BF16), pltpu.VMEM((f, d), BF16),
                        pltpu.SemaphoreType.DMA((3,)), pltpu.SemaphoreType.DMA((GATHER_AHEAD + 1,))],
    )
    return pl.pallas_call(
        functools.partial(_ffn_kernel, e0=layer * n_e),
        grid_spec=grid_spec,
        out_shape=jax.ShapeDtypeStruct((n_rows, d), BF16),
        compiler_params=_cparams("arbitrary"),
        name="expert_ffn",
    )(tile_expert, next_expert, n_used, row_src, u2, flat(w_gate), flat(w_up), flat(w_down))


def _dispatch_plan(e1, e2, tm):
    n = e1.shape[0]
    n_tiles = (2 * n + N_EXPERTS * (tm - 1) + tm - 1) // tm
    e = jnp.concatenate([e1, e2])
    onehot = (e[:, None] == jnp.arange(N_EXPERTS, dtype=jnp.int32)[None, :]).astype(jnp.int32)
    csum = jnp.cumsum(onehot, axis=0)
    pos_in_e = jnp.sum((csum - 1) * onehot, axis=1)
    counts = csum[-1]
    padded = ((counts + tm - 1) // tm) * tm
    ends = jnp.cumsum(padded)
    offs = ends - padded
    dest = offs[e] + pos_in_e
    tok = jnp.concatenate([jnp.arange(n, dtype=jnp.int32)] * 2)
    row_src = jnp.zeros((n_tiles * tm,), jnp.int32).at[dest].set(tok, mode="promise_in_bounds", unique_indices=True)
    n_used = (ends[-1] // tm).astype(jnp.int32)
    tile_start = jnp.arange(n_tiles, dtype=jnp.int32) * tm
    tile_e = jnp.sum((tile_start[:, None] >= ends[None, :]).astype(jnp.int32), axis=1)
    tile_e = jnp.minimum(tile_e, N_EXPERTS - 1)
    last_e = tile_e[jnp.maximum(n_used - 1, 0)]
    tile_e = jnp.where(jnp.arange(n_tiles) < n_used, tile_e, last_e).astype(jnp.int32)
    ids = jnp.arange(N_EXPERTS, dtype=jnp.int32)
    later = (ids[None, :] > ids[:, None]) & (counts[None, :] > 0)
    next_of = jnp.min(jnp.where(later, ids[None, :], N_EXPERTS), axis=1)
    next_of = jnp.where(next_of == N_EXPERTS, -1, next_of).astype(jnp.int32)
    return row_src, dest[:n], dest[n:], tile_e, next_of[tile_e], n_used.reshape(1)


def _take_rows(a, idx):
    return a.at[idx].get(mode="promise_in_bounds")


def _final_kernel(x_ref, y1_ref, y2_ref, w1_ref, w2_ref, mod_ref, lng_ref, lnb_ref, o_ref, *, alpha):
    f = w1_ref[...] * y1_ref[...].astype(F32) + w2_ref[...] * y2_ref[...].astype(F32)
    g2 = mod_ref[0, 5:6, :]
    o_ref[0] = _layer_norm(alpha * x_ref[0] + g2 * f, lng_ref[...], lnb_ref[...])


def _combine_ln(x1, y1, y2, w1, w2, row_off, mod3, row_of_batch, ln_g, ln_b, alpha):
    bsz, n_tok, d = x1.shape
    tm = min(ROW_TILE, n_tok)
    nt = n_tok // tm
    assert row_off % tm == 0
    tspec = pl.BlockSpec((1, tm, d), lambda b, t: (b, t, 0))
    yspec = pl.BlockSpec((tm, d), lambda b, t: (row_off // tm + b * nt + t, 0))
    wspec = pl.BlockSpec((tm, 1), lambda b, t: (row_off // tm + b * nt + t, 0))
    vspec = pl.BlockSpec((1, d), lambda b, t: (0, 0))
    return pl.pallas_call(
        functools.partial(_final_kernel, alpha=alpha),
        grid=(bsz, nt),
        in_specs=[tspec, yspec, yspec, wspec, wspec,
                  pl.BlockSpec((1, N_MOD, d), lambda b, t: (row_of_batch(b), 0, 0)), vspec, vspec],
        out_specs=tspec,
        out_shape=jax.ShapeDtypeStruct((bsz, n_tok, d), F32),
        compiler_params=_cparams("parallel", "parallel"),
        name="combine_post_ln",
    )(x1, y1, y2, w1, w2, mod3, ln_g.reshape(1, d), ln_b.reshape(1, d))


def kernel(x, c, ctx, c_ctx, w_mod, b_mod, w_in, attn_sink, ssm_lambda_re, ssm_lambda_im, ssm_log_dt, ssm_b_re, ssm_b_im, ssm_c_re, ssm_c_im, ssm_d, w_glu, b_glu, g_attn_out, g_ssm_out, w_out, ln1_g, ln1_b, w_router, b_router, w_expert_gate, w_expert_up, w_expert_down, ln2_g, ln2_b):
    depth = w_mod.shape[0]
    bsz, n_tok, d = x.shape
    n_ctx = ctx.shape[1]
    alpha = (2 * depth) ** 0.25
    assert n_tok % ROW_TILE == 0 and n_tok % ATTN_BLOCK == 0 and n_ctx % ATTN_BLOCK == 0 and n_ctx % CHUNK == 0

    mod = _modulation(c, c_ctx, w_mod, b_mod)
    s5_params = (ssm_lambda_re, ssm_lambda_im, ssm_log_dt, ssm_b_re, ssm_b_im, ssm_c_re, ssm_c_im, ssm_d)
    expert_w = (w_expert_gate, w_expert_up, w_expert_down)
    s5_ops = _s5_operators(*s5_params, 0, jnp.zeros((1,), jnp.int32))
    cos_t, sin_t = _rope_tables(n_tok)
    w_router_t = w_router.T.astype(BF16)
    w_in_bf, w_glu_bf, w_out_bf = w_in.astype(BF16), w_glu.astype(BF16), w_out.astype(BF16)
    ws_t_bf = jnp.swapaxes(w_in[:, :, ATTN_WIDTH + 2 * KV_WIDTH:], 1, 2).astype(BF16)
    lat_row = lambda b: b
    ctx_row = lambda b: bsz
    n_lat, n_c = bsz * n_tok, bsz * n_ctx
    lat_blocks = n_lat // S5_TILE
    s5_rows = (lat_blocks + 1) * S5_TILE_ROWS

    xc = ctx
    for i in range(depth):
        last = i == depth - 1
        mod3 = mod[i].reshape(MOD_ROWS, N_MOD, d)
        n_moe = n_lat if last else n_lat + n_c

        q, k, v = _in_projection(x, mod3, lat_row, w_in_bf, i, cos_t, sin_t, True)
        qc, kc, vc = _in_projection(xc, mod3, ctx_row, w_in_bf, i, cos_t, sin_t, False)
        u = _s_projection(x, mod3, lat_row, ws_t_bf, i, s5_rows, 0)
        u = _s_projection(xc, mod3, ctx_row, ws_t_bf, i, s5_rows, lat_blocks, u_prev=u)
        attn_n = _attention(q, k, v, kc, vc, attn_sink[i], g_attn_out[i], True)
        y_s5 = _s5_apply(u, bsz, n_tok, n_ctx, s5_ops)
        ssm_n = _glu(y_s5, bsz, n_tok, 0, True, w_glu_bf, i, b_glu[i], g_ssm_out[i])
        x1, u2, ri, rw = _out_projection(attn_n, ssm_n, x, mod3, lat_row, w_out_bf, i, ln1_g[i], ln1_b[i],
                                         w_router_t, b_router, alpha, n_moe, 0)
        if not last:
            attn_c = _attention(qc, None, None, kc, vc, attn_sink[i], g_attn_out[i], False)
            ssm_c = _glu(y_s5, bsz, n_ctx, lat_blocks, False, w_glu_bf, i, b_glu[i], g_ssm_out[i])
            xc1, u2, ric, rwc = _out_projection(attn_c, ssm_c, xc, mod3, ctx_row, w_out_bf, i, ln1_g[i], ln1_b[i],
                                                w_router_t, b_router, alpha, n_moe, n_lat, u2_prev=u2)
            ri = jnp.concatenate([ri.transpose(1, 0, 2).reshape(2, n_lat), ric.transpose(1, 0, 2).reshape(2, n_c)], axis=1)
            rw = jnp.concatenate([rw.transpose(1, 0, 2).reshape(2, n_lat), rwc.transpose(1, 0, 2).reshape(2, n_c)], axis=1)
        else:
            ri = ri.transpose(1, 0, 2).reshape(2, n_lat)
            rw = rw.transpose(1, 0, 2).reshape(2, n_lat)

        row_src, pos1, pos2, tile_e, next_e, n_used = _dispatch_plan(ri[0], ri[1], MOE_TILE)
        if not last:
            s5_ops = _s5_operators(*s5_params, i + 1, n_used)
        ys = _expert_ffn(u2, row_src, tile_e, next_e, n_used, *expert_w, i)
        y1 = _take_rows(ys, pos1)
        y2 = _take_rows(ys, pos2)
        cw1, cw2 = rw[0].reshape(n_moe, 1), rw[1].reshape(n_moe, 1)
        x = _combine_ln(x1, y1, y2, cw1, cw2, 0, mod3, lat_row, ln2_g[i], ln2_b[i], alpha)
        if not last:
            xc = _combine_ln(xc1, y1, y2, cw1, cw2, n_lat, mod3, ctx_row, ln2_g[i], ln2_b[i], alpha)
    return x
```

```python
import functools
import math

import jax
import jax.numpy as jnp
from jax import lax
from jax.experimental import pallas as pl
from jax.experimental.pallas import tpu as pltpu

F32 = jnp.float32
BF16 = jnp.bfloat16

HEAD_DIM = 128
N_Q_HEADS = 8
N_KV_HEADS = 2
Q_PER_KV = N_Q_HEADS // N_KV_HEADS
ATTN_WIDTH = N_Q_HEADS * HEAD_DIM
KV_WIDTH = N_KV_HEADS * HEAD_DIM
SSM_WIDTH = 1024
ATTN_BLOCK = 128
GRID_W = 64
ROPE_THETA = 10000.0
SSM_P = 16
SSM_G = SSM_WIDTH // SSM_P
SSM_N = 64
CHUNK = 64
GC = SSM_P * CHUNK
N_EXPERTS = 16
N_EXPERT_GROUPS = 4
EXPERTS_PER_GROUP = 4
N_MOD = 6
LN_EPS = 1e-5
NEG_INF = -1e30
LANES = 128
MOD_ROWS = 8
ROW_TILE = 256
MOE_TILE = 256
VMEM_LIMIT = 56 * 1024 * 1024

assert 2 * SSM_N == LANES and 2 * CHUNK == LANES


def _cparams(*sem):
    return pltpu.CompilerParams(dimension_semantics=sem, vmem_limit_bytes=VMEM_LIMIT)


def _dot(a, b):
    return jnp.dot(a, b, preferred_element_type=F32)


def _dot_nt(a, b):
    return lax.dot_general(a, b, (((1,), (1,)), ((), ())), preferred_element_type=F32)


def _mod_kernel(ct_ref, w_ref, b_ref, o_ref, ab_ref, *, n_rows, tn):
    d = ct_ref.shape[0]

    @pl.when((pl.program_id(0) == 0) & (pl.program_id(1) == 0))
    def _():
        ct = ct_ref[...]
        a = ct * jax.nn.sigmoid(ct)
        for r in range(n_rows):
            ab_ref[r] = jnp.broadcast_to(a[:, r:r + 1], (d, LANES))

    o_ref[...] = jnp.zeros(o_ref.shape, F32)
    sub = 8
    for j in range(tn // LANES):
        cols = slice(j * LANES, (j + 1) * LANES)

        def k_step(kc, accs, cols=cols):
            rows = pl.ds(pl.multiple_of(kc * sub, sub), sub)
            w = w_ref[0, rows, cols]
            return tuple(acc + ab_ref[r, rows, :] * w for r, acc in enumerate(accs))

        zero = jnp.zeros((sub, LANES), F32)
        accs = lax.fori_loop(0, d // sub, k_step, (zero,) * n_rows, unroll=8)
        for r in range(n_rows):
            o_ref[0, r:r + 1, cols] = jnp.sum(accs[r], axis=0, keepdims=True) + b_ref[0, :, cols]


def _modulation(c, c_ctx, w_mod, b_mod):
    depth, d, n_out = w_mod.shape
    n_rows = c.shape[0] + 1
    assert n_rows <= MOD_ROWS
    tn = 512
    ct = jnp.zeros((MOD_ROWS, d), F32).at[:c.shape[0]].set(c).at[c.shape[0]].set(c_ctx).T
    return pl.pallas_call(
        functools.partial(_mod_kernel, n_rows=n_rows, tn=tn),
        grid=(depth, n_out // tn),
        in_specs=[
            pl.BlockSpec((d, MOD_ROWS), lambda l, j: (0, 0)),
            pl.BlockSpec((1, d, tn), lambda l, j: (l, 0, j)),
            pl.BlockSpec((1, 1, tn), lambda l, j: (l, 0, j)),
        ],
        out_specs=pl.BlockSpec((1, MOD_ROWS, tn), lambda l, j: (l, 0, j)),
        out_shape=jax.ShapeDtypeStruct((depth, MOD_ROWS, n_out), F32),
        scratch_shapes=[pltpu.VMEM((n_rows, d, LANES), F32)],
        compiler_params=_cparams("arbitrary", "arbitrary"),
        name="modulation",
    )(ct, w_mod, b_mod.reshape(depth, 1, n_out))


def _rope(xh, cos, sin_signed):
    lane = lax.broadcasted_iota(jnp.int32, xh.shape, 1)
    swapped = jnp.where((lane % 64) < 32, pltpu.roll(xh, 96, 1), pltpu.roll(xh, 32, 1))
    return xh * cos + swapped * sin_signed


def _inproj_kernel(x_ref, mod_ref, w_ref, cos_ref, sin_ref, q_ref, k_ref, v_ref, acc_even, acc_odd, *, rope):
    i = pl.program_id(0)
    n_cols = ATTN_WIDTH + 2 * KV_WIDTH

    @pl.when(i == 0)
    def _():
        acc_odd[...] = jnp.zeros(acc_odd.shape, F32)

    def step(acc_mine, acc_prev):
        sh = mod_ref[0, 0:1, :]
        sc = mod_ref[0, 1:2, :]
        u = (x_ref[0] * (1.0 + sc) + sh).astype(BF16)
        acc_mine[...] = _dot(u, w_ref[0, :, 0:n_cols])
        scale = HEAD_DIM ** -0.5
        if rope:
            cos = cos_ref[...]
            sin = sin_ref[...]
        for h in range(N_Q_HEADS):
            qh = acc_prev[:, h * HEAD_DIM:(h + 1) * HEAD_DIM]
            if rope:
                qh = _rope(qh, cos, sin)
            q_ref[0, :, h * HEAD_DIM:(h + 1) * HEAD_DIM] = (qh * scale).astype(BF16)
        for h in range(N_KV_HEADS):
            kh = acc_prev[:, ATTN_WIDTH + h * HEAD_DIM:ATTN_WIDTH + (h + 1) * HEAD_DIM]
            if rope:
                kh = _rope(kh, cos, sin)
            k_ref[0, :, h * HEAD_DIM:(h + 1) * HEAD_DIM] = kh.astype(BF16)
        v_ref[0] = acc_prev[:, ATTN_WIDTH + KV_WIDTH:n_cols].astype(BF16)

    @pl.when(i % 2 == 0)
    def _():
        step(acc_even, acc_odd)

    @pl.when(i % 2 == 1)
    def _():
        step(acc_odd, acc_even)


def _in_projection(x, mod3, row_of_batch, w_in_bf, layer, cos_t, sin_t, rope):
    bsz, n_tok, d = x.shape
    tm = min(ROW_TILE, n_tok)
    nt = n_tok // tm
    n_steps = bsz * nt
    n_cols = ATTN_WIDTH + 2 * KV_WIDTH
    cur = lambda i: jnp.minimum(i, n_steps - 1)
    fin = lambda i: jnp.maximum(i - 1, 0)
    out = lambda w: jax.ShapeDtypeStruct((bsz, n_tok, w), BF16)
    ospec = lambda w: pl.BlockSpec((1, tm, w), lambda i: (fin(i) // nt, fin(i) % nt, 0))
    tspec = pl.BlockSpec((tm, HEAD_DIM), lambda i: (fin(i) % nt, 0))
    return pl.pallas_call(
        functools.partial(_inproj_kernel, rope=rope),
        grid=(n_steps + 1,),
        in_specs=[
            pl.BlockSpec((1, tm, d), lambda i: (cur(i) // nt, cur(i) % nt, 0)),
            pl.BlockSpec((1, N_MOD, d), lambda i: (row_of_batch(cur(i) // nt), 0, 0)),
            pl.BlockSpec((1, d, n_cols), lambda i: (layer, 0, 0)),
            tspec, tspec,
        ],
        out_specs=[ospec(ATTN_WIDTH), ospec(KV_WIDTH), ospec(KV_WIDTH)],
        out_shape=[out(ATTN_WIDTH), out(KV_WIDTH), out(KV_WIDTH)],
        scratch_shapes=[pltpu.VMEM((tm, n_cols), F32), pltpu.VMEM((tm, n_cols), F32)],
        compiler_params=_cparams("arbitrary"),
        name="in_projection",
    )(x, mod3, w_in_bf, cos_t, sin_t)


S5_TILE = 1024
S5_TILE_ROWS = 2 * S5_TILE // LANES


def _to_group_layout(st):
    x4 = st.reshape(SSM_G, SSM_P, S5_TILE // LANES, LANES)
    lane = lax.broadcasted_iota(jnp.int32, x4.shape[:1] + x4.shape[2:], 2)
    roll64 = lambda a: pltpu.roll(a.reshape(-1, LANES), CHUNK, 1).reshape(a.shape)
    out = []
    for k in range(SSM_P // 2):
        a, b = x4[:, 2 * k], x4[:, 2 * k + 1]
        h0 = jnp.where(lane < CHUNK, a, roll64(b))
        h1 = jnp.where(lane < CHUNK, roll64(a), b)
        out.append(jnp.concatenate([h0, h1], axis=1))
    return out


def _from_group_layout(blocks):
    n8 = S5_TILE // LANES
    lane = lax.broadcasted_iota(jnp.int32, (SSM_G, n8, LANES), 2)
    roll64 = lambda a: pltpu.roll(a.reshape(-1, LANES), CHUNK, 1).reshape(a.shape)
    chans = []
    for blk in blocks:
        a0, b1 = blk[:, 0:n8], blk[:, n8:]
        chans.append(jnp.where(lane < CHUNK, a0, roll64(b1)))
        chans.append(jnp.where(lane < CHUNK, roll64(a0), b1))
    return jnp.stack(chans, axis=1).reshape(SSM_WIDTH, S5_TILE)


def _sproj_kernel(x_ref, mod_ref, w_ref, *rest, n_steps):
    u_ref = rest[-1]

    @pl.when(pl.program_id(0) < n_steps)
    def _():
        sh = mod_ref[0, 0:1, :]
        sc = mod_ref[0, 1:2, :]
        x = x_ref[...].reshape(S5_TILE, x_ref.shape[-1])
        u = (x * (1.0 + sc) + sh).astype(BF16)
        st = _dot_nt(w_ref[0], u)
        for k, blk in enumerate(_to_group_layout(st)):
            u_ref[:, :, k * LANES:(k + 1) * LANES] = blk.astype(BF16)

    @pl.when(pl.program_id(0) >= n_steps)
    def _():
        u_ref[...] = jnp.zeros(u_ref.shape, BF16)


def _s_projection(x, mod3, mod_row, ws_t_bf, layer, n_rows, row_blk_off, u_prev=None):
    bsz, n_tok, d = x.shape
    if u_prev is None:
        assert n_tok % S5_TILE == 0
        nj = n_tok // S5_TILE
        n_steps = bsz * nj
        n_fill = n_rows // S5_TILE_ROWS - n_steps
        tile = lambda i: jnp.minimum(i, n_steps - 1)
        xspec = pl.BlockSpec((1, S5_TILE, d), lambda i: (tile(i) // nj, tile(i) % nj, 0))
        mspec = pl.BlockSpec((1, N_MOD, d), lambda i: (mod_row(tile(i) // nj), 0, 0))
    else:
        assert bsz * n_tok == S5_TILE and n_tok % LANES == 0
        n_steps, n_fill = 1, 0
        xspec = pl.BlockSpec((bsz, n_tok, d), lambda i: (0, 0, 0))
        mspec = pl.BlockSpec((1, N_MOD, d), lambda i: (mod_row(0), 0, 0))
    in_specs = [xspec, mspec, pl.BlockSpec((1, SSM_WIDTH, d), lambda i: (layer, 0, 0))]
    args = [x, mod3, ws_t_bf]
    aliases = {}
    if u_prev is not None:
        in_specs.append(pl.BlockSpec(memory_space=pl.ANY))
        args.append(u_prev)
        aliases = {3: 0}
    return pl.pallas_call(
        functools.partial(_sproj_kernel, n_steps=n_steps),
        grid=(n_steps + n_fill,),
        in_specs=in_specs,
        out_specs=pl.BlockSpec((SSM_G, S5_TILE_ROWS, GC), lambda i: (0, row_blk_off + i, 0)),
        out_shape=jax.ShapeDtypeStruct((SSM_G, n_rows, GC), BF16),
        input_output_aliases=aliases,
        compiler_params=_cparams("arbitrary"),
        name="s_projection",
    )(*args)


def _rope_tables(n_tok):
    half = HEAD_DIM // 2
    inv_freq = ROPE_THETA ** (-jnp.arange(0, half, 2, dtype=F32) / half)
    t = jnp.arange(n_tok)
    row = (t // GRID_W).astype(F32)
    col = (t % GRID_W).astype(F32)
    ang_r = row[:, None] * inv_freq[None, :]
    ang_c = col[:, None] * inv_freq[None, :]
    cos_t = jnp.concatenate([jnp.cos(ang_r), jnp.cos(ang_r), jnp.cos(ang_c), jnp.cos(ang_c)], axis=-1)
    sin_t = jnp.concatenate([-jnp.sin(ang_r), jnp.sin(ang_r), -jnp.sin(ang_c), jnp.sin(ang_c)], axis=-1)
    return cos_t, sin_t


def _attn_kernel(sink_ref, q_ref, *refs, has_band, n_blk):
    if has_band:
        kp_ref, ko_ref, kn_ref, vp_ref, vo_ref, vn_ref, kc_ref, vc_ref, g_ref, o_ref = refs
    else:
        kc_ref, vc_ref, g_ref, o_ref = refs
    blk = pl.program_id(1)
    rows = Q_PER_KV * ATTN_BLOCK
    qi = lax.broadcasted_iota(jnp.int32, (rows, ATTN_BLOCK), 0) % ATTN_BLOCK
    kj = lax.broadcasted_iota(jnp.int32, (rows, ATTN_BLOCK), 1)
    row_head = lax.broadcasted_iota(jnp.int32, (rows, 1), 0) // ATTN_BLOCK
    off_p = jnp.where(blk > 0, 0, ATTN_BLOCK)
    off_n = jnp.where(blk < n_blk - 1, 0, ATTN_BLOCK)
    heads = []
    for h in range(N_KV_HEADS):
        hs = slice(h * HEAD_DIM, (h + 1) * HEAD_DIM)
        q = jnp.concatenate(
            [q_ref[0, :, (h * Q_PER_KV + g) * HEAD_DIM:(h * Q_PER_KV + g + 1) * HEAD_DIM] for g in range(Q_PER_KV)],
            axis=0)
        sink = jnp.zeros((rows, 1), F32)
        for g in range(Q_PER_KV):
            sink = jnp.where(row_head == g, sink_ref[h * Q_PER_KV + g], sink)
        s_c = _dot_nt(q, kc_ref[0, :, hs])
        m = jnp.maximum(jnp.max(s_c, axis=-1, keepdims=True), sink)
        if has_band:
            s_p = jnp.where(kj >= qi + off_p, _dot_nt(q, kp_ref[0, :, hs]), NEG_INF)
            s_o = _dot_nt(q, ko_ref[0, :, hs])
            s_n = jnp.where(kj <= qi - off_n, _dot_nt(q, kn_ref[0, :, hs]), NEG_INF)
            m = jnp.maximum(m, jnp.max(jnp.maximum(jnp.maximum(s_p, s_o), s_n), axis=-1, keepdims=True))
        p_c = jnp.exp(s_c - m)
        denom = jnp.sum(p_c, axis=-1, keepdims=True) + jnp.exp(sink - m)
        acc = _dot(p_c.astype(BF16), vc_ref[0, :, hs])
        if has_band:
            for s_x, v_ref in ((s_p, vp_ref), (s_o, vo_ref), (s_n, vn_ref)):
                p_x = jnp.exp(s_x - m)
                denom = denom + jnp.sum(p_x, axis=-1, keepdims=True)
                acc = acc + _dot(p_x.astype(BF16), v_ref[0, :, hs])
        o = acc / denom
        for g in range(Q_PER_KV):
            heads.append(o[g * ATTN_BLOCK:(g + 1) * ATTN_BLOCK, :])
    ss = heads[0] * heads[0]
    for o in heads[1:]:
        ss = ss + o * o
    inv = lax.rsqrt(jnp.sum(ss, axis=-1, keepdims=True) / ATTN_WIDTH + LN_EPS)
    for i, o in enumerate(heads):
        cs = slice(i * HEAD_DIM, (i + 1) * HEAD_DIM)
        o_ref[0, :, cs] = (o * inv * g_ref[:, cs]).astype(BF16)


def _attention(q, k, v, kc, vc, sink, g_attn, has_band):
    bsz, n_tok, _ = q.shape
    n_ctx = kc.shape[1]
    n_blk = n_tok // ATTN_BLOCK
    qspec = pl.BlockSpec((1, ATTN_BLOCK, ATTN_WIDTH), lambda b, n: (b, n, 0))
    kvspec = lambda f: pl.BlockSpec((1, ATTN_BLOCK, KV_WIDTH), lambda b, n: (b, f(n), 0))
    cspec = pl.BlockSpec((1, n_ctx, KV_WIDTH), lambda b, n: (b, 0, 0))
    prev = lambda n: jnp.maximum(n - 1, 0)
    own = lambda n: n
    nxt = lambda n: jnp.minimum(n + 1, n_blk - 1)
    in_specs = [pl.BlockSpec(memory_space=pltpu.SMEM), qspec]
    args = [sink, q]
    if has_band:
        in_specs += [kvspec(prev), kvspec(own), kvspec(nxt), kvspec(prev), kvspec(own), kvspec(nxt)]
        args += [k, k, k, v, v, v]
    in_specs += [cspec, cspec, pl.BlockSpec((1, ATTN_WIDTH), lambda b, n: (0, 0))]
    args += [kc, vc, g_attn.reshape(1, ATTN_WIDTH)]
    return pl.pallas_call(
        functools.partial(_attn_kernel, has_band=has_band, n_blk=n_blk),
        grid=(bsz, n_blk),
        in_specs=in_specs,
        out_specs=qspec,
        out_shape=jax.ShapeDtypeStruct((bsz, n_tok, ATTN_WIDTH), BF16),
        compiler_params=_cparams("parallel", "parallel"),
        name="attention",
    )(*args)


def _cmul(a, b):
    return a[0] * b[0] - a[1] * b[1], a[0] * b[1] + a[1] * b[0]


def _cpow(rho, theta, expo):
    mag = jnp.exp(expo * rho)
    ang = expo * theta
    return mag * jnp.cos(ang), mag * jnp.sin(ang)


def _s5_param_kernel(after_ref, prow_ref, bt_ref, cr_ref, dd_ref, k_ref, w_ref, vt_ref, al_ref):
    del after_ref
    p_dim, lc = SSM_P, CHUNK
    lam_re, lam_im = prow_ref[0, 0:1, :], prow_ref[0, 1:2, :]
    dt = jnp.exp(prow_ref[0, 2:3, :])
    rho, theta = lam_re * dt, lam_im * dt

    sub8 = lax.broadcasted_iota(jnp.int32, (8, LANES), 0)
    asc = _cpow(rho, theta, sub8.astype(F32))
    desc = _cpow(rho, theta, (8 - sub8).astype(F32))
    ex = jnp.where(sub8 == 0, 8, jnp.where(sub8 == 1, 16, jnp.where(sub8 == 2, 32, jnp.where(sub8 == 3, lc, 1))))
    pw = _cpow(rho, theta, ex.astype(F32))
    row = lambda t, i: (t[0][i:i + 1, :], t[1][i:i + 1, :])
    cat = lambda a, b: (jnp.concatenate([a[0], b[0]], axis=0), jnp.concatenate([a[1], b[1]], axis=0))
    for i in range(3):
        step = row(pw, i)
        asc = cat(asc, _cmul(asc, step))
        desc = cat(_cmul(desc, step), desc)
    a_chunk, a_one = row(pw, 3), row(pw, 4)

    den = lam_re * lam_re + lam_im * lam_im
    x_re, x_im = a_one[0] - 1.0, a_one[1]
    beta = ((x_re * lam_re + x_im * lam_im) / den, (x_im * lam_re - x_re * lam_im) / den)
    bb = _cmul(beta, (bt_ref[0, 0], bt_ref[0, 1]))
    cc = (cr_ref[0, 0], cr_ref[0, 1])

    is_fwd = lax.broadcasted_iota(jnp.int32, (lc, LANES), 1) < SSM_N
    sub = lax.broadcasted_iota(jnp.int32, (lc, LANES), 0)
    pw_w = (jnp.where(is_fwd, desc[0], asc[0]), jnp.where(is_fwd, desc[1], asc[1]))
    pw_v = (jnp.where(is_fwd, asc[0], desc[0]), jnp.where(is_fwd, asc[1], desc[1]))

    top = (jnp.where(is_fwd, asc[0], jnp.where(sub == 0, 1.0, 0.0)), jnp.where(is_fwd, asc[1], 0.0))
    keep = (~is_fwd) & (sub > 0)
    bot = (jnp.where(keep, desc[0], 0.0), jnp.where(keep, desc[1], 0.0))
    lag = cat(top, bot)

    cx = (jnp.concatenate([cc[0]] * p_dim, axis=0), jnp.concatenate([cc[1]] * p_dim, axis=0))
    bx = tuple(jnp.concatenate([jnp.broadcast_to(b[q:q + 1, :], (p_dim, LANES)) for q in range(p_dim)], axis=0)
               for b in bb)
    e_re, e_im = _cmul(cx, bx)
    hi = lax.Precision.HIGHEST
    kmat = (jnp.dot(e_re, lag[0].T, preferred_element_type=F32, precision=hi)
            - jnp.dot(e_im, lag[1].T, preferred_element_type=F32, precision=hi))
    lane0 = lax.broadcasted_iota(jnp.int32, kmat.shape, 1) == 0
    k_ref[0] = kmat + jnp.where(lane0, dd_ref[0], 0.0)

    for q in range(p_dim):
        w_re, w_im = _cmul(pw_w, row(bb, q))
        w_ref[0, q * lc:(q + 1) * lc, :] = jnp.concatenate([w_re, w_im], axis=1).astype(BF16)
    for p in range(p_dim):
        v_re, v_im = _cmul(pw_v, row(cc, p))
        vt_ref[0, p * lc:(p + 1) * lc, :] = jnp.concatenate([v_re, -v_im], axis=1).astype(BF16)
    al_ref[0] = jnp.concatenate([a_chunk[0], a_chunk[1]], axis=1)


def _s5_operators(lam_re, lam_im, log_dt, b_re, b_im, c_re, c_im, d_skip, layer, after):
    depth = lam_re.shape[0]
    dg = depth * SSM_G
    g0 = layer * SSM_G
    vec = lambda a: jnp.moveaxis(a, 1, 2).reshape(dg, LANES)
    mat = lambda a: jnp.moveaxis(a, 1, 3).reshape(dg, SSM_P, LANES)
    ldt = jnp.broadcast_to(log_dt[..., None], lam_re.shape)
    prow = jnp.stack([vec(lam_re), vec(lam_im), vec(ldt)], axis=1)
    bt = jnp.stack([mat(jnp.swapaxes(b_re, 3, 4)), mat(jnp.swapaxes(b_im, 3, 4))], axis=1)
    cr = jnp.stack([mat(c_re), mat(c_im)], axis=1)
    dd = (d_skip.reshape(dg, 1, SSM_P) * jnp.eye(SSM_P, dtype=F32)[None]).reshape(dg, SSM_P * SSM_P, 1)
    iblk = lambda *s: pl.BlockSpec((1,) + s, lambda i: (g0 + i,) + (0,) * len(s))
    oblk = lambda *s: pl.BlockSpec((1,) + s, lambda i: (i,) + (0,) * len(s))
    return pl.pallas_call(
        _s5_param_kernel,
        grid=(SSM_G,),
        in_specs=[pl.BlockSpec(memory_space=pltpu.SMEM),
                  iblk(3, LANES), iblk(2, SSM_P, LANES), iblk(2, SSM_P, LANES), iblk(SSM_P * SSM_P, 1)],
        out_specs=[oblk(SSM_P * SSM_P, LANES), oblk(GC, 2 * LANES), oblk(GC, 2 * LANES), oblk(1, 2 * LANES)],
        out_shape=[jax.ShapeDtypeStruct((SSM_G, SSM_P * SSM_P, LANES), F32),
                   jax.ShapeDtypeStruct((SSM_G, GC, 2 * LANES), BF16),
                   jax.ShapeDtypeStruct((SSM_G, GC, 2 * LANES), BF16),
                   jax.ShapeDtypeStruct((SSM_G, 1, 2 * LANES), F32)],
        compiler_params=_cparams("parallel"),
        name="s5_operators",
    )(after, prow, bt, cr, dd)


def _s5_sum_kernel(u_ref, w_ref, s_ref):
    s_ref[0] = _dot(u_ref[0], w_ref[0])


def _s5_scan_kernel(s_ref, al_ref, h_ref, *, n_ctx_chunks, n_chunks):
    a_re, a_im = al_ref[:, :LANES], al_ref[:, LANES:]
    is_fwd = lax.broadcasted_iota(jnp.int32, a_re.shape, 1) < SSM_N

    def body(i, carry):
        h_re, h_im = carry
        cf = i
        cr = jnp.where(i < n_ctx_chunks, n_ctx_chunks - 1 - i, n_chunks - 1 - (i - n_ctx_chunks))
        s_f, s_r = s_ref[cf], s_ref[cr]
        h_ref[cf, :, 0:SSM_N] = h_re[:, 0:SSM_N]
        h_ref[cr, :, SSM_N:LANES] = h_re[:, SSM_N:LANES]
        h_ref[cf, :, LANES:LANES + SSM_N] = h_im[:, 0:SSM_N]
        h_ref[cr, :, LANES + SSM_N:] = h_im[:, SSM_N:LANES]
        s_re = jnp.where(is_fwd, s_f[:, :LANES], s_r[:, :LANES])
        s_im = jnp.where(is_fwd, s_f[:, LANES:], s_r[:, LANES:])
        return a_re * h_re - a_im * h_im + s_re, a_re * h_im + a_im * h_re + s_im

    zero = jnp.zeros(a_re.shape, F32)
    lax.fori_loop(0, n_chunks, body, (zero, zero))


def _toeplitz_blocks(k_ref, m_buf):
    lc = CHUNK
    lane = lax.broadcasted_iota(jnp.int32, (lc, LANES), 1)
    for q in range(SSM_P):
        for pp in range(SSM_P // 2):
            r0 = q * SSM_P + 2 * pp
            ka = jnp.broadcast_to(k_ref[0, r0:r0 + 1, :], (lc, LANES))
            kb = jnp.broadcast_to(k_ref[0, r0 + 1:r0 + 2, :], (lc, LANES))
            ra = pltpu.roll(ka, 0, 1, stride=1, stride_axis=0)
            rb = pltpu.roll(kb, lc, 1, stride=1, stride_axis=0)
            m_buf[q * lc:(q + 1) * lc, pp * LANES:(pp + 1) * LANES] = jnp.where(lane < lc, ra, rb).astype(BF16)


def _s5_out_kernel(u_ref, k_first_ref, k_next_ref, h_ref, vt_ref, y_ref, m_even, m_odd):
    g = pl.program_id(0)

    @pl.when(g == 0)
    def _():
        _toeplitz_blocks(k_first_ref, m_even)

    def step(m_mine, m_next):
        _toeplitz_blocks(k_next_ref, m_next)
        y = _dot(u_ref[0], m_mine[...]) + _dot_nt(h_ref[0].astype(BF16), vt_ref[0])
        y_ref[0] = y.astype(BF16)

    @pl.when(g % 2 == 0)
    def _():
        step(m_even, m_odd)

    @pl.when(g % 2 == 1)
    def _():
        step(m_odd, m_even)


def _rows_to_chunks(a, bsz, n_tok, n_ctx):
    n8, nj, ncb, st = S5_TILE // LANES, n_tok // S5_TILE, n_ctx // LANES, a.shape[-1]
    n_lat = bsz * nj * S5_TILE_ROWS
    lat = a[:, :n_lat].reshape(SSM_G, bsz, nj, 2, n8, st).transpose(2, 4, 3, 0, 1, 5).reshape(-1, SSM_G * bsz, st)
    ctx = a[:, n_lat:].reshape(SSM_G, 2, bsz, ncb, st).transpose(3, 1, 0, 2, 4).reshape(-1, SSM_G * bsz, st)
    return jnp.concatenate([ctx, lat], axis=0)


def _chunks_to_rows(a, bsz, n_tok, n_ctx):
    n8, nj, ncb, st = S5_TILE // LANES, n_tok // S5_TILE, n_ctx // LANES, a.shape[-1]
    ncc = n_ctx // CHUNK
    lat = a[ncc:].reshape(nj, n8, 2, SSM_G, bsz, st).transpose(3, 4, 0, 2, 1, 5).reshape(SSM_G, -1, st)
    ctx = a[:ncc].reshape(ncb, 2, SSM_G, bsz, st).transpose(2, 1, 3, 0, 4).reshape(SSM_G, -1, st)
    return jnp.concatenate([lat, ctx], axis=1)


def _s5_apply(u, bsz, n_tok, n_ctx, ops):
    k_op, w_op, vt_op, al = ops
    ncc, n_chunks = n_ctx // CHUNK, (n_ctx + n_tok) // CHUNK
    rows = u.shape[1]
    assert rows == bsz * n_chunks
    gspec = lambda r, c: pl.BlockSpec((1, r, c), lambda g: (g, 0, 0))
    ospec = gspec
    sums = pl.pallas_call(
        _s5_sum_kernel,
        grid=(SSM_G,),
        in_specs=[gspec(rows, GC), ospec(GC, 2 * LANES)],
        out_specs=gspec(rows, 2 * LANES),
        out_shape=jax.ShapeDtypeStruct((SSM_G, rows, 2 * LANES), F32),
        compiler_params=_cparams("parallel"),
        name="s5_chunk_sums",
    )(u, w_op)
    sums_t = _rows_to_chunks(sums, bsz, n_tok, n_ctx)
    al_rows = jnp.broadcast_to(al, (SSM_G, bsz, 2 * LANES)).reshape(SSM_G * bsz, 2 * LANES)
    rt = 64
    states_t = pl.pallas_call(
        functools.partial(_s5_scan_kernel, n_ctx_chunks=ncc, n_chunks=n_chunks),
        grid=(SSM_G * bsz // rt,),
        in_specs=[pl.BlockSpec((n_chunks, rt, 2 * LANES), lambda r: (0, r, 0)),
                  pl.BlockSpec((rt, 2 * LANES), lambda r: (r, 0))],
        out_specs=pl.BlockSpec((n_chunks, rt, 2 * LANES), lambda r: (0, r, 0)),
        out_shape=jax.ShapeDtypeStruct((n_chunks, SSM_G * bsz, 2 * LANES), F32),
        compiler_params=_cparams("parallel"),
        name="s5_state_scan",
    )(sums_t, al_rows)
    states = _chunks_to_rows(states_t, bsz, n_tok, n_ctx)
    kspec = lambda f: pl.BlockSpec((1, SSM_P * SSM_P, LANES), lambda g: (f(g), 0, 0))
    return pl.pallas_call(
        _s5_out_kernel,
        grid=(SSM_G,),
        in_specs=[gspec(rows, GC), kspec(lambda g: 0), kspec(lambda g: jnp.minimum(g + 1, SSM_G - 1)),
                  gspec(rows, 2 * LANES), ospec(GC, 2 * LANES)],
        out_specs=gspec(rows, GC),
        out_shape=jax.ShapeDtypeStruct((SSM_G, rows, GC), BF16),
        scratch_shapes=[pltpu.VMEM((GC, GC), BF16), pltpu.VMEM((GC, GC), BF16)],
        compiler_params=_cparams("arbitrary"),
        name="s5_chunk_outputs",
    )(u, k_op, k_op, states, vt_op)


def _glu_kernel(y_ref, w_ref, b_ref, g_ref, o_ref):
    blocks = [y_ref[:, :, k * LANES:(k + 1) * LANES].astype(F32) for k in range(SSM_P // 2)]
    y = _from_group_layout(blocks).T
    z = 0.5 * y * (1.0 + jnp.tanh(math.sqrt(2.0 / math.pi) * (y + 0.044715 * (y * y * y))))
    t = _dot(z.astype(BF16), w_ref[0]) + b_ref[...]
    o = z * jax.nn.sigmoid(t)
    inv = lax.rsqrt(jnp.mean(o * o, axis=-1, keepdims=True) + LN_EPS)
    o_ref[...] = (o * inv * g_ref[...]).astype(BF16).reshape(o_ref.shape)


def _glu(y, bsz, n_tok, row_blk_off, per_batch, w_glu_bf, layer, b_glu, g_ssm):
    w = SSM_WIDTH
    if per_batch:
        grid = (bsz, n_tok // S5_TILE)
        ospec = pl.BlockSpec((1, S5_TILE, w), lambda b, j: (b, j, 0))
    else:
        assert bsz * n_tok == S5_TILE
        grid = (1, 1)
        ospec = pl.BlockSpec((bsz, n_tok, w), lambda b, j: (0, 0, 0))
    nj = grid[1]
    row = lambda a: a.reshape(1, w)
    vspec = pl.BlockSpec((1, w), lambda b, j: (0, 0))
    return pl.pallas_call(
        _glu_kernel,
        grid=grid,
        in_specs=[pl.BlockSpec((SSM_G, S5_TILE_ROWS, GC), lambda b, j: (0, row_blk_off + b * nj + j, 0)),
                  pl.BlockSpec((1, w, w), lambda b, j: (layer, 0, 0)), vspec, vspec],
        out_specs=ospec,
        out_shape=jax.ShapeDtypeStruct((bsz, n_tok, w), BF16),
        compiler_params=_cparams("parallel", "parallel"),
        name="s5_glu",
    )(y, w_glu_bf, row(b_glu), row(g_ssm))


def _layer_norm(z, g, b):
    mu = jnp.mean(z, axis=-1, keepdims=True)
    zc = z - mu
    var = jnp.mean(zc * zc, axis=-1, keepdims=True)
    return zc * lax.rsqrt(var + LN_EPS) * g + b


def _first_argmax(vals):
    best_i = jnp.zeros(vals[0].shape, jnp.int32)
    best_v = vals[0]
    for j in range(1, len(vals)):
        better = vals[j] > best_v
        best_i = jnp.where(better, j, best_i)
        best_v = jnp.where(better, vals[j], best_v)
    return best_i, best_v


def _route(logit_rows):
    m = functools.reduce(jnp.maximum, logit_rows)
    p = [jnp.exp(l - m) for l in logit_rows]
    scores = []
    for g in range(N_EXPERT_GROUPS):
        a, b, c, d = p[4 * g:4 * g + 4]
        hi1, lo1, hi2, lo2 = jnp.maximum(a, b), jnp.minimum(a, b), jnp.maximum(c, d), jnp.minimum(c, d)
        scores.append(jnp.maximum(hi1, hi2) + jnp.maximum(jnp.minimum(hi1, hi2), jnp.maximum(lo1, lo2)))
    grp, _ = _first_argmax(scores)
    sel = []
    for j in range(EXPERTS_PER_GROUP):
        v = p[j]
        for g in range(1, N_EXPERT_GROUPS):
            v = jnp.where(grp == g, p[4 * g + j], v)
        sel.append(v)
    i1, v1 = _first_argmax(sel)
    i2, v2 = _first_argmax([jnp.where(i1 == j, -1.0, sel[j]) for j in range(EXPERTS_PER_GROUP)])
    tot = v1 + v2
    return grp * EXPERTS_PER_GROUP + i1, grp * EXPERTS_PER_GROUP + i2, v1 / tot, v2 / tot


def _outproj_kernel(a_ref, s_ref, x_ref, mod_ref, w_ref, lng_ref, lnb_ref, wr_ref, br_ref, *rest, alpha, n_steps):
    x1_ref, u2_ref, ri_ref, rw_ref, acc_even, acc_odd = rest[-6:]
    i = pl.program_id(0)

    @pl.when(i == 0)
    def _():
        acc_odd[...] = jnp.zeros(acc_odd.shape, F32)

    def step(acc_mine, acc_prev):
        acc_mine[...] = _dot(a_ref[0], w_ref[0, 0:ATTN_WIDTH, :]) + _dot(s_ref[0], w_ref[0, ATTN_WIDTH:, :])
        y = acc_prev[...]
        g1 = mod_ref[0, 2:3, :]
        sh2 = mod_ref[0, 3:4, :]
        sc2 = mod_ref[0, 4:5, :]
        x1 = _layer_norm(alpha * x_ref[0] + g1 * y, lng_ref[...], lnb_ref[...])
        x1_ref[0] = x1
        u2 = x1 * (1.0 + sc2) + sh2
        u2_ref[...] = u2
        logits = _dot_nt(wr_ref[...], u2.astype(BF16)) + br_ref[...]
        e1, e2, w1, w2 = _route([logits[e:e + 1, :] for e in range(N_EXPERTS)])
        ri_ref[0, 0:1, :] = e1
        ri_ref[0, 1:2, :] = e2
        rw_ref[0, 0:1, :] = w1
        rw_ref[0, 1:2, :] = w2

    @pl.when((i <= n_steps) & (i % 2 == 0))
    def _():
        step(acc_even, acc_odd)

    @pl.when((i <= n_steps) & (i % 2 == 1))
    def _():
        step(acc_odd, acc_even)

    @pl.when(i > n_steps)
    def _():
        u2_ref[...] = jnp.zeros(u2_ref.shape, F32)


def _out_projection(attn_n, ssm_n, x, mod3, row_of_batch, w_out_bf, layer, ln_g, ln_b, w_router_t_bf, b_router,
                    alpha, u2_rows, row_off, u2_prev=None):
    bsz, n_tok, d = x.shape
    tm = min(ROW_TILE, n_tok)
    nt = n_tok // tm
    n_steps = bsz * nt
    assert row_off % tm == 0 and u2_rows % tm == 0
    n_fill = 0 if u2_prev is not None else u2_rows // tm - n_steps
    cur = lambda i: jnp.minimum(i, n_steps - 1)
    fin = lambda i: jnp.clip(i - 1, 0, n_steps - 1)
    pspec = lambda w: pl.BlockSpec((1, tm, w), lambda i: (cur(i) // nt, cur(i) % nt, 0))
    fspec = lambda w: pl.BlockSpec((1, tm, w), lambda i: (fin(i) // nt, fin(i) % nt, 0))
    vspec = pl.BlockSpec((1, d), lambda i: (0, 0))
    rspec = pl.BlockSpec((1, 2, tm), lambda i: (fin(i) // nt, 0, fin(i) % nt))
    in_specs = [
        pspec(ATTN_WIDTH), pspec(SSM_WIDTH), fspec(d),
        pl.BlockSpec((1, N_MOD, d), lambda i: (row_of_batch(fin(i) // nt), 0, 0)),
        pl.BlockSpec((1,) + w_out_bf.shape[1:], lambda i: (layer, 0, 0)),
        vspec, vspec,
        pl.BlockSpec((N_EXPERTS, d), lambda i: (0, 0)),
        pl.BlockSpec((N_EXPERTS, 1), lambda i: (0, 0)),
    ]
    args = [attn_n, ssm_n, x, mod3, w_out_bf, ln_g.reshape(1, d), ln_b.reshape(1, d), w_router_t_bf,
            b_router.reshape(N_EXPERTS, 1)]
    aliases = {}
    if u2_prev is not None:
        in_specs.append(pl.BlockSpec(memory_space=pl.ANY))
        args.append(u2_prev)
        aliases = {len(args) - 1: 1}
    return pl.pallas_call(
        functools.partial(_outproj_kernel, alpha=alpha, n_steps=n_steps),
        grid=(n_steps + 1 + n_fill,),
        in_specs=in_specs,
        out_specs=[fspec(d), pl.BlockSpec((tm, d), lambda i: (row_off // tm + jnp.maximum(i - 1, 0), 0)),
                   rspec, rspec],
        out_shape=[jax.ShapeDtypeStruct((bsz, n_tok, d), F32), jax.ShapeDtypeStruct((u2_rows, d), F32),
                   jax.ShapeDtypeStruct((bsz, 2, n_tok), jnp.int32), jax.ShapeDtypeStruct((bsz, 2, n_tok), F32)],
        scratch_shapes=[pltpu.VMEM((tm, d), F32), pltpu.VMEM((tm, d), F32)],
        input_output_aliases=aliases,
        compiler_params=_cparams("arbitrary"),
        name="out_projection",
    )(*args)


CAST_ROWS = 256
GATHER_AHEAD = 2


def _ffn_kernel(te_ref, nx_ref, nu_ref, src_ref, u_hbm, wg_hbm, wu_hbm, wd_hbm, y_ref,
                xbuf, stage_g, stage_u, stage_d, wg_bf, wu_bf, wd_bf, sem, gsem, *, e0):
    t = pl.program_id(0)
    e = te_ref[t]
    tm = xbuf.shape[1]
    pairs = ((wg_hbm, stage_g, wg_bf), (wu_hbm, stage_u, wu_bf), (wd_hbm, stage_d, wd_bf))

    def fetch(expert):
        return [pltpu.make_async_copy(hbm.at[e0 + expert], stage, sem.at[i])
                for i, (hbm, stage, _) in enumerate(pairs)]

    def gather_row(base, i, slot):
        r = src_ref[base + i]
        pltpu.make_async_copy(u_hbm.at[pl.ds(r, 1)], xbuf.at[slot, pl.ds(i, 1)], gsem.at[slot]).start()

    def wait_rows(slot):
        pltpu.make_async_copy(u_hbm.at[pl.ds(0, tm)], xbuf.at[slot], gsem.at[slot]).wait()

    last_tile = pl.num_programs(0) - 1

    @pl.when(t == 0)
    def _():
        for ahead in range(GATHER_AHEAD):
            def one_row(i, carry, ahead=ahead):
                gather_row(jnp.minimum(ahead, last_tile) * tm, i, ahead)
                return carry
            lax.fori_loop(0, tm, one_row, 0, unroll=8)
        for cp in fetch(e):
            cp.start()

    first_of_run = (t == 0) | (te_ref[jnp.maximum(t - 1, 0)] != e)

    @pl.when(first_of_run & (t < nu_ref[0]))
    def _():
        for cp in fetch(e):
            cp.wait()
        for _, stage, dst in pairs:
            def cast_rows(i, carry, stage=stage, dst=dst):
                rows = pl.ds(pl.multiple_of(i * CAST_ROWS, CAST_ROWS), CAST_ROWS)
                dst[rows, :] = stage[rows, :].astype(BF16)
                return carry
            lax.fori_loop(0, stage.shape[0] // CAST_ROWS, cast_rows, 0)

        @pl.when(nx_ref[t] >= 0)
        def _():
            for cp in fetch(nx_ref[t]):
                cp.start()

    @pl.when(t < nu_ref[0])
    def _():
        n_slots = GATHER_AHEAD + 1
        slot = t % n_slots
        wait_rows(slot)
        x = xbuf[slot].astype(BF16)
        base = jnp.minimum(t + GATHER_AHEAD, last_tile) * tm
        for i in range(tm):
            gather_row(base, i, (t + GATHER_AHEAD) % n_slots)
        g = _dot(x, wg_bf[...])
        u = _dot(x, wu_bf[...])
        h = (g * jax.nn.sigmoid(g) * u).astype(BF16)
        y_ref[...] = _dot(h, wd_bf[...]).astype(BF16)

    @pl.when(t == nu_ref[0] - 1)
    def _():
        for ahead in range(1, GATHER_AHEAD + 1):
            wait_rows((t + ahead) % (GATHER_AHEAD + 1))

    @pl.when(t >= nu_ref[0])
    def _():
        y_ref[...] = jnp.zeros(y_ref.shape, BF16)


def _expert_ffn(u2, row_src, tile_expert, next_expert, n_used, w_gate, w_up, w_down, layer):
    d = u2.shape[1]
    n_rows = row_src.shape[0]
    n_e, f = w_gate.shape[1], w_gate.shape[3]
    tm = MOE_TILE
    flat = lambda w: w.reshape((-1,) + w.shape[2:])
    hbm = pl.BlockSpec(memory_space=pl.ANY)
    grid_spec = pltpu.PrefetchScalarGridSpec(
        num_scalar_prefetch=4,
        grid=(n_rows // tm,),
        in_specs=[hbm, hbm, hbm, hbm],
        out_specs=pl.BlockSpec((tm, d), lambda t, te, nx, nu, src: (t, 0)),
        scratch_shapes=[pltpu.VMEM((GATHER_AHEAD + 1, tm, d), F32),
                        pltpu.VMEM((d, f), F32), pltpu.VMEM((d, f), F32), pltpu.VMEM((f, d), F32),
                        pltpu.VMEM((d, f), BF16), pltpu.VMEM((d, f), BF16), pltpu.VMEM((f, d), BF16),
                        pltpu.SemaphoreType.DMA((3,)), pltpu.SemaphoreType.DMA((GATHER_AHEAD + 1,))],
    )
    return pl.pallas_call(
        functools.partial(_ffn_kernel, e0=layer * n_e),
        grid_spec=grid_spec,
        out_shape=jax.ShapeDtypeStruct((n_rows, d), BF16),
        compiler_params=_cparams("arbitrary"),
        name="expert_ffn",
    )(tile_expert, next_expert, n_used, row_src, u2, flat(w_gate), flat(w_up), flat(w_down))


def _dispatch_plan(e1, e2, tm):
    n = e1.shape[0]
    n_tiles = (2 * n + N_EXPERTS * (tm - 1) + tm - 1) // tm
    e = jnp.concatenate([e1, e2])
    onehot = (e[:, None] == jnp.arange(N_EXPERTS, dtype=jnp.int32)[None, :]).astype(jnp.int32)
    csum = jnp.cumsum(onehot, axis=0)
    pos_in_e = jnp.sum((csum - 1) * onehot, axis=1)
    counts = csum[-1]
    padded = ((counts + tm - 1) // tm) * tm
    ends = jnp.cumsum(padded)
    offs = ends - padded
    dest = offs[e] + pos_in_e
    tok = jnp.concatenate([jnp.arange(n, dtype=jnp.int32)] * 2)
    row_src = jnp.zeros((n_tiles * tm,), jnp.int32).at[dest].set(tok, mode="promise_in_bounds", unique_indices=True)
    n_used = (ends[-1] // tm).astype(jnp.int32)
    tile_start = jnp.arange(n_tiles, dtype=jnp.int32) * tm
    tile_e = jnp.sum((tile_start[:, None] >= ends[None, :]).astype(jnp.int32), axis=1)
    tile_e = jnp.minimum(tile_e, N_EXPERTS - 1)
    last_e = tile_e[jnp.maximum(n_used - 1, 0)]
    tile_e = jnp.where(jnp.arange(n_tiles) < n_used, tile_e, last_e).astype(jnp.int32)
    ids = jnp.arange(N_EXPERTS, dtype=jnp.int32)
    later = (ids[None, :] > ids[:, None]) & (counts[None, :] > 0)
    next_of = jnp.min(jnp.where(later, ids[None, :], N_EXPERTS), axis=1)
    next_of = jnp.where(next_of == N_EXPERTS, -1, next_of).astype(jnp.int32)
    return row_src, dest[:n], dest[n:], tile_e, next_of[tile_e], n_used.reshape(1)


def _take_rows(a, idx):
    return a.at[idx].get(mode="promise_in_bounds")


def _final_kernel(x_ref, y1_ref, y2_ref, w1_ref, w2_ref, mod_ref, lng_ref, lnb_ref, o_ref, *, alpha):
    f = w1_ref[...] * y1_ref[...].astype(F32) + w2_ref[...] * y2_ref[...].astype(F32)
    g2 = mod_ref[0, 5:6, :]
    o_ref[0] = _layer_norm(alpha * x_ref[0] + g2 * f, lng_ref[...], lnb_ref[...])


def _combine_ln(x1, y1, y2, w1, w2, row_off, mod3, row_of_batch, ln_g, ln_b, alpha):
    bsz, n_tok, d = x1.shape
    tm = min(ROW_TILE, n_tok)
    nt = n_tok // tm
    assert row_off % tm == 0
    tspec = pl.BlockSpec((1, tm, d), lambda b, t: (b, t, 0))
    yspec = pl.BlockSpec((tm, d), lambda b, t: (row_off // tm + b * nt + t, 0))
    wspec = pl.BlockSpec((tm, 1), lambda b, t: (row_off // tm + b * nt + t, 0))
    vspec = pl.BlockSpec((1, d), lambda b, t: (0, 0))
    return pl.pallas_call(
        functools.partial(_final_kernel, alpha=alpha),
        grid=(bsz, nt),
        in_specs=[tspec, yspec, yspec, wspec, wspec,
                  pl.BlockSpec((1, N_MOD, d), lambda b, t: (row_of_batch(b), 0, 0)), vspec, vspec],
        out_specs=tspec,
        out_shape=jax.ShapeDtypeStruct((bsz, n_tok, d), F32),
        compiler_params=_cparams("parallel", "parallel"),
        name="combine_post_ln",
    )(x1, y1, y2, w1, w2, mod3, ln_g.reshape(1, d), ln_b.reshape(1, d))


def kernel(x, c, ctx, c_ctx, w_mod, b_mod, w_in, attn_sink, ssm_lambda_re, ssm_lambda_im, ssm_log_dt, ssm_b_re, ssm_b_im, ssm_c_re, ssm_c_im, ssm_d, w_glu, b_glu, g_attn_out, g_ssm_out, w_out, ln1_g, ln1_b, w_router, b_router, w_expert_gate, w_expert_up, w_expert_down, ln2_g, ln2_b):
    depth = w_mod.shape[0]
    bsz, n_tok, d = x.shape
    n_ctx = ctx.shape[1]
    alpha = (2 * depth) ** 0.25
    assert n_tok % ROW_TILE == 0 and n_tok % ATTN_BLOCK == 0 and n_ctx % ATTN_BLOCK == 0 and n_ctx % CHUNK == 0

    mod = _modulation(c, c_ctx, w_mod, b_mod)
    s5_params = (ssm_lambda_re, ssm_lambda_im, ssm_log_dt, ssm_b_re, ssm_b_im, ssm_c_re, ssm_c_im, ssm_d)
    expert_w = (w_expert_gate, w_expert_up, w_expert_down)
    s5_ops = _s5_operators(*s5_params, 0, jnp.zeros((1,), jnp.int32))
    cos_t, sin_t = _rope_tables(n_tok)
    w_router_t = w_router.T.astype(BF16)
    w_glu_bf, w_out_bf = w_glu.astype(BF16), w_out.astype(BF16)
    n_qkv = ATTN_WIDTH + 2 * KV_WIDTH
    w_in_bf = w_in[:, :, :n_qkv].astype(BF16)
    ws_t_bf = jnp.swapaxes(w_in[:, :, n_qkv:].astype(BF16), 1, 2)
    lat_row = lambda b: b
    ctx_row = lambda b: bsz
    n_lat, n_c = bsz * n_tok, bsz * n_ctx
    lat_blocks = n_lat // S5_TILE
    s5_rows = (lat_blocks + 1) * S5_TILE_ROWS

    xc = ctx
    for i in range(depth):
        last = i == depth - 1
        mod3 = mod[i].reshape(MOD_ROWS, N_MOD, d)
        n_moe = n_lat if last else n_lat + n_c

        q, k, v = _in_projection(x, mod3, lat_row, w_in_bf, i, cos_t, sin_t, True)
        qc, kc, vc = _in_projection(xc, mod3, ctx_row, w_in_bf, i, cos_t, sin_t, False)
        u = _s_projection(x, mod3, lat_row, ws_t_bf, i, s5_rows, 0)
        u = _s_projection(xc, mod3, ctx_row, ws_t_bf, i, s5_rows, lat_blocks, u_prev=u)
        attn_n = _attention(q, k, v, kc, vc, attn_sink[i], g_attn_out[i], True)
        y_s5 = _s5_apply(u, bsz, n_tok, n_ctx, s5_ops)
        ssm_n = _glu(y_s5, bsz, n_tok, 0, True, w_glu_bf, i, b_glu[i], g_ssm_out[i])
        x1, u2, ri, rw = _out_projection(attn_n, ssm_n, x, mod3, lat_row, w_out_bf, i, ln1_g[i], ln1_b[i],
                                         w_router_t, b_router, alpha, n_moe, 0)
        if not last:
            attn_c = _attention(qc, None, None, kc, vc, attn_sink[i], g_attn_out[i], False)
            ssm_c = _glu(y_s5, bsz, n_ctx, lat_blocks, False, w_glu_bf, i, b_glu[i], g_ssm_out[i])
            xc1, u2, ric, rwc = _out_projection(attn_c, ssm_c, xc, mod3, ctx_row, w_out_bf, i, ln1_g[i], ln1_b[i],
                                                w_router_t, b_router, alpha, n_moe, n_lat, u2_prev=u2)
            ri = jnp.concatenate([ri.transpose(1, 0, 2).reshape(2, n_lat), ric.transpose(1, 0, 2).reshape(2, n_c)], axis=1)
            rw = jnp.concatenate([rw.transpose(1, 0, 2).reshape(2, n_lat), rwc.transpose(1, 0, 2).reshape(2, n_c)], axis=1)
        else:
            ri = ri.transpose(1, 0, 2).reshape(2, n_lat)
            rw = rw.transpose(1, 0, 2).reshape(2, n_lat)

        row_src, pos1, pos2, tile_e, next_e, n_used = _dispatch_plan(ri[0], ri[1], MOE_TILE)
        if not last:
            s5_ops = _s5_operators(*s5_params, i + 1, n_used)
        ys = _expert_ffn(u2, row_src, tile_e, next_e, n_used, *expert_w, i)
        y1 = _take_rows(ys, pos1)
        y2 = _take_rows(ys, pos2)
        cw1, cw2 = rw[0].reshape(n_moe, 1), rw[1].reshape(n_moe, 1)
        x = _combine_ln(x1, y1, y2, cw1, cw2, 0, mod3, lat_row, ln2_g[i], ln2_b[i], alpha)
        if not last:
            xc = _combine_ln(xc1, y1, y2, cw1, cw2, n_lat, mod3, ctx_row, ln2_g[i], ln2_b[i], alpha)
    return x
```

```python
import functools
import math

import jax
import jax.numpy as jnp
from jax import lax
from jax.experimental import pallas as pl
from jax.experimental.pallas import tpu as pltpu

F32 = jnp.float32
BF16 = jnp.bfloat16

HEAD_DIM = 128
N_Q_HEADS = 8
N_KV_HEADS = 2
Q_PER_KV = N_Q_HEADS // N_KV_HEADS
ATTN_WIDTH = N_Q_HEADS * HEAD_DIM
KV_WIDTH = N_KV_HEADS * HEAD_DIM
SSM_WIDTH = 1024
ATTN_BLOCK = 128
GRID_W = 64
ROPE_THETA = 10000.0
SSM_P = 16
SSM_G = SSM_WIDTH // SSM_P
SSM_N = 64
CHUNK = 64
GC = SSM_P * CHUNK
N_EXPERTS = 16
N_EXPERT_GROUPS = 4
EXPERTS_PER_GROUP = 4
N_MOD = 6
LN_EPS = 1e-5
NEG_INF = -1e30
LANES = 128
MOD_ROWS = 8
ROW_TILE = 256
MOE_TILE = 256
VMEM_LIMIT = 56 * 1024 * 1024

assert 2 * SSM_N == LANES and 2 * CHUNK == LANES


def _cparams(*sem):
    return pltpu.CompilerParams(dimension_semantics=sem, vmem_limit_bytes=VMEM_LIMIT)


def _dot(a, b):
    return jnp.dot(a, b, preferred_element_type=F32)


def _dot_nt(a, b):
    return lax.dot_general(a, b, (((1,), (1,)), ((), ())), preferred_element_type=F32)


def _mod_kernel(ct_ref, w_ref, b_ref, o_ref, ab_ref, *, n_rows, tn):
    d = ct_ref.shape[0]

    @pl.when((pl.program_id(0) == 0) & (pl.program_id(1) == 0))
    def _():
        ct = ct_ref[...]
        a = ct * jax.nn.sigmoid(ct)
        for r in range(n_rows):
            ab_ref[r] = jnp.broadcast_to(a[:, r:r + 1], (d, LANES))

    o_ref[...] = jnp.zeros(o_ref.shape, F32)
    sub = 8
    width = 2 * LANES
    for j in range(tn // width):
        cols = slice(j * width, (j + 1) * width)

        def k_step(kc, accs, cols=cols):
            rows = pl.ds(pl.multiple_of(kc * sub, sub), sub)
            w = w_ref[0, rows, cols]
            return tuple(acc + jnp.concatenate([ab_ref[r, rows, :]] * 2, axis=1) * w for r, acc in enumerate(accs))

        zero = jnp.zeros((sub, width), F32)
        accs = lax.fori_loop(0, d // sub, k_step, (zero,) * n_rows, unroll=8)
        for r in range(n_rows):
            o_ref[0, r:r + 1, cols] = jnp.sum(accs[r], axis=0, keepdims=True) + b_ref[0, :, cols]


def _modulation(c, c_ctx, w_mod, b_mod):
    depth, d, n_out = w_mod.shape
    n_rows = c.shape[0] + 1
    assert n_rows <= MOD_ROWS
    tn = 512
    ct = jnp.zeros((MOD_ROWS, d), F32).at[:c.shape[0]].set(c).at[c.shape[0]].set(c_ctx).T
    return pl.pallas_call(
        functools.partial(_mod_kernel, n_rows=n_rows, tn=tn),
        grid=(depth, n_out // tn),
        in_specs=[
            pl.BlockSpec((d, MOD_ROWS), lambda l, j: (0, 0)),
            pl.BlockSpec((1, d, tn), lambda l, j: (l, 0, j)),
            pl.BlockSpec((1, 1, tn), lambda l, j: (l, 0, j)),
        ],
        out_specs=pl.BlockSpec((1, MOD_ROWS, tn), lambda l, j: (l, 0, j)),
        out_shape=jax.ShapeDtypeStruct((depth, MOD_ROWS, n_out), F32),
        scratch_shapes=[pltpu.VMEM((n_rows, d, LANES), F32)],
        compiler_params=_cparams("arbitrary", "arbitrary"),
        name="modulation",
    )(ct, w_mod, b_mod.reshape(depth, 1, n_out))


def _rope(xh, cos, sin_signed):
    lane = lax.broadcasted_iota(jnp.int32, xh.shape, 1)
    swapped = jnp.where((lane % 64) < 32, pltpu.roll(xh, 96, 1), pltpu.roll(xh, 32, 1))
    return xh * cos + swapped * sin_signed


def _inproj_kernel(x_ref, mod_ref, w_ref, cos_ref, sin_ref, q_ref, k_ref, v_ref, acc_even, acc_odd, *, rope):
    i = pl.program_id(0)
    n_cols = ATTN_WIDTH + 2 * KV_WIDTH

    @pl.when(i == 0)
    def _():
        acc_odd[...] = jnp.zeros(acc_odd.shape, F32)

    def step(acc_mine, acc_prev):
        sh = mod_ref[0, 0:1, :]
        sc = mod_ref[0, 1:2, :]
        u = (x_ref[0] * (1.0 + sc) + sh).astype(BF16)
        acc_mine[...] = _dot(u, w_ref[0, :, 0:n_cols])
        scale = HEAD_DIM ** -0.5
        if rope:
            cos = cos_ref[...]
            sin = sin_ref[...]
        for h in range(N_Q_HEADS):
            qh = acc_prev[:, h * HEAD_DIM:(h + 1) * HEAD_DIM]
            if rope:
                qh = _rope(qh, cos, sin)
            q_ref[0, :, h * HEAD_DIM:(h + 1) * HEAD_DIM] = (qh * scale).astype(BF16)
        for h in range(N_KV_HEADS):
            kh = acc_prev[:, ATTN_WIDTH + h * HEAD_DIM:ATTN_WIDTH + (h + 1) * HEAD_DIM]
            if rope:
                kh = _rope(kh, cos, sin)
            k_ref[0, :, h * HEAD_DIM:(h + 1) * HEAD_DIM] = kh.astype(BF16)
        v_ref[0] = acc_prev[:, ATTN_WIDTH + KV_WIDTH:n_cols].astype(BF16)

    @pl.when(i % 2 == 0)
    def _():
        step(acc_even, acc_odd)

    @pl.when(i % 2 == 1)
    def _():
        step(acc_odd, acc_even)


def _in_projection(x, mod3, row_of_batch, w_in_bf, layer, cos_t, sin_t, rope):
    bsz, n_tok, d = x.shape
    tm = min(ROW_TILE, n_tok)
    nt = n_tok // tm
    n_steps = bsz * nt
    n_cols = ATTN_WIDTH + 2 * KV_WIDTH
    cur = lambda i: jnp.minimum(i, n_steps - 1)
    fin = lambda i: jnp.maximum(i - 1, 0)
    out = lambda w: jax.ShapeDtypeStruct((bsz, n_tok, w), BF16)
    ospec = lambda w: pl.BlockSpec((1, tm, w), lambda i: (fin(i) // nt, fin(i) % nt, 0))
    tspec = pl.BlockSpec((tm, HEAD_DIM), lambda i: (fin(i) % nt, 0))
    return pl.pallas_call(
        functools.partial(_inproj_kernel, rope=rope),
        grid=(n_steps + 1,),
        in_specs=[
            pl.BlockSpec((1, tm, d), lambda i: (cur(i) // nt, cur(i) % nt, 0)),
            pl.BlockSpec((1, N_MOD, d), lambda i: (row_of_batch(cur(i) // nt), 0, 0)),
            pl.BlockSpec((1, d, n_cols), lambda i: (layer, 0, 0)),
            tspec, tspec,
        ],
        out_specs=[ospec(ATTN_WIDTH), ospec(KV_WIDTH), ospec(KV_WIDTH)],
        out_shape=[out(ATTN_WIDTH), out(KV_WIDTH), out(KV_WIDTH)],
        scratch_shapes=[pltpu.VMEM((tm, n_cols), F32), pltpu.VMEM((tm, n_cols), F32)],
        compiler_params=_cparams("arbitrary"),
        name="in_projection",
    )(x, mod3, w_in_bf, cos_t, sin_t)


S5_TILE = 1024
S5_TILE_ROWS = 2 * S5_TILE // LANES


def _to_group_layout(st):
    x4 = st.reshape(SSM_G, SSM_P, S5_TILE // LANES, LANES)
    lane = lax.broadcasted_iota(jnp.int32, x4.shape[:1] + x4.shape[2:], 2)
    roll64 = lambda a: pltpu.roll(a.reshape(-1, LANES), CHUNK, 1).reshape(a.shape)
    out = []
    for k in range(SSM_P // 2):
        a, b = x4[:, 2 * k], x4[:, 2 * k + 1]
        h0 = jnp.where(lane < CHUNK, a, roll64(b))
        h1 = jnp.where(lane < CHUNK, roll64(a), b)
        out.append(jnp.concatenate([h0, h1], axis=1))
    return out


def _from_group_layout(blocks):
    n8 = S5_TILE // LANES
    lane = lax.broadcasted_iota(jnp.int32, (SSM_G, n8, LANES), 2)
    roll64 = lambda a: pltpu.roll(a.reshape(-1, LANES), CHUNK, 1).reshape(a.shape)
    chans = []
    for blk in blocks:
        a0, b1 = blk[:, 0:n8], blk[:, n8:]
        chans.append(jnp.where(lane < CHUNK, a0, roll64(b1)))
        chans.append(jnp.where(lane < CHUNK, roll64(a0), b1))
    return jnp.stack(chans, axis=1).reshape(SSM_WIDTH, S5_TILE)


def _sproj_kernel(x_ref, mod_ref, w_ref, *rest, n_steps):
    u_ref = rest[-1]

    @pl.when(pl.program_id(0) < n_steps)
    def _():
        sh = mod_ref[0, 0:1, :]
        sc = mod_ref[0, 1:2, :]
        x = x_ref[...].reshape(S5_TILE, x_ref.shape[-1])
        u = (x * (1.0 + sc) + sh).astype(BF16)
        st = _dot_nt(w_ref[0], u)
        for k, blk in enumerate(_to_group_layout(st)):
            u_ref[:, :, k * LANES:(k + 1) * LANES] = blk.astype(BF16)

    @pl.when(pl.program_id(0) >= n_steps)
    def _():
        u_ref[...] = jnp.zeros(u_ref.shape, BF16)


def _s_projection(x, mod3, mod_row, ws_t_bf, layer, n_rows, row_blk_off, u_prev=None):
    bsz, n_tok, d = x.shape
    if u_prev is None:
        assert n_tok % S5_TILE == 0
        nj = n_tok // S5_TILE
        n_steps = bsz * nj
        n_fill = n_rows // S5_TILE_ROWS - n_steps
        tile = lambda i: jnp.minimum(i, n_steps - 1)
        xspec = pl.BlockSpec((1, S5_TILE, d), lambda i: (tile(i) // nj, tile(i) % nj, 0))
        mspec = pl.BlockSpec((1, N_MOD, d), lambda i: (mod_row(tile(i) // nj), 0, 0))
    else:
        assert bsz * n_tok == S5_TILE and n_tok % LANES == 0
        n_steps, n_fill = 1, 0
        xspec = pl.BlockSpec((bsz, n_tok, d), lambda i: (0, 0, 0))
        mspec = pl.BlockSpec((1, N_MOD, d), lambda i: (mod_row(0), 0, 0))
    in_specs = [xspec, mspec, pl.BlockSpec((1, SSM_WIDTH, d), lambda i: (layer, 0, 0))]
    args = [x, mod3, ws_t_bf]
    aliases = {}
    if u_prev is not None:
        in_specs.append(pl.BlockSpec(memory_space=pl.ANY))
        args.append(u_prev)
        aliases = {3: 0}
    return pl.pallas_call(
        functools.partial(_sproj_kernel, n_steps=n_steps),
        grid=(n_steps + n_fill,),
        in_specs=in_specs,
        out_specs=pl.BlockSpec((SSM_G, S5_TILE_ROWS, GC), lambda i: (0, row_blk_off + i, 0)),
        out_shape=jax.ShapeDtypeStruct((SSM_G, n_rows, GC), BF16),
        input_output_aliases=aliases,
        compiler_params=_cparams("arbitrary"),
        name="s_projection",
    )(*args)


def _rope_tables(n_tok):
    half = HEAD_DIM // 2
    inv_freq = ROPE_THETA ** (-jnp.arange(0, half, 2, dtype=F32) / half)
    t = jnp.arange(n_tok)
    row = (t // GRID_W).astype(F32)
    col = (t % GRID_W).astype(F32)
    ang_r = row[:, None] * inv_freq[None, :]
    ang_c = col[:, None] * inv_freq[None, :]
    cos_t = jnp.concatenate([jnp.cos(ang_r), jnp.cos(ang_r), jnp.cos(ang_c), jnp.cos(ang_c)], axis=-1)
    sin_t = jnp.concatenate([-jnp.sin(ang_r), jnp.sin(ang_r), -jnp.sin(ang_c), jnp.sin(ang_c)], axis=-1)
    return cos_t, sin_t


def _attn_kernel(sink_ref, q_ref, *refs, has_band, n_blk):
    if has_band:
        kp_ref, ko_ref, kn_ref, vp_ref, vo_ref, vn_ref, kc_ref, vc_ref, g_ref, o_ref = refs
    else:
        kc_ref, vc_ref, g_ref, o_ref = refs
    blk = pl.program_id(1)
    rows = Q_PER_KV * ATTN_BLOCK
    qi = lax.broadcasted_iota(jnp.int32, (rows, ATTN_BLOCK), 0) % ATTN_BLOCK
    kj = lax.broadcasted_iota(jnp.int32, (rows, ATTN_BLOCK), 1)
    row_head = lax.broadcasted_iota(jnp.int32, (rows, 1), 0) // ATTN_BLOCK
    off_p = jnp.where(blk > 0, 0, ATTN_BLOCK)
    off_n = jnp.where(blk < n_blk - 1, 0, ATTN_BLOCK)
    heads = []
    for h in range(N_KV_HEADS):
        hs = slice(h * HEAD_DIM, (h + 1) * HEAD_DIM)
        q = jnp.concatenate(
            [q_ref[0, :, (h * Q_PER_KV + g) * HEAD_DIM:(h * Q_PER_KV + g + 1) * HEAD_DIM] for g in range(Q_PER_KV)],
            axis=0)
        sink = jnp.zeros((rows, 1), F32)
        for g in range(Q_PER_KV):
            sink = jnp.where(row_head == g, sink_ref[h * Q_PER_KV + g], sink)
        s_c = _dot_nt(q, kc_ref[0, :, hs])
        m = jnp.maximum(jnp.max(s_c, axis=-1, keepdims=True), sink)
        if has_band:
            s_p = jnp.where(kj >= qi + off_p, _dot_nt(q, kp_ref[0, :, hs]), NEG_INF)
            s_o = _dot_nt(q, ko_ref[0, :, hs])
            s_n = jnp.where(kj <= qi - off_n, _dot_nt(q, kn_ref[0, :, hs]), NEG_INF)
            m = jnp.maximum(m, jnp.max(jnp.maximum(jnp.maximum(s_p, s_o), s_n), axis=-1, keepdims=True))
        p_c = jnp.exp(s_c - m)
        denom = jnp.sum(p_c, axis=-1, keepdims=True) + jnp.exp(sink - m)
        acc = _dot(p_c.astype(BF16), vc_ref[0, :, hs])
        if has_band:
            for s_x, v_ref in ((s_p, vp_ref), (s_o, vo_ref), (s_n, vn_ref)):
                p_x = jnp.exp(s_x - m)
                denom = denom + jnp.sum(p_x, axis=-1, keepdims=True)
                acc = acc + _dot(p_x.astype(BF16), v_ref[0, :, hs])
        o = acc / denom
        for g in range(Q_PER_KV):
            heads.append(o[g * ATTN_BLOCK:(g + 1) * ATTN_BLOCK, :])
    ss = heads[0] * heads[0]
    for o in heads[1:]:
        ss = ss + o * o
    inv = lax.rsqrt(jnp.sum(ss, axis=-1, keepdims=True) / ATTN_WIDTH + LN_EPS)
    for i, o in enumerate(heads):
        cs = slice(i * HEAD_DIM, (i + 1) * HEAD_DIM)
        o_ref[0, :, cs] = (o * inv * g_ref[:, cs]).astype(BF16)


def _attention(q, k, v, kc, vc, sink, g_attn, has_band):
    bsz, n_tok, _ = q.shape
    n_ctx = kc.shape[1]
    n_blk = n_tok // ATTN_BLOCK
    qspec = pl.BlockSpec((1, ATTN_BLOCK, ATTN_WIDTH), lambda b, n: (b, n, 0))
    kvspec = lambda f: pl.BlockSpec((1, ATTN_BLOCK, KV_WIDTH), lambda b, n: (b, f(n), 0))
    cspec = pl.BlockSpec((1, n_ctx, KV_WIDTH), lambda b, n: (b, 0, 0))
    prev = lambda n: jnp.maximum(n - 1, 0)
    own = lambda n: n
    nxt = lambda n: jnp.minimum(n + 1, n_blk - 1)
    in_specs = [pl.BlockSpec(memory_space=pltpu.SMEM), qspec]
    args = [sink, q]
    if has_band:
        in_specs += [kvspec(prev), kvspec(own), kvspec(nxt), kvspec(prev), kvspec(own), kvspec(nxt)]
        args += [k, k, k, v, v, v]
    in_specs += [cspec, cspec, pl.BlockSpec((1, ATTN_WIDTH), lambda b, n: (0, 0))]
    args += [kc, vc, g_attn.reshape(1, ATTN_WIDTH)]
    return pl.pallas_call(
        functools.partial(_attn_kernel, has_band=has_band, n_blk=n_blk),
        grid=(bsz, n_blk),
        in_specs=in_specs,
        out_specs=qspec,
        out_shape=jax.ShapeDtypeStruct((bsz, n_tok, ATTN_WIDTH), BF16),
        compiler_params=_cparams("parallel", "parallel"),
        name="attention",
    )(*args)


def _cmul(a, b):
    return a[0] * b[0] - a[1] * b[1], a[0] * b[1] + a[1] * b[0]


def _cpow(rho, theta, expo):
    mag = jnp.exp(expo * rho)
    ang = expo * theta
    return mag * jnp.cos(ang), mag * jnp.sin(ang)


def _s5_param_kernel(after_ref, prow_ref, bt_ref, cr_ref, dd_ref, k_ref, w_ref, vt_ref, al_ref):
    del after_ref
    p_dim, lc = SSM_P, CHUNK
    lam_re, lam_im = prow_ref[0, 0:1, :], prow_ref[0, 1:2, :]
    dt = jnp.exp(prow_ref[0, 2:3, :])
    rho, theta = lam_re * dt, lam_im * dt

    sub8 = lax.broadcasted_iota(jnp.int32, (8, LANES), 0)
    asc = _cpow(rho, theta, sub8.astype(F32))
    desc = _cpow(rho, theta, (8 - sub8).astype(F32))
    ex = jnp.where(sub8 == 0, 8, jnp.where(sub8 == 1, 16, jnp.where(sub8 == 2, 32, jnp.where(sub8 == 3, lc, 1))))
    pw = _cpow(rho, theta, ex.astype(F32))
    row = lambda t, i: (t[0][i:i + 1, :], t[1][i:i + 1, :])
    cat = lambda a, b: (jnp.concatenate([a[0], b[0]], axis=0), jnp.concatenate([a[1], b[1]], axis=0))
    for i in range(3):
        step = row(pw, i)
        asc = cat(asc, _cmul(asc, step))
        desc = cat(_cmul(desc, step), desc)
    a_chunk, a_one = row(pw, 3), row(pw, 4)

    den = lam_re * lam_re + lam_im * lam_im
    x_re, x_im = a_one[0] - 1.0, a_one[1]
    beta = ((x_re * lam_re + x_im * lam_im) / den, (x_im * lam_re - x_re * lam_im) / den)
    bb = _cmul(beta, (bt_ref[0, 0], bt_ref[0, 1]))
    cc = (cr_ref[0, 0], cr_ref[0, 1])

    is_fwd = lax.broadcasted_iota(jnp.int32, (lc, LANES), 1) < SSM_N
    sub = lax.broadcasted_iota(jnp.int32, (lc, LANES), 0)
    pw_w = (jnp.where(is_fwd, desc[0], asc[0]), jnp.where(is_fwd, desc[1], asc[1]))
    pw_v = (jnp.where(is_fwd, asc[0], desc[0]), jnp.where(is_fwd, asc[1], desc[1]))

    top = (jnp.where(is_fwd, asc[0], jnp.where(sub == 0, 1.0, 0.0)), jnp.where(is_fwd, asc[1], 0.0))
    keep = (~is_fwd) & (sub > 0)
    bot = (jnp.where(keep, desc[0], 0.0), jnp.where(keep, desc[1], 0.0))
    lag = cat(top, bot)

    cx = (jnp.concatenate([cc[0]] * p_dim, axis=0), jnp.concatenate([cc[1]] * p_dim, axis=0))
    bx = tuple(jnp.concatenate([jnp.broadcast_to(b[q:q + 1, :], (p_dim, LANES)) for q in range(p_dim)], axis=0)
               for b in bb)
    e_re, e_im = _cmul(cx, bx)
    hi = lax.Precision.HIGHEST
    kmat = (jnp.dot(e_re, lag[0].T, preferred_element_type=F32, precision=hi)
            - jnp.dot(e_im, lag[1].T, preferred_element_type=F32, precision=hi))
    lane0 = lax.broadcasted_iota(jnp.int32, kmat.shape, 1) == 0
    k_ref[0] = kmat + jnp.where(lane0, dd_ref[0], 0.0)

    for q in range(p_dim):
        w_re, w_im = _cmul(pw_w, row(bb, q))
        w_ref[0, q * lc:(q + 1) * lc, :] = jnp.concatenate([w_re, w_im], axis=1).astype(BF16)
    for p in range(p_dim):
        v_re, v_im = _cmul(pw_v, row(cc, p))
        vt_ref[0, p * lc:(p + 1) * lc, :] = jnp.concatenate([v_re, -v_im], axis=1).astype(BF16)
    al_ref[0] = jnp.concatenate([a_chunk[0], a_chunk[1]], axis=1)


def _s5_operators(lam_re, lam_im, log_dt, b_re, b_im, c_re, c_im, d_skip, layer, after):
    depth = lam_re.shape[0]
    dg = depth * SSM_G
    g0 = layer * SSM_G
    vec = lambda a: jnp.moveaxis(a, 1, 2).reshape(dg, LANES)
    mat = lambda a: jnp.moveaxis(a, 1, 3).reshape(dg, SSM_P, LANES)
    ldt = jnp.broadcast_to(log_dt[..., None], lam_re.shape)
    prow = jnp.stack([vec(lam_re), vec(lam_im), vec(ldt)], axis=1)
    bt = jnp.stack([mat(jnp.swapaxes(b_re, 3, 4)), mat(jnp.swapaxes(b_im, 3, 4))], axis=1)
    cr = jnp.stack([mat(c_re), mat(c_im)], axis=1)
    dd = (d_skip.reshape(dg, 1, SSM_P) * jnp.eye(SSM_P, dtype=F32)[None]).reshape(dg, SSM_P * SSM_P, 1)
    iblk = lambda *s: pl.BlockSpec((1,) + s, lambda i: (g0 + i,) + (0,) * len(s))
    oblk = lambda *s: pl.BlockSpec((1,) + s, lambda i: (i,) + (0,) * len(s))
    return pl.pallas_call(
        _s5_param_kernel,
        grid=(SSM_G,),
        in_specs=[pl.BlockSpec(memory_space=pltpu.SMEM),
                  iblk(3, LANES), iblk(2, SSM_P, LANES), iblk(2, SSM_P, LANES), iblk(SSM_P * SSM_P, 1)],
        out_specs=[oblk(SSM_P * SSM_P, LANES), oblk(GC, 2 * LANES), oblk(GC, 2 * LANES), oblk(1, 2 * LANES)],
        out_shape=[jax.ShapeDtypeStruct((SSM_G, SSM_P * SSM_P, LANES), F32),
                   jax.ShapeDtypeStruct((SSM_G, GC, 2 * LANES), BF16),
                   jax.ShapeDtypeStruct((SSM_G, GC, 2 * LANES), BF16),
                   jax.ShapeDtypeStruct((SSM_G, 1, 2 * LANES), F32)],
        compiler_params=_cparams("parallel"),
        name="s5_operators",
    )(after, prow, bt, cr, dd)


def _exact_f32_dot(a, b):
    return jnp.dot(a, b, preferred_element_type=F32, precision=lax.Precision.HIGHEST)


def _s5_sum_kernel(u_ref, w_ref, p_ref, s_ref):
    s_ref[0] = _exact_f32_dot(p_ref[...], _dot(u_ref[0], w_ref[0]))


def _s5_scan_kernel(s_ref, al_ref, h_ref, *, bsz, n_ctx_chunks, n_chunks):
    a_re, a_im = al_ref[:, :, :LANES], al_ref[:, :, LANES:]
    shape = (s_ref.shape[0], bsz, LANES)
    is_fwd = lax.broadcasted_iota(jnp.int32, shape, 2) < SSM_N
    pair = 2 * bsz
    n_ctx_pairs, n_pairs = n_ctx_chunks // 2, n_chunks // 2

    def advance(h, s):
        return a_re * h[0] - a_im * h[1] + s[0], a_re * h[1] + a_im * h[0] + s[1]

    def body(k, h):
        kr = jnp.where(k < n_ctx_pairs, n_ctx_pairs - 1 - k, n_pairs - 1 - (k - n_ctx_pairs))
        rf = pl.ds(pl.multiple_of(k * pair, pair), pair)
        rr = pl.ds(pl.multiple_of(kr * pair, pair), pair)
        s_f, s_r = s_ref[:, rf, :], s_ref[:, rr, :]
        lo, hi = slice(0, bsz), slice(bsz, pair)
        pick = lambda first, second: (jnp.where(is_fwd, s_f[:, first, :LANES], s_r[:, second, :LANES]),
                                      jnp.where(is_fwd, s_f[:, first, LANES:], s_r[:, second, LANES:]))
        h_mid = advance(h, pick(lo, hi))
        h_out = advance(h_mid, pick(hi, lo))
        for part, off in ((0, 0), (1, LANES)):
            rows_f = jnp.concatenate([h[part], h_mid[part]], axis=1)
            rows_r = jnp.concatenate([h_mid[part], h[part]], axis=1)
            h_ref[:, rf, off:off + SSM_N] = rows_f[:, :, 0:SSM_N]
            h_ref[:, rr, off + SSM_N:off + LANES] = rows_r[:, :, SSM_N:LANES]
        return h_out

    zero = jnp.zeros(shape, F32)
    lax.fori_loop(0, n_pairs, body, (zero, zero))


def _toeplitz_blocks(k_ref, m_buf):
    lc = CHUNK
    lane = lax.broadcasted_iota(jnp.int32, (lc, LANES), 1)
    for q in range(SSM_P):
        for pp in range(SSM_P // 2):
            r0 = q * SSM_P + 2 * pp
            ka = jnp.broadcast_to(k_ref[0, r0:r0 + 1, :], (lc, LANES))
            kb = jnp.broadcast_to(k_ref[0, r0 + 1:r0 + 2, :], (lc, LANES))
            ra = pltpu.roll(ka, 0, 1, stride=1, stride_axis=0)
            rb = pltpu.roll(kb, lc, 1, stride=1, stride_axis=0)
            m_buf[q * lc:(q + 1) * lc, pp * LANES:(pp + 1) * LANES] = jnp.where(lane < lc, ra, rb).astype(BF16)


def _s5_out_kernel(u_ref, k_first_ref, k_next_ref, h_ref, pt_ref, vt_ref, y_ref, m_even, m_odd):
    g = pl.program_id(0)

    @pl.when(g == 0)
    def _():
        _toeplitz_blocks(k_first_ref, m_even)

    def step(m_mine, m_next):
        _toeplitz_blocks(k_next_ref, m_next)
        h = _exact_f32_dot(pt_ref[...], h_ref[0])
        y = _dot(u_ref[0], m_mine[...]) + _dot_nt(h.astype(BF16), vt_ref[0])
        y_ref[0] = y.astype(BF16)

    @pl.when(g % 2 == 0)
    def _():
        step(m_even, m_odd)

    @pl.when(g % 2 == 1)
    def _():
        step(m_odd, m_even)


def _chunk_major_rows(bsz, n_tok, n_ctx):
    n8, nj, ncb = S5_TILE // LANES, n_tok // S5_TILE, n_ctx // LANES
    ncc, n_chunks = n_ctx // CHUNK, (n_ctx + n_tok) // CHUNK
    n_lat = bsz * nj * S5_TILE_ROWS
    perm = []
    for c in range(n_chunks):
        for b in range(bsz):
            if c < ncc:
                blk, h = divmod(c, 2)
                perm.append(n_lat + h * bsz * ncb + b * ncb + blk)
            else:
                blk, h = divmod(c - ncc, 2)
                j, c8 = divmod(blk, n8)
                perm.append((b * nj + j) * S5_TILE_ROWS + h * n8 + c8)
    return perm


def _s5_apply(u, bsz, n_tok, n_ctx, ops):
    k_op, w_op, vt_op, al = ops
    ncc, n_chunks = n_ctx // CHUNK, (n_ctx + n_tok) // CHUNK
    rows = u.shape[1]
    assert rows == bsz * n_chunks and ncc % 2 == 0 and n_chunks % 2 == 0 and (2 * bsz) % 8 == 0
    gspec = lambda r, c: pl.BlockSpec((1, r, c), lambda g: (g, 0, 0))
    ospec = gspec
    to_chunks = jnp.zeros((rows, rows), F32).at[jnp.arange(rows), jnp.array(_chunk_major_rows(bsz, n_tok, n_ctx))].set(1.0)
    pspec = pl.BlockSpec((rows, rows), lambda g: (0, 0))
    sums = pl.pallas_call(
        _s5_sum_kernel,
        grid=(SSM_G,),
        in_specs=[gspec(rows, GC), ospec(GC, 2 * LANES), pspec],
        out_specs=gspec(rows, 2 * LANES),
        out_shape=jax.ShapeDtypeStruct((SSM_G, rows, 2 * LANES), F32),
        compiler_params=_cparams("parallel"),
        name="s5_chunk_sums",
    )(u, w_op, to_chunks)
    gt = 8
    sspec = pl.BlockSpec((gt, rows, 2 * LANES), lambda r: (r, 0, 0))
    states = pl.pallas_call(
        functools.partial(_s5_scan_kernel, bsz=bsz, n_ctx_chunks=ncc, n_chunks=n_chunks),
        grid=(SSM_G // gt,),
        in_specs=[sspec, pl.BlockSpec((gt, 1, 2 * LANES), lambda r: (r, 0, 0))],
        out_specs=sspec,
        out_shape=jax.ShapeDtypeStruct((SSM_G, rows, 2 * LANES), F32),
        compiler_params=_cparams("parallel"),
        name="s5_state_scan",
    )(sums, al)
    kspec = lambda f: pl.BlockSpec((1, SSM_P * SSM_P, LANES), lambda g: (f(g), 0, 0))
    return pl.pallas_call(
        _s5_out_kernel,
        grid=(SSM_G,),
        in_specs=[gspec(rows, GC), kspec(lambda g: 0), kspec(lambda g: jnp.minimum(g + 1, SSM_G - 1)),
                  gspec(rows, 2 * LANES), pspec, ospec(GC, 2 * LANES)],
        out_specs=gspec(rows, GC),
        out_shape=jax.ShapeDtypeStruct((SSM_G, rows, GC), BF16),
        scratch_shapes=[pltpu.VMEM((GC, GC), BF16), pltpu.VMEM((GC, GC), BF16)],
        compiler_params=_cparams("arbitrary"),
        name="s5_chunk_outputs",
    )(u, k_op, k_op, states, to_chunks.T, vt_op)


def _glu_kernel(y_ref, w_ref, b_ref, g_ref, o_ref):
    blocks = [y_ref[:, :, k * LANES:(k + 1) * LANES].astype(F32) for k in range(SSM_P // 2)]
    y = _from_group_layout(blocks).T
    z = 0.5 * y * (1.0 + jnp.tanh(math.sqrt(2.0 / math.pi) * (y + 0.044715 * (y * y * y))))
    t = _dot(z.astype(BF16), w_ref[0]) + b_ref[...]
    o = z * jax.nn.sigmoid(t)
    inv = lax.rsqrt(jnp.mean(o * o, axis=-1, keepdims=True) + LN_EPS)
    o_ref[...] = (o * inv * g_ref[...]).astype(BF16).reshape(o_ref.shape)


def _glu(y, bsz, n_tok, row_blk_off, per_batch, w_glu_bf, layer, b_glu, g_ssm):
    w = SSM_WIDTH
    if per_batch:
        grid = (bsz, n_tok // S5_TILE)
        ospec = pl.BlockSpec((1, S5_TILE, w), lambda b, j: (b, j, 0))
    else:
        assert bsz * n_tok == S5_TILE
        grid = (1, 1)
        ospec = pl.BlockSpec((bsz, n_tok, w), lambda b, j: (0, 0, 0))
    nj = grid[1]
    row = lambda a: a.reshape(1, w)
    vspec = pl.BlockSpec((1, w), lambda b, j: (0, 0))
    return pl.pallas_call(
        _glu_kernel,
        grid=grid,
        in_specs=[pl.BlockSpec((SSM_G, S5_TILE_ROWS, GC), lambda b, j: (0, row_blk_off + b * nj + j, 0)),
                  pl.BlockSpec((1, w, w), lambda b, j: (layer, 0, 0)), vspec, vspec],
        out_specs=ospec,
        out_shape=jax.ShapeDtypeStruct((bsz, n_tok, w), BF16),
        compiler_params=_cparams("parallel", "parallel"),
        name="s5_glu",
    )(y, w_glu_bf, row(b_glu), row(g_ssm))


def _layer_norm(z, g, b):
    mu = jnp.mean(z, axis=-1, keepdims=True)
    zc = z - mu
    var = jnp.mean(zc * zc, axis=-1, keepdims=True)
    return zc * lax.rsqrt(var + LN_EPS) * g + b


def _first_argmax(vals):
    best_i = jnp.zeros(vals[0].shape, jnp.int32)
    best_v = vals[0]
    for j in range(1, len(vals)):
        better = vals[j] > best_v
        best_i = jnp.where(better, j, best_i)
        best_v = jnp.where(better, vals[j], best_v)
    return best_i, best_v


def _route(logit_rows):
    m = functools.reduce(jnp.maximum, logit_rows)
    p = [jnp.exp(l - m) for l in logit_rows]
    scores = []
    for g in range(N_EXPERT_GROUPS):
        a, b, c, d = p[4 * g:4 * g + 4]
        hi1, lo1, hi2, lo2 = jnp.maximum(a, b), jnp.minimum(a, b), jnp.maximum(c, d), jnp.minimum(c, d)
        scores.append(jnp.maximum(hi1, hi2) + jnp.maximum(jnp.minimum(hi1, hi2), jnp.maximum(lo1, lo2)))
    grp, _ = _first_argmax(scores)
    sel = []
    for j in range(EXPERTS_PER_GROUP):
        v = p[j]
        for g in range(1, N_EXPERT_GROUPS):
            v = jnp.where(grp == g, p[4 * g + j], v)
        sel.append(v)
    i1, v1 = _first_argmax(sel)
    i2, v2 = _first_argmax([jnp.where(i1 == j, -1.0, sel[j]) for j in range(EXPERTS_PER_GROUP)])
    tot = v1 + v2
    return grp * EXPERTS_PER_GROUP + i1, grp * EXPERTS_PER_GROUP + i2, v1 / tot, v2 / tot


def _outproj_kernel(a_ref, s_ref, x_ref, mod_ref, w_ref, lng_ref, lnb_ref, wr_ref, br_ref, *rest, alpha, n_steps):
    x1_ref, u2_ref, ri_ref, rw_ref, acc_even, acc_odd = rest[-6:]
    i = pl.program_id(0)

    @pl.when(i == 0)
    def _():
        acc_odd[...] = jnp.zeros(acc_odd.shape, F32)

    def step(acc_mine, acc_prev):
        acc_mine[...] = _dot(a_ref[0], w_ref[0, 0:ATTN_WIDTH, :]) + _dot(s_ref[0], w_ref[0, ATTN_WIDTH:, :])
        y = acc_prev[...]
        g1 = mod_ref[0, 2:3, :]
        sh2 = mod_ref[0, 3:4, :]
        sc2 = mod_ref[0, 4:5, :]
        x1 = _layer_norm(alpha * x_ref[0] + g1 * y, lng_ref[...], lnb_ref[...])
        x1_ref[0] = x1
        u2 = x1 * (1.0 + sc2) + sh2
        u2_ref[...] = u2
        logits = _dot_nt(wr_ref[...], u2.astype(BF16)) + br_ref[...]
        e1, e2, w1, w2 = _route([logits[e:e + 1, :] for e in range(N_EXPERTS)])
        ri_ref[0, 0:1, :] = e1
        ri_ref[0, 1:2, :] = e2
        rw_ref[0, 0:1, :] = w1
        rw_ref[0, 1:2, :] = w2

    @pl.when((i <= n_steps) & (i % 2 == 0))
    def _():
        step(acc_even, acc_odd)

    @pl.when((i <= n_steps) & (i % 2 == 1))
    def _():
        step(acc_odd, acc_even)

    @pl.when(i > n_steps)
    def _():
        u2_ref[...] = jnp.zeros(u2_ref.shape, F32)


def _out_projection(attn_n, ssm_n, x, mod3, row_of_batch, w_out_bf, layer, ln_g, ln_b, w_router_t_bf, b_router,
                    alpha, u2_rows, row_off, u2_prev=None):
    bsz, n_tok, d = x.shape
    tm = min(ROW_TILE, n_tok)
    nt = n_tok // tm
    n_steps = bsz * nt
    assert row_off % tm == 0 and u2_rows % tm == 0
    n_fill = 0 if u2_prev is not None else u2_rows // tm - n_steps
    cur = lambda i: jnp.minimum(i, n_steps - 1)
    fin = lambda i: jnp.clip(i - 1, 0, n_steps - 1)
    pspec = lambda w: pl.BlockSpec((1, tm, w), lambda i: (cur(i) // nt, cur(i) % nt, 0))
    fspec = lambda w: pl.BlockSpec((1, tm, w), lambda i: (fin(i) // nt, fin(i) % nt, 0))
    vspec = pl.BlockSpec((1, d), lambda i: (0, 0))
    rspec = pl.BlockSpec((1, 2, tm), lambda i: (fin(i) // nt, 0, fin(i) % nt))
    in_specs = [
        pspec(ATTN_WIDTH), pspec(SSM_WIDTH), fspec(d),
        pl.BlockSpec((1, N_MOD, d), lambda i: (row_of_batch(fin(i) // nt), 0, 0)),
        pl.BlockSpec((1,) + w_out_bf.shape[1:], lambda i: (layer, 0, 0)),
        vspec, vspec,
        pl.BlockSpec((N_EXPERTS, d), lambda i: (0, 0)),
        pl.BlockSpec((N_EXPERTS, 1), lambda i: (0, 0)),
    ]
    args = [attn_n, ssm_n, x, mod3, w_out_bf, ln_g.reshape(1, d), ln_b.reshape(1, d), w_router_t_bf,
            b_router.reshape(N_EXPERTS, 1)]
    aliases = {}
    if u2_prev is not None:
        in_specs.append(pl.BlockSpec(memory_space=pl.ANY))
        args.append(u2_prev)
        aliases = {len(args) - 1: 1}
    return pl.pallas_call(
        functools.partial(_outproj_kernel, alpha=alpha, n_steps=n_steps),
        grid=(n_steps + 1 + n_fill,),
        in_specs=in_specs,
        out_specs=[fspec(d), pl.BlockSpec((tm, d), lambda i: (row_off // tm + jnp.maximum(i - 1, 0), 0)),
                   rspec, rspec],
        out_shape=[jax.ShapeDtypeStruct((bsz, n_tok, d), F32), jax.ShapeDtypeStruct((u2_rows, d), F32),
                   jax.ShapeDtypeStruct((bsz, 2, n_tok), jnp.int32), jax.ShapeDtypeStruct((bsz, 2, n_tok), F32)],
        scratch_shapes=[pltpu.VMEM((tm, d), F32), pltpu.VMEM((tm, d), F32)],
        input_output_aliases=aliases,
        compiler_params=_cparams("arbitrary"),
        name="out_projection",
    )(*args)


CAST_ROWS = 256
GATHER_AHEAD = 2


def _ffn_kernel(te_ref, nx_ref, nu_ref, src_ref, u_hbm, wg_hbm, wu_hbm, wd_hbm, y_ref,
                xbuf, stage_g, stage_u, stage_d, wg_bf, wu_bf, wd_bf, sem, gsem, *, e0):
    t = pl.program_id(0)
    e = te_ref[t]
    tm = xbuf.shape[1]
    pairs = ((wg_hbm, stage_g, wg_bf), (wu_hbm, stage_u, wu_bf), (wd_hbm, stage_d, wd_bf))

    def fetch(expert):
        return [pltpu.make_async_copy(hbm.at[e0 + expert], stage, sem.at[i])
                for i, (hbm, stage, _) in enumerate(pairs)]

    def gather_row(base, i, slot):
        r = src_ref[base + i]
        pltpu.make_async_copy(u_hbm.at[pl.ds(r, 1)], xbuf.at[slot, pl.ds(i, 1)], gsem.at[slot]).start()

    def wait_rows(slot):
        pltpu.make_async_copy(u_hbm.at[pl.ds(0, tm)], xbuf.at[slot], gsem.at[slot]).wait()

    last_tile = pl.num_programs(0) - 1

    @pl.when(t == 0)
    def _():
        for ahead in range(GATHER_AHEAD):
            def one_row(i, carry, ahead=ahead):
                gather_row(jnp.minimum(ahead, last_tile) * tm, i, ahead)
                return carry
            lax.fori_loop(0, tm, one_row, 0, unroll=8)
        for cp in fetch(e):
            cp.start()

    first_of_run = (t == 0) | (te_ref[jnp.maximum(t - 1, 0)] != e)

    @pl.when(first_of_run & (t < nu_ref[0]))
    def _():
        for cp in fetch(e):
            cp.wait()
        for _, stage, dst in pairs:
            def cast_rows(i, carry, stage=stage, dst=dst):
                rows = pl.ds(pl.multiple_of(i * CAST_ROWS, CAST_ROWS), CAST_ROWS)
                dst[rows, :] = stage[rows, :].astype(BF16)
                return carry
            lax.fori_loop(0, stage.shape[0] // CAST_ROWS, cast_rows, 0)

        @pl.when(nx_ref[t] >= 0)
        def _():
            for cp in fetch(nx_ref[t]):
                cp.start()

    @pl.when(t < nu_ref[0])
    def _():
        n_slots = GATHER_AHEAD + 1
        slot = t % n_slots
        wait_rows(slot)
        x = xbuf[slot].astype(BF16)
        base = jnp.minimum(t + GATHER_AHEAD, last_tile) * tm
        for i in range(tm):
            gather_row(base, i, (t + GATHER_AHEAD) % n_slots)
        g = _dot(x, wg_bf[...])
        u = _dot(x, wu_bf[...])
        h = (g * jax.nn.sigmoid(g) * u).astype(BF16)
        y_ref[...] = _dot(h, wd_bf[...]).astype(BF16)

    @pl.when(t == nu_ref[0] - 1)
    def _():
        for ahead in range(1, GATHER_AHEAD + 1):
            wait_rows((t + ahead) % (GATHER_AHEAD + 1))

    @pl.when(t >= nu_ref[0])
    def _():
        y_ref[...] = jnp.zeros(y_ref.shape, BF16)


def _expert_ffn(u2, row_src, tile_expert, next_expert, n_used, w_gate, w_up, w_down, layer):
    d = u2.shape[1]
    n_rows = row_src.shape[0]
    n_e, f = w_gate.shape[1], w_gate.shape[3]
    tm = MOE_TILE
    flat = lambda w: w.reshape((-1,) + w.shape[2:])
    hbm = pl.BlockSpec(memory_space=pl.ANY)
    grid_spec = pltpu.PrefetchScalarGridSpec(
        num_scalar_prefetch=4,
        grid=(n_rows // tm,),
        in_specs=[hbm, hbm, hbm, hbm],
        out_specs=pl.BlockSpec((tm, d), lambda t, te, nx, nu, src: (t, 0)),
        scratch_shapes=[pltpu.VMEM((GATHER_AHEAD + 1, tm, d), F32),
                        pltpu.VMEM((d, f), F32), pltpu.VMEM((d, f), F32), pltpu.VMEM((f, d), F32),
                        pltpu.VMEM((d, f), BF16), pltpu.VMEM((d, f), BF16), pltpu.VMEM((f, d), BF16),
                        pltpu.SemaphoreType.DMA((3,)), pltpu.SemaphoreType.DMA((GATHER_AHEAD + 1,))],
    )
    return pl.pallas_call(
        functools.partial(_ffn_kernel, e0=layer * n_e),
        grid_spec=grid_spec,
        out_shape=jax.ShapeDtypeStruct((n_rows, d), BF16),
        compiler_params=_cparams("arbitrary"),
        name="expert_ffn",
    )(tile_expert, next_expert, n_used, row_src, u2, flat(w_gate), flat(w_up), flat(w_down))


def _dispatch_plan(e1, e2, tm):
    n = e1.shape[0]
    n_tiles = (2 * n + N_EXPERTS * (tm - 1) + tm - 1) // tm
    e = jnp.concatenate([e1, e2])
    onehot = (e[:, None] == jnp.arange(N_EXPERTS, dtype=jnp.int32)[None, :]).astype(jnp.int32)
    csum = jnp.cumsum(onehot, axis=0)
    pos_in_e = jnp.sum((csum - 1) * onehot, axis=1)
    counts = csum[-1]
    padded = ((counts + tm - 1) // tm) * tm
    ends = jnp.cumsum(padded)
    offs = ends - padded
    dest = offs[e] + pos_in_e
    tok = jnp.concatenate([jnp.arange(n, dtype=jnp.int32)] * 2)
    row_src = jnp.zeros((n_tiles * tm,), jnp.int32).at[dest].set(tok, mode="promise_in_bounds", unique_indices=True)
    n_used = (ends[-1] // tm).astype(jnp.int32)
    tile_start = jnp.arange(n_tiles, dtype=jnp.int32) * tm
    tile_e = jnp.sum((tile_start[:, None] >= ends[None, :]).astype(jnp.int32), axis=1)
    tile_e = jnp.minimum(tile_e, N_EXPERTS - 1)
    last_e = tile_e[jnp.maximum(n_used - 1, 0)]
    tile_e = jnp.where(jnp.arange(n_tiles) < n_used, tile_e, last_e).astype(jnp.int32)
    ids = jnp.arange(N_EXPERTS, dtype=jnp.int32)
    later = (ids[None, :] > ids[:, None]) & (counts[None, :] > 0)
    next_of = jnp.min(jnp.where(later, ids[None, :], N_EXPERTS), axis=1)
    next_of = jnp.where(next_of == N_EXPERTS, -1, next_of).astype(jnp.int32)
    return row_src, dest[:n], dest[n:], tile_e, next_of[tile_e], n_used.reshape(1)


def _take_rows(a, idx):
    return a.at[idx].get(mode="promise_in_bounds")


def _final_kernel(x_ref, y1_ref, y2_ref, w1_ref, w2_ref, mod_ref, lng_ref, lnb_ref, o_ref, *, alpha):
    f = w1_ref[...] * y1_ref[...].astype(F32) + w2_ref[...] * y2_ref[...].astype(F32)
    g2 = mod_ref[0, 5:6, :]
    o_ref[0] = _layer_norm(alpha * x_ref[0] + g2 * f, lng_ref[...], lnb_ref[...])


def _combine_ln(x1, y1, y2, w1, w2, row_off, mod3, row_of_batch, ln_g, ln_b, alpha):
    bsz, n_tok, d = x1.shape
    tm = min(ROW_TILE, n_tok)
    nt = n_tok // tm
    assert row_off % tm == 0
    tspec = pl.BlockSpec((1, tm, d), lambda b, t: (b, t, 0))
    yspec = pl.BlockSpec((tm, d), lambda b, t: (row_off // tm + b * nt + t, 0))
    wspec = pl.BlockSpec((tm, 1), lambda b, t: (row_off // tm + b * nt + t, 0))
    vspec = pl.BlockSpec((1, d), lambda b, t: (0, 0))
    return pl.pallas_call(
        functools.partial(_final_kernel, alpha=alpha),
        grid=(bsz, nt),
        in_specs=[tspec, yspec, yspec, wspec, wspec,
                  pl.BlockSpec((1, N_MOD, d), lambda b, t: (row_of_batch(b), 0, 0)), vspec, vspec],
        out_specs=tspec,
        out_shape=jax.ShapeDtypeStruct((bsz, n_tok, d), F32),
        compiler_params=_cparams("parallel", "parallel"),
        name="combine_post_ln",
    )(x1, y1, y2, w1, w2, mod3, ln_g.reshape(1, d), ln_b.reshape(1, d))


def kernel(x, c, ctx, c_ctx, w_mod, b_mod, w_in, attn_sink, ssm_lambda_re, ssm_lambda_im, ssm_log_dt, ssm_b_re, ssm_b_im, ssm_c_re, ssm_c_im, ssm_d, w_glu, b_glu, g_attn_out, g_ssm_out, w_out, ln1_g, ln1_b, w_router, b_router, w_expert_gate, w_expert_up, w_expert_down, ln2_g, ln2_b):
    depth = w_mod.shape[0]
    bsz, n_tok, d = x.shape
    n_ctx = ctx.shape[1]
    alpha = (2 * depth) ** 0.25
    assert n_tok % ROW_TILE == 0 and n_tok % ATTN_BLOCK == 0 and n_ctx % ATTN_BLOCK == 0 and n_ctx % CHUNK == 0

    mod = _modulation(c, c_ctx, w_mod, b_mod)
    s5_params = (ssm_lambda_re, ssm_lambda_im, ssm_log_dt, ssm_b_re, ssm_b_im, ssm_c_re, ssm_c_im, ssm_d)
    expert_w = (w_expert_gate, w_expert_up, w_expert_down)
    s5_ops = _s5_operators(*s5_params, 0, jnp.zeros((1,), jnp.int32))
    cos_t, sin_t = _rope_tables(n_tok)
    w_router_t = w_router.T.astype(BF16)
    w_glu_bf, w_out_bf = w_glu.astype(BF16), w_out.astype(BF16)
    n_qkv = ATTN_WIDTH + 2 * KV_WIDTH
    w_in_bf = w_in[:, :, :n_qkv].astype(BF16)
    ws_t_bf = jnp.swapaxes(w_in[:, :, n_qkv:].astype(BF16), 1, 2)
    lat_row = lambda b: b
    ctx_row = lambda b: bsz
    n_lat, n_c = bsz * n_tok, bsz * n_ctx
    lat_blocks = n_lat // S5_TILE
    s5_rows = (lat_blocks + 1) * S5_TILE_ROWS

    xc = ctx
    for i in range(depth):
        last = i == depth - 1
        mod3 = mod[i].reshape(MOD_ROWS, N_MOD, d)
        n_moe = n_lat if last else n_lat + n_c

        q, k, v = _in_projection(x, mod3, lat_row, w_in_bf, i, cos_t, sin_t, True)
        qc, kc, vc = _in_projection(xc, mod3, ctx_row, w_in_bf, i, cos_t, sin_t, False)
        u = _s_projection(x, mod3, lat_row, ws_t_bf, i, s5_rows, 0)
        u = _s_projection(xc, mod3, ctx_row, ws_t_bf, i, s5_rows, lat_blocks, u_prev=u)
        attn_n = _attention(q, k, v, kc, vc, attn_sink[i], g_attn_out[i], True)
        y_s5 = _s5_apply(u, bsz, n_tok, n_ctx, s5_ops)
        ssm_n = _glu(y_s5, bsz, n_tok, 0, True, w_glu_bf, i, b_glu[i], g_ssm_out[i])
        x1, u2, ri, rw = _out_projection(attn_n, ssm_n, x, mod3, lat_row, w_out_bf, i, ln1_g[i], ln1_b[i],
                                         w_router_t, b_router, alpha, n_moe, 0)
        if not last:
            attn_c = _attention(qc, None, None, kc, vc, attn_sink[i], g_attn_out[i], False)
            ssm_c = _glu(y_s5, bsz, n_ctx, lat_blocks, False, w_glu_bf, i, b_glu[i], g_ssm_out[i])
            xc1, u2, ric, rwc = _out_projection(attn_c, ssm_c, xc, mod3, ctx_row, w_out_bf, i, ln1_g[i], ln1_b[i],
                                                w_router_t, b_router, alpha, n_moe, n_lat, u2_prev=u2)
            ri = jnp.concatenate([ri.transpose(1, 0, 2).reshape(2, n_lat), ric.transpose(1, 0, 2).reshape(2, n_c)], axis=1)
            rw = jnp.concatenate([rw.transpose(1, 0, 2).reshape(2, n_lat), rwc.transpose(1, 0, 2).reshape(2, n_c)], axis=1)
        else:
            ri = ri.transpose(1, 0, 2).reshape(2, n_lat)
            rw = rw.transpose(1, 0, 2).reshape(2, n_lat)

        row_src, pos1, pos2, tile_e, next_e, n_used = _dispatch_plan(ri[0], ri[1], MOE_TILE)
        if not last:
            s5_ops = _s5_operators(*s5_params, i + 1, n_used)
        ys = _expert_ffn(u2, row_src, tile_e, next_e, n_used, *expert_w, i)
        y1 = _take_rows(ys, pos1)
        y2 = _take_rows(ys, pos2)
        cw1, cw2 = rw[0].reshape(n_moe, 1), rw[1].reshape(n_moe, 1)
        x = _combine_ln(x1, y1, y2, cw1, cw2, 0, mod3, lat_row, ln2_g[i], ln2_b[i], alpha)
        if not last:
            xc = _combine_ln(xc1, y1, y2, cw1, cw2, n_lat, mod3, ctx_row, ln2_g[i], ln2_b[i], alpha)
    return x
```

```python
import functools
import math

import jax
import jax.numpy as jnp
from jax import lax
from jax.experimental import pallas as pl
from jax.experimental.pallas import tpu as pltpu

F32 = jnp.float32
BF16 = jnp.bfloat16

HEAD_DIM = 128
N_Q_HEADS = 8
N_KV_HEADS = 2
Q_PER_KV = N_Q_HEADS // N_KV_HEADS
ATTN_WIDTH = N_Q_HEADS * HEAD_DIM
KV_WIDTH = N_KV_HEADS * HEAD_DIM
SSM_WIDTH = 1024
ATTN_BLOCK = 128
GRID_W = 64
ROPE_THETA = 10000.0
SSM_P = 16
SSM_G = SSM_WIDTH // SSM_P
SSM_N = 64
CHUNK = 64
GC = SSM_P * CHUNK
N_EXPERTS = 16
N_EXPERT_GROUPS = 4
EXPERTS_PER_GROUP = 4
N_MOD = 6
LN_EPS = 1e-5
NEG_INF = -1e30
LANES = 128
MOD_ROWS = 8
ROW_TILE = 256
MOE_TILE = 256
VMEM_LIMIT = 56 * 1024 * 1024

assert 2 * SSM_N == LANES and 2 * CHUNK == LANES


def _cparams(*sem):
    return pltpu.CompilerParams(dimension_semantics=sem, vmem_limit_bytes=VMEM_LIMIT)


def _dot(a, b):
    return jnp.dot(a, b, preferred_element_type=F32)


def _dot_nt(a, b):
    return lax.dot_general(a, b, (((1,), (1,)), ((), ())), preferred_element_type=F32)


def _mod_kernel(ct_ref, w_ref, b_ref, o_ref, ab_ref, *, n_rows, tn):
    d = ct_ref.shape[0]

    @pl.when((pl.program_id(0) == 0) & (pl.program_id(1) == 0))
    def _():
        ct = ct_ref[...]
        a = ct * jax.nn.sigmoid(ct)
        for r in range(n_rows):
            ab_ref[r] = jnp.broadcast_to(a[:, r:r + 1], (d, LANES))

    o_ref[...] = jnp.zeros(o_ref.shape, F32)
    sub = 8
    width = 2 * LANES
    for j in range(tn // width):
        cols = slice(j * width, (j + 1) * width)

        def k_step(kc, accs, cols=cols):
            rows = pl.ds(pl.multiple_of(kc * sub, sub), sub)
            w = w_ref[0, rows, cols]
            return tuple(acc + jnp.concatenate([ab_ref[r, rows, :]] * 2, axis=1) * w for r, acc in enumerate(accs))

        zero = jnp.zeros((sub, width), F32)
        accs = lax.fori_loop(0, d // sub, k_step, (zero,) * n_rows, unroll=8)
        for r in range(n_rows):
            o_ref[0, r:r + 1, cols] = jnp.sum(accs[r], axis=0, keepdims=True) + b_ref[0, :, cols]


def _modulation(c, c_ctx, w_mod, b_mod):
    depth, d, n_out = w_mod.shape
    n_rows = c.shape[0] + 1
    assert n_rows <= MOD_ROWS
    tn = 512
    ct = jnp.zeros((MOD_ROWS, d), F32).at[:c.shape[0]].set(c).at[c.shape[0]].set(c_ctx).T
    return pl.pallas_call(
        functools.partial(_mod_kernel, n_rows=n_rows, tn=tn),
        grid=(depth, n_out // tn),
        in_specs=[
            pl.BlockSpec((d, MOD_ROWS), lambda l, j: (0, 0)),
            pl.BlockSpec((1, d, tn), lambda l, j: (l, 0, j)),
            pl.BlockSpec((1, 1, tn), lambda l, j: (l, 0, j)),
        ],
        out_specs=pl.BlockSpec((1, MOD_ROWS, tn), lambda l, j: (l, 0, j)),
        out_shape=jax.ShapeDtypeStruct((depth, MOD_ROWS, n_out), F32),
        scratch_shapes=[pltpu.VMEM((n_rows, d, LANES), F32)],
        compiler_params=_cparams("arbitrary", "arbitrary"),
        name="modulation",
    )(ct, w_mod, b_mod.reshape(depth, 1, n_out))


def _rope(xh, cos, sin_signed):
    lane = lax.broadcasted_iota(jnp.int32, xh.shape, 1)
    swapped = jnp.where((lane % 64) < 32, pltpu.roll(xh, 96, 1), pltpu.roll(xh, 32, 1))
    return xh * cos + swapped * sin_signed


def _inproj_kernel(x_ref, mod_ref, w_ref, cos_ref, sin_ref, q_ref, k_ref, v_ref, acc_even, acc_odd, *, rope):
    i = pl.program_id(0)
    n_cols = ATTN_WIDTH + 2 * KV_WIDTH

    @pl.when(i == 0)
    def _():
        acc_odd[...] = jnp.zeros(acc_odd.shape, F32)

    def step(acc_mine, acc_prev):
        sh = mod_ref[0, 0:1, :]
        sc = mod_ref[0, 1:2, :]
        u = (x_ref[0] * (1.0 + sc) + sh).astype(BF16)
        acc_mine[...] = _dot(u, w_ref[0, :, 0:n_cols])
        scale = HEAD_DIM ** -0.5
        if rope:
            cos = cos_ref[...]
            sin = sin_ref[...]
        for h in range(N_Q_HEADS):
            qh = acc_prev[:, h * HEAD_DIM:(h + 1) * HEAD_DIM]
            if rope:
                qh = _rope(qh, cos, sin)
            q_ref[0, :, h * HEAD_DIM:(h + 1) * HEAD_DIM] = (qh * scale).astype(BF16)
        for h in range(N_KV_HEADS):
            kh = acc_prev[:, ATTN_WIDTH + h * HEAD_DIM:ATTN_WIDTH + (h + 1) * HEAD_DIM]
            if rope:
                kh = _rope(kh, cos, sin)
            k_ref[0, :, h * HEAD_DIM:(h + 1) * HEAD_DIM] = kh.astype(BF16)
        v_ref[0] = acc_prev[:, ATTN_WIDTH + KV_WIDTH:n_cols].astype(BF16)

    @pl.when(i % 2 == 0)
    def _():
        step(acc_even, acc_odd)

    @pl.when(i % 2 == 1)
    def _():
        step(acc_odd, acc_even)


def _in_projection(x, mod3, row_of_batch, w_in_bf, layer, cos_t, sin_t, rope):
    bsz, n_tok, d = x.shape
    tm = min(ROW_TILE, n_tok)
    nt = n_tok // tm
    n_steps = bsz * nt
    n_cols = ATTN_WIDTH + 2 * KV_WIDTH
    cur = lambda i: jnp.minimum(i, n_steps - 1)
    fin = lambda i: jnp.maximum(i - 1, 0)
    out = lambda w: jax.ShapeDtypeStruct((bsz, n_tok, w), BF16)
    ospec = lambda w: pl.BlockSpec((1, tm, w), lambda i: (fin(i) // nt, fin(i) % nt, 0))
    tspec = pl.BlockSpec((tm, HEAD_DIM), lambda i: (fin(i) % nt, 0))
    return pl.pallas_call(
        functools.partial(_inproj_kernel, rope=rope),
        grid=(n_steps + 1,),
        in_specs=[
            pl.BlockSpec((1, tm, d), lambda i: (cur(i) // nt, cur(i) % nt, 0)),
            pl.BlockSpec((1, N_MOD, d), lambda i: (row_of_batch(cur(i) // nt), 0, 0)),
            pl.BlockSpec((1, d, n_cols), lambda i: (layer, 0, 0)),
            tspec, tspec,
        ],
        out_specs=[ospec(ATTN_WIDTH), ospec(KV_WIDTH), ospec(KV_WIDTH)],
        out_shape=[out(ATTN_WIDTH), out(KV_WIDTH), out(KV_WIDTH)],
        scratch_shapes=[pltpu.VMEM((tm, n_cols), F32), pltpu.VMEM((tm, n_cols), F32)],
        compiler_params=_cparams("arbitrary"),
        name="in_projection",
    )(x, mod3, w_in_bf, cos_t, sin_t)


S5_TILE = 1024
S5_TILE_ROWS = 2 * S5_TILE // LANES


def _to_group_layout(st):
    x4 = st.reshape(SSM_G, SSM_P, S5_TILE // LANES, LANES)
    lane = lax.broadcasted_iota(jnp.int32, x4.shape[:1] + x4.shape[2:], 2)
    roll64 = lambda a: pltpu.roll(a.reshape(-1, LANES), CHUNK, 1).reshape(a.shape)
    out = []
    for k in range(SSM_P // 2):
        a, b = x4[:, 2 * k], x4[:, 2 * k + 1]
        h0 = jnp.where(lane < CHUNK, a, roll64(b))
        h1 = jnp.where(lane < CHUNK, roll64(a), b)
        out.append(jnp.concatenate([h0, h1], axis=1))
    return out


def _from_group_layout(blocks):
    n8 = S5_TILE // LANES
    lane = lax.broadcasted_iota(jnp.int32, (SSM_G, n8, LANES), 2)
    roll64 = lambda a: pltpu.roll(a.reshape(-1, LANES), CHUNK, 1).reshape(a.shape)
    chans = []
    for blk in blocks:
        a0, b1 = blk[:, 0:n8], blk[:, n8:]
        chans.append(jnp.where(lane < CHUNK, a0, roll64(b1)))
        chans.append(jnp.where(lane < CHUNK, roll64(a0), b1))
    return jnp.stack(chans, axis=1).reshape(SSM_WIDTH, S5_TILE)


def _sproj_kernel(x_ref, mod_ref, w_ref, *rest, n_steps):
    u_ref = rest[-1]

    @pl.when(pl.program_id(0) < n_steps)
    def _():
        sh = mod_ref[0, 0:1, :]
        sc = mod_ref[0, 1:2, :]
        x = x_ref[...].reshape(S5_TILE, x_ref.shape[-1])
        u = (x * (1.0 + sc) + sh).astype(BF16)
        st = _dot_nt(w_ref[0], u)
        for k, blk in enumerate(_to_group_layout(st)):
            u_ref[:, :, k * LANES:(k + 1) * LANES] = blk.astype(BF16)

    @pl.when(pl.program_id(0) >= n_steps)
    def _():
        u_ref[...] = jnp.zeros(u_ref.shape, BF16)


def _s_projection(x, mod3, mod_row, ws_t_bf, layer, n_rows, row_blk_off, u_prev=None):
    bsz, n_tok, d = x.shape
    if u_prev is None:
        assert n_tok % S5_TILE == 0
        nj = n_tok // S5_TILE
        n_steps = bsz * nj
        n_fill = n_rows // S5_TILE_ROWS - n_steps
        tile = lambda i: jnp.minimum(i, n_steps - 1)
        xspec = pl.BlockSpec((1, S5_TILE, d), lambda i: (tile(i) // nj, tile(i) % nj, 0))
        mspec = pl.BlockSpec((1, N_MOD, d), lambda i: (mod_row(tile(i) // nj), 0, 0))
    else:
        assert bsz * n_tok == S5_TILE and n_tok % LANES == 0
        n_steps, n_fill = 1, 0
        xspec = pl.BlockSpec((bsz, n_tok, d), lambda i: (0, 0, 0))
        mspec = pl.BlockSpec((1, N_MOD, d), lambda i: (mod_row(0), 0, 0))
    in_specs = [xspec, mspec, pl.BlockSpec((1, SSM_WIDTH, d), lambda i: (layer, 0, 0))]
    args = [x, mod3, ws_t_bf]
    aliases = {}
    if u_prev is not None:
        in_specs.append(pl.BlockSpec(memory_space=pl.ANY))
        args.append(u_prev)
        aliases = {3: 0}
    return pl.pallas_call(
        functools.partial(_sproj_kernel, n_steps=n_steps),
        grid=(n_steps + n_fill,),
        in_specs=in_specs,
        out_specs=pl.BlockSpec((SSM_G, S5_TILE_ROWS, GC), lambda i: (0, row_blk_off + i, 0)),
        out_shape=jax.ShapeDtypeStruct((SSM_G, n_rows, GC), BF16),
        input_output_aliases=aliases,
        compiler_params=_cparams("arbitrary"),
        name="s_projection",
    )(*args)


def _rope_tables(n_tok):
    half = HEAD_DIM // 2
    inv_freq = ROPE_THETA ** (-jnp.arange(0, half, 2, dtype=F32) / half)
    t = jnp.arange(n_tok)
    row = (t // GRID_W).astype(F32)
    col = (t % GRID_W).astype(F32)
    ang_r = row[:, None] * inv_freq[None, :]
    ang_c = col[:, None] * inv_freq[None, :]
    cos_t = jnp.concatenate([jnp.cos(ang_r), jnp.cos(ang_r), jnp.cos(ang_c), jnp.cos(ang_c)], axis=-1)
    sin_t = jnp.concatenate([-jnp.sin(ang_r), jnp.sin(ang_r), -jnp.sin(ang_c), jnp.sin(ang_c)], axis=-1)
    return cos_t, sin_t


def _attn_kernel(sink_ref, q_ref, *refs, has_band, n_blk):
    if has_band:
        kp_ref, ko_ref, kn_ref, vp_ref, vo_ref, vn_ref, kc_ref, vc_ref, g_ref, o_ref = refs
    else:
        kc_ref, vc_ref, g_ref, o_ref = refs
    blk = pl.program_id(1)
    rows = Q_PER_KV * ATTN_BLOCK
    qi = lax.broadcasted_iota(jnp.int32, (rows, ATTN_BLOCK), 0) % ATTN_BLOCK
    kj = lax.broadcasted_iota(jnp.int32, (rows, ATTN_BLOCK), 1)
    row_head = lax.broadcasted_iota(jnp.int32, (rows, 1), 0) // ATTN_BLOCK
    off_p = jnp.where(blk > 0, 0, ATTN_BLOCK)
    off_n = jnp.where(blk < n_blk - 1, 0, ATTN_BLOCK)
    heads = []
    for h in range(N_KV_HEADS):
        hs = slice(h * HEAD_DIM, (h + 1) * HEAD_DIM)
        q = jnp.concatenate(
            [q_ref[0, :, (h * Q_PER_KV + g) * HEAD_DIM:(h * Q_PER_KV + g + 1) * HEAD_DIM] for g in range(Q_PER_KV)],
            axis=0)
        sink = jnp.zeros((rows, 1), F32)
        for g in range(Q_PER_KV):
            sink = jnp.where(row_head == g, sink_ref[h * Q_PER_KV + g], sink)
        s_c = _dot_nt(q, kc_ref[0, :, hs])
        m = jnp.maximum(jnp.max(s_c, axis=-1, keepdims=True), sink)
        if has_band:
            s_p = jnp.where(kj >= qi + off_p, _dot_nt(q, kp_ref[0, :, hs]), NEG_INF)
            s_o = _dot_nt(q, ko_ref[0, :, hs])
            s_n = jnp.where(kj <= qi - off_n, _dot_nt(q, kn_ref[0, :, hs]), NEG_INF)
            m = jnp.maximum(m, jnp.max(jnp.maximum(jnp.maximum(s_p, s_o), s_n), axis=-1, keepdims=True))
        p_c = jnp.exp(s_c - m)
        denom = jnp.sum(p_c, axis=-1, keepdims=True) + jnp.exp(sink - m)
        acc = _dot(p_c.astype(BF16), vc_ref[0, :, hs])
        if has_band:
            for s_x, v_ref in ((s_p, vp_ref), (s_o, vo_ref), (s_n, vn_ref)):
                p_x = jnp.exp(s_x - m)
                denom = denom + jnp.sum(p_x, axis=-1, keepdims=True)
                acc = acc + _dot(p_x.astype(BF16), v_ref[0, :, hs])
        o = acc / denom
        for g in range(Q_PER_KV):
            heads.append(o[g * ATTN_BLOCK:(g + 1) * ATTN_BLOCK, :])
    ss = heads[0] * heads[0]
    for o in heads[1:]:
        ss = ss + o * o
    inv = lax.rsqrt(jnp.sum(ss, axis=-1, keepdims=True) / ATTN_WIDTH + LN_EPS)
    for i, o in enumerate(heads):
        cs = slice(i * HEAD_DIM, (i + 1) * HEAD_DIM)
        o_ref[0, :, cs] = (o * inv * g_ref[:, cs]).astype(BF16)


def _attention(q, k, v, kc, vc, sink, g_attn, has_band):
    bsz, n_tok, _ = q.shape
    n_ctx = kc.shape[1]
    n_blk = n_tok // ATTN_BLOCK
    qspec = pl.BlockSpec((1, ATTN_BLOCK, ATTN_WIDTH), lambda b, n: (b, n, 0))
    kvspec = lambda f: pl.BlockSpec((1, ATTN_BLOCK, KV_WIDTH), lambda b, n: (b, f(n), 0))
    cspec = pl.BlockSpec((1, n_ctx, KV_WIDTH), lambda b, n: (b, 0, 0))
    prev = lambda n: jnp.maximum(n - 1, 0)
    own = lambda n: n
    nxt = lambda n: jnp.minimum(n + 1, n_blk - 1)
    in_specs = [pl.BlockSpec(memory_space=pltpu.SMEM), qspec]
    args = [sink, q]
    if has_band:
        in_specs += [kvspec(prev), kvspec(own), kvspec(nxt), kvspec(prev), kvspec(own), kvspec(nxt)]
        args += [k, k, k, v, v, v]
    in_specs += [cspec, cspec, pl.BlockSpec((1, ATTN_WIDTH), lambda b, n: (0, 0))]
    args += [kc, vc, g_attn.reshape(1, ATTN_WIDTH)]
    return pl.pallas_call(
        functools.partial(_attn_kernel, has_band=has_band, n_blk=n_blk),
        grid=(bsz, n_blk),
        in_specs=in_specs,
        out_specs=qspec,
        out_shape=jax.ShapeDtypeStruct((bsz, n_tok, ATTN_WIDTH), BF16),
        compiler_params=_cparams("parallel", "parallel"),
        name="attention",
    )(*args)


def _cmul(a, b):
    return a[0] * b[0] - a[1] * b[1], a[0] * b[1] + a[1] * b[0]


def _cpow(rho, theta, expo):
    mag = jnp.exp(expo * rho)
    ang = expo * theta
    return mag * jnp.cos(ang), mag * jnp.sin(ang)


def _s5_param_kernel(after_ref, prow_ref, bt_ref, cr_ref, dd_ref, k_ref, w_ref, vt_ref, al_ref):
    del after_ref
    p_dim, lc = SSM_P, CHUNK
    lam_re, lam_im = prow_ref[0, 0:1, :], prow_ref[0, 1:2, :]
    dt = jnp.exp(prow_ref[0, 2:3, :])
    rho, theta = lam_re * dt, lam_im * dt

    sub8 = lax.broadcasted_iota(jnp.int32, (8, LANES), 0)
    asc = _cpow(rho, theta, sub8.astype(F32))
    desc = _cpow(rho, theta, (8 - sub8).astype(F32))
    ex = jnp.where(sub8 == 0, 8, jnp.where(sub8 == 1, 16, jnp.where(sub8 == 2, 32, jnp.where(sub8 == 3, lc, 1))))
    pw = _cpow(rho, theta, ex.astype(F32))
    row = lambda t, i: (t[0][i:i + 1, :], t[1][i:i + 1, :])
    cat = lambda a, b: (jnp.concatenate([a[0], b[0]], axis=0), jnp.concatenate([a[1], b[1]], axis=0))
    for i in range(3):
        step = row(pw, i)
        asc = cat(asc, _cmul(asc, step))
        desc = cat(_cmul(desc, step), desc)
    a_chunk, a_one = row(pw, 3), row(pw, 4)

    den = lam_re * lam_re + lam_im * lam_im
    x_re, x_im = a_one[0] - 1.0, a_one[1]
    beta = ((x_re * lam_re + x_im * lam_im) / den, (x_im * lam_re - x_re * lam_im) / den)
    bb = _cmul(beta, (bt_ref[0, 0], bt_ref[0, 1]))
    cc = (cr_ref[0, 0], cr_ref[0, 1])

    is_fwd = lax.broadcasted_iota(jnp.int32, (lc, LANES), 1) < SSM_N
    sub = lax.broadcasted_iota(jnp.int32, (lc, LANES), 0)
    pw_w = (jnp.where(is_fwd, desc[0], asc[0]), jnp.where(is_fwd, desc[1], asc[1]))
    pw_v = (jnp.where(is_fwd, asc[0], desc[0]), jnp.where(is_fwd, asc[1], desc[1]))

    top = (jnp.where(is_fwd, asc[0], jnp.where(sub == 0, 1.0, 0.0)), jnp.where(is_fwd, asc[1], 0.0))
    keep = (~is_fwd) & (sub > 0)
    bot = (jnp.where(keep, desc[0], 0.0), jnp.where(keep, desc[1], 0.0))
    lag = cat(top, bot)

    cx = (jnp.concatenate([cc[0]] * p_dim, axis=0), jnp.concatenate([cc[1]] * p_dim, axis=0))
    bx = tuple(jnp.concatenate([jnp.broadcast_to(b[q:q + 1, :], (p_dim, LANES)) for q in range(p_dim)], axis=0)
               for b in bb)
    e_re, e_im = _cmul(cx, bx)
    hi = lax.Precision.HIGHEST
    kmat = (jnp.dot(e_re, lag[0].T, preferred_element_type=F32, precision=hi)
            - jnp.dot(e_im, lag[1].T, preferred_element_type=F32, precision=hi))
    lane0 = lax.broadcasted_iota(jnp.int32, kmat.shape, 1) == 0
    k_ref[0] = kmat + jnp.where(lane0, dd_ref[0], 0.0)

    for q in range(p_dim):
        w_re, w_im = _cmul(pw_w, row(bb, q))
        w_ref[0, q * lc:(q + 1) * lc, :] = jnp.concatenate([w_re, w_im], axis=1).astype(BF16)
    for p in range(p_dim):
        v_re, v_im = _cmul(pw_v, row(cc, p))
        vt_ref[0, p * lc:(p + 1) * lc, :] = jnp.concatenate([v_re, -v_im], axis=1).astype(BF16)
    al_ref[0] = jnp.concatenate([a_chunk[0], a_chunk[1]], axis=1)


def _s5_operators(lam_re, lam_im, log_dt, b_re, b_im, c_re, c_im, d_skip, layer, after):
    depth = lam_re.shape[0]
    dg = depth * SSM_G
    g0 = layer * SSM_G
    vec = lambda a: jnp.moveaxis(a, 1, 2).reshape(dg, LANES)
    mat = lambda a: jnp.moveaxis(a, 1, 3).reshape(dg, SSM_P, LANES)
    ldt = jnp.broadcast_to(log_dt[..., None], lam_re.shape)
    prow = jnp.stack([vec(lam_re), vec(lam_im), vec(ldt)], axis=1)
    bt = jnp.stack([mat(jnp.swapaxes(b_re, 3, 4)), mat(jnp.swapaxes(b_im, 3, 4))], axis=1)
    cr = jnp.stack([mat(c_re), mat(c_im)], axis=1)
    dd = (d_skip.reshape(dg, 1, SSM_P) * jnp.eye(SSM_P, dtype=F32)[None]).reshape(dg, SSM_P * SSM_P, 1)
    iblk = lambda *s: pl.BlockSpec((1,) + s, lambda i: (g0 + i,) + (0,) * len(s))
    oblk = lambda *s: pl.BlockSpec((1,) + s, lambda i: (i,) + (0,) * len(s))
    return pl.pallas_call(
        _s5_param_kernel,
        grid=(SSM_G,),
        in_specs=[pl.BlockSpec(memory_space=pltpu.SMEM),
                  iblk(3, LANES), iblk(2, SSM_P, LANES), iblk(2, SSM_P, LANES), iblk(SSM_P * SSM_P, 1)],
        out_specs=[oblk(SSM_P * SSM_P, LANES), oblk(GC, 2 * LANES), oblk(GC, 2 * LANES), oblk(1, 2 * LANES)],
        out_shape=[jax.ShapeDtypeStruct((SSM_G, SSM_P * SSM_P, LANES), F32),
                   jax.ShapeDtypeStruct((SSM_G, GC, 2 * LANES), BF16),
                   jax.ShapeDtypeStruct((SSM_G, GC, 2 * LANES), BF16),
                   jax.ShapeDtypeStruct((SSM_G, 1, 2 * LANES), F32)],
        compiler_params=_cparams("parallel"),
        name="s5_operators",
    )(after, prow, bt, cr, dd)


def _exact_f32_dot(a, b):
    return jnp.dot(a, b, preferred_element_type=F32, precision=lax.Precision.HIGHEST)


def _s5_sum_kernel(u_ref, w_ref, p_ref, s_ref):
    s_ref[0] = _exact_f32_dot(p_ref[...], _dot(u_ref[0], w_ref[0]))


def _s5_scan_kernel(s_ref, al_ref, h_ref, *, bsz, n_ctx_chunks, n_chunks):
    a_re, a_im = al_ref[:, :, :LANES], al_ref[:, :, LANES:]
    shape = (s_ref.shape[0], bsz, LANES)
    is_fwd = lax.broadcasted_iota(jnp.int32, shape, 2) < SSM_N
    pair = 2 * bsz
    n_ctx_pairs, n_pairs = n_ctx_chunks // 2, n_chunks // 2

    def advance(h, s):
        return a_re * h[0] - a_im * h[1] + s[0], a_re * h[1] + a_im * h[0] + s[1]

    def body(k, h):
        kr = jnp.where(k < n_ctx_pairs, n_ctx_pairs - 1 - k, n_pairs - 1 - (k - n_ctx_pairs))
        rf = pl.ds(pl.multiple_of(k * pair, pair), pair)
        rr = pl.ds(pl.multiple_of(kr * pair, pair), pair)
        s_f, s_r = s_ref[:, rf, :], s_ref[:, rr, :]
        lo, hi = slice(0, bsz), slice(bsz, pair)
        pick = lambda first, second: (jnp.where(is_fwd, s_f[:, first, :LANES], s_r[:, second, :LANES]),
                                      jnp.where(is_fwd, s_f[:, first, LANES:], s_r[:, second, LANES:]))
        h_mid = advance(h, pick(lo, hi))
        h_out = advance(h_mid, pick(hi, lo))
        for part, off in ((0, 0), (1, LANES)):
            rows_f = jnp.concatenate([h[part], h_mid[part]], axis=1)
            rows_r = jnp.concatenate([h_mid[part], h[part]], axis=1)
            h_ref[:, rf, off:off + SSM_N] = rows_f[:, :, 0:SSM_N]
            h_ref[:, rr, off + SSM_N:off + LANES] = rows_r[:, :, SSM_N:LANES]
        return h_out

    zero = jnp.zeros(shape, F32)
    lax.fori_loop(0, n_pairs, body, (zero, zero))


def _toeplitz_blocks(k_ref, m_buf):
    lc = CHUNK
    lane = lax.broadcasted_iota(jnp.int32, (lc, LANES), 1)
    for q in range(SSM_P):
        for pp in range(SSM_P // 2):
            r0 = q * SSM_P + 2 * pp
            ka = jnp.broadcast_to(k_ref[0, r0:r0 + 1, :], (lc, LANES))
            kb = jnp.broadcast_to(k_ref[0, r0 + 1:r0 + 2, :], (lc, LANES))
            ra = pltpu.roll(ka, 0, 1, stride=1, stride_axis=0)
            rb = pltpu.roll(kb, lc, 1, stride=1, stride_axis=0)
            m_buf[q * lc:(q + 1) * lc, pp * LANES:(pp + 1) * LANES] = jnp.where(lane < lc, ra, rb).astype(BF16)


def _s5_out_kernel(u_ref, k_first_ref, k_next_ref, h_ref, pt_ref, vt_ref, y_ref, m_even, m_odd):
    g = pl.program_id(0)

    @pl.when(g == 0)
    def _():
        _toeplitz_blocks(k_first_ref, m_even)

    def step(m_mine, m_next):
        _toeplitz_blocks(k_next_ref, m_next)
        h = _exact_f32_dot(pt_ref[...], h_ref[0])
        y = _dot(u_ref[0], m_mine[...]) + _dot_nt(h.astype(BF16), vt_ref[0])
        y_ref[0] = y.astype(BF16)

    @pl.when(g % 2 == 0)
    def _():
        step(m_even, m_odd)

    @pl.when(g % 2 == 1)
    def _():
        step(m_odd, m_even)


def _chunk_major_rows(bsz, n_tok, n_ctx):
    n8, nj, ncb = S5_TILE // LANES, n_tok // S5_TILE, n_ctx // LANES
    ncc, n_chunks = n_ctx // CHUNK, (n_ctx + n_tok) // CHUNK
    n_lat = bsz * nj * S5_TILE_ROWS
    perm = []
    for c in range(n_chunks):
        for b in range(bsz):
            if c < ncc:
                blk, h = divmod(c, 2)
                perm.append(n_lat + h * bsz * ncb + b * ncb + blk)
            else:
                blk, h = divmod(c - ncc, 2)
                j, c8 = divmod(blk, n8)
                perm.append((b * nj + j) * S5_TILE_ROWS + h * n8 + c8)
    return perm


def _s5_apply(u, bsz, n_tok, n_ctx, ops):
    k_op, w_op, vt_op, al = ops
    ncc, n_chunks = n_ctx // CHUNK, (n_ctx + n_tok) // CHUNK
    rows = u.shape[1]
    assert rows == bsz * n_chunks and ncc % 2 == 0 and n_chunks % 2 == 0 and (2 * bsz) % 8 == 0
    gspec = lambda r, c: pl.BlockSpec((1, r, c), lambda g: (g, 0, 0))
    ospec = gspec
    to_chunks = jnp.zeros((rows, rows), F32).at[jnp.arange(rows), jnp.array(_chunk_major_rows(bsz, n_tok, n_ctx))].set(1.0)
    pspec = pl.BlockSpec((rows, rows), lambda g: (0, 0))
    sums = pl.pallas_call(
        _s5_sum_kernel,
        grid=(SSM_G,),
        in_specs=[gspec(rows, GC), ospec(GC, 2 * LANES), pspec],
        out_specs=gspec(rows, 2 * LANES),
        out_shape=jax.ShapeDtypeStruct((SSM_G, rows, 2 * LANES), F32),
        compiler_params=_cparams("parallel"),
        name="s5_chunk_sums",
    )(u, w_op, to_chunks)
    gt = 8
    sspec = pl.BlockSpec((gt, rows, 2 * LANES), lambda r: (r, 0, 0))
    states = pl.pallas_call(
        functools.partial(_s5_scan_kernel, bsz=bsz, n_ctx_chunks=ncc, n_chunks=n_chunks),
        grid=(SSM_G // gt,),
        in_specs=[sspec, pl.BlockSpec((gt, 1, 2 * LANES), lambda r: (r, 0, 0))],
        out_specs=sspec,
        out_shape=jax.ShapeDtypeStruct((SSM_G, rows, 2 * LANES), F32),
        compiler_params=_cparams("parallel"),
        name="s5_state_scan",
    )(sums, al)
    kspec = lambda f: pl.BlockSpec((1, SSM_P * SSM_P, LANES), lambda g: (f(g), 0, 0))
    return pl.pallas_call(
        _s5_out_kernel,
        grid=(SSM_G,),
        in_specs=[gspec(rows, GC), kspec(lambda g: 0), kspec(lambda g: jnp.minimum(g + 1, SSM_G - 1)),
                  gspec(rows, 2 * LANES), pspec, ospec(GC, 2 * LANES)],
        out_specs=gspec(rows, GC),
        out_shape=jax.ShapeDtypeStruct((SSM_G, rows, GC), BF16),
        scratch_shapes=[pltpu.VMEM((GC, GC), BF16), pltpu.VMEM((GC, GC), BF16)],
        compiler_params=_cparams("arbitrary"),
        name="s5_chunk_outputs",
    )(u, k_op, k_op, states, to_chunks.T, vt_op)


def _glu_kernel(y_ref, w_ref, b_ref, g_ref, o_ref):
    blocks = [y_ref[:, :, k * LANES:(k + 1) * LANES].astype(F32) for k in range(SSM_P // 2)]
    y = _from_group_layout(blocks).T
    z = 0.5 * y * (1.0 + jnp.tanh(math.sqrt(2.0 / math.pi) * (y + 0.044715 * (y * y * y))))
    t = _dot(z.astype(BF16), w_ref[0]) + b_ref[...]
    o = z * jax.nn.sigmoid(t)
    inv = lax.rsqrt(jnp.mean(o * o, axis=-1, keepdims=True) + LN_EPS)
    o_ref[...] = (o * inv * g_ref[...]).astype(BF16).reshape(o_ref.shape)


def _glu(y, bsz, n_tok, row_blk_off, per_batch, w_glu_bf, layer, b_glu, g_ssm):
    w = SSM_WIDTH
    if per_batch:
        grid = (bsz, n_tok // S5_TILE)
        ospec = pl.BlockSpec((1, S5_TILE, w), lambda b, j: (b, j, 0))
    else:
        assert bsz * n_tok == S5_TILE
        grid = (1, 1)
        ospec = pl.BlockSpec((bsz, n_tok, w), lambda b, j: (0, 0, 0))
    nj = grid[1]
    row = lambda a: a.reshape(1, w)
    vspec = pl.BlockSpec((1, w), lambda b, j: (0, 0))
    return pl.pallas_call(
        _glu_kernel,
        grid=grid,
        in_specs=[pl.BlockSpec((SSM_G, S5_TILE_ROWS, GC), lambda b, j: (0, row_blk_off + b * nj + j, 0)),
                  pl.BlockSpec((1, w, w), lambda b, j: (layer, 0, 0)), vspec, vspec],
        out_specs=ospec,
        out_shape=jax.ShapeDtypeStruct((bsz, n_tok, w), BF16),
        compiler_params=_cparams("parallel", "parallel"),
        name="s5_glu",
    )(y, w_glu_bf, row(b_glu), row(g_ssm))


def _layer_norm(z, g, b):
    mu = jnp.mean(z, axis=-1, keepdims=True)
    zc = z - mu
    var = jnp.mean(zc * zc, axis=-1, keepdims=True)
    return zc * lax.rsqrt(var + LN_EPS) * g + b


def _first_argmax(vals):
    best_i = jnp.zeros(vals[0].shape, jnp.int32)
    best_v = vals[0]
    for j in range(1, len(vals)):
        better = vals[j] > best_v
        best_i = jnp.where(better, j, best_i)
        best_v = jnp.where(better, vals[j], best_v)
    return best_i, best_v


def _route(logit_rows):
    m = functools.reduce(jnp.maximum, logit_rows)
    p = [jnp.exp(l - m) for l in logit_rows]
    scores = []
    for g in range(N_EXPERT_GROUPS):
        a, b, c, d = p[4 * g:4 * g + 4]
        hi1, lo1, hi2, lo2 = jnp.maximum(a, b), jnp.minimum(a, b), jnp.maximum(c, d), jnp.minimum(c, d)
        scores.append(jnp.maximum(hi1, hi2) + jnp.maximum(jnp.minimum(hi1, hi2), jnp.maximum(lo1, lo2)))
    grp, _ = _first_argmax(scores)
    sel = []
    for j in range(EXPERTS_PER_GROUP):
        v = p[j]
        for g in range(1, N_EXPERT_GROUPS):
            v = jnp.where(grp == g, p[4 * g + j], v)
        sel.append(v)
    i1, v1 = _first_argmax(sel)
    i2, v2 = _first_argmax([jnp.where(i1 == j, -1.0, sel[j]) for j in range(EXPERTS_PER_GROUP)])
    tot = v1 + v2
    return grp * EXPERTS_PER_GROUP + i1, grp * EXPERTS_PER_GROUP + i2, v1 / tot, v2 / tot


def _outproj_kernel(a_ref, s_ref, x_ref, mod_ref, w_ref, lng_ref, lnb_ref, wr_ref, br_ref, *rest, alpha, n_steps):
    x1_ref, u2_ref, ri_ref, rw_ref, acc_even, acc_odd = rest[-6:]
    i = pl.program_id(0)

    @pl.when(i == 0)
    def _():
        acc_odd[...] = jnp.zeros(acc_odd.shape, F32)

    def step(acc_mine, acc_prev):
        acc_mine[...] = _dot(a_ref[0], w_ref[0, 0:ATTN_WIDTH, :]) + _dot(s_ref[0], w_ref[0, ATTN_WIDTH:, :])
        y = acc_prev[...]
        g1 = mod_ref[0, 2:3, :]
        sh2 = mod_ref[0, 3:4, :]
        sc2 = mod_ref[0, 4:5, :]
        x1 = _layer_norm(alpha * x_ref[0] + g1 * y, lng_ref[...], lnb_ref[...])
        x1_ref[0] = x1
        u2 = x1 * (1.0 + sc2) + sh2
        u2_ref[...] = u2
        logits = _dot_nt(wr_ref[...], u2.astype(BF16)) + br_ref[...]
        e1, e2, w1, w2 = _route([logits[e:e + 1, :] for e in range(N_EXPERTS)])
        ri_ref[0, 0:1, :] = e1
        ri_ref[0, 1:2, :] = e2
        rw_ref[0, 0:1, :] = w1
        rw_ref[0, 1:2, :] = w2

    @pl.when((i <= n_steps) & (i % 2 == 0))
    def _():
        step(acc_even, acc_odd)

    @pl.when((i <= n_steps) & (i % 2 == 1))
    def _():
        step(acc_odd, acc_even)

    @pl.when(i > n_steps)
    def _():
        u2_ref[...] = jnp.zeros(u2_ref.shape, F32)


def _out_projection(attn_n, ssm_n, x, mod3, row_of_batch, w_out_bf, layer, ln_g, ln_b, w_router_t_bf, b_router,
                    alpha, u2_rows, row_off, u2_prev=None):
    bsz, n_tok, d = x.shape
    tm = min(ROW_TILE, n_tok)
    nt = n_tok // tm
    n_steps = bsz * nt
    assert row_off % tm == 0 and u2_rows % tm == 0
    n_fill = 0 if u2_prev is not None else u2_rows // tm - n_steps
    cur = lambda i: jnp.minimum(i, n_steps - 1)
    fin = lambda i: jnp.clip(i - 1, 0, n_steps - 1)
    pspec = lambda w: pl.BlockSpec((1, tm, w), lambda i: (cur(i) // nt, cur(i) % nt, 0))
    fspec = lambda w: pl.BlockSpec((1, tm, w), lambda i: (fin(i) // nt, fin(i) % nt, 0))
    vspec = pl.BlockSpec((1, d), lambda i: (0, 0))
    rspec = pl.BlockSpec((1, 2, tm), lambda i: (fin(i) // nt, 0, fin(i) % nt))
    in_specs = [
        pspec(ATTN_WIDTH), pspec(SSM_WIDTH), fspec(d),
        pl.BlockSpec((1, N_MOD, d), lambda i: (row_of_batch(fin(i) // nt), 0, 0)),
        pl.BlockSpec((1,) + w_out_bf.shape[1:], lambda i: (layer, 0, 0)),
        vspec, vspec,
        pl.BlockSpec((N_EXPERTS, d), lambda i: (0, 0)),
        pl.BlockSpec((N_EXPERTS, 1), lambda i: (0, 0)),
    ]
    args = [attn_n, ssm_n, x, mod3, w_out_bf, ln_g.reshape(1, d), ln_b.reshape(1, d), w_router_t_bf,
            b_router.reshape(N_EXPERTS, 1)]
    aliases = {}
    if u2_prev is not None:
        in_specs.append(pl.BlockSpec(memory_space=pl.ANY))
        args.append(u2_prev)
        aliases = {len(args) - 1: 1}
    return pl.pallas_call(
        functools.partial(_outproj_kernel, alpha=alpha, n_steps=n_steps),
        grid=(n_steps + 1 + n_fill,),
        in_specs=in_specs,
        out_specs=[fspec(d), pl.BlockSpec((tm, d), lambda i: (row_off // tm + jnp.maximum(i - 1, 0), 0)),
                   rspec, rspec],
        out_shape=[jax.ShapeDtypeStruct((bsz, n_tok, d), F32), jax.ShapeDtypeStruct((u2_rows, d), F32),
                   jax.ShapeDtypeStruct((bsz, 2, n_tok), jnp.int32), jax.ShapeDtypeStruct((bsz, 2, n_tok), F32)],
        scratch_shapes=[pltpu.VMEM((tm, d), F32), pltpu.VMEM((tm, d), F32)],
        input_output_aliases=aliases,
        compiler_params=_cparams("arbitrary"),
        name="out_projection",
    )(*args)


CAST_ROWS = 256
GATHER_AHEAD = 2


def _ffn_kernel(te_ref, nx_ref, nu_ref, src_ref, u_hbm, wg_hbm, wu_hbm, wd_hbm, y_ref,
                xbuf, stage_g, stage_u, stage_d, wg_bf, wu_bf, wd_bf, sem, gsem, *, e0):
    t = pl.program_id(0)
    e = te_ref[t]
    tm = xbuf.shape[1]
    pairs = ((wg_hbm, stage_g, wg_bf), (wu_hbm, stage_u, wu_bf), (wd_hbm, stage_d, wd_bf))

    def fetch(expert):
        return [pltpu.make_async_copy(hbm.at[e0 + expert], stage, sem.at[i])
                for i, (hbm, stage, _) in enumerate(pairs)]

    def gather_row(base, i, slot):
        r = src_ref[base + i]
        pltpu.make_async_copy(u_hbm.at[pl.ds(r, 1)], xbuf.at[slot, pl.ds(i, 1)], gsem.at[slot]).start()

    def wait_rows(slot):
        pltpu.make_async_copy(u_hbm.at[pl.ds(0, tm)], xbuf.at[slot], gsem.at[slot]).wait()

    last_tile = pl.num_programs(0) - 1

    @pl.when(t == 0)
    def _():
        for ahead in range(GATHER_AHEAD):
            def one_row(i, carry, ahead=ahead):
                gather_row(jnp.minimum(ahead, last_tile) * tm, i, ahead)
                return carry
            lax.fori_loop(0, tm, one_row, 0, unroll=8)
        for cp in fetch(e):
            cp.start()

    first_of_run = (t == 0) | (te_ref[jnp.maximum(t - 1, 0)] != e)

    @pl.when(first_of_run & (t < nu_ref[0]))
    def _():
        for cp in fetch(e):
            cp.wait()
        for _, stage, dst in pairs:
            def cast_rows(i, carry, stage=stage, dst=dst):
                rows = pl.ds(pl.multiple_of(i * CAST_ROWS, CAST_ROWS), CAST_ROWS)
                dst[rows, :] = stage[rows, :].astype(BF16)
                return carry
            lax.fori_loop(0, stage.shape[0] // CAST_ROWS, cast_rows, 0)

        @pl.when(nx_ref[t] >= 0)
        def _():
            for cp in fetch(nx_ref[t]):
                cp.start()

    @pl.when(t < nu_ref[0])
    def _():
        n_slots = GATHER_AHEAD + 1
        slot = t % n_slots
        wait_rows(slot)
        x = xbuf[slot].astype(BF16)
        base = jnp.minimum(t + GATHER_AHEAD, last_tile) * tm
        for i in range(tm):
            gather_row(base, i, (t + GATHER_AHEAD) % n_slots)
        g = _dot(x, wg_bf[...])
        u = _dot(x, wu_bf[...])
        h = (g * jax.nn.sigmoid(g) * u).astype(BF16)
        y_ref[...] = _dot(h, wd_bf[...])

    @pl.when(t == nu_ref[0] - 1)
    def _():
        for ahead in range(1, GATHER_AHEAD + 1):
            wait_rows((t + ahead) % (GATHER_AHEAD + 1))

    @pl.when(t >= nu_ref[0])
    def _():
        y_ref[...] = jnp.zeros(y_ref.shape, F32)


def _expert_ffn(u2, row_src, tile_expert, next_expert, n_used, w_gate, w_up, w_down, layer):
    d = u2.shape[1]
    n_rows = row_src.shape[0]
    n_e, f = w_gate.shape[1], w_gate.shape[3]
    tm = MOE_TILE
    flat = lambda w: w.reshape((-1,) + w.shape[2:])
    hbm = pl.BlockSpec(memory_space=pl.ANY)
    grid_spec = pltpu.PrefetchScalarGridSpec(
        num_scalar_prefetch=4,
        grid=(n_rows // tm,),
        in_specs=[hbm, hbm, hbm, hbm],
        out_specs=pl.BlockSpec((tm, d), lambda t, te, nx, nu, src: (t, 0)),
        scratch_shapes=[pltpu.VMEM((GATHER_AHEAD + 1, tm, d), F32),
                        pltpu.VMEM((d, f), F32), pltpu.VMEM((d, f), F32), pltpu.VMEM((f, d), F32),
                        pltpu.VMEM((d, f), BF16), pltpu.VMEM((d, f), BF16), pltpu.VMEM((f, d), BF16),
                        pltpu.SemaphoreType.DMA((3,)), pltpu.SemaphoreType.DMA((GATHER_AHEAD + 1,))],
    )
    return pl.pallas_call(
        functools.partial(_ffn_kernel, e0=layer * n_e),
        grid_spec=grid_spec,
        out_shape=jax.ShapeDtypeStruct((n_rows, d), F32),
        compiler_params=_cparams("arbitrary"),
        name="expert_ffn",
    )(tile_expert, next_expert, n_used, row_src, u2, flat(w_gate), flat(w_up), flat(w_down))


def _dispatch_plan(e1, e2, tm):
    n = e1.shape[0]
    n_tiles = (2 * n + N_EXPERTS * (tm - 1) + tm - 1) // tm
    e = jnp.concatenate([e1, e2])
    onehot = (e[:, None] == jnp.arange(N_EXPERTS, dtype=jnp.int32)[None, :]).astype(jnp.int32)
    csum = jnp.cumsum(onehot, axis=0)
    pos_in_e = jnp.sum((csum - 1) * onehot, axis=1)
    counts = csum[-1]
    padded = ((counts + tm - 1) // tm) * tm
    ends = jnp.cumsum(padded)
    offs = ends - padded
    dest = offs[e] + pos_in_e
    tok = jnp.concatenate([jnp.arange(n, dtype=jnp.int32)] * 2)
    row_src = jnp.zeros((n_tiles * tm,), jnp.int32).at[dest].set(tok, mode="promise_in_bounds", unique_indices=True)
    n_used = (ends[-1] // tm).astype(jnp.int32)
    tile_start = jnp.arange(n_tiles, dtype=jnp.int32) * tm
    tile_e = jnp.sum((tile_start[:, None] >= ends[None, :]).astype(jnp.int32), axis=1)
    tile_e = jnp.minimum(tile_e, N_EXPERTS - 1)
    last_e = tile_e[jnp.maximum(n_used - 1, 0)]
    tile_e = jnp.where(jnp.arange(n_tiles) < n_used, tile_e, last_e).astype(jnp.int32)
    ids = jnp.arange(N_EXPERTS, dtype=jnp.int32)
    later = (ids[None, :] > ids[:, None]) & (counts[None, :] > 0)
    next_of = jnp.min(jnp.where(later, ids[None, :], N_EXPERTS), axis=1)
    next_of = jnp.where(next_of == N_EXPERTS, -1, next_of).astype(jnp.int32)
    return row_src, dest[:n], dest[n:], tile_e, next_of[tile_e], n_used.reshape(1)


def _final_kernel(p1_ref, p2_ref, x_ref, ys_hbm, w1_ref, w2_ref, mod_ref, lng_ref, lnb_ref, o_ref, ybuf, gsem,
                  *, alpha, row0):
    i = pl.program_id(0)
    tm = x_ref.shape[1]
    last_tile = pl.num_programs(0) - 1
    n_slots = GATHER_AHEAD + 1

    def gather_row(tile, r, slot):
        for k, p_ref in enumerate((p1_ref, p2_ref)):
            src = p_ref[row0 + tile * tm + r]
            pltpu.make_async_copy(ys_hbm.at[pl.ds(src, 1)], ybuf.at[slot, k, pl.ds(r, 1)], gsem.at[slot]).start()

    def wait_rows(slot):
        for k in range(2):
            pltpu.make_async_copy(ys_hbm.at[pl.ds(0, tm)], ybuf.at[slot, k], gsem.at[slot]).wait()

    @pl.when(i == 0)
    def _():
        for ahead in range(GATHER_AHEAD):
            def one_row(r, carry, ahead=ahead):
                gather_row(jnp.minimum(ahead, last_tile), r, ahead)
                return carry
            lax.fori_loop(0, tm, one_row, 0, unroll=8)

    slot = i % n_slots
    wait_rows(slot)
    nxt = jnp.minimum(i + GATHER_AHEAD, last_tile)
    for r in range(tm):
        gather_row(nxt, r, (i + GATHER_AHEAD) % n_slots)
    f = w1_ref[...] * ybuf[slot, 0] + w2_ref[...] * ybuf[slot, 1]
    g2 = mod_ref[0, 5:6, :]
    o_ref[0] = _layer_norm(alpha * x_ref[0] + g2 * f, lng_ref[...], lnb_ref[...])

    @pl.when(i == last_tile)
    def _():
        for ahead in range(1, GATHER_AHEAD + 1):
            wait_rows((i + ahead) % n_slots)


def _combine_ln(x1, ys, pos1, pos2, w1, w2, row_off, mod3, row_of_batch, ln_g, ln_b, alpha):
    bsz, n_tok, d = x1.shape
    tm = min(ROW_TILE, n_tok)
    nt = n_tok // tm
    assert row_off % tm == 0
    tspec = pl.BlockSpec((1, tm, d), lambda i, p1, p2: (i // nt, i % nt, 0))
    wspec = pl.BlockSpec((tm, 1), lambda i, p1, p2: (row_off // tm + i, 0))
    vspec = pl.BlockSpec((1, d), lambda i, p1, p2: (0, 0))
    grid_spec = pltpu.PrefetchScalarGridSpec(
        num_scalar_prefetch=2,
        grid=(bsz * nt,),
        in_specs=[tspec, pl.BlockSpec(memory_space=pl.ANY), wspec, wspec,
                  pl.BlockSpec((1, N_MOD, d), lambda i, p1, p2: (row_of_batch(i // nt), 0, 0)), vspec, vspec],
        out_specs=tspec,
        scratch_shapes=[pltpu.VMEM((GATHER_AHEAD + 1, 2, tm, d), F32),
                        pltpu.SemaphoreType.DMA((GATHER_AHEAD + 1,))],
    )
    return pl.pallas_call(
        functools.partial(_final_kernel, alpha=alpha, row0=row_off),
        grid_spec=grid_spec,
        out_shape=jax.ShapeDtypeStruct((bsz, n_tok, d), F32),
        compiler_params=_cparams("arbitrary"),
        name="combine_post_ln",
    )(pos1, pos2, x1, ys, w1, w2, mod3, ln_g.reshape(1, d), ln_b.reshape(1, d))


def kernel(x, c, ctx, c_ctx, w_mod, b_mod, w_in, attn_sink, ssm_lambda_re, ssm_lambda_im, ssm_log_dt, ssm_b_re, ssm_b_im, ssm_c_re, ssm_c_im, ssm_d, w_glu, b_glu, g_attn_out, g_ssm_out, w_out, ln1_g, ln1_b, w_router, b_router, w_expert_gate, w_expert_up, w_expert_down, ln2_g, ln2_b):
    depth = w_mod.shape[0]
    bsz, n_tok, d = x.shape
    n_ctx = ctx.shape[1]
    alpha = (2 * depth) ** 0.25
    assert n_tok % ROW_TILE == 0 and n_tok % ATTN_BLOCK == 0 and n_ctx % ATTN_BLOCK == 0 and n_ctx % CHUNK == 0

    mod = _modulation(c, c_ctx, w_mod, b_mod)
    s5_params = (ssm_lambda_re, ssm_lambda_im, ssm_log_dt, ssm_b_re, ssm_b_im, ssm_c_re, ssm_c_im, ssm_d)
    expert_w = (w_expert_gate, w_expert_up, w_expert_down)
    s5_ops = _s5_operators(*s5_params, 0, jnp.zeros((1,), jnp.int32))
    cos_t, sin_t = _rope_tables(n_tok)
    w_router_t = w_router.T.astype(BF16)
    w_glu_bf, w_out_bf = w_glu.astype(BF16), w_out.astype(BF16)
    n_qkv = ATTN_WIDTH + 2 * KV_WIDTH
    w_in_bf = w_in[:, :, :n_qkv].astype(BF16)
    ws_t_bf = jnp.swapaxes(w_in[:, :, n_qkv:].astype(BF16), 1, 2)
    lat_row = lambda b: b
    ctx_row = lambda b: bsz
    n_lat, n_c = bsz * n_tok, bsz * n_ctx
    lat_blocks = n_lat // S5_TILE
    s5_rows = (lat_blocks + 1) * S5_TILE_ROWS

    xc = ctx
    for i in range(depth):
        last = i == depth - 1
        mod3 = mod[i].reshape(MOD_ROWS, N_MOD, d)
        n_moe = n_lat if last else n_lat + n_c

        q, k, v = _in_projection(x, mod3, lat_row, w_in_bf, i, cos_t, sin_t, True)
        qc, kc, vc = _in_projection(xc, mod3, ctx_row, w_in_bf, i, cos_t, sin_t, False)
        u = _s_projection(x, mod3, lat_row, ws_t_bf, i, s5_rows, 0)
        u = _s_projection(xc, mod3, ctx_row, ws_t_bf, i, s5_rows, lat_blocks, u_prev=u)
        attn_n = _attention(q, k, v, kc, vc, attn_sink[i], g_attn_out[i], True)
        y_s5 = _s5_apply(u, bsz, n_tok, n_ctx, s5_ops)
        ssm_n = _glu(y_s5, bsz, n_tok, 0, True, w_glu_bf, i, b_glu[i], g_ssm_out[i])
        x1, u2, ri, rw = _out_projection(attn_n, ssm_n, x, mod3, lat_row, w_out_bf, i, ln1_g[i], ln1_b[i],
                                         w_router_t, b_router, alpha, n_moe, 0)
        if not last:
            attn_c = _attention(qc, None, None, kc, vc, attn_sink[i], g_attn_out[i], False)
            ssm_c = _glu(y_s5, bsz, n_ctx, lat_blocks, False, w_glu_bf, i, b_glu[i], g_ssm_out[i])
            xc1, u2, ric, rwc = _out_projection(attn_c, ssm_c, xc, mod3, ctx_row, w_out_bf, i, ln1_g[i], ln1_b[i],
                                                w_router_t, b_router, alpha, n_moe, n_lat, u2_prev=u2)
            ri = jnp.concatenate([ri.transpose(1, 0, 2).reshape(2, n_lat), ric.transpose(1, 0, 2).reshape(2, n_c)], axis=1)
            rw = jnp.concatenate([rw.transpose(1, 0, 2).reshape(2, n_lat), rwc.transpose(1, 0, 2).reshape(2, n_c)], axis=1)
        else:
            ri = ri.transpose(1, 0, 2).reshape(2, n_lat)
            rw = rw.transpose(1, 0, 2).reshape(2, n_lat)

        row_src, pos1, pos2, tile_e, next_e, n_used = _dispatch_plan(ri[0], ri[1], MOE_TILE)
        if not last:
            s5_ops = _s5_operators(*s5_params, i + 1, n_used)
        ys = _expert_ffn(u2, row_src, tile_e, next_e, n_used, *expert_w, i)
        cw1, cw2 = rw[0].reshape(n_moe, 1), rw[1].reshape(n_moe, 1)
        x = _combine_ln(x1, ys, pos1, pos2, cw1, cw2, 0, mod3, lat_row, ln2_g[i], ln2_b[i], alpha)
        if not last:
            xc = _combine_ln(xc1, ys, pos1, pos2, cw1, cw2, n_lat, mod3, ctx_row, ln2_g[i], ln2_b[i], alpha)
    return x
```

```python
import functools
import math

import jax
import jax.numpy as jnp
from jax import lax
from jax.experimental import pallas as pl
from jax.experimental.pallas import tpu as pltpu

F32 = jnp.float32
BF16 = jnp.bfloat16

HEAD_DIM = 128
N_Q_HEADS = 8
N_KV_HEADS = 2
Q_PER_KV = N_Q_HEADS // N_KV_HEADS
ATTN_WIDTH = N_Q_HEADS * HEAD_DIM
KV_WIDTH = N_KV_HEADS * HEAD_DIM
SSM_WIDTH = 1024
ATTN_BLOCK = 128
GRID_W = 64
ROPE_THETA = 10000.0
SSM_P = 16
SSM_G = SSM_WIDTH // SSM_P
SSM_N = 64
CHUNK = 64
GC = SSM_P * CHUNK
N_EXPERTS = 16
N_EXPERT_GROUPS = 4
EXPERTS_PER_GROUP = 4
N_MOD = 6
LN_EPS = 1e-5
NEG_INF = -1e30
LANES = 128
MOD_ROWS = 8
ROW_TILE = 256
MOE_TILE = 256
VMEM_LIMIT = 56 * 1024 * 1024

assert 2 * SSM_N == LANES and 2 * CHUNK == LANES


def _cparams(*sem):
    return pltpu.CompilerParams(dimension_semantics=sem, vmem_limit_bytes=VMEM_LIMIT)


def _dot(a, b):
    return jnp.dot(a, b, preferred_element_type=F32)


def _dot_nt(a, b):
    return lax.dot_general(a, b, (((1,), (1,)), ((), ())), preferred_element_type=F32)


def _mod_kernel(ct_ref, w_ref, b_ref, o_ref, ab_ref, *, n_rows, tn):
    d = ct_ref.shape[0]

    @pl.when((pl.program_id(0) == 0) & (pl.program_id(1) == 0))
    def _():
        ct = ct_ref[...]
        a = ct * jax.nn.sigmoid(ct)
        for r in range(n_rows):
            ab_ref[r] = jnp.broadcast_to(a[:, r:r + 1], (d, LANES))

    o_ref[...] = jnp.zeros(o_ref.shape, F32)
    sub = 8
    width = 2 * LANES
    for j in range(tn // width):
        cols = slice(j * width, (j + 1) * width)

        def k_step(kc, accs, cols=cols):
            rows = pl.ds(pl.multiple_of(kc * sub, sub), sub)
            w = w_ref[0, rows, cols]
            return tuple(acc + jnp.concatenate([ab_ref[r, rows, :]] * 2, axis=1) * w for r, acc in enumerate(accs))

        zero = jnp.zeros((sub, width), F32)
        accs = lax.fori_loop(0, d // sub, k_step, (zero,) * n_rows, unroll=8)
        for r in range(n_rows):
            o_ref[0, r:r + 1, cols] = jnp.sum(accs[r], axis=0, keepdims=True) + b_ref[0, :, cols]


def _modulation(c, c_ctx, w_mod, b_mod):
    depth, d, n_out = w_mod.shape
    n_rows = c.shape[0] + 1
    assert n_rows <= MOD_ROWS
    tn = 512
    ct = jnp.zeros((MOD_ROWS, d), F32).at[:c.shape[0]].set(c).at[c.shape[0]].set(c_ctx).T
    return pl.pallas_call(
        functools.partial(_mod_kernel, n_rows=n_rows, tn=tn),
        grid=(depth, n_out // tn),
        in_specs=[
            pl.BlockSpec((d, MOD_ROWS), lambda l, j: (0, 0)),
            pl.BlockSpec((1, d, tn), lambda l, j: (l, 0, j)),
            pl.BlockSpec((1, 1, tn), lambda l, j: (l, 0, j)),
        ],
        out_specs=pl.BlockSpec((1, MOD_ROWS, tn), lambda l, j: (l, 0, j)),
        out_shape=jax.ShapeDtypeStruct((depth, MOD_ROWS, n_out), F32),
        scratch_shapes=[pltpu.VMEM((n_rows, d, LANES), F32)],
        compiler_params=_cparams("arbitrary", "arbitrary"),
        name="modulation",
    )(ct, w_mod, b_mod.reshape(depth, 1, n_out))


def _rope(xh, cos, sin_signed):
    lane = lax.broadcasted_iota(jnp.int32, xh.shape, 1)
    swapped = jnp.where((lane % 64) < 32, pltpu.roll(xh, 96, 1), pltpu.roll(xh, 32, 1))
    return xh * cos + swapped * sin_signed


def _inproj_kernel(x_ref, mod_ref, w_ref, cos_ref, sin_ref, q_ref, k_ref, v_ref, acc_even, acc_odd, *, rope):
    i = pl.program_id(0)
    n_cols = ATTN_WIDTH + 2 * KV_WIDTH

    @pl.when(i == 0)
    def _():
        acc_odd[...] = jnp.zeros(acc_odd.shape, F32)

    def step(acc_mine, acc_prev):
        sh = mod_ref[0, 0:1, :]
        sc = mod_ref[0, 1:2, :]
        u = (x_ref[0] * (1.0 + sc) + sh).astype(BF16)
        acc_mine[...] = _dot(u, w_ref[0, :, 0:n_cols])
        scale = HEAD_DIM ** -0.5
        if rope:
            cos = cos_ref[...]
            sin = sin_ref[...]
        for h in range(N_Q_HEADS):
            qh = acc_prev[:, h * HEAD_DIM:(h + 1) * HEAD_DIM]
            if rope:
                qh = _rope(qh, cos, sin)
            q_ref[0, :, h * HEAD_DIM:(h + 1) * HEAD_DIM] = (qh * scale).astype(BF16)
        for h in range(N_KV_HEADS):
            kh = acc_prev[:, ATTN_WIDTH + h * HEAD_DIM:ATTN_WIDTH + (h + 1) * HEAD_DIM]
            if rope:
                kh = _rope(kh, cos, sin)
            k_ref[0, :, h * HEAD_DIM:(h + 1) * HEAD_DIM] = kh.astype(BF16)
        v_ref[0] = acc_prev[:, ATTN_WIDTH + KV_WIDTH:n_cols].astype(BF16)

    @pl.when(i % 2 == 0)
    def _():
        step(acc_even, acc_odd)

    @pl.when(i % 2 == 1)
    def _():
        step(acc_odd, acc_even)


def _in_projection(x, mod3, row_of_batch, w_in_bf, layer, cos_t, sin_t, rope):
    bsz, n_tok, d = x.shape
    tm = min(ROW_TILE, n_tok)
    nt = n_tok // tm
    n_steps = bsz * nt
    n_cols = ATTN_WIDTH + 2 * KV_WIDTH
    cur = lambda i: jnp.minimum(i, n_steps - 1)
    fin = lambda i: jnp.maximum(i - 1, 0)
    out = lambda w: jax.ShapeDtypeStruct((bsz, n_tok, w), BF16)
    ospec = lambda w: pl.BlockSpec((1, tm, w), lambda i: (fin(i) // nt, fin(i) % nt, 0))
    tspec = pl.BlockSpec((tm, HEAD_DIM), lambda i: (fin(i) % nt, 0))
    return pl.pallas_call(
        functools.partial(_inproj_kernel, rope=rope),
        grid=(n_steps + 1,),
        in_specs=[
            pl.BlockSpec((1, tm, d), lambda i: (cur(i) // nt, cur(i) % nt, 0)),
            pl.BlockSpec((1, N_MOD, d), lambda i: (row_of_batch(cur(i) // nt), 0, 0)),
            pl.BlockSpec((1, d, n_cols), lambda i: (layer, 0, 0)),
            tspec, tspec,
        ],
        out_specs=[ospec(ATTN_WIDTH), ospec(KV_WIDTH), ospec(KV_WIDTH)],
        out_shape=[out(ATTN_WIDTH), out(KV_WIDTH), out(KV_WIDTH)],
        scratch_shapes=[pltpu.VMEM((tm, n_cols), F32), pltpu.VMEM((tm, n_cols), F32)],
        compiler_params=_cparams("arbitrary"),
        name="in_projection",
    )(x, mod3, w_in_bf, cos_t, sin_t)


S5_TILE = 1024
S5_TILE_ROWS = 2 * S5_TILE // LANES


def _to_group_layout(st):
    x4 = st.reshape(SSM_G, SSM_P, S5_TILE // LANES, LANES)
    lane = lax.broadcasted_iota(jnp.int32, x4.shape[:1] + x4.shape[2:], 2)
    roll64 = lambda a: pltpu.roll(a.reshape(-1, LANES), CHUNK, 1).reshape(a.shape)
    out = []
    for k in range(SSM_P // 2):
        a, b = x4[:, 2 * k], x4[:, 2 * k + 1]
        h0 = jnp.where(lane < CHUNK, a, roll64(b))
        h1 = jnp.where(lane < CHUNK, roll64(a), b)
        out.append(jnp.concatenate([h0, h1], axis=1))
    return out


def _from_group_layout(blocks):
    n8 = S5_TILE // LANES
    lane = lax.broadcasted_iota(jnp.int32, (SSM_G, n8, LANES), 2)
    roll64 = lambda a: pltpu.roll(a.reshape(-1, LANES), CHUNK, 1).reshape(a.shape)
    chans = []
    for blk in blocks:
        a0, b1 = blk[:, 0:n8], blk[:, n8:]
        chans.append(jnp.where(lane < CHUNK, a0, roll64(b1)))
        chans.append(jnp.where(lane < CHUNK, roll64(a0), b1))
    return jnp.stack(chans, axis=1).reshape(SSM_WIDTH, S5_TILE)


def _sproj_kernel(x_ref, mod_ref, w_ref, *rest, n_steps):
    u_ref = rest[-1]

    @pl.when(pl.program_id(0) < n_steps)
    def _():
        sh = mod_ref[0, 0:1, :]
        sc = mod_ref[0, 1:2, :]
        x = x_ref[...].reshape(S5_TILE, x_ref.shape[-1])
        u = (x * (1.0 + sc) + sh).astype(BF16)
        st = _dot_nt(w_ref[0], u)
        for k, blk in enumerate(_to_group_layout(st)):
            u_ref[:, :, k * LANES:(k + 1) * LANES] = blk.astype(BF16)

    @pl.when(pl.program_id(0) >= n_steps)
    def _():
        u_ref[...] = jnp.zeros(u_ref.shape, BF16)


def _s_projection(x, mod3, mod_row, ws_t_bf, layer, n_rows, row_blk_off, u_prev=None):
    bsz, n_tok, d = x.shape
    if u_prev is None:
        assert n_tok % S5_TILE == 0
        nj = n_tok // S5_TILE
        n_steps = bsz * nj
        n_fill = n_rows // S5_TILE_ROWS - n_steps
        tile = lambda i: jnp.minimum(i, n_steps - 1)
        xspec = pl.BlockSpec((1, S5_TILE, d), lambda i: (tile(i) // nj, tile(i) % nj, 0))
        mspec = pl.BlockSpec((1, N_MOD, d), lambda i: (mod_row(tile(i) // nj), 0, 0))
    else:
        assert bsz * n_tok == S5_TILE and n_tok % LANES == 0
        n_steps, n_fill = 1, 0
        xspec = pl.BlockSpec((bsz, n_tok, d), lambda i: (0, 0, 0))
        mspec = pl.BlockSpec((1, N_MOD, d), lambda i: (mod_row(0), 0, 0))
    in_specs = [xspec, mspec, pl.BlockSpec((1, SSM_WIDTH, d), lambda i: (layer, 0, 0))]
    args = [x, mod3, ws_t_bf]
    aliases = {}
    if u_prev is not None:
        in_specs.append(pl.BlockSpec(memory_space=pl.ANY))
        args.append(u_prev)
        aliases = {3: 0}
    return pl.pallas_call(
        functools.partial(_sproj_kernel, n_steps=n_steps),
        grid=(n_steps + n_fill,),
        in_specs=in_specs,
        out_specs=pl.BlockSpec((SSM_G, S5_TILE_ROWS, GC), lambda i: (0, row_blk_off + i, 0)),
        out_shape=jax.ShapeDtypeStruct((SSM_G, n_rows, GC), BF16),
        input_output_aliases=aliases,
        compiler_params=_cparams("arbitrary"),
        name="s_projection",
    )(*args)


def _rope_tables(n_tok):
    half = HEAD_DIM // 2
    inv_freq = ROPE_THETA ** (-jnp.arange(0, half, 2, dtype=F32) / half)
    t = jnp.arange(n_tok)
    row = (t // GRID_W).astype(F32)
    col = (t % GRID_W).astype(F32)
    ang_r = row[:, None] * inv_freq[None, :]
    ang_c = col[:, None] * inv_freq[None, :]
    cos_t = jnp.concatenate([jnp.cos(ang_r), jnp.cos(ang_r), jnp.cos(ang_c), jnp.cos(ang_c)], axis=-1)
    sin_t = jnp.concatenate([-jnp.sin(ang_r), jnp.sin(ang_r), -jnp.sin(ang_c), jnp.sin(ang_c)], axis=-1)
    return cos_t, sin_t


def _attn_kernel(sink_ref, q_ref, *refs, has_band, n_blk):
    if has_band:
        kp_ref, ko_ref, kn_ref, vp_ref, vo_ref, vn_ref, kc_ref, vc_ref, g_ref, o_ref = refs
    else:
        kc_ref, vc_ref, g_ref, o_ref = refs
    blk = pl.program_id(1)
    rows = Q_PER_KV * ATTN_BLOCK
    qi = lax.broadcasted_iota(jnp.int32, (rows, ATTN_BLOCK), 0) % ATTN_BLOCK
    kj = lax.broadcasted_iota(jnp.int32, (rows, ATTN_BLOCK), 1)
    row_head = lax.broadcasted_iota(jnp.int32, (rows, 1), 0) // ATTN_BLOCK
    off_p = jnp.where(blk > 0, 0, ATTN_BLOCK)
    off_n = jnp.where(blk < n_blk - 1, 0, ATTN_BLOCK)
    heads = []
    for h in range(N_KV_HEADS):
        hs = slice(h * HEAD_DIM, (h + 1) * HEAD_DIM)
        q = jnp.concatenate(
            [q_ref[0, :, (h * Q_PER_KV + g) * HEAD_DIM:(h * Q_PER_KV + g + 1) * HEAD_DIM] for g in range(Q_PER_KV)],
            axis=0)
        sink = jnp.zeros((rows, 1), F32)
        for g in range(Q_PER_KV):
            sink = jnp.where(row_head == g, sink_ref[h * Q_PER_KV + g], sink)
        s_c = _dot_nt(q, kc_ref[0, :, hs])
        m = jnp.maximum(jnp.max(s_c, axis=-1, keepdims=True), sink)
        if has_band:
            s_p = jnp.where(kj >= qi + off_p, _dot_nt(q, kp_ref[0, :, hs]), NEG_INF)
            s_o = _dot_nt(q, ko_ref[0, :, hs])
            s_n = jnp.where(kj <= qi - off_n, _dot_nt(q, kn_ref[0, :, hs]), NEG_INF)
            m = jnp.maximum(m, jnp.max(jnp.maximum(jnp.maximum(s_p, s_o), s_n), axis=-1, keepdims=True))
        p_c = jnp.exp(s_c - m)
        denom = jnp.sum(p_c, axis=-1, keepdims=True) + jnp.exp(sink - m)
        acc = _dot(p_c.astype(BF16), vc_ref[0, :, hs])
        if has_band:
            for s_x, v_ref in ((s_p, vp_ref), (s_o, vo_ref), (s_n, vn_ref)):
                p_x = jnp.exp(s_x - m)
                denom = denom + jnp.sum(p_x, axis=-1, keepdims=True)
                acc = acc + _dot(p_x.astype(BF16), v_ref[0, :, hs])
        o = acc / denom
        for g in range(Q_PER_KV):
            heads.append(o[g * ATTN_BLOCK:(g + 1) * ATTN_BLOCK, :])
    ss = heads[0] * heads[0]
    for o in heads[1:]:
        ss = ss + o * o
    inv = lax.rsqrt(jnp.sum(ss, axis=-1, keepdims=True) / ATTN_WIDTH + LN_EPS)
    for i, o in enumerate(heads):
        cs = slice(i * HEAD_DIM, (i + 1) * HEAD_DIM)
        o_ref[0, :, cs] = (o * inv * g_ref[:, cs]).astype(BF16)


def _attention(q, k, v, kc, vc, sink, g_attn, has_band):
    bsz, n_tok, _ = q.shape
    n_ctx = kc.shape[1]
    n_blk = n_tok // ATTN_BLOCK
    qspec = pl.BlockSpec((1, ATTN_BLOCK, ATTN_WIDTH), lambda b, n: (b, n, 0))
    kvspec = lambda f: pl.BlockSpec((1, ATTN_BLOCK, KV_WIDTH), lambda b, n: (b, f(n), 0))
    cspec = pl.BlockSpec((1, n_ctx, KV_WIDTH), lambda b, n: (b, 0, 0))
    prev = lambda n: jnp.maximum(n - 1, 0)
    own = lambda n: n
    nxt = lambda n: jnp.minimum(n + 1, n_blk - 1)
    in_specs = [pl.BlockSpec(memory_space=pltpu.SMEM), qspec]
    args = [sink, q]
    if has_band:
        in_specs += [kvspec(prev), kvspec(own), kvspec(nxt), kvspec(prev), kvspec(own), kvspec(nxt)]
        args += [k, k, k, v, v, v]
    in_specs += [cspec, cspec, pl.BlockSpec((1, ATTN_WIDTH), lambda b, n: (0, 0))]
    args += [kc, vc, g_attn.reshape(1, ATTN_WIDTH)]
    return pl.pallas_call(
        functools.partial(_attn_kernel, has_band=has_band, n_blk=n_blk),
        grid=(bsz, n_blk),
        in_specs=in_specs,
        out_specs=qspec,
        out_shape=jax.ShapeDtypeStruct((bsz, n_tok, ATTN_WIDTH), BF16),
        compiler_params=_cparams("parallel", "parallel"),
        name="attention",
    )(*args)


def _cmul(a, b):
    return a[0] * b[0] - a[1] * b[1], a[0] * b[1] + a[1] * b[0]


def _cpow(rho, theta, expo):
    mag = jnp.exp(expo * rho)
    ang = expo * theta
    return mag * jnp.cos(ang), mag * jnp.sin(ang)


def _s5_param_kernel(after_ref, prow_ref, bt_ref, cr_ref, dd_ref, k_ref, w_ref, vt_ref, al_ref):
    del after_ref
    p_dim, lc = SSM_P, CHUNK
    lam_re, lam_im = prow_ref[0, 0:1, :], prow_ref[0, 1:2, :]
    dt = jnp.exp(prow_ref[0, 2:3, :])
    rho, theta = lam_re * dt, lam_im * dt

    sub8 = lax.broadcasted_iota(jnp.int32, (8, LANES), 0)
    asc = _cpow(rho, theta, sub8.astype(F32))
    desc = _cpow(rho, theta, (8 - sub8).astype(F32))
    ex = jnp.where(sub8 == 0, 8, jnp.where(sub8 == 1, 16, jnp.where(sub8 == 2, 32, jnp.where(sub8 == 3, lc, 1))))
    pw = _cpow(rho, theta, ex.astype(F32))
    row = lambda t, i: (t[0][i:i + 1, :], t[1][i:i + 1, :])
    cat = lambda a, b: (jnp.concatenate([a[0], b[0]], axis=0), jnp.concatenate([a[1], b[1]], axis=0))
    for i in range(3):
        step = row(pw, i)
        asc = cat(asc, _cmul(asc, step))
        desc = cat(_cmul(desc, step), desc)
    a_chunk, a_one = row(pw, 3), row(pw, 4)

    den = lam_re * lam_re + lam_im * lam_im
    x_re, x_im = a_one[0] - 1.0, a_one[1]
    beta = ((x_re * lam_re + x_im * lam_im) / den, (x_im * lam_re - x_re * lam_im) / den)
    bb = _cmul(beta, (bt_ref[0, 0], bt_ref[0, 1]))
    cc = (cr_ref[0, 0], cr_ref[0, 1])

    is_fwd = lax.broadcasted_iota(jnp.int32, (lc, LANES), 1) < SSM_N
    sub = lax.broadcasted_iota(jnp.int32, (lc, LANES), 0)
    pw_w = (jnp.where(is_fwd, desc[0], asc[0]), jnp.where(is_fwd, desc[1], asc[1]))
    pw_v = (jnp.where(is_fwd, asc[0], desc[0]), jnp.where(is_fwd, asc[1], desc[1]))

    top = (jnp.where(is_fwd, asc[0], jnp.where(sub == 0, 1.0, 0.0)), jnp.where(is_fwd, asc[1], 0.0))
    keep = (~is_fwd) & (sub > 0)
    bot = (jnp.where(keep, desc[0], 0.0), jnp.where(keep, desc[1], 0.0))
    lag = cat(top, bot)

    cx = (jnp.concatenate([cc[0]] * p_dim, axis=0), jnp.concatenate([cc[1]] * p_dim, axis=0))
    bx = tuple(jnp.concatenate([jnp.broadcast_to(b[q:q + 1, :], (p_dim, LANES)) for q in range(p_dim)], axis=0)
               for b in bb)
    e_re, e_im = _cmul(cx, bx)
    hi = lax.Precision.HIGHEST
    kmat = (jnp.dot(e_re, lag[0].T, preferred_element_type=F32, precision=hi)
            - jnp.dot(e_im, lag[1].T, preferred_element_type=F32, precision=hi))
    lane0 = lax.broadcasted_iota(jnp.int32, kmat.shape, 1) == 0
    k_ref[0] = kmat + jnp.where(lane0, dd_ref[0], 0.0)

    for q in range(p_dim):
        w_re, w_im = _cmul(pw_w, row(bb, q))
        w_ref[0, q * lc:(q + 1) * lc, :] = jnp.concatenate([w_re, w_im], axis=1).astype(BF16)
    for p in range(p_dim):
        v_re, v_im = _cmul(pw_v, row(cc, p))
        vt_ref[0, p * lc:(p + 1) * lc, :] = jnp.concatenate([v_re, -v_im], axis=1).astype(BF16)
    al_ref[0] = jnp.concatenate([a_chunk[0], a_chunk[1]], axis=1)


def _s5_operators(lam_re, lam_im, log_dt, b_re, b_im, c_re, c_im, d_skip, layer, after):
    depth = lam_re.shape[0]
    dg = depth * SSM_G
    g0 = layer * SSM_G
    vec = lambda a: jnp.moveaxis(a, 1, 2).reshape(dg, LANES)
    mat = lambda a: jnp.moveaxis(a, 1, 3).reshape(dg, SSM_P, LANES)
    ldt = jnp.broadcast_to(log_dt[..., None], lam_re.shape)
    prow = jnp.stack([vec(lam_re), vec(lam_im), vec(ldt)], axis=1)
    bt = jnp.stack([mat(jnp.swapaxes(b_re, 3, 4)), mat(jnp.swapaxes(b_im, 3, 4))], axis=1)
    cr = jnp.stack([mat(c_re), mat(c_im)], axis=1)
    dd = (d_skip.reshape(dg, 1, SSM_P) * jnp.eye(SSM_P, dtype=F32)[None]).reshape(dg, SSM_P * SSM_P, 1)
    iblk = lambda *s: pl.BlockSpec((1,) + s, lambda i: (g0 + i,) + (0,) * len(s))
    oblk = lambda *s: pl.BlockSpec((1,) + s, lambda i: (i,) + (0,) * len(s))
    return pl.pallas_call(
        _s5_param_kernel,
        grid=(SSM_G,),
        in_specs=[pl.BlockSpec(memory_space=pltpu.SMEM),
                  iblk(3, LANES), iblk(2, SSM_P, LANES), iblk(2, SSM_P, LANES), iblk(SSM_P * SSM_P, 1)],
        out_specs=[oblk(SSM_P * SSM_P, LANES), oblk(GC, 2 * LANES), oblk(GC, 2 * LANES), oblk(1, 2 * LANES)],
        out_shape=[jax.ShapeDtypeStruct((SSM_G, SSM_P * SSM_P, LANES), F32),
                   jax.ShapeDtypeStruct((SSM_G, GC, 2 * LANES), BF16),
                   jax.ShapeDtypeStruct((SSM_G, GC, 2 * LANES), BF16),
                   jax.ShapeDtypeStruct((SSM_G, 1, 2 * LANES), F32)],
        compiler_params=_cparams("parallel"),
        name="s5_operators",
    )(after, prow, bt, cr, dd)


def _exact_f32_dot(a, b):
    return jnp.dot(a, b, preferred_element_type=F32, precision=lax.Precision.HIGHEST)


def _s5_sum_kernel(u_ref, w_ref, p_ref, s_ref):
    s_ref[0] = _exact_f32_dot(p_ref[...], _dot(u_ref[0], w_ref[0]))


def _s5_scan_kernel(s_ref, al_ref, h_ref, *, bsz, n_ctx_chunks, n_chunks):
    a_re, a_im = al_ref[:, :, :LANES], al_ref[:, :, LANES:]
    shape = (s_ref.shape[0], bsz, LANES)
    is_fwd = lax.broadcasted_iota(jnp.int32, shape, 2) < SSM_N
    pair = 2 * bsz
    n_ctx_pairs, n_pairs = n_ctx_chunks // 2, n_chunks // 2

    def advance(h, s):
        return a_re * h[0] - a_im * h[1] + s[0], a_re * h[1] + a_im * h[0] + s[1]

    def body(k, h):
        kr = jnp.where(k < n_ctx_pairs, n_ctx_pairs - 1 - k, n_pairs - 1 - (k - n_ctx_pairs))
        rf = pl.ds(pl.multiple_of(k * pair, pair), pair)
        rr = pl.ds(pl.multiple_of(kr * pair, pair), pair)
        s_f, s_r = s_ref[:, rf, :], s_ref[:, rr, :]
        lo, hi = slice(0, bsz), slice(bsz, pair)
        pick = lambda first, second: (jnp.where(is_fwd, s_f[:, first, :LANES], s_r[:, second, :LANES]),
                                      jnp.where(is_fwd, s_f[:, first, LANES:], s_r[:, second, LANES:]))
        h_mid = advance(h, pick(lo, hi))
        h_out = advance(h_mid, pick(hi, lo))
        for part, off in ((0, 0), (1, LANES)):
            rows_f = jnp.concatenate([h[part], h_mid[part]], axis=1)
            rows_r = jnp.concatenate([h_mid[part], h[part]], axis=1)
            h_ref[:, rf, off:off + SSM_N] = rows_f[:, :, 0:SSM_N]
            h_ref[:, rr, off + SSM_N:off + LANES] = rows_r[:, :, SSM_N:LANES]
        return h_out

    zero = jnp.zeros(shape, F32)
    lax.fori_loop(0, n_pairs, body, (zero, zero))


def _toeplitz_blocks(k_ref, m_buf):
    lc = CHUNK
    lane = lax.broadcasted_iota(jnp.int32, (lc, LANES), 1)
    for q in range(SSM_P):
        for pp in range(SSM_P // 2):
            r0 = q * SSM_P + 2 * pp
            ka = jnp.broadcast_to(k_ref[0, r0:r0 + 1, :], (lc, LANES))
            kb = jnp.broadcast_to(k_ref[0, r0 + 1:r0 + 2, :], (lc, LANES))
            ra = pltpu.roll(ka, 0, 1, stride=1, stride_axis=0)
            rb = pltpu.roll(kb, lc, 1, stride=1, stride_axis=0)
            m_buf[q * lc:(q + 1) * lc, pp * LANES:(pp + 1) * LANES] = jnp.where(lane < lc, ra, rb).astype(BF16)


def _s5_out_kernel(u_ref, k_first_ref, k_next_ref, h_ref, pt_ref, vt_ref, y_ref, m_even, m_odd):
    g = pl.program_id(0)

    @pl.when(g == 0)
    def _():
        _toeplitz_blocks(k_first_ref, m_even)

    def step(m_mine, m_next):
        _toeplitz_blocks(k_next_ref, m_next)
        h = _exact_f32_dot(pt_ref[...], h_ref[0])
        y = _dot(u_ref[0], m_mine[...]) + _dot_nt(h.astype(BF16), vt_ref[0])
        y_ref[0] = y.astype(BF16)

    @pl.when(g % 2 == 0)
    def _():
        step(m_even, m_odd)

    @pl.when(g % 2 == 1)
    def _():
        step(m_odd, m_even)


def _chunk_major_rows(bsz, n_tok, n_ctx):
    n8, nj, ncb = S5_TILE // LANES, n_tok // S5_TILE, n_ctx // LANES
    ncc, n_chunks = n_ctx // CHUNK, (n_ctx + n_tok) // CHUNK
    n_lat = bsz * nj * S5_TILE_ROWS
    perm = []
    for c in range(n_chunks):
        for b in range(bsz):
            if c < ncc:
                blk, h = divmod(c, 2)
                perm.append(n_lat + h * bsz * ncb + b * ncb + blk)
            else:
                blk, h = divmod(c - ncc, 2)
                j, c8 = divmod(blk, n8)
                perm.append((b * nj + j) * S5_TILE_ROWS + h * n8 + c8)
    return perm


def _s5_apply(u, bsz, n_tok, n_ctx, ops):
    k_op, w_op, vt_op, al = ops
    ncc, n_chunks = n_ctx // CHUNK, (n_ctx + n_tok) // CHUNK
    rows = u.shape[1]
    assert rows == bsz * n_chunks and ncc % 2 == 0 and n_chunks % 2 == 0 and (2 * bsz) % 8 == 0
    gspec = lambda r, c: pl.BlockSpec((1, r, c), lambda g: (g, 0, 0))
    ospec = gspec
    to_chunks = jnp.zeros((rows, rows), F32).at[jnp.arange(rows), jnp.array(_chunk_major_rows(bsz, n_tok, n_ctx))].set(1.0)
    pspec = pl.BlockSpec((rows, rows), lambda g: (0, 0))
    sums = pl.pallas_call(
        _s5_sum_kernel,
        grid=(SSM_G,),
        in_specs=[gspec(rows, GC), ospec(GC, 2 * LANES), pspec],
        out_specs=gspec(rows, 2 * LANES),
        out_shape=jax.ShapeDtypeStruct((SSM_G, rows, 2 * LANES), F32),
        compiler_params=_cparams("parallel"),
        name="s5_chunk_sums",
    )(u, w_op, to_chunks)
    gt = 8
    sspec = pl.BlockSpec((gt, rows, 2 * LANES), lambda r: (r, 0, 0))
    states = pl.pallas_call(
        functools.partial(_s5_scan_kernel, bsz=bsz, n_ctx_chunks=ncc, n_chunks=n_chunks),
        grid=(SSM_G // gt,),
        in_specs=[sspec, pl.BlockSpec((gt, 1, 2 * LANES), lambda r: (r, 0, 0))],
        out_specs=sspec,
        out_shape=jax.ShapeDtypeStruct((SSM_G, rows, 2 * LANES), F32),
        compiler_params=_cparams("parallel"),
        name="s5_state_scan",
    )(sums, al)
    kspec = lambda f: pl.BlockSpec((1, SSM_P * SSM_P, LANES), lambda g: (f(g), 0, 0))
    return pl.pallas_call(
        _s5_out_kernel,
        grid=(SSM_G,),
        in_specs=[gspec(rows, GC), kspec(lambda g: 0), kspec(lambda g: jnp.minimum(g + 1, SSM_G - 1)),
                  gspec(rows, 2 * LANES), pspec, ospec(GC, 2 * LANES)],
        out_specs=gspec(rows, GC),
        out_shape=jax.ShapeDtypeStruct((SSM_G, rows, GC), BF16),
        scratch_shapes=[pltpu.VMEM((GC, GC), BF16), pltpu.VMEM((GC, GC), BF16)],
        compiler_params=_cparams("arbitrary"),
        name="s5_chunk_outputs",
    )(u, k_op, k_op, states, to_chunks.T, vt_op)


def _glu_kernel(y_ref, w_ref, b_ref, g_ref, o_ref):
    blocks = [y_ref[:, :, k * LANES:(k + 1) * LANES].astype(F32) for k in range(SSM_P // 2)]
    y = _from_group_layout(blocks).T
    z = 0.5 * y * (1.0 + jnp.tanh(math.sqrt(2.0 / math.pi) * (y + 0.044715 * (y * y * y))))
    t = _dot(z.astype(BF16), w_ref[0]) + b_ref[...]
    o = z * jax.nn.sigmoid(t)
    inv = lax.rsqrt(jnp.mean(o * o, axis=-1, keepdims=True) + LN_EPS)
    o_ref[...] = (o * inv * g_ref[...]).astype(BF16).reshape(o_ref.shape)


def _glu(y, bsz, n_tok, row_blk_off, per_batch, w_glu_bf, layer, b_glu, g_ssm):
    w = SSM_WIDTH
    if per_batch:
        grid = (bsz, n_tok // S5_TILE)
        ospec = pl.BlockSpec((1, S5_TILE, w), lambda b, j: (b, j, 0))
    else:
        assert bsz * n_tok == S5_TILE
        grid = (1, 1)
        ospec = pl.BlockSpec((bsz, n_tok, w), lambda b, j: (0, 0, 0))
    nj = grid[1]
    row = lambda a: a.reshape(1, w)
    vspec = pl.BlockSpec((1, w), lambda b, j: (0, 0))
    return pl.pallas_call(
        _glu_kernel,
        grid=grid,
        in_specs=[pl.BlockSpec((SSM_G, S5_TILE_ROWS, GC), lambda b, j: (0, row_blk_off + b * nj + j, 0)),
                  pl.BlockSpec((1, w, w), lambda b, j: (layer, 0, 0)), vspec, vspec],
        out_specs=ospec,
        out_shape=jax.ShapeDtypeStruct((bsz, n_tok, w), BF16),
        compiler_params=_cparams("parallel", "parallel"),
        name="s5_glu",
    )(y, w_glu_bf, row(b_glu), row(g_ssm))


def _layer_norm(z, g, b):
    mu = jnp.mean(z, axis=-1, keepdims=True)
    zc = z - mu
    var = jnp.mean(zc * zc, axis=-1, keepdims=True)
    return zc * lax.rsqrt(var + LN_EPS) * g + b


def _first_argmax(vals):
    best_i = jnp.zeros(vals[0].shape, jnp.int32)
    best_v = vals[0]
    for j in range(1, len(vals)):
        better = vals[j] > best_v
        best_i = jnp.where(better, j, best_i)
        best_v = jnp.where(better, vals[j], best_v)
    return best_i, best_v


def _route(logit_rows):
    m = functools.reduce(jnp.maximum, logit_rows)
    p = [jnp.exp(l - m) for l in logit_rows]
    scores = []
    for g in range(N_EXPERT_GROUPS):
        a, b, c, d = p[4 * g:4 * g + 4]
        hi1, lo1, hi2, lo2 = jnp.maximum(a, b), jnp.minimum(a, b), jnp.maximum(c, d), jnp.minimum(c, d)
        scores.append(jnp.maximum(hi1, hi2) + jnp.maximum(jnp.minimum(hi1, hi2), jnp.maximum(lo1, lo2)))
    grp, _ = _first_argmax(scores)
    sel = []
    for j in range(EXPERTS_PER_GROUP):
        v = p[j]
        for g in range(1, N_EXPERT_GROUPS):
            v = jnp.where(grp == g, p[4 * g + j], v)
        sel.append(v)
    i1, v1 = _first_argmax(sel)
    i2, v2 = _first_argmax([jnp.where(i1 == j, -1.0, sel[j]) for j in range(EXPERTS_PER_GROUP)])
    tot = v1 + v2
    return grp * EXPERTS_PER_GROUP + i1, grp * EXPERTS_PER_GROUP + i2, v1 / tot, v2 / tot


def _outproj_kernel(a_ref, s_ref, x_ref, mod_ref, w_ref, lng_ref, lnb_ref, wr_ref, br_ref, *rest, alpha, n_steps):
    x1_ref, u2_ref, ri_ref, rw_ref, acc_even, acc_odd = rest[-6:]
    i = pl.program_id(0)

    @pl.when(i == 0)
    def _():
        acc_odd[...] = jnp.zeros(acc_odd.shape, F32)

    def step(acc_mine, acc_prev):
        acc_mine[...] = _dot(a_ref[0], w_ref[0, 0:ATTN_WIDTH, :]) + _dot(s_ref[0], w_ref[0, ATTN_WIDTH:, :])
        y = acc_prev[...]
        g1 = mod_ref[0, 2:3, :]
        sh2 = mod_ref[0, 3:4, :]
        sc2 = mod_ref[0, 4:5, :]
        x1 = _layer_norm(alpha * x_ref[0] + g1 * y, lng_ref[...], lnb_ref[...])
        x1_ref[0] = x1
        u2 = x1 * (1.0 + sc2) + sh2
        u2_ref[...] = u2
        logits = _dot_nt(wr_ref[...], u2.astype(BF16)) + br_ref[...]
        e1, e2, w1, w2 = _route([logits[e:e + 1, :] for e in range(N_EXPERTS)])
        ri_ref[0, 0:1, :] = e1
        ri_ref[0, 1:2, :] = e2
        rw_ref[0, 0:1, :] = w1
        rw_ref[0, 1:2, :] = w2

    @pl.when((i <= n_steps) & (i % 2 == 0))
    def _():
        step(acc_even, acc_odd)

    @pl.when((i <= n_steps) & (i % 2 == 1))
    def _():
        step(acc_odd, acc_even)

    @pl.when(i > n_steps)
    def _():
        u2_ref[...] = jnp.zeros(u2_ref.shape, F32)


def _out_projection(attn_n, ssm_n, x, mod3, row_of_batch, w_out_bf, layer, ln_g, ln_b, w_router_t_bf, b_router,
                    alpha, u2_rows, row_off, u2_prev=None):
    bsz, n_tok, d = x.shape
    tm = min(ROW_TILE, n_tok)
    nt = n_tok // tm
    n_steps = bsz * nt
    assert row_off % tm == 0 and u2_rows % tm == 0
    n_fill = 0 if u2_prev is not None else u2_rows // tm - n_steps
    cur = lambda i: jnp.minimum(i, n_steps - 1)
    fin = lambda i: jnp.clip(i - 1, 0, n_steps - 1)
    pspec = lambda w: pl.BlockSpec((1, tm, w), lambda i: (cur(i) // nt, cur(i) % nt, 0))
    fspec = lambda w: pl.BlockSpec((1, tm, w), lambda i: (fin(i) // nt, fin(i) % nt, 0))
    vspec = pl.BlockSpec((1, d), lambda i: (0, 0))
    rspec = pl.BlockSpec((1, 2, tm), lambda i: (fin(i) // nt, 0, fin(i) % nt))
    in_specs = [
        pspec(ATTN_WIDTH), pspec(SSM_WIDTH), fspec(d),
        pl.BlockSpec((1, N_MOD, d), lambda i: (row_of_batch(fin(i) // nt), 0, 0)),
        pl.BlockSpec((1,) + w_out_bf.shape[1:], lambda i: (layer, 0, 0)),
        vspec, vspec,
        pl.BlockSpec((N_EXPERTS, d), lambda i: (0, 0)),
        pl.BlockSpec((N_EXPERTS, 1), lambda i: (0, 0)),
    ]
    args = [attn_n, ssm_n, x, mod3, w_out_bf, ln_g.reshape(1, d), ln_b.reshape(1, d), w_router_t_bf,
            b_router.reshape(N_EXPERTS, 1)]
    aliases = {}
    if u2_prev is not None:
        in_specs.append(pl.BlockSpec(memory_space=pl.ANY))
        args.append(u2_prev)
        aliases = {len(args) - 1: 1}
    return pl.pallas_call(
        functools.partial(_outproj_kernel, alpha=alpha, n_steps=n_steps),
        grid=(n_steps + 1 + n_fill,),
        in_specs=in_specs,
        out_specs=[fspec(d), pl.BlockSpec((tm, d), lambda i: (row_off // tm + jnp.maximum(i - 1, 0), 0)),
                   rspec, rspec],
        out_shape=[jax.ShapeDtypeStruct((bsz, n_tok, d), F32), jax.ShapeDtypeStruct((u2_rows, d), F32),
                   jax.ShapeDtypeStruct((bsz, 2, n_tok), jnp.int32), jax.ShapeDtypeStruct((bsz, 2, n_tok), F32)],
        scratch_shapes=[pltpu.VMEM((tm, d), F32), pltpu.VMEM((tm, d), F32)],
        input_output_aliases=aliases,
        compiler_params=_cparams("arbitrary"),
        name="out_projection",
    )(*args)


CAST_ROWS = 256
GATHER_AHEAD = 2


def _ffn_kernel(te_ref, nx_ref, nu_ref, dest_ref, u_hbm, wg_hbm, wu_hbm, wd_hbm, y_ref,
                xbuf, stage_g, stage_u, stage_d, wg_bf, wu_bf, wd_bf, src_ref, sem, gsem, *, e0):
    t = pl.program_id(0)

    @pl.when(t == 0)
    def _():
        n_tok = dest_ref.shape[0] // 2

        def clear(r, carry):
            src_ref[r] = 0
            return carry
        lax.fori_loop(0, src_ref.shape[0], clear, 0, unroll=8)

        def invert(i, carry):
            src_ref[dest_ref[i]] = i
            src_ref[dest_ref[n_tok + i]] = i
            return carry
        lax.fori_loop(0, n_tok, invert, 0, unroll=4)

    e = te_ref[t]
    tm = xbuf.shape[1]
    pairs = ((wg_hbm, stage_g, wg_bf), (wu_hbm, stage_u, wu_bf), (wd_hbm, stage_d, wd_bf))

    def fetch(expert):
        return [pltpu.make_async_copy(hbm.at[e0 + expert], stage, sem.at[i])
                for i, (hbm, stage, _) in enumerate(pairs)]

    def gather_row(base, i, slot):
        r = src_ref[base + i]
        pltpu.make_async_copy(u_hbm.at[pl.ds(r, 1)], xbuf.at[slot, pl.ds(i, 1)], gsem.at[slot]).start()

    def wait_rows(slot):
        pltpu.make_async_copy(u_hbm.at[pl.ds(0, tm)], xbuf.at[slot], gsem.at[slot]).wait()

    last_tile = pl.num_programs(0) - 1

    @pl.when(t == 0)
    def _():
        for ahead in range(GATHER_AHEAD):
            def one_row(i, carry, ahead=ahead):
                gather_row(jnp.minimum(ahead, last_tile) * tm, i, ahead)
                return carry
            lax.fori_loop(0, tm, one_row, 0, unroll=8)
        for cp in fetch(e):
            cp.start()

    first_of_run = (t == 0) | (te_ref[jnp.maximum(t - 1, 0)] != e)

    @pl.when(first_of_run & (t < nu_ref[0]))
    def _():
        for cp in fetch(e):
            cp.wait()
        for _, stage, dst in pairs:
            def cast_rows(i, carry, stage=stage, dst=dst):
                rows = pl.ds(pl.multiple_of(i * CAST_ROWS, CAST_ROWS), CAST_ROWS)
                dst[rows, :] = stage[rows, :].astype(BF16)
                return carry
            lax.fori_loop(0, stage.shape[0] // CAST_ROWS, cast_rows, 0)

        @pl.when(nx_ref[t] >= 0)
        def _():
            for cp in fetch(nx_ref[t]):
                cp.start()

    @pl.when(t < nu_ref[0])
    def _():
        n_slots = GATHER_AHEAD + 1
        slot = t % n_slots
        wait_rows(slot)
        x = xbuf[slot].astype(BF16)
        base = jnp.minimum(t + GATHER_AHEAD, last_tile) * tm
        for i in range(tm):
            gather_row(base, i, (t + GATHER_AHEAD) % n_slots)
        g = _dot(x, wg_bf[...])
        u = _dot(x, wu_bf[...])
        h = (g * jax.nn.sigmoid(g) * u).astype(BF16)
        y_ref[...] = _dot(h, wd_bf[...])

    @pl.when(t == nu_ref[0] - 1)
    def _():
        for ahead in range(1, GATHER_AHEAD + 1):
            wait_rows((t + ahead) % (GATHER_AHEAD + 1))

    @pl.when(t >= nu_ref[0])
    def _():
        y_ref[...] = jnp.zeros(y_ref.shape, F32)


def _expert_ffn(u2, dest, n_rows, tile_expert, next_expert, n_used, w_gate, w_up, w_down, layer):
    d = u2.shape[1]
    n_e, f = w_gate.shape[1], w_gate.shape[3]
    tm = MOE_TILE
    flat = lambda w: w.reshape((-1,) + w.shape[2:])
    hbm = pl.BlockSpec(memory_space=pl.ANY)
    grid_spec = pltpu.PrefetchScalarGridSpec(
        num_scalar_prefetch=4,
        grid=(n_rows // tm,),
        in_specs=[hbm, hbm, hbm, hbm],
        out_specs=pl.BlockSpec((tm, d), lambda t, te, nx, nu, src: (t, 0)),
        scratch_shapes=[pltpu.VMEM((GATHER_AHEAD + 1, tm, d), F32),
                        pltpu.VMEM((d, f), F32), pltpu.VMEM((d, f), F32), pltpu.VMEM((f, d), F32),
                        pltpu.VMEM((d, f), BF16), pltpu.VMEM((d, f), BF16), pltpu.VMEM((f, d), BF16),
                        pltpu.SMEM((n_rows,), jnp.int32),
                        pltpu.SemaphoreType.DMA((3,)), pltpu.SemaphoreType.DMA((GATHER_AHEAD + 1,))],
    )
    return pl.pallas_call(
        functools.partial(_ffn_kernel, e0=layer * n_e),
        grid_spec=grid_spec,
        out_shape=jax.ShapeDtypeStruct((n_rows, d), F32),
        compiler_params=_cparams("arbitrary"),
        name="expert_ffn",
    )(tile_expert, next_expert, n_used, dest, u2, flat(w_gate), flat(w_up), flat(w_down))


def _dispatch_plan(e1, e2, tm):
    n = e1.shape[0]
    n_tiles = (2 * n + N_EXPERTS * (tm - 1) + tm - 1) // tm
    e = jnp.concatenate([e1, e2])
    onehot = (e[:, None] == jnp.arange(N_EXPERTS, dtype=jnp.int32)[None, :]).astype(jnp.int32)
    csum = jnp.cumsum(onehot, axis=0)
    pos_in_e = jnp.sum((csum - 1) * onehot, axis=1)
    counts = csum[-1]
    padded = ((counts + tm - 1) // tm) * tm
    ends = jnp.cumsum(padded)
    offs = ends - padded
    dest = (offs[e] + pos_in_e).astype(jnp.int32)
    n_used = (ends[-1] // tm).astype(jnp.int32)
    tile_start = jnp.arange(n_tiles, dtype=jnp.int32) * tm
    tile_e = jnp.sum((tile_start[:, None] >= ends[None, :]).astype(jnp.int32), axis=1)
    tile_e = jnp.minimum(tile_e, N_EXPERTS - 1)
    last_e = tile_e[jnp.maximum(n_used - 1, 0)]
    tile_e = jnp.where(jnp.arange(n_tiles) < n_used, tile_e, last_e).astype(jnp.int32)
    ids = jnp.arange(N_EXPERTS, dtype=jnp.int32)
    later = (ids[None, :] > ids[:, None]) & (counts[None, :] > 0)
    next_of = jnp.min(jnp.where(later, ids[None, :], N_EXPERTS), axis=1)
    next_of = jnp.where(next_of == N_EXPERTS, -1, next_of).astype(jnp.int32)
    return dest, n_tiles * tm, tile_e, next_of[tile_e], n_used.reshape(1)


def _final_kernel(p1_ref, p2_ref, x_ref, ys_hbm, w1_ref, w2_ref, mod_ref, lng_ref, lnb_ref, o_ref, ybuf, gsem,
                  *, alpha, row0):
    i = pl.program_id(0)
    tm = x_ref.shape[1]
    last_tile = pl.num_programs(0) - 1
    n_slots = GATHER_AHEAD + 1

    def gather_row(tile, r, slot):
        for k, p_ref in enumerate((p1_ref, p2_ref)):
            src = p_ref[row0 + tile * tm + r]
            pltpu.make_async_copy(ys_hbm.at[pl.ds(src, 1)], ybuf.at[slot, k, pl.ds(r, 1)], gsem.at[slot]).start()

    def wait_rows(slot):
        for k in range(2):
            pltpu.make_async_copy(ys_hbm.at[pl.ds(0, tm)], ybuf.at[slot, k], gsem.at[slot]).wait()

    @pl.when(i == 0)
    def _():
        for ahead in range(GATHER_AHEAD):
            def one_row(r, carry, ahead=ahead):
                gather_row(jnp.minimum(ahead, last_tile), r, ahead)
                return carry
            lax.fori_loop(0, tm, one_row, 0, unroll=8)

    slot = i % n_slots
    wait_rows(slot)
    nxt = jnp.minimum(i + GATHER_AHEAD, last_tile)
    for r in range(tm):
        gather_row(nxt, r, (i + GATHER_AHEAD) % n_slots)
    f = w1_ref[...] * ybuf[slot, 0] + w2_ref[...] * ybuf[slot, 1]
    g2 = mod_ref[0, 5:6, :]
    o_ref[0] = _layer_norm(alpha * x_ref[0] + g2 * f, lng_ref[...], lnb_ref[...])

    @pl.when(i == last_tile)
    def _():
        for ahead in range(1, GATHER_AHEAD + 1):
            wait_rows((i + ahead) % n_slots)


def _combine_ln(x1, ys, pos1, pos2, w1, w2, row_off, mod3, row_of_batch, ln_g, ln_b, alpha):
    bsz, n_tok, d = x1.shape
    tm = min(ROW_TILE, n_tok)
    nt = n_tok // tm
    assert row_off % tm == 0
    tspec = pl.BlockSpec((1, tm, d), lambda i, p1, p2: (i // nt, i % nt, 0))
    wspec = pl.BlockSpec((tm, 1), lambda i, p1, p2: (row_off // tm + i, 0))
    vspec = pl.BlockSpec((1, d), lambda i, p1, p2: (0, 0))
    grid_spec = pltpu.PrefetchScalarGridSpec(
        num_scalar_prefetch=2,
        grid=(bsz * nt,),
        in_specs=[tspec, pl.BlockSpec(memory_space=pl.ANY), wspec, wspec,
                  pl.BlockSpec((1, N_MOD, d), lambda i, p1, p2: (row_of_batch(i // nt), 0, 0)), vspec, vspec],
        out_specs=tspec,
        scratch_shapes=[pltpu.VMEM((GATHER_AHEAD + 1, 2, tm, d), F32),
                        pltpu.SemaphoreType.DMA((GATHER_AHEAD + 1,))],
    )
    return pl.pallas_call(
        functools.partial(_final_kernel, alpha=alpha, row0=row_off),
        grid_spec=grid_spec,
        out_shape=jax.ShapeDtypeStruct((bsz, n_tok, d), F32),
        compiler_params=_cparams("arbitrary"),
        name="combine_post_ln",
    )(pos1, pos2, x1, ys, w1, w2, mod3, ln_g.reshape(1, d), ln_b.reshape(1, d))


def kernel(x, c, ctx, c_ctx, w_mod, b_mod, w_in, attn_sink, ssm_lambda_re, ssm_lambda_im, ssm_log_dt, ssm_b_re, ssm_b_im, ssm_c_re, ssm_c_im, ssm_d, w_glu, b_glu, g_attn_out, g_ssm_out, w_out, ln1_g, ln1_b, w_router, b_router, w_expert_gate, w_expert_up, w_expert_down, ln2_g, ln2_b):
    depth = w_mod.shape[0]
    bsz, n_tok, d = x.shape
    n_ctx = ctx.shape[1]
    alpha = (2 * depth) ** 0.25
    assert n_tok % ROW_TILE == 0 and n_tok % ATTN_BLOCK == 0 and n_ctx % ATTN_BLOCK == 0 and n_ctx % CHUNK == 0

    mod = _modulation(c, c_ctx, w_mod, b_mod)
    s5_params = (ssm_lambda_re, ssm_lambda_im, ssm_log_dt, ssm_b_re, ssm_b_im, ssm_c_re, ssm_c_im, ssm_d)
    expert_w = (w_expert_gate, w_expert_up, w_expert_down)
    s5_ops = _s5_operators(*s5_params, 0, jnp.zeros((1,), jnp.int32))
    cos_t, sin_t = _rope_tables(n_tok)
    w_router_t = w_router.T.astype(BF16)
    w_glu_bf, w_out_bf = w_glu.astype(BF16), w_out.astype(BF16)
    n_qkv = ATTN_WIDTH + 2 * KV_WIDTH
    w_in_bf = w_in[:, :, :n_qkv].astype(BF16)
    ws_t_bf = jnp.swapaxes(w_in[:, :, n_qkv:].astype(BF16), 1, 2)
    lat_row = lambda b: b
    ctx_row = lambda b: bsz
    n_lat, n_c = bsz * n_tok, bsz * n_ctx
    lat_blocks = n_lat // S5_TILE
    s5_rows = (lat_blocks + 1) * S5_TILE_ROWS

    xc = ctx
    for i in range(depth):
        last = i == depth - 1
        mod3 = mod[i].reshape(MOD_ROWS, N_MOD, d)
        n_moe = n_lat if last else n_lat + n_c

        q, k, v = _in_projection(x, mod3, lat_row, w_in_bf, i, cos_t, sin_t, True)
        qc, kc, vc = _in_projection(xc, mod3, ctx_row, w_in_bf, i, cos_t, sin_t, False)
        u = _s_projection(x, mod3, lat_row, ws_t_bf, i, s5_rows, 0)
        u = _s_projection(xc, mod3, ctx_row, ws_t_bf, i, s5_rows, lat_blocks, u_prev=u)
        attn_n = _attention(q, k, v, kc, vc, attn_sink[i], g_attn_out[i], True)
        y_s5 = _s5_apply(u, bsz, n_tok, n_ctx, s5_ops)
        ssm_n = _glu(y_s5, bsz, n_tok, 0, True, w_glu_bf, i, b_glu[i], g_ssm_out[i])
        x1, u2, ri, rw = _out_projection(attn_n, ssm_n, x, mod3, lat_row, w_out_bf, i, ln1_g[i], ln1_b[i],
                                         w_router_t, b_router, alpha, n_moe, 0)
        if not last:
            attn_c = _attention(qc, None, None, kc, vc, attn_sink[i], g_attn_out[i], False)
            ssm_c = _glu(y_s5, bsz, n_ctx, lat_blocks, False, w_glu_bf, i, b_glu[i], g_ssm_out[i])
            xc1, u2, ric, rwc = _out_projection(attn_c, ssm_c, xc, mod3, ctx_row, w_out_bf, i, ln1_g[i], ln1_b[i],
                                                w_router_t, b_router, alpha, n_moe, n_lat, u2_prev=u2)
            ri = jnp.concatenate([ri.transpose(1, 0, 2).reshape(2, n_lat), ric.transpose(1, 0, 2).reshape(2, n_c)], axis=1)
            rw = jnp.concatenate([rw.transpose(1, 0, 2).reshape(2, n_lat), rwc.transpose(1, 0, 2).reshape(2, n_c)], axis=1)
        else:
            ri = ri.transpose(1, 0, 2).reshape(2, n_lat)
            rw = rw.transpose(1, 0, 2).reshape(2, n_lat)

        dest, n_sorted, tile_e, next_e, n_used = _dispatch_plan(ri[0], ri[1], MOE_TILE)
        pos1, pos2 = dest[:n_moe], dest[n_moe:]
        if not last:
            s5_ops = _s5_operators(*s5_params, i + 1, n_used)
        ys = _expert_ffn(u2, dest, n_sorted, tile_e, next_e, n_used, *expert_w, i)
        cw1, cw2 = rw[0].reshape(n_moe, 1), rw[1].reshape(n_moe, 1)
        x = _combine_ln(x1, ys, pos1, pos2, cw1, cw2, 0, mod3, lat_row, ln2_g[i], ln2_b[i], alpha)
        if not last:
            xc = _combine_ln(xc1, ys, pos1, pos2, cw1, cw2, n_lat, mod3, ctx_row, ln2_g[i], ln2_b[i], alpha)
    return x
```

```python
import functools
import math

import jax
import jax.numpy as jnp
from jax import lax
from jax.experimental import pallas as pl
from jax.experimental.pallas import tpu as pltpu

F32 = jnp.float32
BF16 = jnp.bfloat16

HEAD_DIM = 128
N_Q_HEADS = 8
N_KV_HEADS = 2
Q_PER_KV = N_Q_HEADS // N_KV_HEADS
ATTN_WIDTH = N_Q_HEADS * HEAD_DIM
KV_WIDTH = N_KV_HEADS * HEAD_DIM
SSM_WIDTH = 1024
ATTN_BLOCK = 128
GRID_W = 64
ROPE_THETA = 10000.0
SSM_P = 16
SSM_G = SSM_WIDTH // SSM_P
SSM_N = 64
CHUNK = 64
GC = SSM_P * CHUNK
N_EXPERTS = 16
N_EXPERT_GROUPS = 4
EXPERTS_PER_GROUP = 4
N_MOD = 6
LN_EPS = 1e-5
NEG_INF = -1e30
LANES = 128
MOD_ROWS = 8
ROW_TILE = 256
MOE_TILE = 256
VMEM_LIMIT = 56 * 1024 * 1024

assert 2 * SSM_N == LANES and 2 * CHUNK == LANES


def _cparams(*sem):
    return pltpu.CompilerParams(dimension_semantics=sem, vmem_limit_bytes=VMEM_LIMIT)


def _dot(a, b):
    return jnp.dot(a, b, preferred_element_type=F32)


def _dot_nt(a, b):
    return lax.dot_general(a, b, (((1,), (1,)), ((), ())), preferred_element_type=F32)


def _mod_kernel(ct_ref, w_ref, b_ref, o_ref, ab_ref, *, n_rows, tn):
    d = ct_ref.shape[0]

    @pl.when((pl.program_id(0) == 0) & (pl.program_id(1) == 0))
    def _():
        ct = ct_ref[...]
        a = ct * jax.nn.sigmoid(ct)
        for r in range(n_rows):
            ab_ref[r] = jnp.broadcast_to(a[:, r:r + 1], (d, LANES))

    o_ref[...] = jnp.zeros(o_ref.shape, F32)
    sub = 8
    width = 2 * LANES
    for j in range(tn // width):
        cols = slice(j * width, (j + 1) * width)

        def k_step(kc, accs, cols=cols):
            rows = pl.ds(pl.multiple_of(kc * sub, sub), sub)
            w = w_ref[0, rows, cols]
            return tuple(acc + jnp.concatenate([ab_ref[r, rows, :]] * 2, axis=1) * w for r, acc in enumerate(accs))

        zero = jnp.zeros((sub, width), F32)
        accs = lax.fori_loop(0, d // sub, k_step, (zero,) * n_rows, unroll=8)
        for r in range(n_rows):
            o_ref[0, r:r + 1, cols] = jnp.sum(accs[r], axis=0, keepdims=True) + b_ref[0, :, cols]


def _modulation(c, c_ctx, w_mod, b_mod):
    depth, d, n_out = w_mod.shape
    n_rows = c.shape[0] + 1
    assert n_rows <= MOD_ROWS
    tn = 512
    ct = jnp.zeros((MOD_ROWS, d), F32).at[:c.shape[0]].set(c).at[c.shape[0]].set(c_ctx).T
    return pl.pallas_call(
        functools.partial(_mod_kernel, n_rows=n_rows, tn=tn),
        grid=(depth, n_out // tn),
        in_specs=[
            pl.BlockSpec((d, MOD_ROWS), lambda l, j: (0, 0)),
            pl.BlockSpec((1, d, tn), lambda l, j: (l, 0, j)),
            pl.BlockSpec((1, 1, tn), lambda l, j: (l, 0, j)),
        ],
        out_specs=pl.BlockSpec((1, MOD_ROWS, tn), lambda l, j: (l, 0, j)),
        out_shape=jax.ShapeDtypeStruct((depth, MOD_ROWS, n_out), F32),
        scratch_shapes=[pltpu.VMEM((n_rows, d, LANES), F32)],
        compiler_params=_cparams("arbitrary", "arbitrary"),
        name="modulation",
    )(ct, w_mod, b_mod.reshape(depth, 1, n_out))


def _rope(xh, cos, sin_signed):
    lane = lax.broadcasted_iota(jnp.int32, xh.shape, 1)
    swapped = jnp.where((lane % 64) < 32, pltpu.roll(xh, 96, 1), pltpu.roll(xh, 32, 1))
    return xh * cos + swapped * sin_signed


def _inproj_kernel(x_ref, mod_ref, w_ref, cos_ref, sin_ref, q_ref, k_ref, v_ref, acc_even, acc_odd, *, rope):
    i = pl.program_id(0)
    n_cols = ATTN_WIDTH + 2 * KV_WIDTH

    @pl.when(i == 0)
    def _():
        acc_odd[...] = jnp.zeros(acc_odd.shape, F32)

    def step(acc_mine, acc_prev):
        sh = mod_ref[0, 0:1, :]
        sc = mod_ref[0, 1:2, :]
        u = (x_ref[0] * (1.0 + sc) + sh).astype(BF16)
        acc_mine[...] = _dot(u, w_ref[0, :, 0:n_cols])
        scale = HEAD_DIM ** -0.5
        if rope:
            cos = cos_ref[...]
            sin = sin_ref[...]
        for h in range(N_Q_HEADS):
            qh = acc_prev[:, h * HEAD_DIM:(h + 1) * HEAD_DIM]
            if rope:
                qh = _rope(qh, cos, sin)
            q_ref[0, :, h * HEAD_DIM:(h + 1) * HEAD_DIM] = (qh * scale).astype(BF16)
        for h in range(N_KV_HEADS):
            kh = acc_prev[:, ATTN_WIDTH + h * HEAD_DIM:ATTN_WIDTH + (h + 1) * HEAD_DIM]
            if rope:
                kh = _rope(kh, cos, sin)
            k_ref[0, :, h * HEAD_DIM:(h + 1) * HEAD_DIM] = kh.astype(BF16)
        v_ref[0] = acc_prev[:, ATTN_WIDTH + KV_WIDTH:n_cols].astype(BF16)

    @pl.when(i % 2 == 0)
    def _():
        step(acc_even, acc_odd)

    @pl.when(i % 2 == 1)
    def _():
        step(acc_odd, acc_even)


def _in_projection(x, mod3, row_of_batch, w_in_bf, layer, cos_t, sin_t, rope):
    bsz, n_tok, d = x.shape
    tm = min(ROW_TILE, n_tok)
    nt = n_tok // tm
    n_steps = bsz * nt
    n_cols = ATTN_WIDTH + 2 * KV_WIDTH
    cur = lambda i: jnp.minimum(i, n_steps - 1)
    fin = lambda i: jnp.maximum(i - 1, 0)
    out = lambda w: jax.ShapeDtypeStruct((bsz, n_tok, w), BF16)
    ospec = lambda w: pl.BlockSpec((1, tm, w), lambda i: (fin(i) // nt, fin(i) % nt, 0))
    tspec = pl.BlockSpec((tm, HEAD_DIM), lambda i: (fin(i) % nt, 0))
    return pl.pallas_call(
        functools.partial(_inproj_kernel, rope=rope),
        grid=(n_steps + 1,),
        in_specs=[
            pl.BlockSpec((1, tm, d), lambda i: (cur(i) // nt, cur(i) % nt, 0)),
            pl.BlockSpec((1, N_MOD, d), lambda i: (row_of_batch(cur(i) // nt), 0, 0)),
            pl.BlockSpec((1, d, n_cols), lambda i: (layer, 0, 0)),
            tspec, tspec,
        ],
        out_specs=[ospec(ATTN_WIDTH), ospec(KV_WIDTH), ospec(KV_WIDTH)],
        out_shape=[out(ATTN_WIDTH), out(KV_WIDTH), out(KV_WIDTH)],
        scratch_shapes=[pltpu.VMEM((tm, n_cols), F32), pltpu.VMEM((tm, n_cols), F32)],
        compiler_params=_cparams("arbitrary"),
        name="in_projection",
    )(x, mod3, w_in_bf, cos_t, sin_t)


S5_TILE = 1024
S5_TILE_ROWS = 2 * S5_TILE // LANES


def _to_group_layout(st):
    x4 = st.reshape(SSM_G, SSM_P, S5_TILE // LANES, LANES)
    lane = lax.broadcasted_iota(jnp.int32, x4.shape[:1] + x4.shape[2:], 2)
    roll64 = lambda a: pltpu.roll(a.reshape(-1, LANES), CHUNK, 1).reshape(a.shape)
    out = []
    for k in range(SSM_P // 2):
        a, b = x4[:, 2 * k], x4[:, 2 * k + 1]
        h0 = jnp.where(lane < CHUNK, a, roll64(b))
        h1 = jnp.where(lane < CHUNK, roll64(a), b)
        out.append(jnp.concatenate([h0, h1], axis=1))
    return out


def _from_group_layout(blocks):
    n8 = S5_TILE // LANES
    lane = lax.broadcasted_iota(jnp.int32, (SSM_G, n8, LANES), 2)
    roll64 = lambda a: pltpu.roll(a.reshape(-1, LANES), CHUNK, 1).reshape(a.shape)
    chans = []
    for blk in blocks:
        a0, b1 = blk[:, 0:n8], blk[:, n8:]
        chans.append(jnp.where(lane < CHUNK, a0, roll64(b1)))
        chans.append(jnp.where(lane < CHUNK, roll64(a0), b1))
    return jnp.stack(chans, axis=1).reshape(SSM_WIDTH, S5_TILE)


def _sproj_kernel(x_ref, mod_ref, w_ref, *rest, n_steps):
    u_ref = rest[-1]

    @pl.when(pl.program_id(0) < n_steps)
    def _():
        sh = mod_ref[0, 0:1, :]
        sc = mod_ref[0, 1:2, :]
        x = x_ref[...].reshape(S5_TILE, x_ref.shape[-1])
        u = (x * (1.0 + sc) + sh).astype(BF16)
        st = _dot_nt(w_ref[0], u)
        for k, blk in enumerate(_to_group_layout(st)):
            u_ref[:, :, k * LANES:(k + 1) * LANES] = blk.astype(BF16)

    @pl.when(pl.program_id(0) >= n_steps)
    def _():
        u_ref[...] = jnp.zeros(u_ref.shape, BF16)


def _s_projection(x, mod3, mod_row, ws_t_bf, layer, n_rows, row_blk_off, u_prev=None):
    bsz, n_tok, d = x.shape
    if u_prev is None:
        assert n_tok % S5_TILE == 0
        nj = n_tok // S5_TILE
        n_steps = bsz * nj
        n_fill = n_rows // S5_TILE_ROWS - n_steps
        tile = lambda i: jnp.minimum(i, n_steps - 1)
        xspec = pl.BlockSpec((1, S5_TILE, d), lambda i: (tile(i) // nj, tile(i) % nj, 0))
        mspec = pl.BlockSpec((1, N_MOD, d), lambda i: (mod_row(tile(i) // nj), 0, 0))
    else:
        assert bsz * n_tok == S5_TILE and n_tok % LANES == 0
        n_steps, n_fill = 1, 0
        xspec = pl.BlockSpec((bsz, n_tok, d), lambda i: (0, 0, 0))
        mspec = pl.BlockSpec((1, N_MOD, d), lambda i: (mod_row(0), 0, 0))
    in_specs = [xspec, mspec, pl.BlockSpec((1, SSM_WIDTH, d), lambda i: (layer, 0, 0))]
    args = [x, mod3, ws_t_bf]
    aliases = {}
    if u_prev is not None:
        in_specs.append(pl.BlockSpec(memory_space=pl.ANY))
        args.append(u_prev)
        aliases = {3: 0}
    return pl.pallas_call(
        functools.partial(_sproj_kernel, n_steps=n_steps),
        grid=(n_steps + n_fill,),
        in_specs=in_specs,
        out_specs=pl.BlockSpec((SSM_G, S5_TILE_ROWS, GC), lambda i: (0, row_blk_off + i, 0)),
        out_shape=jax.ShapeDtypeStruct((SSM_G, n_rows, GC), BF16),
        input_output_aliases=aliases,
        compiler_params=_cparams("arbitrary"),
        name="s_projection",
    )(*args)


def _rope_tables(n_tok):
    half = HEAD_DIM // 2
    inv_freq = ROPE_THETA ** (-jnp.arange(0, half, 2, dtype=F32) / half)
    t = jnp.arange(n_tok)
    row = (t // GRID_W).astype(F32)
    col = (t % GRID_W).astype(F32)
    ang_r = row[:, None] * inv_freq[None, :]
    ang_c = col[:, None] * inv_freq[None, :]
    cos_t = jnp.concatenate([jnp.cos(ang_r), jnp.cos(ang_r), jnp.cos(ang_c), jnp.cos(ang_c)], axis=-1)
    sin_t = jnp.concatenate([-jnp.sin(ang_r), jnp.sin(ang_r), -jnp.sin(ang_c), jnp.sin(ang_c)], axis=-1)
    return cos_t, sin_t


def _attn_kernel(sink_ref, q_ref, *refs, has_band, n_blk):
    if has_band:
        kp_ref, ko_ref, kn_ref, vp_ref, vo_ref, vn_ref, kc_ref, vc_ref, g_ref, o_ref = refs
    else:
        kc_ref, vc_ref, g_ref, o_ref = refs
    blk = pl.program_id(1)
    rows = Q_PER_KV * ATTN_BLOCK
    qi = lax.broadcasted_iota(jnp.int32, (rows, ATTN_BLOCK), 0) % ATTN_BLOCK
    kj = lax.broadcasted_iota(jnp.int32, (rows, ATTN_BLOCK), 1)
    row_head = lax.broadcasted_iota(jnp.int32, (rows, 1), 0) // ATTN_BLOCK
    off_p = jnp.where(blk > 0, 0, ATTN_BLOCK)
    off_n = jnp.where(blk < n_blk - 1, 0, ATTN_BLOCK)
    heads = []
    for h in range(N_KV_HEADS):
        hs = slice(h * HEAD_DIM, (h + 1) * HEAD_DIM)
        q = jnp.concatenate(
            [q_ref[0, :, (h * Q_PER_KV + g) * HEAD_DIM:(h * Q_PER_KV + g + 1) * HEAD_DIM] for g in range(Q_PER_KV)],
            axis=0)
        sink = jnp.zeros((rows, 1), F32)
        for g in range(Q_PER_KV):
            sink = jnp.where(row_head == g, sink_ref[h * Q_PER_KV + g], sink)
        s_c = _dot_nt(q, kc_ref[0, :, hs])
        m = jnp.maximum(jnp.max(s_c, axis=-1, keepdims=True), sink)
        if has_band:
            s_p = jnp.where(kj >= qi + off_p, _dot_nt(q, kp_ref[0, :, hs]), NEG_INF)
            s_o = _dot_nt(q, ko_ref[0, :, hs])
            s_n = jnp.where(kj <= qi - off_n, _dot_nt(q, kn_ref[0, :, hs]), NEG_INF)
            m = jnp.maximum(m, jnp.max(jnp.maximum(jnp.maximum(s_p, s_o), s_n), axis=-1, keepdims=True))
        p_c = jnp.exp(s_c - m)
        denom = jnp.sum(p_c, axis=-1, keepdims=True) + jnp.exp(sink - m)
        acc = _dot(p_c.astype(BF16), vc_ref[0, :, hs])
        if has_band:
            for s_x, v_ref in ((s_p, vp_ref), (s_o, vo_ref), (s_n, vn_ref)):
                p_x = jnp.exp(s_x - m)
                denom = denom + jnp.sum(p_x, axis=-1, keepdims=True)
                acc = acc + _dot(p_x.astype(BF16), v_ref[0, :, hs])
        o = acc / denom
        for g in range(Q_PER_KV):
            heads.append(o[g * ATTN_BLOCK:(g + 1) * ATTN_BLOCK, :])
    ss = heads[0] * heads[0]
    for o in heads[1:]:
        ss = ss + o * o
    inv = lax.rsqrt(jnp.sum(ss, axis=-1, keepdims=True) / ATTN_WIDTH + LN_EPS)
    for i, o in enumerate(heads):
        cs = slice(i * HEAD_DIM, (i + 1) * HEAD_DIM)
        o_ref[0, :, cs] = (o * inv * g_ref[:, cs]).astype(BF16)


def _attention(q, k, v, kc, vc, sink, g_attn, has_band):
    bsz, n_tok, _ = q.shape
    n_ctx = kc.shape[1]
    n_blk = n_tok // ATTN_BLOCK
    qspec = pl.BlockSpec((1, ATTN_BLOCK, ATTN_WIDTH), lambda b, n: (b, n, 0))
    kvspec = lambda f: pl.BlockSpec((1, ATTN_BLOCK, KV_WIDTH), lambda b, n: (b, f(n), 0))
    cspec = pl.BlockSpec((1, n_ctx, KV_WIDTH), lambda b, n: (b, 0, 0))
    prev = lambda n: jnp.maximum(n - 1, 0)
    own = lambda n: n
    nxt = lambda n: jnp.minimum(n + 1, n_blk - 1)
    in_specs = [pl.BlockSpec(memory_space=pltpu.SMEM), qspec]
    args = [sink, q]
    if has_band:
        in_specs += [kvspec(prev), kvspec(own), kvspec(nxt), kvspec(prev), kvspec(own), kvspec(nxt)]
        args += [k, k, k, v, v, v]
    in_specs += [cspec, cspec, pl.BlockSpec((1, ATTN_WIDTH), lambda b, n: (0, 0))]
    args += [kc, vc, g_attn.reshape(1, ATTN_WIDTH)]
    return pl.pallas_call(
        functools.partial(_attn_kernel, has_band=has_band, n_blk=n_blk),
        grid=(bsz, n_blk),
        in_specs=in_specs,
        out_specs=qspec,
        out_shape=jax.ShapeDtypeStruct((bsz, n_tok, ATTN_WIDTH), BF16),
        compiler_params=_cparams("parallel", "parallel"),
        name="attention",
    )(*args)


def _cmul(a, b):
    return a[0] * b[0] - a[1] * b[1], a[0] * b[1] + a[1] * b[0]


def _cpow(rho, theta, expo):
    mag = jnp.exp(expo * rho)
    ang = expo * theta
    return mag * jnp.cos(ang), mag * jnp.sin(ang)


def _s5_param_kernel(after_ref, prow_ref, bt_ref, cr_ref, dd_ref, k_ref, w_ref, vt_ref, al_ref):
    del after_ref
    p_dim, lc = SSM_P, CHUNK
    lam_re, lam_im = prow_ref[0, 0:1, :], prow_ref[0, 1:2, :]
    dt = jnp.exp(prow_ref[0, 2:3, :])
    rho, theta = lam_re * dt, lam_im * dt

    sub8 = lax.broadcasted_iota(jnp.int32, (8, LANES), 0)
    asc = _cpow(rho, theta, sub8.astype(F32))
    desc = _cpow(rho, theta, (8 - sub8).astype(F32))
    ex = jnp.where(sub8 == 0, 8, jnp.where(sub8 == 1, 16, jnp.where(sub8 == 2, 32, jnp.where(sub8 == 3, lc, 1))))
    pw = _cpow(rho, theta, ex.astype(F32))
    row = lambda t, i: (t[0][i:i + 1, :], t[1][i:i + 1, :])
    cat = lambda a, b: (jnp.concatenate([a[0], b[0]], axis=0), jnp.concatenate([a[1], b[1]], axis=0))
    for i in range(3):
        step = row(pw, i)
        asc = cat(asc, _cmul(asc, step))
        desc = cat(_cmul(desc, step), desc)
    a_chunk, a_one = row(pw, 3), row(pw, 4)

    den = lam_re * lam_re + lam_im * lam_im
    x_re, x_im = a_one[0] - 1.0, a_one[1]
    beta = ((x_re * lam_re + x_im * lam_im) / den, (x_im * lam_re - x_re * lam_im) / den)
    bb = _cmul(beta, (bt_ref[0, 0], bt_ref[0, 1]))
    cc = (cr_ref[0, 0], cr_ref[0, 1])

    is_fwd = lax.broadcasted_iota(jnp.int32, (lc, LANES), 1) < SSM_N
    sub = lax.broadcasted_iota(jnp.int32, (lc, LANES), 0)
    pw_w = (jnp.where(is_fwd, desc[0], asc[0]), jnp.where(is_fwd, desc[1], asc[1]))
    pw_v = (jnp.where(is_fwd, asc[0], desc[0]), jnp.where(is_fwd, asc[1], desc[1]))

    top = (jnp.where(is_fwd, asc[0], jnp.where(sub == 0, 1.0, 0.0)), jnp.where(is_fwd, asc[1], 0.0))
    keep = (~is_fwd) & (sub > 0)
    bot = (jnp.where(keep, desc[0], 0.0), jnp.where(keep, desc[1], 0.0))
    lag = cat(top, bot)

    cx = (jnp.concatenate([cc[0]] * p_dim, axis=0), jnp.concatenate([cc[1]] * p_dim, axis=0))
    bx = tuple(jnp.concatenate([jnp.broadcast_to(b[q:q + 1, :], (p_dim, LANES)) for q in range(p_dim)], axis=0)
               for b in bb)
    e_re, e_im = _cmul(cx, bx)
    hi = lax.Precision.HIGHEST
    kmat = (jnp.dot(e_re, lag[0].T, preferred_element_type=F32, precision=hi)
            - jnp.dot(e_im, lag[1].T, preferred_element_type=F32, precision=hi))
    lane0 = lax.broadcasted_iota(jnp.int32, kmat.shape, 1) == 0
    k_ref[0] = kmat + jnp.where(lane0, dd_ref[0], 0.0)

    for q in range(p_dim):
        w_re, w_im = _cmul(pw_w, row(bb, q))
        w_ref[0, q * lc:(q + 1) * lc, :] = jnp.concatenate([w_re, w_im], axis=1).astype(BF16)
    for p in range(p_dim):
        v_re, v_im = _cmul(pw_v, row(cc, p))
        vt_ref[0, p * lc:(p + 1) * lc, :] = jnp.concatenate([v_re, -v_im], axis=1).astype(BF16)
    al_ref[0] = jnp.concatenate([a_chunk[0], a_chunk[1]], axis=1)


def _s5_operators(lam_re, lam_im, log_dt, b_re, b_im, c_re, c_im, d_skip, layer, after):
    depth = lam_re.shape[0]
    dg = depth * SSM_G
    g0 = layer * SSM_G
    vec = lambda a: jnp.moveaxis(a, 1, 2).reshape(dg, LANES)
    mat = lambda a: jnp.moveaxis(a, 1, 3).reshape(dg, SSM_P, LANES)
    ldt = jnp.broadcast_to(log_dt[..., None], lam_re.shape)
    prow = jnp.stack([vec(lam_re), vec(lam_im), vec(ldt)], axis=1)
    bt = jnp.stack([mat(jnp.swapaxes(b_re, 3, 4)), mat(jnp.swapaxes(b_im, 3, 4))], axis=1)
    cr = jnp.stack([mat(c_re), mat(c_im)], axis=1)
    dd = (d_skip.reshape(dg, 1, SSM_P) * jnp.eye(SSM_P, dtype=F32)[None]).reshape(dg, SSM_P * SSM_P, 1)
    iblk = lambda *s: pl.BlockSpec((1,) + s, lambda i: (g0 + i,) + (0,) * len(s))
    oblk = lambda *s: pl.BlockSpec((1,) + s, lambda i: (i,) + (0,) * len(s))
    return pl.pallas_call(
        _s5_param_kernel,
        grid=(SSM_G,),
        in_specs=[pl.BlockSpec(memory_space=pltpu.SMEM),
                  iblk(3, LANES), iblk(2, SSM_P, LANES), iblk(2, SSM_P, LANES), iblk(SSM_P * SSM_P, 1)],
        out_specs=[oblk(SSM_P * SSM_P, LANES), oblk(GC, 2 * LANES), oblk(GC, 2 * LANES), oblk(1, 2 * LANES)],
        out_shape=[jax.ShapeDtypeStruct((SSM_G, SSM_P * SSM_P, LANES), F32),
                   jax.ShapeDtypeStruct((SSM_G, GC, 2 * LANES), BF16),
                   jax.ShapeDtypeStruct((SSM_G, GC, 2 * LANES), BF16),
                   jax.ShapeDtypeStruct((SSM_G, 1, 2 * LANES), F32)],
        compiler_params=_cparams("parallel"),
        name="s5_operators",
    )(after, prow, bt, cr, dd)


def _exact_f32_dot(a, b):
    return jnp.dot(a, b, preferred_element_type=F32, precision=lax.Precision.HIGHEST)


def _s5_sum_kernel(u_ref, w_ref, p_ref, s_ref):
    s_ref[0] = _exact_f32_dot(p_ref[...], _dot(u_ref[0], w_ref[0]))


def _s5_scan_kernel(s_ref, al_ref, h_ref, *, bsz, n_ctx_chunks, n_chunks):
    a_re, a_im = al_ref[:, :, :LANES], al_ref[:, :, LANES:]
    shape = (s_ref.shape[0], bsz, LANES)
    is_fwd = lax.broadcasted_iota(jnp.int32, shape, 2) < SSM_N
    pair = 2 * bsz
    n_ctx_pairs, n_pairs = n_ctx_chunks // 2, n_chunks // 2

    def advance(h, s):
        return a_re * h[0] - a_im * h[1] + s[0], a_re * h[1] + a_im * h[0] + s[1]

    def body(k, h):
        kr = jnp.where(k < n_ctx_pairs, n_ctx_pairs - 1 - k, n_pairs - 1 - (k - n_ctx_pairs))
        rf = pl.ds(pl.multiple_of(k * pair, pair), pair)
        rr = pl.ds(pl.multiple_of(kr * pair, pair), pair)
        s_f, s_r = s_ref[:, rf, :], s_ref[:, rr, :]
        lo, hi = slice(0, bsz), slice(bsz, pair)
        pick = lambda first, second: (jnp.where(is_fwd, s_f[:, first, :LANES], s_r[:, second, :LANES]),
                                      jnp.where(is_fwd, s_f[:, first, LANES:], s_r[:, second, LANES:]))
        h_mid = advance(h, pick(lo, hi))
        h_out = advance(h_mid, pick(hi, lo))
        for part, off in ((0, 0), (1, LANES)):
            rows_f = jnp.concatenate([h[part], h_mid[part]], axis=1)
            rows_r = jnp.concatenate([h_mid[part], h[part]], axis=1)
            h_ref[:, rf, off:off + SSM_N] = rows_f[:, :, 0:SSM_N]
            h_ref[:, rr, off + SSM_N:off + LANES] = rows_r[:, :, SSM_N:LANES]
        return h_out

    zero = jnp.zeros(shape, F32)
    lax.fori_loop(0, n_pairs, body, (zero, zero))


def _toeplitz_blocks(k_ref, m_buf):
    lc = CHUNK
    lane = lax.broadcasted_iota(jnp.int32, (lc, LANES), 1)
    for q in range(SSM_P):
        for pp in range(SSM_P // 2):
            r0 = q * SSM_P + 2 * pp
            ka = jnp.broadcast_to(k_ref[0, r0:r0 + 1, :], (lc, LANES))
            kb = jnp.broadcast_to(k_ref[0, r0 + 1:r0 + 2, :], (lc, LANES))
            ra = pltpu.roll(ka, 0, 1, stride=1, stride_axis=0)
            rb = pltpu.roll(kb, lc, 1, stride=1, stride_axis=0)
            m_buf[q * lc:(q + 1) * lc, pp * LANES:(pp + 1) * LANES] = jnp.where(lane < lc, ra, rb).astype(BF16)


def _s5_out_kernel(u_ref, k_first_ref, k_next_ref, h_ref, pt_ref, vt_ref, y_ref, m_even, m_odd):
    g = pl.program_id(0)

    @pl.when(g == 0)
    def _():
        _toeplitz_blocks(k_first_ref, m_even)

    def step(m_mine, m_next):
        _toeplitz_blocks(k_next_ref, m_next)
        h = _exact_f32_dot(pt_ref[...], h_ref[0])
        y = _dot(u_ref[0], m_mine[...]) + _dot_nt(h.astype(BF16), vt_ref[0])
        y_ref[0] = y.astype(BF16)

    @pl.when(g % 2 == 0)
    def _():
        step(m_even, m_odd)

    @pl.when(g % 2 == 1)
    def _():
        step(m_odd, m_even)


def _chunk_major_rows(bsz, n_tok, n_ctx):
    n8, nj, ncb = S5_TILE // LANES, n_tok // S5_TILE, n_ctx // LANES
    ncc, n_chunks = n_ctx // CHUNK, (n_ctx + n_tok) // CHUNK
    n_lat = bsz * nj * S5_TILE_ROWS
    perm = []
    for c in range(n_chunks):
        for b in range(bsz):
            if c < ncc:
                blk, h = divmod(c, 2)
                perm.append(n_lat + h * bsz * ncb + b * ncb + blk)
            else:
                blk, h = divmod(c - ncc, 2)
                j, c8 = divmod(blk, n8)
                perm.append((b * nj + j) * S5_TILE_ROWS + h * n8 + c8)
    return perm


def _s5_apply(u, bsz, n_tok, n_ctx, ops):
    k_op, w_op, vt_op, al = ops
    ncc, n_chunks = n_ctx // CHUNK, (n_ctx + n_tok) // CHUNK
    rows = u.shape[1]
    assert rows == bsz * n_chunks and ncc % 2 == 0 and n_chunks % 2 == 0 and (2 * bsz) % 8 == 0
    gspec = lambda r, c: pl.BlockSpec((1, r, c), lambda g: (g, 0, 0))
    ospec = gspec
    to_chunks = jnp.zeros((rows, rows), F32).at[jnp.arange(rows), jnp.array(_chunk_major_rows(bsz, n_tok, n_ctx))].set(1.0)
    pspec = pl.BlockSpec((rows, rows), lambda g: (0, 0))
    sums = pl.pallas_call(
        _s5_sum_kernel,
        grid=(SSM_G,),
        in_specs=[gspec(rows, GC), ospec(GC, 2 * LANES), pspec],
        out_specs=gspec(rows, 2 * LANES),
        out_shape=jax.ShapeDtypeStruct((SSM_G, rows, 2 * LANES), F32),
        compiler_params=_cparams("parallel"),
        name="s5_chunk_sums",
    )(u, w_op, to_chunks)
    gt = 8
    sspec = pl.BlockSpec((gt, rows, 2 * LANES), lambda r: (r, 0, 0))
    states = pl.pallas_call(
        functools.partial(_s5_scan_kernel, bsz=bsz, n_ctx_chunks=ncc, n_chunks=n_chunks),
        grid=(SSM_G // gt,),
        in_specs=[sspec, pl.BlockSpec((gt, 1, 2 * LANES), lambda r: (r, 0, 0))],
        out_specs=sspec,
        out_shape=jax.ShapeDtypeStruct((SSM_G, rows, 2 * LANES), F32),
        compiler_params=_cparams("parallel"),
        name="s5_state_scan",
    )(sums, al)
    kspec = lambda f: pl.BlockSpec((1, SSM_P * SSM_P, LANES), lambda g: (f(g), 0, 0))
    return pl.pallas_call(
        _s5_out_kernel,
        grid=(SSM_G,),
        in_specs=[gspec(rows, GC), kspec(lambda g: 0), kspec(lambda g: jnp.minimum(g + 1, SSM_G - 1)),
                  gspec(rows, 2 * LANES), pspec, ospec(GC, 2 * LANES)],
        out_specs=gspec(rows, GC),
        out_shape=jax.ShapeDtypeStruct((SSM_G, rows, GC), BF16),
        scratch_shapes=[pltpu.VMEM((GC, GC), BF16), pltpu.VMEM((GC, GC), BF16)],
        compiler_params=_cparams("arbitrary"),
        name="s5_chunk_outputs",
    )(u, k_op, k_op, states, to_chunks.T, vt_op)


def _glu_kernel(y_ref, w_ref, b_ref, g_ref, o_ref):
    blocks = [y_ref[:, :, k * LANES:(k + 1) * LANES].astype(F32) for k in range(SSM_P // 2)]
    y = _from_group_layout(blocks).T
    z = 0.5 * y * (1.0 + jnp.tanh(math.sqrt(2.0 / math.pi) * (y + 0.044715 * (y * y * y))))
    t = _dot(z.astype(BF16), w_ref[0]) + b_ref[...]
    o = z * jax.nn.sigmoid(t)
    inv = lax.rsqrt(jnp.mean(o * o, axis=-1, keepdims=True) + LN_EPS)
    o_ref[...] = (o * inv * g_ref[...]).astype(BF16).reshape(o_ref.shape)


def _glu(y, bsz, n_tok, row_blk_off, per_batch, w_glu_bf, layer, b_glu, g_ssm):
    w = SSM_WIDTH
    if per_batch:
        grid = (bsz, n_tok // S5_TILE)
        ospec = pl.BlockSpec((1, S5_TILE, w), lambda b, j: (b, j, 0))
    else:
        assert bsz * n_tok == S5_TILE
        grid = (1, 1)
        ospec = pl.BlockSpec((bsz, n_tok, w), lambda b, j: (0, 0, 0))
    nj = grid[1]
    row = lambda a: a.reshape(1, w)
    vspec = pl.BlockSpec((1, w), lambda b, j: (0, 0))
    return pl.pallas_call(
        _glu_kernel,
        grid=grid,
        in_specs=[pl.BlockSpec((SSM_G, S5_TILE_ROWS, GC), lambda b, j: (0, row_blk_off + b * nj + j, 0)),
                  pl.BlockSpec((1, w, w), lambda b, j: (layer, 0, 0)), vspec, vspec],
        out_specs=ospec,
        out_shape=jax.ShapeDtypeStruct((bsz, n_tok, w), BF16),
        compiler_params=_cparams("parallel", "parallel"),
        name="s5_glu",
    )(y, w_glu_bf, row(b_glu), row(g_ssm))


def _layer_norm(z, g, b):
    mu = jnp.mean(z, axis=-1, keepdims=True)
    zc = z - mu
    var = jnp.mean(zc * zc, axis=-1, keepdims=True)
    return zc * lax.rsqrt(var + LN_EPS) * g + b


def _first_argmax(vals):
    best_i = jnp.zeros(vals[0].shape, jnp.int32)
    best_v = vals[0]
    for j in range(1, len(vals)):
        better = vals[j] > best_v
        best_i = jnp.where(better, j, best_i)
        best_v = jnp.where(better, vals[j], best_v)
    return best_i, best_v


def _route(logit_rows):
    m = functools.reduce(jnp.maximum, logit_rows)
    p = [jnp.exp(l - m) for l in logit_rows]
    scores = []
    for g in range(N_EXPERT_GROUPS):
        a, b, c, d = p[4 * g:4 * g + 4]
        hi1, lo1, hi2, lo2 = jnp.maximum(a, b), jnp.minimum(a, b), jnp.maximum(c, d), jnp.minimum(c, d)
        scores.append(jnp.maximum(hi1, hi2) + jnp.maximum(jnp.minimum(hi1, hi2), jnp.maximum(lo1, lo2)))
    grp, _ = _first_argmax(scores)
    sel = []
    for j in range(EXPERTS_PER_GROUP):
        v = p[j]
        for g in range(1, N_EXPERT_GROUPS):
            v = jnp.where(grp == g, p[4 * g + j], v)
        sel.append(v)
    i1, v1 = _first_argmax(sel)
    i2, v2 = _first_argmax([jnp.where(i1 == j, -1.0, sel[j]) for j in range(EXPERTS_PER_GROUP)])
    tot = v1 + v2
    return grp * EXPERTS_PER_GROUP + i1, grp * EXPERTS_PER_GROUP + i2, v1 / tot, v2 / tot


def _outproj_kernel(a_ref, s_ref, x_ref, mod_ref, w_ref, lng_ref, lnb_ref, wr_ref, br_ref, *rest, alpha, n_steps):
    x1_ref, u2_ref, ri_ref, rw_ref, acc_even, acc_odd = rest[-6:]
    i = pl.program_id(0)

    @pl.when(i == 0)
    def _():
        acc_odd[...] = jnp.zeros(acc_odd.shape, F32)

    def step(acc_mine, acc_prev):
        acc_mine[...] = _dot(a_ref[0], w_ref[0, 0:ATTN_WIDTH, :]) + _dot(s_ref[0], w_ref[0, ATTN_WIDTH:, :])
        y = acc_prev[...]
        g1 = mod_ref[0, 2:3, :]
        sh2 = mod_ref[0, 3:4, :]
        sc2 = mod_ref[0, 4:5, :]
        x1 = _layer_norm(alpha * x_ref[0] + g1 * y, lng_ref[...], lnb_ref[...])
        x1_ref[0] = x1
        u2 = x1 * (1.0 + sc2) + sh2
        u2_ref[...] = u2
        logits = _dot_nt(wr_ref[...], u2.astype(BF16)) + br_ref[...]
        e1, e2, w1, w2 = _route([logits[e:e + 1, :] for e in range(N_EXPERTS)])
        ri_ref[0, 0:1, :] = e1
        ri_ref[0, 1:2, :] = e2
        rw_ref[0, 0:1, :] = w1
        rw_ref[0, 1:2, :] = w2

    @pl.when((i <= n_steps) & (i % 2 == 0))
    def _():
        step(acc_even, acc_odd)

    @pl.when((i <= n_steps) & (i % 2 == 1))
    def _():
        step(acc_odd, acc_even)

    @pl.when(i > n_steps)
    def _():
        u2_ref[...] = jnp.zeros(u2_ref.shape, F32)


def _out_projection(attn_n, ssm_n, x, mod3, row_of_batch, w_out_bf, layer, ln_g, ln_b, w_router_t_bf, b_router,
                    alpha, u2_rows, row_off, u2_prev=None):
    bsz, n_tok, d = x.shape
    tm = min(ROW_TILE, n_tok)
    nt = n_tok // tm
    n_steps = bsz * nt
    assert row_off % tm == 0 and u2_rows % tm == 0
    n_fill = 0 if u2_prev is not None else u2_rows // tm - n_steps
    cur = lambda i: jnp.minimum(i, n_steps - 1)
    fin = lambda i: jnp.clip(i - 1, 0, n_steps - 1)
    pspec = lambda w: pl.BlockSpec((1, tm, w), lambda i: (cur(i) // nt, cur(i) % nt, 0))
    fspec = lambda w: pl.BlockSpec((1, tm, w), lambda i: (fin(i) // nt, fin(i) % nt, 0))
    vspec = pl.BlockSpec((1, d), lambda i: (0, 0))
    rspec = pl.BlockSpec((1, 2, tm), lambda i: (fin(i) // nt, 0, fin(i) % nt))
    in_specs = [
        pspec(ATTN_WIDTH), pspec(SSM_WIDTH), fspec(d),
        pl.BlockSpec((1, N_MOD, d), lambda i: (row_of_batch(fin(i) // nt), 0, 0)),
        pl.BlockSpec((1,) + w_out_bf.shape[1:], lambda i: (layer, 0, 0)),
        vspec, vspec,
        pl.BlockSpec((N_EXPERTS, d), lambda i: (0, 0)),
        pl.BlockSpec((N_EXPERTS, 1), lambda i: (0, 0)),
    ]
    args = [attn_n, ssm_n, x, mod3, w_out_bf, ln_g.reshape(1, d), ln_b.reshape(1, d), w_router_t_bf,
            b_router.reshape(N_EXPERTS, 1)]
    aliases = {}
    if u2_prev is not None:
        in_specs.append(pl.BlockSpec(memory_space=pl.ANY))
        args.append(u2_prev)
        aliases = {len(args) - 1: 1}
    return pl.pallas_call(
        functools.partial(_outproj_kernel, alpha=alpha, n_steps=n_steps),
        grid=(n_steps + 1 + n_fill,),
        in_specs=in_specs,
        out_specs=[fspec(d), pl.BlockSpec((tm, d), lambda i: (row_off // tm + jnp.maximum(i - 1, 0), 0)),
                   rspec, rspec],
        out_shape=[jax.ShapeDtypeStruct((bsz, n_tok, d), F32), jax.ShapeDtypeStruct((u2_rows, d), F32),
                   jax.ShapeDtypeStruct((bsz, 2, n_tok), jnp.int32), jax.ShapeDtypeStruct((bsz, 2, n_tok), F32)],
        scratch_shapes=[pltpu.VMEM((tm, d), F32), pltpu.VMEM((tm, d), F32)],
        input_output_aliases=aliases,
        compiler_params=_cparams("arbitrary"),
        name="out_projection",
    )(*args)


CAST_ROWS = 256
GATHER_AHEAD = 2


def _ffn_kernel(te_ref, nx_ref, nu_ref, dest_ref, u_hbm, zeros_hbm, wg_hbm, wu_hbm, wd_hbm, y_ref,
                xbuf, stage_g, stage_u, stage_d, wg_bf, wu_bf, wd_bf, src_ref, sem, gsem, csem, *, e0):
    t = pl.program_id(0)

    @pl.when(t == 0)
    def _():
        n_tok = dest_ref.shape[0] // 2

        clear = pltpu.make_async_copy(zeros_hbm, src_ref, csem.at[0])
        clear.start()
        clear.wait()

        def invert(i, carry):
            src_ref[dest_ref[i]] = i
            src_ref[dest_ref[n_tok + i]] = i
            return carry
        lax.fori_loop(0, n_tok, invert, 0, unroll=4)

    e = te_ref[t]
    tm = xbuf.shape[1]
    pairs = ((wg_hbm, stage_g, wg_bf), (wu_hbm, stage_u, wu_bf), (wd_hbm, stage_d, wd_bf))

    def fetch(expert):
        return [pltpu.make_async_copy(hbm.at[e0 + expert], stage, sem.at[i])
                for i, (hbm, stage, _) in enumerate(pairs)]

    def gather_row(base, i, slot):
        r = src_ref[base + i]
        pltpu.make_async_copy(u_hbm.at[pl.ds(r, 1)], xbuf.at[slot, pl.ds(i, 1)], gsem.at[slot]).start()

    def wait_rows(slot):
        pltpu.make_async_copy(u_hbm.at[pl.ds(0, tm)], xbuf.at[slot], gsem.at[slot]).wait()

    last_tile = pl.num_programs(0) - 1

    @pl.when(t == 0)
    def _():
        for ahead in range(GATHER_AHEAD):
            def one_row(i, carry, ahead=ahead):
                gather_row(jnp.minimum(ahead, last_tile) * tm, i, ahead)
                return carry
            lax.fori_loop(0, tm, one_row, 0, unroll=8)
        for cp in fetch(e):
            cp.start()

    first_of_run = (t == 0) | (te_ref[jnp.maximum(t - 1, 0)] != e)

    @pl.when(first_of_run & (t < nu_ref[0]))
    def _():
        for cp in fetch(e):
            cp.wait()
        for _, stage, dst in pairs:
            def cast_rows(i, carry, stage=stage, dst=dst):
                rows = pl.ds(pl.multiple_of(i * CAST_ROWS, CAST_ROWS), CAST_ROWS)
                dst[rows, :] = stage[rows, :].astype(BF16)
                return carry
            lax.fori_loop(0, stage.shape[0] // CAST_ROWS, cast_rows, 0)

        @pl.when(nx_ref[t] >= 0)
        def _():
            for cp in fetch(nx_ref[t]):
                cp.start()

    @pl.when(t < nu_ref[0])
    def _():
        n_slots = GATHER_AHEAD + 1
        slot = t % n_slots
        wait_rows(slot)
        x = xbuf[slot].astype(BF16)
        base = jnp.minimum(t + GATHER_AHEAD, last_tile) * tm
        for i in range(tm):
            gather_row(base, i, (t + GATHER_AHEAD) % n_slots)
        g = _dot(x, wg_bf[...])
        u = _dot(x, wu_bf[...])
        h = (g * jax.nn.sigmoid(g) * u).astype(BF16)
        y_ref[...] = _dot(h, wd_bf[...])

    @pl.when(t == nu_ref[0] - 1)
    def _():
        for ahead in range(1, GATHER_AHEAD + 1):
            wait_rows((t + ahead) % (GATHER_AHEAD + 1))

    @pl.when(t >= nu_ref[0])
    def _():
        y_ref[...] = jnp.zeros(y_ref.shape, F32)


def _expert_ffn(u2, dest, n_rows, tile_expert, next_expert, n_used, w_gate, w_up, w_down, layer):
    d = u2.shape[1]
    n_e, f = w_gate.shape[1], w_gate.shape[3]
    tm = MOE_TILE
    flat = lambda w: w.reshape((-1,) + w.shape[2:])
    hbm = pl.BlockSpec(memory_space=pl.ANY)
    grid_spec = pltpu.PrefetchScalarGridSpec(
        num_scalar_prefetch=4,
        grid=(n_rows // tm,),
        in_specs=[hbm, hbm, hbm, hbm, hbm],
        out_specs=pl.BlockSpec((tm, d), lambda t, te, nx, nu, src: (t, 0)),
        scratch_shapes=[pltpu.VMEM((GATHER_AHEAD + 1, tm, d), F32),
                        pltpu.VMEM((d, f), F32), pltpu.VMEM((d, f), F32), pltpu.VMEM((f, d), F32),
                        pltpu.VMEM((d, f), BF16), pltpu.VMEM((d, f), BF16), pltpu.VMEM((f, d), BF16),
                        pltpu.SMEM((n_rows,), jnp.int32),
                        pltpu.SemaphoreType.DMA((3,)), pltpu.SemaphoreType.DMA((GATHER_AHEAD + 1,)),
                        pltpu.SemaphoreType.DMA((1,))],
    )
    return pl.pallas_call(
        functools.partial(_ffn_kernel, e0=layer * n_e),
        grid_spec=grid_spec,
        out_shape=jax.ShapeDtypeStruct((n_rows, d), F32),
        compiler_params=_cparams("arbitrary"),
        name="expert_ffn",
    )(tile_expert, next_expert, n_used, dest, u2, jnp.zeros((n_rows,), jnp.int32), flat(w_gate), flat(w_up),
      flat(w_down))


def _dispatch_plan(e1, e2, tm):
    n = e1.shape[0]
    n_tiles = (2 * n + N_EXPERTS * (tm - 1) + tm - 1) // tm
    e = jnp.concatenate([e1, e2])
    onehot = (e[:, None] == jnp.arange(N_EXPERTS, dtype=jnp.int32)[None, :]).astype(jnp.int32)
    csum = jnp.cumsum(onehot, axis=0)
    pos_in_e = jnp.sum((csum - 1) * onehot, axis=1)
    counts = csum[-1]
    padded = ((counts + tm - 1) // tm) * tm
    ends = jnp.cumsum(padded)
    offs = ends - padded
    dest = (offs[e] + pos_in_e).astype(jnp.int32)
    n_used = (ends[-1] // tm).astype(jnp.int32)
    tile_start = jnp.arange(n_tiles, dtype=jnp.int32) * tm
    tile_e = jnp.sum((tile_start[:, None] >= ends[None, :]).astype(jnp.int32), axis=1)
    tile_e = jnp.minimum(tile_e, N_EXPERTS - 1)
    last_e = tile_e[jnp.maximum(n_used - 1, 0)]
    tile_e = jnp.where(jnp.arange(n_tiles) < n_used, tile_e, last_e).astype(jnp.int32)
    ids = jnp.arange(N_EXPERTS, dtype=jnp.int32)
    later = (ids[None, :] > ids[:, None]) & (counts[None, :] > 0)
    next_of = jnp.min(jnp.where(later, ids[None, :], N_EXPERTS), axis=1)
    next_of = jnp.where(next_of == N_EXPERTS, -1, next_of).astype(jnp.int32)
    return dest, n_tiles * tm, tile_e, next_of[tile_e], n_used.reshape(1)


def _final_kernel(p1_ref, p2_ref, x_ref, ys_hbm, w1_ref, w2_ref, mod_ref, lng_ref, lnb_ref, o_ref, ybuf, gsem,
                  *, alpha, row0):
    i = pl.program_id(0)
    tm = x_ref.shape[1]
    last_tile = pl.num_programs(0) - 1
    n_slots = GATHER_AHEAD + 1

    def gather_row(tile, r, slot):
        for k, p_ref in enumerate((p1_ref, p2_ref)):
            src = p_ref[row0 + tile * tm + r]
            pltpu.make_async_copy(ys_hbm.at[pl.ds(src, 1)], ybuf.at[slot, k, pl.ds(r, 1)], gsem.at[slot]).start()

    def wait_rows(slot):
        for k in range(2):
            pltpu.make_async_copy(ys_hbm.at[pl.ds(0, tm)], ybuf.at[slot, k], gsem.at[slot]).wait()

    @pl.when(i == 0)
    def _():
        for ahead in range(GATHER_AHEAD):
            def one_row(r, carry, ahead=ahead):
                gather_row(jnp.minimum(ahead, last_tile), r, ahead)
                return carry
            lax.fori_loop(0, tm, one_row, 0, unroll=8)

    slot = i % n_slots
    wait_rows(slot)
    nxt = jnp.minimum(i + GATHER_AHEAD, last_tile)
    for r in range(tm):
        gather_row(nxt, r, (i + GATHER_AHEAD) % n_slots)
    f = w1_ref[...] * ybuf[slot, 0] + w2_ref[...] * ybuf[slot, 1]
    g2 = mod_ref[0, 5:6, :]
    o_ref[0] = _layer_norm(alpha * x_ref[0] + g2 * f, lng_ref[...], lnb_ref[...])

    @pl.when(i == last_tile)
    def _():
        for ahead in range(1, GATHER_AHEAD + 1):
            wait_rows((i + ahead) % n_slots)


def _combine_ln(x1, ys, pos1, pos2, w1, w2, row_off, mod3, row_of_batch, ln_g, ln_b, alpha):
    bsz, n_tok, d = x1.shape
    tm = min(ROW_TILE, n_tok)
    nt = n_tok // tm
    assert row_off % tm == 0
    tspec = pl.BlockSpec((1, tm, d), lambda i, p1, p2: (i // nt, i % nt, 0))
    wspec = pl.BlockSpec((tm, 1), lambda i, p1, p2: (row_off // tm + i, 0))
    vspec = pl.BlockSpec((1, d), lambda i, p1, p2: (0, 0))
    grid_spec = pltpu.PrefetchScalarGridSpec(
        num_scalar_prefetch=2,
        grid=(bsz * nt,),
        in_specs=[tspec, pl.BlockSpec(memory_space=pl.ANY), wspec, wspec,
                  pl.BlockSpec((1, N_MOD, d), lambda i, p1, p2: (row_of_batch(i // nt), 0, 0)), vspec, vspec],
        out_specs=tspec,
        scratch_shapes=[pltpu.VMEM((GATHER_AHEAD + 1, 2, tm, d), F32),
                        pltpu.SemaphoreType.DMA((GATHER_AHEAD + 1,))],
    )
    return pl.pallas_call(
        functools.partial(_final_kernel, alpha=alpha, row0=row_off),
        grid_spec=grid_spec,
        out_shape=jax.ShapeDtypeStruct((bsz, n_tok, d), F32),
        compiler_params=_cparams("arbitrary"),
        name="combine_post_ln",
    )(pos1, pos2, x1, ys, w1, w2, mod3, ln_g.reshape(1, d), ln_b.reshape(1, d))


def kernel(x, c, ctx, c_ctx, w_mod, b_mod, w_in, attn_sink, ssm_lambda_re, ssm_lambda_im, ssm_log_dt, ssm_b_re, ssm_b_im, ssm_c_re, ssm_c_im, ssm_d, w_glu, b_glu, g_attn_out, g_ssm_out, w_out, ln1_g, ln1_b, w_router, b_router, w_expert_gate, w_expert_up, w_expert_down, ln2_g, ln2_b):
    depth = w_mod.shape[0]
    bsz, n_tok, d = x.shape
    n_ctx = ctx.shape[1]
    alpha = (2 * depth) ** 0.25
    assert n_tok % ROW_TILE == 0 and n_tok % ATTN_BLOCK == 0 and n_ctx % ATTN_BLOCK == 0 and n_ctx % CHUNK == 0

    mod = _modulation(c, c_ctx, w_mod, b_mod)
    s5_params = (ssm_lambda_re, ssm_lambda_im, ssm_log_dt, ssm_b_re, ssm_b_im, ssm_c_re, ssm_c_im, ssm_d)
    expert_w = (w_expert_gate, w_expert_up, w_expert_down)
    s5_ops = _s5_operators(*s5_params, 0, jnp.zeros((1,), jnp.int32))
    cos_t, sin_t = _rope_tables(n_tok)
    w_router_t = w_router.T.astype(BF16)
    w_glu_bf, w_out_bf = w_glu.astype(BF16), w_out.astype(BF16)
    n_qkv = ATTN_WIDTH + 2 * KV_WIDTH
    w_in_bf = w_in[:, :, :n_qkv].astype(BF16)
    ws_t_bf = jnp.swapaxes(w_in[:, :, n_qkv:].astype(BF16), 1, 2)
    lat_row = lambda b: b
    ctx_row = lambda b: bsz
    n_lat, n_c = bsz * n_tok, bsz * n_ctx
    lat_blocks = n_lat // S5_TILE
    s5_rows = (lat_blocks + 1) * S5_TILE_ROWS

    xc = ctx
    for i in range(depth):
        last = i == depth - 1
        mod3 = mod[i].reshape(MOD_ROWS, N_MOD, d)
        n_moe = n_lat if last else n_lat + n_c

        q, k, v = _in_projection(x, mod3, lat_row, w_in_bf, i, cos_t, sin_t, True)
        qc, kc, vc = _in_projection(xc, mod3, ctx_row, w_in_bf, i, cos_t, sin_t, False)
        u = _s_projection(x, mod3, lat_row, ws_t_bf, i, s5_rows, 0)
        u = _s_projection(xc, mod3, ctx_row, ws_t_bf, i, s5_rows, lat_blocks, u_prev=u)
        attn_n = _attention(q, k, v, kc, vc, attn_sink[i], g_attn_out[i], True)
        y_s5 = _s5_apply(u, bsz, n_tok, n_ctx, s5_ops)
        ssm_n = _glu(y_s5, bsz, n_tok, 0, True, w_glu_bf, i, b_glu[i], g_ssm_out[i])
        x1, u2, ri, rw = _out_projection(attn_n, ssm_n, x, mod3, lat_row, w_out_bf, i, ln1_g[i], ln1_b[i],
                                         w_router_t, b_router, alpha, n_moe, 0)
        if not last:
            attn_c = _attention(qc, None, None, kc, vc, attn_sink[i], g_attn_out[i], False)
            ssm_c = _glu(y_s5, bsz, n_ctx, lat_blocks, False, w_glu_bf, i, b_glu[i], g_ssm_out[i])
            xc1, u2, ric, rwc = _out_projection(attn_c, ssm_c, xc, mod3, ctx_row, w_out_bf, i, ln1_g[i], ln1_b[i],
                                                w_router_t, b_router, alpha, n_moe, n_lat, u2_prev=u2)
            ri = jnp.concatenate([ri.transpose(1, 0, 2).reshape(2, n_lat), ric.transpose(1, 0, 2).reshape(2, n_c)], axis=1)
            rw = jnp.concatenate([rw.transpose(1, 0, 2).reshape(2, n_lat), rwc.transpose(1, 0, 2).reshape(2, n_c)], axis=1)
        else:
            ri = ri.transpose(1, 0, 2).reshape(2, n_lat)
            rw = rw.transpose(1, 0, 2).reshape(2, n_lat)

        dest, n_sorted, tile_e, next_e, n_used = _dispatch_plan(ri[0], ri[1], MOE_TILE)
        pos1, pos2 = dest[:n_moe], dest[n_moe:]
        if not last:
            s5_ops = _s5_operators(*s5_params, i + 1, n_used)
        ys = _expert_ffn(u2, dest, n_sorted, tile_e, next_e, n_used, *expert_w, i)
        cw1, cw2 = rw[0].reshape(n_moe, 1), rw[1].reshape(n_moe, 1)
        x = _combine_ln(x1, ys, pos1, pos2, cw1, cw2, 0, mod3, lat_row, ln2_g[i], ln2_b[i], alpha)
        if not last:
            xc = _combine_ln(xc1, ys, pos1, pos2, cw1, cw2, n_lat, mod3, ctx_row, ln2_g[i], ln2_b[i], alpha)
    return x
```

```python
import functools
import math

import jax
import jax.numpy as jnp
from jax import lax
from jax.experimental import pallas as pl
from jax.experimental.pallas import tpu as pltpu

F32 = jnp.float32
BF16 = jnp.bfloat16

HEAD_DIM = 128
N_Q_HEADS = 8
N_KV_HEADS = 2
Q_PER_KV = N_Q_HEADS // N_KV_HEADS
ATTN_WIDTH = N_Q_HEADS * HEAD_DIM
KV_WIDTH = N_KV_HEADS * HEAD_DIM
SSM_WIDTH = 1024
ATTN_BLOCK = 128
GRID_W = 64
ROPE_THETA = 10000.0
SSM_P = 16
SSM_G = SSM_WIDTH // SSM_P
SSM_N = 64
CHUNK = 64
GC = SSM_P * CHUNK
N_EXPERTS = 16
N_EXPERT_GROUPS = 4
EXPERTS_PER_GROUP = 4
N_MOD = 6
LN_EPS = 1e-5
NEG_INF = -1e30
LANES = 128
MOD_ROWS = 8
ROW_TILE = 256
MOE_TILE = 256
VMEM_LIMIT = 56 * 1024 * 1024

assert 2 * SSM_N == LANES and 2 * CHUNK == LANES


def _cparams(*sem):
    return pltpu.CompilerParams(dimension_semantics=sem, vmem_limit_bytes=VMEM_LIMIT)


def _dot(a, b):
    return jnp.dot(a, b, preferred_element_type=F32)


def _dot_nt(a, b):
    return lax.dot_general(a, b, (((1,), (1,)), ((), ())), preferred_element_type=F32)


def _mod_kernel(ct_ref, w_ref, b_ref, o_ref, ab_ref, *, n_rows, tn):
    d = ct_ref.shape[0]

    @pl.when((pl.program_id(0) == 0) & (pl.program_id(1) == 0))
    def _():
        ct = ct_ref[...]
        a = ct * jax.nn.sigmoid(ct)
        for r in range(n_rows):
            ab_ref[r] = jnp.broadcast_to(a[:, r:r + 1], (d, LANES))

    o_ref[...] = jnp.zeros(o_ref.shape, F32)
    sub = 8
    width = 2 * LANES
    for j in range(tn // width):
        cols = slice(j * width, (j + 1) * width)

        def k_step(kc, accs, cols=cols):
            rows = pl.ds(pl.multiple_of(kc * sub, sub), sub)
            w = w_ref[0, rows, cols]
            return tuple(acc + jnp.concatenate([ab_ref[r, rows, :]] * 2, axis=1) * w for r, acc in enumerate(accs))

        zero = jnp.zeros((sub, width), F32)
        accs = lax.fori_loop(0, d // sub, k_step, (zero,) * n_rows, unroll=8)
        for r in range(n_rows):
            o_ref[0, r:r + 1, cols] = jnp.sum(accs[r], axis=0, keepdims=True) + b_ref[0, :, cols]


def _modulation(c, c_ctx, w_mod, b_mod):
    depth, d, n_out = w_mod.shape
    n_rows = c.shape[0] + 1
    assert n_rows <= MOD_ROWS
    tn = 512
    ct = jnp.zeros((MOD_ROWS, d), F32).at[:c.shape[0]].set(c).at[c.shape[0]].set(c_ctx).T
    return pl.pallas_call(
        functools.partial(_mod_kernel, n_rows=n_rows, tn=tn),
        grid=(depth, n_out // tn),
        in_specs=[
            pl.BlockSpec((d, MOD_ROWS), lambda l, j: (0, 0)),
            pl.BlockSpec((1, d, tn), lambda l, j: (l, 0, j)),
            pl.BlockSpec((1, 1, tn), lambda l, j: (l, 0, j)),
        ],
        out_specs=pl.BlockSpec((1, MOD_ROWS, tn), lambda l, j: (l, 0, j)),
        out_shape=jax.ShapeDtypeStruct((depth, MOD_ROWS, n_out), F32),
        scratch_shapes=[pltpu.VMEM((n_rows, d, LANES), F32)],
        compiler_params=_cparams("arbitrary", "arbitrary"),
        name="modulation",
    )(ct, w_mod, b_mod.reshape(depth, 1, n_out))


def _rope(xh, cos, sin_signed):
    lane = lax.broadcasted_iota(jnp.int32, xh.shape, 1)
    swapped = jnp.where((lane % 64) < 32, pltpu.roll(xh, 96, 1), pltpu.roll(xh, 32, 1))
    return xh * cos + swapped * sin_signed


def _inproj_kernel(x_ref, mod_ref, w_ref, cos_ref, sin_ref, q_ref, k_ref, v_ref, acc_even, acc_odd, *, rope):
    i = pl.program_id(0)
    n_cols = ATTN_WIDTH + 2 * KV_WIDTH

    @pl.when(i == 0)
    def _():
        acc_odd[...] = jnp.zeros(acc_odd.shape, F32)

    def step(acc_mine, acc_prev):
        sh = mod_ref[0, 0:1, :]
        sc = mod_ref[0, 1:2, :]
        u = (x_ref[0] * (1.0 + sc) + sh).astype(BF16)
        acc_mine[...] = _dot(u, w_ref[0, :, 0:n_cols])
        scale = HEAD_DIM ** -0.5
        if rope:
            cos = cos_ref[...]
            sin = sin_ref[...]
        for h in range(N_Q_HEADS):
            qh = acc_prev[:, h * HEAD_DIM:(h + 1) * HEAD_DIM]
            if rope:
                qh = _rope(qh, cos, sin)
            q_ref[0, :, h * HEAD_DIM:(h + 1) * HEAD_DIM] = (qh * scale).astype(BF16)
        for h in range(N_KV_HEADS):
            kh = acc_prev[:, ATTN_WIDTH + h * HEAD_DIM:ATTN_WIDTH + (h + 1) * HEAD_DIM]
            if rope:
                kh = _rope(kh, cos, sin)
            k_ref[0, :, h * HEAD_DIM:(h + 1) * HEAD_DIM] = kh.astype(BF16)
        v_ref[0] = acc_prev[:, ATTN_WIDTH + KV_WIDTH:n_cols].astype(BF16)

    @pl.when(i % 2 == 0)
    def _():
        step(acc_even, acc_odd)

    @pl.when(i % 2 == 1)
    def _():
        step(acc_odd, acc_even)


def _in_projection(x, mod3, row_of_batch, w_in_bf, layer, cos_t, sin_t, rope):
    bsz, n_tok, d = x.shape
    tm = min(ROW_TILE, n_tok)
    nt = n_tok // tm
    n_steps = bsz * nt
    n_cols = ATTN_WIDTH + 2 * KV_WIDTH
    cur = lambda i: jnp.minimum(i, n_steps - 1)
    fin = lambda i: jnp.maximum(i - 1, 0)
    out = lambda w: jax.ShapeDtypeStruct((bsz, n_tok, w), BF16)
    ospec = lambda w: pl.BlockSpec((1, tm, w), lambda i: (fin(i) // nt, fin(i) % nt, 0))
    tspec = pl.BlockSpec((tm, HEAD_DIM), lambda i: (fin(i) % nt, 0))
    return pl.pallas_call(
        functools.partial(_inproj_kernel, rope=rope),
        grid=(n_steps + 1,),
        in_specs=[
            pl.BlockSpec((1, tm, d), lambda i: (cur(i) // nt, cur(i) % nt, 0)),
            pl.BlockSpec((1, N_MOD, d), lambda i: (row_of_batch(cur(i) // nt), 0, 0)),
            pl.BlockSpec((1, d, n_cols), lambda i: (layer, 0, 0)),
            tspec, tspec,
        ],
        out_specs=[ospec(ATTN_WIDTH), ospec(KV_WIDTH), ospec(KV_WIDTH)],
        out_shape=[out(ATTN_WIDTH), out(KV_WIDTH), out(KV_WIDTH)],
        scratch_shapes=[pltpu.VMEM((tm, n_cols), F32), pltpu.VMEM((tm, n_cols), F32)],
        compiler_params=_cparams("arbitrary"),
        name="in_projection",
    )(x, mod3, w_in_bf, cos_t, sin_t)


S5_TILE = 1024
S5_TILE_ROWS = 2 * S5_TILE // LANES


def _to_group_layout(st):
    x4 = st.reshape(SSM_G, SSM_P, S5_TILE // LANES, LANES)
    lane = lax.broadcasted_iota(jnp.int32, x4.shape[:1] + x4.shape[2:], 2)
    roll64 = lambda a: pltpu.roll(a.reshape(-1, LANES), CHUNK, 1).reshape(a.shape)
    out = []
    for k in range(SSM_P // 2):
        a, b = x4[:, 2 * k], x4[:, 2 * k + 1]
        h0 = jnp.where(lane < CHUNK, a, roll64(b))
        h1 = jnp.where(lane < CHUNK, roll64(a), b)
        out.append(jnp.concatenate([h0, h1], axis=1))
    return out


def _from_group_layout(blocks):
    n8 = S5_TILE // LANES
    lane = lax.broadcasted_iota(jnp.int32, (SSM_G, n8, LANES), 2)
    roll64 = lambda a: pltpu.roll(a.reshape(-1, LANES), CHUNK, 1).reshape(a.shape)
    chans = []
    for blk in blocks:
        a0, b1 = blk[:, 0:n8], blk[:, n8:]
        chans.append(jnp.where(lane < CHUNK, a0, roll64(b1)))
        chans.append(jnp.where(lane < CHUNK, roll64(a0), b1))
    return jnp.stack(chans, axis=1).reshape(SSM_WIDTH, S5_TILE)


def _sproj_kernel(x_ref, mod_ref, w_ref, *rest, n_steps):
    u_ref = rest[-1]

    @pl.when(pl.program_id(0) < n_steps)
    def _():
        sh = mod_ref[0, 0:1, :]
        sc = mod_ref[0, 1:2, :]
        x = x_ref[...].reshape(S5_TILE, x_ref.shape[-1])
        u = (x * (1.0 + sc) + sh).astype(BF16)
        st = _dot_nt(w_ref[0], u)
        for k, blk in enumerate(_to_group_layout(st)):
            u_ref[:, :, k * LANES:(k + 1) * LANES] = blk.astype(BF16)

    @pl.when(pl.program_id(0) >= n_steps)
    def _():
        u_ref[...] = jnp.zeros(u_ref.shape, BF16)


def _s_projection(x, mod3, mod_row, ws_t_bf, layer, n_rows, row_blk_off, u_prev=None):
    bsz, n_tok, d = x.shape
    if u_prev is None:
        assert n_tok % S5_TILE == 0
        nj = n_tok // S5_TILE
        n_steps = bsz * nj
        n_fill = n_rows // S5_TILE_ROWS - n_steps
        tile = lambda i: jnp.minimum(i, n_steps - 1)
        xspec = pl.BlockSpec((1, S5_TILE, d), lambda i: (tile(i) // nj, tile(i) % nj, 0))
        mspec = pl.BlockSpec((1, N_MOD, d), lambda i: (mod_row(tile(i) // nj), 0, 0))
    else:
        assert bsz * n_tok == S5_TILE and n_tok % LANES == 0
        n_steps, n_fill = 1, 0
        xspec = pl.BlockSpec((bsz, n_tok, d), lambda i: (0, 0, 0))
        mspec = pl.BlockSpec((1, N_MOD, d), lambda i: (mod_row(0), 0, 0))
    in_specs = [xspec, mspec, pl.BlockSpec((1, SSM_WIDTH, d), lambda i: (layer, 0, 0))]
    args = [x, mod3, ws_t_bf]
    aliases = {}
    if u_prev is not None:
        in_specs.append(pl.BlockSpec(memory_space=pl.ANY))
        args.append(u_prev)
        aliases = {3: 0}
    return pl.pallas_call(
        functools.partial(_sproj_kernel, n_steps=n_steps),
        grid=(n_steps + n_fill,),
        in_specs=in_specs,
        out_specs=pl.BlockSpec((SSM_G, S5_TILE_ROWS, GC), lambda i: (0, row_blk_off + i, 0)),
        out_shape=jax.ShapeDtypeStruct((SSM_G, n_rows, GC), BF16),
        input_output_aliases=aliases,
        compiler_params=_cparams("arbitrary"),
        name="s_projection",
    )(*args)


def _rope_tables(n_tok):
    half = HEAD_DIM // 2
    inv_freq = ROPE_THETA ** (-jnp.arange(0, half, 2, dtype=F32) / half)
    t = jnp.arange(n_tok)
    row = (t // GRID_W).astype(F32)
    col = (t % GRID_W).astype(F32)
    ang_r = row[:, None] * inv_freq[None, :]
    ang_c = col[:, None] * inv_freq[None, :]
    cos_t = jnp.concatenate([jnp.cos(ang_r), jnp.cos(ang_r), jnp.cos(ang_c), jnp.cos(ang_c)], axis=-1)
    sin_t = jnp.concatenate([-jnp.sin(ang_r), jnp.sin(ang_r), -jnp.sin(ang_c), jnp.sin(ang_c)], axis=-1)
    return cos_t, sin_t


def _attn_kernel(sink_ref, q_ref, *refs, has_band, n_blk):
    if has_band:
        kp_ref, ko_ref, kn_ref, vp_ref, vo_ref, vn_ref, kc_ref, vc_ref, g_ref, o_ref = refs
    else:
        kc_ref, vc_ref, g_ref, o_ref = refs
    blk = pl.program_id(1)
    rows = Q_PER_KV * ATTN_BLOCK
    qi = lax.broadcasted_iota(jnp.int32, (rows, ATTN_BLOCK), 0) % ATTN_BLOCK
    kj = lax.broadcasted_iota(jnp.int32, (rows, ATTN_BLOCK), 1)
    row_head = lax.broadcasted_iota(jnp.int32, (rows, 1), 0) // ATTN_BLOCK
    off_p = jnp.where(blk > 0, 0, ATTN_BLOCK)
    off_n = jnp.where(blk < n_blk - 1, 0, ATTN_BLOCK)
    heads = []
    for h in range(N_KV_HEADS):
        hs = slice(h * HEAD_DIM, (h + 1) * HEAD_DIM)
        q = jnp.concatenate(
            [q_ref[0, :, (h * Q_PER_KV + g) * HEAD_DIM:(h * Q_PER_KV + g + 1) * HEAD_DIM] for g in range(Q_PER_KV)],
            axis=0)
        sink = jnp.zeros((rows, 1), F32)
        for g in range(Q_PER_KV):
            sink = jnp.where(row_head == g, sink_ref[h * Q_PER_KV + g], sink)
        s_c = _dot_nt(q, kc_ref[0, :, hs])
        m = jnp.maximum(jnp.max(s_c, axis=-1, keepdims=True), sink)
        if has_band:
            s_p = jnp.where(kj >= qi + off_p, _dot_nt(q, kp_ref[0, :, hs]), NEG_INF)
            s_o = _dot_nt(q, ko_ref[0, :, hs])
            s_n = jnp.where(kj <= qi - off_n, _dot_nt(q, kn_ref[0, :, hs]), NEG_INF)
            m = jnp.maximum(m, jnp.max(jnp.maximum(jnp.maximum(s_p, s_o), s_n), axis=-1, keepdims=True))
        p_c = jnp.exp(s_c - m)
        denom = jnp.sum(p_c, axis=-1, keepdims=True) + jnp.exp(sink - m)
        acc = _dot(p_c.astype(BF16), vc_ref[0, :, hs])
        if has_band:
            for s_x, v_ref in ((s_p, vp_ref), (s_o, vo_ref), (s_n, vn_ref)):
                p_x = jnp.exp(s_x - m)
                denom = denom + jnp.sum(p_x, axis=-1, keepdims=True)
                acc = acc + _dot(p_x.astype(BF16), v_ref[0, :, hs])
        o = acc / denom
        for g in range(Q_PER_KV):
            heads.append(o[g * ATTN_BLOCK:(g + 1) * ATTN_BLOCK, :])
    ss = heads[0] * heads[0]
    for o in heads[1:]:
        ss = ss + o * o
    inv = lax.rsqrt(jnp.sum(ss, axis=-1, keepdims=True) / ATTN_WIDTH + LN_EPS)
    for i, o in enumerate(heads):
        cs = slice(i * HEAD_DIM, (i + 1) * HEAD_DIM)
        o_ref[0, :, cs] = (o * inv * g_ref[:, cs]).astype(BF16)


def _attention(q, k, v, kc, vc, sink, g_attn, has_band):
    bsz, n_tok, _ = q.shape
    n_ctx = kc.shape[1]
    n_blk = n_tok // ATTN_BLOCK
    qspec = pl.BlockSpec((1, ATTN_BLOCK, ATTN_WIDTH), lambda b, n: (b, n, 0))
    kvspec = lambda f: pl.BlockSpec((1, ATTN_BLOCK, KV_WIDTH), lambda b, n: (b, f(n), 0))
    cspec = pl.BlockSpec((1, n_ctx, KV_WIDTH), lambda b, n: (b, 0, 0))
    prev = lambda n: jnp.maximum(n - 1, 0)
    own = lambda n: n
    nxt = lambda n: jnp.minimum(n + 1, n_blk - 1)
    in_specs = [pl.BlockSpec(memory_space=pltpu.SMEM), qspec]
    args = [sink, q]
    if has_band:
        in_specs += [kvspec(prev), kvspec(own), kvspec(nxt), kvspec(prev), kvspec(own), kvspec(nxt)]
        args += [k, k, k, v, v, v]
    in_specs += [cspec, cspec, pl.BlockSpec((1, ATTN_WIDTH), lambda b, n: (0, 0))]
    args += [kc, vc, g_attn.reshape(1, ATTN_WIDTH)]
    return pl.pallas_call(
        functools.partial(_attn_kernel, has_band=has_band, n_blk=n_blk),
        grid=(bsz, n_blk),
        in_specs=in_specs,
        out_specs=qspec,
        out_shape=jax.ShapeDtypeStruct((bsz, n_tok, ATTN_WIDTH), BF16),
        compiler_params=_cparams("parallel", "parallel"),
        name="attention",
    )(*args)


def _cmul(a, b):
    return a[0] * b[0] - a[1] * b[1], a[0] * b[1] + a[1] * b[0]


def _cpow(rho, theta, expo):
    mag = jnp.exp(expo * rho)
    ang = expo * theta
    return mag * jnp.cos(ang), mag * jnp.sin(ang)


def _s5_param_kernel(after_ref, prow_ref, bt_ref, cr_ref, dd_ref, k_ref, w_ref, vt_ref, al_ref):
    del after_ref
    p_dim, lc = SSM_P, CHUNK
    lam_re, lam_im = prow_ref[0, 0:1, :], prow_ref[0, 1:2, :]
    dt = jnp.exp(prow_ref[0, 2:3, :])
    rho, theta = lam_re * dt, lam_im * dt

    sub8 = lax.broadcasted_iota(jnp.int32, (8, LANES), 0)
    asc = _cpow(rho, theta, sub8.astype(F32))
    desc = _cpow(rho, theta, (8 - sub8).astype(F32))
    ex = jnp.where(sub8 == 0, 8, jnp.where(sub8 == 1, 16, jnp.where(sub8 == 2, 32, jnp.where(sub8 == 3, lc, 1))))
    pw = _cpow(rho, theta, ex.astype(F32))
    row = lambda t, i: (t[0][i:i + 1, :], t[1][i:i + 1, :])
    cat = lambda a, b: (jnp.concatenate([a[0], b[0]], axis=0), jnp.concatenate([a[1], b[1]], axis=0))
    for i in range(3):
        step = row(pw, i)
        asc = cat(asc, _cmul(asc, step))
        desc = cat(_cmul(desc, step), desc)
    a_chunk, a_one = row(pw, 3), row(pw, 4)

    den = lam_re * lam_re + lam_im * lam_im
    x_re, x_im = a_one[0] - 1.0, a_one[1]
    beta = ((x_re * lam_re + x_im * lam_im) / den, (x_im * lam_re - x_re * lam_im) / den)
    bb = _cmul(beta, (bt_ref[0, 0], bt_ref[0, 1]))
    cc = (cr_ref[0, 0], cr_ref[0, 1])

    is_fwd = lax.broadcasted_iota(jnp.int32, (lc, LANES), 1) < SSM_N
    sub = lax.broadcasted_iota(jnp.int32, (lc, LANES), 0)
    pw_w = (jnp.where(is_fwd, desc[0], asc[0]), jnp.where(is_fwd, desc[1], asc[1]))
    pw_v = (jnp.where(is_fwd, asc[0], desc[0]), jnp.where(is_fwd, asc[1], desc[1]))

    top = (jnp.where(is_fwd, asc[0], jnp.where(sub == 0, 1.0, 0.0)), jnp.where(is_fwd, asc[1], 0.0))
    keep = (~is_fwd) & (sub > 0)
    bot = (jnp.where(keep, desc[0], 0.0), jnp.where(keep, desc[1], 0.0))
    lag = cat(top, bot)

    cx = (jnp.concatenate([cc[0]] * p_dim, axis=0), jnp.concatenate([cc[1]] * p_dim, axis=0))
    bx = tuple(jnp.concatenate([jnp.broadcast_to(b[q:q + 1, :], (p_dim, LANES)) for q in range(p_dim)], axis=0)
               for b in bb)
    e_re, e_im = _cmul(cx, bx)
    hi = lax.Precision.HIGHEST
    kmat = (jnp.dot(e_re, lag[0].T, preferred_element_type=F32, precision=hi)
            - jnp.dot(e_im, lag[1].T, preferred_element_type=F32, precision=hi))
    lane0 = lax.broadcasted_iota(jnp.int32, kmat.shape, 1) == 0
    k_ref[0] = kmat + jnp.where(lane0, dd_ref[0], 0.0)

    for q in range(p_dim):
        w_re, w_im = _cmul(pw_w, row(bb, q))
        w_ref[0, q * lc:(q + 1) * lc, :] = jnp.concatenate([w_re, w_im], axis=1).astype(BF16)
    for p in range(p_dim):
        v_re, v_im = _cmul(pw_v, row(cc, p))
        vt_ref[0, p * lc:(p + 1) * lc, :] = jnp.concatenate([v_re, -v_im], axis=1).astype(BF16)
    al_ref[0] = jnp.concatenate([a_chunk[0], a_chunk[1]], axis=1)


def _s5_operators(lam_re, lam_im, log_dt, b_re, b_im, c_re, c_im, d_skip, layer, after):
    depth = lam_re.shape[0]
    dg = depth * SSM_G
    g0 = layer * SSM_G
    vec = lambda a: jnp.moveaxis(a, 1, 2).reshape(dg, LANES)
    mat = lambda a: jnp.moveaxis(a, 1, 3).reshape(dg, SSM_P, LANES)
    ldt = jnp.broadcast_to(log_dt[..., None], lam_re.shape)
    prow = jnp.stack([vec(lam_re), vec(lam_im), vec(ldt)], axis=1)
    bt = jnp.stack([mat(jnp.swapaxes(b_re, 3, 4)), mat(jnp.swapaxes(b_im, 3, 4))], axis=1)
    cr = jnp.stack([mat(c_re), mat(c_im)], axis=1)
    dd = (d_skip.reshape(dg, 1, SSM_P) * jnp.eye(SSM_P, dtype=F32)[None]).reshape(dg, SSM_P * SSM_P, 1)
    iblk = lambda *s: pl.BlockSpec((1,) + s, lambda i: (g0 + i,) + (0,) * len(s))
    oblk = lambda *s: pl.BlockSpec((1,) + s, lambda i: (i,) + (0,) * len(s))
    return pl.pallas_call(
        _s5_param_kernel,
        grid=(SSM_G,),
        in_specs=[pl.BlockSpec(memory_space=pltpu.SMEM),
                  iblk(3, LANES), iblk(2, SSM_P, LANES), iblk(2, SSM_P, LANES), iblk(SSM_P * SSM_P, 1)],
        out_specs=[oblk(SSM_P * SSM_P, LANES), oblk(GC, 2 * LANES), oblk(GC, 2 * LANES), oblk(1, 2 * LANES)],
        out_shape=[jax.ShapeDtypeStruct((SSM_G, SSM_P * SSM_P, LANES), F32),
                   jax.ShapeDtypeStruct((SSM_G, GC, 2 * LANES), BF16),
                   jax.ShapeDtypeStruct((SSM_G, GC, 2 * LANES), BF16),
                   jax.ShapeDtypeStruct((SSM_G, 1, 2 * LANES), F32)],
        compiler_params=_cparams("parallel"),
        name="s5_operators",
    )(after, prow, bt, cr, dd)


def _exact_f32_dot(a, b):
    return jnp.dot(a, b, preferred_element_type=F32, precision=lax.Precision.HIGHEST)


def _s5_sum_kernel(u_ref, w_ref, p_ref, s_ref):
    s_ref[0] = _exact_f32_dot(p_ref[...], _dot(u_ref[0], w_ref[0]))


def _s5_scan_kernel(s_ref, al_ref, h_ref, *, bsz, n_ctx_chunks, n_chunks):
    a_re, a_im = al_ref[:, :, :LANES], al_ref[:, :, LANES:]
    shape = (s_ref.shape[0], bsz, LANES)
    is_fwd = lax.broadcasted_iota(jnp.int32, shape, 2) < SSM_N
    pair = 2 * bsz
    n_ctx_pairs, n_pairs = n_ctx_chunks // 2, n_chunks // 2

    def advance(h, s):
        return a_re * h[0] - a_im * h[1] + s[0], a_re * h[1] + a_im * h[0] + s[1]

    def body(k, h):
        kr = jnp.where(k < n_ctx_pairs, n_ctx_pairs - 1 - k, n_pairs - 1 - (k - n_ctx_pairs))
        rf = pl.ds(pl.multiple_of(k * pair, pair), pair)
        rr = pl.ds(pl.multiple_of(kr * pair, pair), pair)
        s_f, s_r = s_ref[:, rf, :], s_ref[:, rr, :]
        lo, hi = slice(0, bsz), slice(bsz, pair)
        pick = lambda first, second: (jnp.where(is_fwd, s_f[:, first, :LANES], s_r[:, second, :LANES]),
                                      jnp.where(is_fwd, s_f[:, first, LANES:], s_r[:, second, LANES:]))
        h_mid = advance(h, pick(lo, hi))
        h_out = advance(h_mid, pick(hi, lo))
        for part, off in ((0, 0), (1, LANES)):
            rows_f = jnp.concatenate([h[part], h_mid[part]], axis=1)
            rows_r = jnp.concatenate([h_mid[part], h[part]], axis=1)
            h_ref[:, rf, off:off + SSM_N] = rows_f[:, :, 0:SSM_N]
            h_ref[:, rr, off + SSM_N:off + LANES] = rows_r[:, :, SSM_N:LANES]
        return h_out

    zero = jnp.zeros(shape, F32)
    lax.fori_loop(0, n_pairs, body, (zero, zero))


def _toeplitz_blocks(k_ref, m_buf):
    lc = CHUNK
    lane = lax.broadcasted_iota(jnp.int32, (lc, LANES), 1)
    for q in range(SSM_P):
        for pp in range(SSM_P // 2):
            r0 = q * SSM_P + 2 * pp
            ka = jnp.broadcast_to(k_ref[0, r0:r0 + 1, :], (lc, LANES))
            kb = jnp.broadcast_to(k_ref[0, r0 + 1:r0 + 2, :], (lc, LANES))
            ra = pltpu.roll(ka, 0, 1, stride=1, stride_axis=0)
            rb = pltpu.roll(kb, lc, 1, stride=1, stride_axis=0)
            m_buf[q * lc:(q + 1) * lc, pp * LANES:(pp + 1) * LANES] = jnp.where(lane < lc, ra, rb).astype(BF16)


def _s5_out_kernel(u_ref, k_first_ref, k_next_ref, h_ref, pt_ref, vt_ref, y_ref, m_even, m_odd):
    g = pl.program_id(0)

    @pl.when(g == 0)
    def _():
        _toeplitz_blocks(k_first_ref, m_even)

    def step(m_mine, m_next):
        _toeplitz_blocks(k_next_ref, m_next)
        h = _exact_f32_dot(pt_ref[...], h_ref[0])
        y = _dot(u_ref[0], m_mine[...]) + _dot_nt(h.astype(BF16), vt_ref[0])
        y_ref[0] = y.astype(BF16)

    @pl.when(g % 2 == 0)
    def _():
        step(m_even, m_odd)

    @pl.when(g % 2 == 1)
    def _():
        step(m_odd, m_even)


def _chunk_major_rows(bsz, n_tok, n_ctx):
    n8, nj, ncb = S5_TILE // LANES, n_tok // S5_TILE, n_ctx // LANES
    ncc, n_chunks = n_ctx // CHUNK, (n_ctx + n_tok) // CHUNK
    n_lat = bsz * nj * S5_TILE_ROWS
    perm = []
    for c in range(n_chunks):
        for b in range(bsz):
            if c < ncc:
                blk, h = divmod(c, 2)
                perm.append(n_lat + h * bsz * ncb + b * ncb + blk)
            else:
                blk, h = divmod(c - ncc, 2)
                j, c8 = divmod(blk, n8)
                perm.append((b * nj + j) * S5_TILE_ROWS + h * n8 + c8)
    return perm


def _s5_apply(u, bsz, n_tok, n_ctx, ops):
    k_op, w_op, vt_op, al = ops
    ncc, n_chunks = n_ctx // CHUNK, (n_ctx + n_tok) // CHUNK
    rows = u.shape[1]
    assert rows == bsz * n_chunks and ncc % 2 == 0 and n_chunks % 2 == 0 and (2 * bsz) % 8 == 0
    gspec = lambda r, c: pl.BlockSpec((1, r, c), lambda g: (g, 0, 0))
    ospec = gspec
    to_chunks = jnp.zeros((rows, rows), F32).at[jnp.arange(rows), jnp.array(_chunk_major_rows(bsz, n_tok, n_ctx))].set(1.0)
    pspec = pl.BlockSpec((rows, rows), lambda g: (0, 0))
    sums = pl.pallas_call(
        _s5_sum_kernel,
        grid=(SSM_G,),
        in_specs=[gspec(rows, GC), ospec(GC, 2 * LANES), pspec],
        out_specs=gspec(rows, 2 * LANES),
        out_shape=jax.ShapeDtypeStruct((SSM_G, rows, 2 * LANES), F32),
        compiler_params=_cparams("parallel"),
        name="s5_chunk_sums",
    )(u, w_op, to_chunks)
    gt = 8
    sspec = pl.BlockSpec((gt, rows, 2 * LANES), lambda r: (r, 0, 0))
    states = pl.pallas_call(
        functools.partial(_s5_scan_kernel, bsz=bsz, n_ctx_chunks=ncc, n_chunks=n_chunks),
        grid=(SSM_G // gt,),
        in_specs=[sspec, pl.BlockSpec((gt, 1, 2 * LANES), lambda r: (r, 0, 0))],
        out_specs=sspec,
        out_shape=jax.ShapeDtypeStruct((SSM_G, rows, 2 * LANES), F32),
        compiler_params=_cparams("parallel"),
        name="s5_state_scan",
    )(sums, al)
    kspec = lambda f: pl.BlockSpec((1, SSM_P * SSM_P, LANES), lambda g: (f(g), 0, 0))
    return pl.pallas_call(
        _s5_out_kernel,
        grid=(SSM_G,),
        in_specs=[gspec(rows, GC), kspec(lambda g: 0), kspec(lambda g: jnp.minimum(g + 1, SSM_G - 1)),
                  gspec(rows, 2 * LANES), pspec, ospec(GC, 2 * LANES)],
        out_specs=gspec(rows, GC),
        out_shape=jax.ShapeDtypeStruct((SSM_G, rows, GC), BF16),
        scratch_shapes=[pltpu.VMEM((GC, GC), BF16), pltpu.VMEM((GC, GC), BF16)],
        compiler_params=_cparams("arbitrary"),
        name="s5_chunk_outputs",
    )(u, k_op, k_op, states, to_chunks.T, vt_op)


def _glu_kernel(y_ref, w_ref, b_ref, g_ref, o_ref):
    blocks = [y_ref[:, :, k * LANES:(k + 1) * LANES].astype(F32) for k in range(SSM_P // 2)]
    y = _from_group_layout(blocks).T
    z = 0.5 * y * (1.0 + jnp.tanh(math.sqrt(2.0 / math.pi) * (y + 0.044715 * (y * y * y))))
    t = _dot(z.astype(BF16), w_ref[0]) + b_ref[...]
    o = z * jax.nn.sigmoid(t)
    inv = lax.rsqrt(jnp.mean(o * o, axis=-1, keepdims=True) + LN_EPS)
    o_ref[...] = (o * inv * g_ref[...]).astype(BF16).reshape(o_ref.shape)


def _glu(y, bsz, n_tok, row_blk_off, per_batch, w_glu_bf, layer, b_glu, g_ssm):
    w = SSM_WIDTH
    if per_batch:
        grid = (bsz, n_tok // S5_TILE)
        ospec = pl.BlockSpec((1, S5_TILE, w), lambda b, j: (b, j, 0))
    else:
        assert bsz * n_tok == S5_TILE
        grid = (1, 1)
        ospec = pl.BlockSpec((bsz, n_tok, w), lambda b, j: (0, 0, 0))
    nj = grid[1]
    row = lambda a: a.reshape(1, w)
    vspec = pl.BlockSpec((1, w), lambda b, j: (0, 0))
    return pl.pallas_call(
        _glu_kernel,
        grid=grid,
        in_specs=[pl.BlockSpec((SSM_G, S5_TILE_ROWS, GC), lambda b, j: (0, row_blk_off + b * nj + j, 0)),
                  pl.BlockSpec((1, w, w), lambda b, j: (layer, 0, 0)), vspec, vspec],
        out_specs=ospec,
        out_shape=jax.ShapeDtypeStruct((bsz, n_tok, w), BF16),
        compiler_params=_cparams("parallel", "parallel"),
        name="s5_glu",
    )(y, w_glu_bf, row(b_glu), row(g_ssm))


def _layer_norm(z, g, b):
    mu = jnp.mean(z, axis=-1, keepdims=True)
    zc = z - mu
    var = jnp.mean(zc * zc, axis=-1, keepdims=True)
    return zc * lax.rsqrt(var + LN_EPS) * g + b


def _first_argmax(vals):
    best_i = jnp.zeros(vals[0].shape, jnp.int32)
    best_v = vals[0]
    for j in range(1, len(vals)):
        better = vals[j] > best_v
        best_i = jnp.where(better, j, best_i)
        best_v = jnp.where(better, vals[j], best_v)
    return best_i, best_v


def _route(logit_rows):
    m = functools.reduce(jnp.maximum, logit_rows)
    p = [jnp.exp(l - m) for l in logit_rows]
    scores = []
    for g in range(N_EXPERT_GROUPS):
        a, b, c, d = p[4 * g:4 * g + 4]
        hi1, lo1, hi2, lo2 = jnp.maximum(a, b), jnp.minimum(a, b), jnp.maximum(c, d), jnp.minimum(c, d)
        scores.append(jnp.maximum(hi1, hi2) + jnp.maximum(jnp.minimum(hi1, hi2), jnp.maximum(lo1, lo2)))
    grp, _ = _first_argmax(scores)
    sel = []
    for j in range(EXPERTS_PER_GROUP):
        v = p[j]
        for g in range(1, N_EXPERT_GROUPS):
            v = jnp.where(grp == g, p[4 * g + j], v)
        sel.append(v)
    i1, v1 = _first_argmax(sel)
    i2, v2 = _first_argmax([jnp.where(i1 == j, -1.0, sel[j]) for j in range(EXPERTS_PER_GROUP)])
    tot = v1 + v2
    return grp * EXPERTS_PER_GROUP + i1, grp * EXPERTS_PER_GROUP + i2, v1 / tot, v2 / tot


def _outproj_kernel(a_ref, s_ref, x_ref, mod_ref, w_ref, lng_ref, lnb_ref, wr_ref, br_ref, *rest, alpha, n_steps):
    x1_ref, u2_ref, ri_ref, rw_ref, acc_even, acc_odd = rest[-6:]
    i = pl.program_id(0)

    @pl.when(i == 0)
    def _():
        acc_odd[...] = jnp.zeros(acc_odd.shape, F32)

    def step(acc_mine, acc_prev):
        acc_mine[...] = _dot(a_ref[0], w_ref[0, 0:ATTN_WIDTH, :]) + _dot(s_ref[0], w_ref[0, ATTN_WIDTH:, :])
        y = acc_prev[...]
        g1 = mod_ref[0, 2:3, :]
        sh2 = mod_ref[0, 3:4, :]
        sc2 = mod_ref[0, 4:5, :]
        x1 = _layer_norm(alpha * x_ref[0] + g1 * y, lng_ref[...], lnb_ref[...])
        x1_ref[0] = x1
        u2 = x1 * (1.0 + sc2) + sh2
        u2_ref[...] = u2
        logits = _dot_nt(wr_ref[...], u2.astype(BF16)) + br_ref[...]
        e1, e2, w1, w2 = _route([logits[e:e + 1, :] for e in range(N_EXPERTS)])
        ri_ref[0, 0:1, :] = e1
        ri_ref[0, 1:2, :] = e2
        rw_ref[0, 0:1, :] = w1
        rw_ref[0, 1:2, :] = w2

    @pl.when((i <= n_steps) & (i % 2 == 0))
    def _():
        step(acc_even, acc_odd)

    @pl.when((i <= n_steps) & (i % 2 == 1))
    def _():
        step(acc_odd, acc_even)

    @pl.when(i > n_steps)
    def _():
        u2_ref[...] = jnp.zeros(u2_ref.shape, F32)


def _out_projection(attn_n, ssm_n, x, mod3, row_of_batch, w_out_bf, layer, ln_g, ln_b, w_router_t_bf, b_router,
                    alpha, u2_rows, row_off, u2_prev=None):
    bsz, n_tok, d = x.shape
    tm = min(ROW_TILE, n_tok)
    nt = n_tok // tm
    n_steps = bsz * nt
    assert row_off % tm == 0 and u2_rows % tm == 0
    n_fill = 0 if u2_prev is not None else u2_rows // tm - n_steps
    cur = lambda i: jnp.minimum(i, n_steps - 1)
    fin = lambda i: jnp.clip(i - 1, 0, n_steps - 1)
    pspec = lambda w: pl.BlockSpec((1, tm, w), lambda i: (cur(i) // nt, cur(i) % nt, 0))
    fspec = lambda w: pl.BlockSpec((1, tm, w), lambda i: (fin(i) // nt, fin(i) % nt, 0))
    vspec = pl.BlockSpec((1, d), lambda i: (0, 0))
    rspec = pl.BlockSpec((1, 2, tm), lambda i: (fin(i) // nt, 0, fin(i) % nt))
    in_specs = [
        pspec(ATTN_WIDTH), pspec(SSM_WIDTH), fspec(d),
        pl.BlockSpec((1, N_MOD, d), lambda i: (row_of_batch(fin(i) // nt), 0, 0)),
        pl.BlockSpec((1,) + w_out_bf.shape[1:], lambda i: (layer, 0, 0)),
        vspec, vspec,
        pl.BlockSpec((N_EXPERTS, d), lambda i: (0, 0)),
        pl.BlockSpec((N_EXPERTS, 1), lambda i: (0, 0)),
    ]
    args = [attn_n, ssm_n, x, mod3, w_out_bf, ln_g.reshape(1, d), ln_b.reshape(1, d), w_router_t_bf,
            b_router.reshape(N_EXPERTS, 1)]
    aliases = {}
    if u2_prev is not None:
        in_specs.append(pl.BlockSpec(memory_space=pl.ANY))
        args.append(u2_prev)
        aliases = {len(args) - 1: 1}
    return pl.pallas_call(
        functools.partial(_outproj_kernel, alpha=alpha, n_steps=n_steps),
        grid=(n_steps + 1 + n_fill,),
        in_specs=in_specs,
        out_specs=[fspec(d), pl.BlockSpec((tm, d), lambda i: (row_off // tm + jnp.maximum(i - 1, 0), 0)),
                   rspec, rspec],
        out_shape=[jax.ShapeDtypeStruct((bsz, n_tok, d), F32), jax.ShapeDtypeStruct((u2_rows, d), F32),
                   jax.ShapeDtypeStruct((bsz, 2, n_tok), jnp.int32), jax.ShapeDtypeStruct((bsz, 2, n_tok), F32)],
        scratch_shapes=[pltpu.VMEM((tm, d), F32), pltpu.VMEM((tm, d), F32)],
        input_output_aliases=aliases,
        compiler_params=_cparams("arbitrary"),
        name="out_projection",
    )(*args)


CAST_ROWS = 256
GATHER_AHEAD = 2


def _ffn_kernel(te_ref, nx_ref, nu_ref, dest_ref, u_hbm, zeros_hbm, wg_hbm, wu_hbm, wd_hbm, y_ref,
                xbuf, stage_g, stage_u, stage_d, wg_bf, wu_bf, wd_bf, src_ref, sem, gsem, csem, *, e0):
    t = pl.program_id(0)

    @pl.when(t == 0)
    def _():
        n_tok = dest_ref.shape[0] // 2

        clear = pltpu.make_async_copy(zeros_hbm, src_ref, csem.at[0])
        clear.start()
        clear.wait()

        group = 8

        def invert(j, carry):
            base = j * group
            rows = [dest_ref[half + base + k] for half in (0, n_tok) for k in range(group)]
            for k, row in enumerate(rows):
                src_ref[row] = base + k % group
            return carry
        lax.fori_loop(0, n_tok // group, invert, 0)

    e = te_ref[t]
    tm = xbuf.shape[1]
    pairs = ((wg_hbm, stage_g, wg_bf), (wu_hbm, stage_u, wu_bf), (wd_hbm, stage_d, wd_bf))

    def fetch(expert):
        return [pltpu.make_async_copy(hbm.at[e0 + expert], stage, sem.at[i])
                for i, (hbm, stage, _) in enumerate(pairs)]

    def gather_row(base, i, slot):
        r = src_ref[base + i]
        pltpu.make_async_copy(u_hbm.at[pl.ds(r, 1)], xbuf.at[slot, pl.ds(i, 1)], gsem.at[slot]).start()

    def wait_rows(slot):
        pltpu.make_async_copy(u_hbm.at[pl.ds(0, tm)], xbuf.at[slot], gsem.at[slot]).wait()

    last_tile = pl.num_programs(0) - 1

    @pl.when(t == 0)
    def _():
        for ahead in range(GATHER_AHEAD):
            def one_row(i, carry, ahead=ahead):
                gather_row(jnp.minimum(ahead, last_tile) * tm, i, ahead)
                return carry
            lax.fori_loop(0, tm, one_row, 0, unroll=8)
        for cp in fetch(e):
            cp.start()

    first_of_run = (t == 0) | (te_ref[jnp.maximum(t - 1, 0)] != e)

    @pl.when(first_of_run & (t < nu_ref[0]))
    def _():
        for cp in fetch(e):
            cp.wait()
        for _, stage, dst in pairs:
            def cast_rows(i, carry, stage=stage, dst=dst):
                rows = pl.ds(pl.multiple_of(i * CAST_ROWS, CAST_ROWS), CAST_ROWS)
                dst[rows, :] = stage[rows, :].astype(BF16)
                return carry
            lax.fori_loop(0, stage.shape[0] // CAST_ROWS, cast_rows, 0)

        @pl.when(nx_ref[t] >= 0)
        def _():
            for cp in fetch(nx_ref[t]):
                cp.start()

    @pl.when(t < nu_ref[0])
    def _():
        n_slots = GATHER_AHEAD + 1
        slot = t % n_slots
        wait_rows(slot)
        x = xbuf[slot].astype(BF16)
        base = jnp.minimum(t + GATHER_AHEAD, last_tile) * tm
        for i in range(tm):
            gather_row(base, i, (t + GATHER_AHEAD) % n_slots)
        g = _dot(x, wg_bf[...])
        u = _dot(x, wu_bf[...])
        h = (g * jax.nn.sigmoid(g) * u).astype(BF16)
        y_ref[...] = _dot(h, wd_bf[...])

    @pl.when(t == nu_ref[0] - 1)
    def _():
        for ahead in range(1, GATHER_AHEAD + 1):
            wait_rows((t + ahead) % (GATHER_AHEAD + 1))

    @pl.when(t >= nu_ref[0])
    def _():
        y_ref[...] = jnp.zeros(y_ref.shape, F32)


def _expert_ffn(u2, dest, n_rows, tile_expert, next_expert, n_used, w_gate, w_up, w_down, layer):
    d = u2.shape[1]
    n_e, f = w_gate.shape[1], w_gate.shape[3]
    tm = MOE_TILE
    flat = lambda w: w.reshape((-1,) + w.shape[2:])
    hbm = pl.BlockSpec(memory_space=pl.ANY)
    grid_spec = pltpu.PrefetchScalarGridSpec(
        num_scalar_prefetch=4,
        grid=(n_rows // tm,),
        in_specs=[hbm, hbm, hbm, hbm, hbm],
        out_specs=pl.BlockSpec((tm, d), lambda t, te, nx, nu, src: (t, 0)),
        scratch_shapes=[pltpu.VMEM((GATHER_AHEAD + 1, tm, d), F32),
                        pltpu.VMEM((d, f), F32), pltpu.VMEM((d, f), F32), pltpu.VMEM((f, d), F32),
                        pltpu.VMEM((d, f), BF16), pltpu.VMEM((d, f), BF16), pltpu.VMEM((f, d), BF16),
                        pltpu.SMEM((n_rows,), jnp.int32),
                        pltpu.SemaphoreType.DMA((3,)), pltpu.SemaphoreType.DMA((GATHER_AHEAD + 1,)),
                        pltpu.SemaphoreType.DMA((1,))],
    )
    return pl.pallas_call(
        functools.partial(_ffn_kernel, e0=layer * n_e),
        grid_spec=grid_spec,
        out_shape=jax.ShapeDtypeStruct((n_rows, d), F32),
        compiler_params=_cparams("arbitrary"),
        name="expert_ffn",
    )(tile_expert, next_expert, n_used, dest, u2, jnp.zeros((n_rows,), jnp.int32), flat(w_gate), flat(w_up),
      flat(w_down))


def _dispatch_plan(e1, e2, tm):
    n = e1.shape[0]
    n_tiles = (2 * n + N_EXPERTS * (tm - 1) + tm - 1) // tm
    e = jnp.concatenate([e1, e2])
    onehot = (e[:, None] == jnp.arange(N_EXPERTS, dtype=jnp.int32)[None, :]).astype(jnp.int32)
    csum = jnp.cumsum(onehot, axis=0)
    pos_in_e = jnp.sum((csum - 1) * onehot, axis=1)
    counts = csum[-1]
    padded = ((counts + tm - 1) // tm) * tm
    ends = jnp.cumsum(padded)
    offs = ends - padded
    dest = (offs[e] + pos_in_e).astype(jnp.int32)
    n_used = (ends[-1] // tm).astype(jnp.int32)
    tile_start = jnp.arange(n_tiles, dtype=jnp.int32) * tm
    tile_e = jnp.sum((tile_start[:, None] >= ends[None, :]).astype(jnp.int32), axis=1)
    tile_e = jnp.minimum(tile_e, N_EXPERTS - 1)
    last_e = tile_e[jnp.maximum(n_used - 1, 0)]
    tile_e = jnp.where(jnp.arange(n_tiles) < n_used, tile_e, last_e).astype(jnp.int32)
    ids = jnp.arange(N_EXPERTS, dtype=jnp.int32)
    later = (ids[None, :] > ids[:, None]) & (counts[None, :] > 0)
    next_of = jnp.min(jnp.where(later, ids[None, :], N_EXPERTS), axis=1)
    next_of = jnp.where(next_of == N_EXPERTS, -1, next_of).astype(jnp.int32)
    return dest, n_tiles * tm, tile_e, next_of[tile_e], n_used.reshape(1)


def _final_kernel(p1_ref, p2_ref, x_ref, ys_hbm, w1_ref, w2_ref, mod_ref, lng_ref, lnb_ref, o_ref, ybuf, gsem,
                  *, alpha, row0):
    i = pl.program_id(0)
    tm = x_ref.shape[1]
    last_tile = pl.num_programs(0) - 1
    n_slots = GATHER_AHEAD + 1

    def gather_row(tile, r, slot):
        for k, p_ref in enumerate((p1_ref, p2_ref)):
            src = p_ref[row0 + tile * tm + r]
            pltpu.make_async_copy(ys_hbm.at[pl.ds(src, 1)], ybuf.at[slot, k, pl.ds(r, 1)], gsem.at[slot]).start()

    def wait_rows(slot):
        for k in range(2):
            pltpu.make_async_copy(ys_hbm.at[pl.ds(0, tm)], ybuf.at[slot, k], gsem.at[slot]).wait()

    @pl.when(i == 0)
    def _():
        for ahead in range(GATHER_AHEAD):
            def one_row(r, carry, ahead=ahead):
                gather_row(jnp.minimum(ahead, last_tile), r, ahead)
                return carry
            lax.fori_loop(0, tm, one_row, 0, unroll=8)

    slot = i % n_slots
    wait_rows(slot)
    nxt = jnp.minimum(i + GATHER_AHEAD, last_tile)
    for r in range(tm):
        gather_row(nxt, r, (i + GATHER_AHEAD) % n_slots)
    f = w1_ref[...] * ybuf[slot, 0] + w2_ref[...] * ybuf[slot, 1]
    g2 = mod_ref[0, 5:6, :]
    o_ref[0] = _layer_norm(alpha * x_ref[0] + g2 * f, lng_ref[...], lnb_ref[...])

    @pl.when(i == last_tile)
    def _():
        for ahead in range(1, GATHER_AHEAD + 1):
            wait_rows((i + ahead) % n_slots)


def _combine_ln(x1, ys, pos1, pos2, w1, w2, row_off, mod3, row_of_batch, ln_g, ln_b, alpha):
    bsz, n_tok, d = x1.shape
    tm = min(ROW_TILE, n_tok)
    nt = n_tok // tm
    assert row_off % tm == 0
    tspec = pl.BlockSpec((1, tm, d), lambda i, p1, p2: (i // nt, i % nt, 0))
    wspec = pl.BlockSpec((tm, 1), lambda i, p1, p2: (row_off // tm + i, 0))
    vspec = pl.BlockSpec((1, d), lambda i, p1, p2: (0, 0))
    grid_spec = pltpu.PrefetchScalarGridSpec(
        num_scalar_prefetch=2,
        grid=(bsz * nt,),
        in_specs=[tspec, pl.BlockSpec(memory_space=pl.ANY), wspec, wspec,
                  pl.BlockSpec((1, N_MOD, d), lambda i, p1, p2: (row_of_batch(i // nt), 0, 0)), vspec, vspec],
        out_specs=tspec,
        scratch_shapes=[pltpu.VMEM((GATHER_AHEAD + 1, 2, tm, d), F32),
                        pltpu.SemaphoreType.DMA((GATHER_AHEAD + 1,))],
    )
    return pl.pallas_call(
        functools.partial(_final_kernel, alpha=alpha, row0=row_off),
        grid_spec=grid_spec,
        out_shape=jax.ShapeDtypeStruct((bsz, n_tok, d), F32),
        compiler_params=_cparams("arbitrary"),
        name="combine_post_ln",
    )(pos1, pos2, x1, ys, w1, w2, mod3, ln_g.reshape(1, d), ln_b.reshape(1, d))


def kernel(x, c, ctx, c_ctx, w_mod, b_mod, w_in, attn_sink, ssm_lambda_re, ssm_lambda_im, ssm_log_dt, ssm_b_re, ssm_b_im, ssm_c_re, ssm_c_im, ssm_d, w_glu, b_glu, g_attn_out, g_ssm_out, w_out, ln1_g, ln1_b, w_router, b_router, w_expert_gate, w_expert_up, w_expert_down, ln2_g, ln2_b):
    depth = w_mod.shape[0]
    bsz, n_tok, d = x.shape
    n_ctx = ctx.shape[1]
    alpha = (2 * depth) ** 0.25
    assert n_tok % ROW_TILE == 0 and n_tok % ATTN_BLOCK == 0 and n_ctx % ATTN_BLOCK == 0 and n_ctx % CHUNK == 0

    mod = _modulation(c, c_ctx, w_mod, b_mod)
    s5_params = (ssm_lambda_re, ssm_lambda_im, ssm_log_dt, ssm_b_re, ssm_b_im, ssm_c_re, ssm_c_im, ssm_d)
    expert_w = (w_expert_gate, w_expert_up, w_expert_down)
    s5_ops = _s5_operators(*s5_params, 0, jnp.zeros((1,), jnp.int32))
    cos_t, sin_t = _rope_tables(n_tok)
    w_router_t = w_router.T.astype(BF16)
    w_glu_bf, w_out_bf = w_glu.astype(BF16), w_out.astype(BF16)
    n_qkv = ATTN_WIDTH + 2 * KV_WIDTH
    w_in_bf = w_in[:, :, :n_qkv].astype(BF16)
    ws_t_bf = jnp.swapaxes(w_in[:, :, n_qkv:].astype(BF16), 1, 2)
    lat_row = lambda b: b
    ctx_row = lambda b: bsz
    n_lat, n_c = bsz * n_tok, bsz * n_ctx
    lat_blocks = n_lat // S5_TILE
    s5_rows = (lat_blocks + 1) * S5_TILE_ROWS

    xc = ctx
    for i in range(depth):
        last = i == depth - 1
        mod3 = mod[i].reshape(MOD_ROWS, N_MOD, d)
        n_moe = n_lat if last else n_lat + n_c

        q, k, v = _in_projection(x, mod3, lat_row, w_in_bf, i, cos_t, sin_t, True)
        qc, kc, vc = _in_projection(xc, mod3, ctx_row, w_in_bf, i, cos_t, sin_t, False)
        u = _s_projection(x, mod3, lat_row, ws_t_bf, i, s5_rows, 0)
        u = _s_projection(xc, mod3, ctx_row, ws_t_bf, i, s5_rows, lat_blocks, u_prev=u)
        attn_n = _attention(q, k, v, kc, vc, attn_sink[i], g_attn_out[i], True)
        y_s5 = _s5_apply(u, bsz, n_tok, n_ctx, s5_ops)
        ssm_n = _glu(y_s5, bsz, n_tok, 0, True, w_glu_bf, i, b_glu[i], g_ssm_out[i])
        x1, u2, ri, rw = _out_projection(attn_n, ssm_n, x, mod3, lat_row, w_out_bf, i, ln1_g[i], ln1_b[i],
                                         w_router_t, b_router, alpha, n_moe, 0)
        if not last:
            attn_c = _attention(qc, None, None, kc, vc, attn_sink[i], g_attn_out[i], False)
            ssm_c = _glu(y_s5, bsz, n_ctx, lat_blocks, False, w_glu_bf, i, b_glu[i], g_ssm_out[i])
            xc1, u2, ric, rwc = _out_projection(attn_c, ssm_c, xc, mod3, ctx_row, w_out_bf, i, ln1_g[i], ln1_b[i],
                                                w_router_t, b_router, alpha, n_moe, n_lat, u2_prev=u2)
            ri = jnp.concatenate([ri.transpose(1, 0, 2).reshape(2, n_lat), ric.transpose(1, 0, 2).reshape(2, n_c)], axis=1)
            rw = jnp.concatenate([rw.transpose(1, 0, 2).reshape(2, n_lat), rwc.transpose(1, 0, 2).reshape(2, n_c)], axis=1)
        else:
            ri = ri.transpose(1, 0, 2).reshape(2, n_lat)
            rw = rw.transpose(1, 0, 2).reshape(2, n_lat)

        dest, n_sorted, tile_e, next_e, n_used = _dispatch_plan(ri[0], ri[1], MOE_TILE)
        pos1, pos2 = dest[:n_moe], dest[n_moe:]
        if not last:
            s5_ops = _s5_operators(*s5_params, i + 1, n_used)
        ys = _expert_ffn(u2, dest, n_sorted, tile_e, next_e, n_used, *expert_w, i)
        cw1, cw2 = rw[0].reshape(n_moe, 1), rw[1].reshape(n_moe, 1)
        x = _combine_ln(x1, ys, pos1, pos2, cw1, cw2, 0, mod3, lat_row, ln2_g[i], ln2_b[i], alpha)
        if not last:
            xc = _combine_ln(xc1, ys, pos1, pos2, cw1, cw2, n_lat, mod3, ctx_row, ln2_g[i], ln2_b[i], alpha)
    return x
```
